```python
import math
import jax, jax.numpy as jnp
from jax import lax
import numpy as np

D_MODEL = 1024
BATCH = 8
SEQ = 2048
DEPTH = 1

CHUNK = 64
Q_BLOCK = 128
NORM_EPS = 1e-6

DA_HEADS = 8
DA_HEAD_DIM = 64
DA_V_DIM = 2 * DA_HEAD_DIM
ROPE_THETA = 500000.0
ROPE_DIM = DA_HEAD_DIM // 4
QK_WIDTH = DA_HEADS * 2 * DA_HEAD_DIM
V_WIDTH = DA_HEADS * DA_V_DIM

POOL_WINDOWS = (2, 4, 8, 16)
POOL_GROUPS = len(POOL_WINDOWS)
POOL_GROUP_DIM = 128
POOL_WIDTH = POOL_GROUPS * POOL_GROUP_DIM

N_BRANCHES = 2
IN_WIDTH = 2 * QK_WIDTH + V_WIDTH + POOL_WIDTH + N_BRANCHES * D_MODEL

N_EXPERTS = 32
TOP_K = 4
D_FF = D_MODEL
SWIGLU_LIMIT = 7.0
SWIGLU_ALPHA = 1.702

N_MOD = 6

kernel_name = "hybrid_diffattn_pool_moe_block"


def rms_norm(x, g):
    xf = x.astype(jnp.float32)
    y = xf * lax.rsqrt(jnp.mean(xf * xf, axis=-1, keepdims=True) + NORM_EPS)
    return (y * g.astype(jnp.float32)).astype(x.dtype)


def rope_tables(positions, dtype):
    inv_freq = ROPE_THETA ** (-jnp.arange(0, ROPE_DIM, 2, dtype=jnp.float32) / ROPE_DIM)
    ang = positions.astype(jnp.float32)[..., None] * inv_freq
    return jnp.cos(ang)[:, :, None, :].astype(dtype), jnp.sin(ang)[:, :, None, :].astype(dtype)


def apply_partial_rope(t, cos, sin):
    rot, rest = t[..., :ROPE_DIM], t[..., ROPE_DIM:]
    r1, r2 = rot[..., :ROPE_DIM // 2], rot[..., ROPE_DIM // 2:]
    rot = jnp.concatenate([r1 * cos - r2 * sin, r2 * cos + r1 * sin], axis=-1)
    return jnp.concatenate([rot, rest], axis=-1)


def diff_attention(q, k, v, lam, g_sub, lambda_init):
    B, S = q.shape[0], q.shape[1]
    nb = S // Q_BLOCK
    scale = DA_HEAD_DIM ** -0.5
    qb = q.reshape(B, nb, Q_BLOCK, 2 * DA_HEADS, DA_HEAD_DIM).transpose(1, 0, 3, 2, 4)
    kt = k.transpose(0, 2, 1, 3)
    vt = v.transpose(0, 2, 1, 3)
    key_chunk = jnp.arange(S) // CHUNK

    def block(args):
        q_blk, i = args
        s = jnp.einsum('bnqd,bnkd->bnqk', q_blk, kt).astype(jnp.float32) * scale
        q_chunk = (i * Q_BLOCK + jnp.arange(Q_BLOCK)) // CHUNK
        mask = key_chunk[None, :] <= q_chunk[:, None]
        s = jnp.where(mask, s, -jnp.inf)
        p = jax.nn.softmax(s, axis=-1).reshape(B, DA_HEADS, 2, Q_BLOCK, S)
        a = p[:, :, 0] - lam * p[:, :, 1]
        return jnp.einsum('bhqk,bhkd->bhqd', a.astype(vt.dtype), vt)

    o = lax.map(block, (qb, jnp.arange(nb)))
    o = o.transpose(1, 0, 3, 2, 4).reshape(B, S, DA_HEADS, DA_V_DIM)
    o = rms_norm(o, g_sub) * (1.0 - lambda_init)
    return o.reshape(B, S, V_WIDTH)


def multiscale_pool(p, w_group, pool_scale):
    B, S, _ = p.shape
    pf = p.astype(jnp.float32).reshape(B, S, POOL_GROUPS, POOL_GROUP_DIM)
    cs = jnp.concatenate([jnp.zeros((B, 1, POOL_GROUPS, POOL_GROUP_DIM), jnp.float32),
                          jnp.cumsum(pf, axis=1)], axis=1)
    t = jnp.arange(S)
    outs = []
    for g, w in enumerate(POOL_WINDOWS):
        start = jnp.maximum(t + 1 - w, 0)
        cnt = (t + 1 - start).astype(jnp.float32)
        mean = (cs[:, 1:, g] - cs[:, start, g]) / cnt[None, :, None]
        outs.append(mean - pf[:, :, g])
    mixed = jnp.stack(outs, axis=2).astype(p.dtype)
    y = jnp.einsum('bsgc,gcd->bsgd', mixed, w_group)
    return y.reshape(B, S, POOL_WIDTH) * pool_scale


def routed_experts(h, w_router, b_router, w_gate_up, b_gate_up, w_down, b_down):
    B, S, D = h.shape
    xt = h.reshape(-1, D)
    T = xt.shape[0]
    logits = (xt @ w_router).astype(jnp.float32) + b_router.astype(jnp.float32)
    top_v, top_i = lax.top_k(logits, TOP_K)
    probs = jax.nn.softmax(top_v, axis=-1)
    eid = top_i.reshape(-1)
    tok = jnp.repeat(jnp.arange(T), TOP_K)
    order = jnp.argsort(eid)
    eid_s, tok_s = eid[order], tok[order]
    w_s = probs.reshape(-1)[order]
    group_sizes = jnp.bincount(eid, length=N_EXPERTS).astype(jnp.int32)
    xs = xt[tok_s]
    gu = lax.ragged_dot(xs, w_gate_up, group_sizes) + b_gate_up[eid_s]
    gate, up = gu[:, :D_FF], gu[:, D_FF:]
    gate = jnp.minimum(gate, SWIGLU_LIMIT)
    up = jnp.clip(up, -SWIGLU_LIMIT, SWIGLU_LIMIT)
    act = (up + 1.0) * (gate * jax.nn.sigmoid(SWIGLU_ALPHA * gate))
    y = lax.ragged_dot(act, w_down, group_sizes) + b_down[eid_s]
    out = jnp.zeros((T, D), y.dtype).at[tok_s].add(y * w_s[:, None].astype(y.dtype))
    return out.reshape(B, S, D).astype(h.dtype)


def setup_inputs(seed: int = 0) -> dict:
    key = jax.random.key(seed)
    ks = jax.random.split(key, 32)
    f32 = jnp.float32
    nrm = lambda k, shape, s: jax.random.normal(k, shape, f32) * s
    gain = lambda k, shape: 1.0 + 0.05 * jax.random.normal(k, shape, f32)
    L = DEPTH
    offsets = jax.random.randint(ks[2], (BATCH, 1), 0, 64) * CHUNK
    positions = (offsets + jnp.arange(SEQ)[None, :]).astype(jnp.int32)
    return {
        "x": nrm(ks[0], (BATCH, SEQ, D_MODEL), 1.0),
        "c": nrm(ks[1], (BATCH, D_MODEL), 1.0),
        "positions": positions,
        "w_mod": nrm(ks[3], (L, D_MODEL, N_MOD * D_MODEL), D_MODEL ** -0.5),
        "b_mod": nrm(ks[4], (L, N_MOD * D_MODEL), 0.02),
        "g_pre_mix": gain(ks[5], (L, D_MODEL)),
        "w_in": nrm(ks[6], (L, D_MODEL, IN_WIDTH), D_MODEL ** -0.5),
        "lambda_q1": nrm(ks[7], (L, DA_HEAD_DIM), 0.1),
        "lambda_k1": nrm(ks[8], (L, DA_HEAD_DIM), 0.1),
        "lambda_q2": nrm(ks[9], (L, DA_HEAD_DIM), 0.1),
        "lambda_k2": nrm(ks[10], (L, DA_HEAD_DIM), 0.1),
        "g_sub": gain(ks[11], (L, DA_V_DIM)),
        "w_pool_group": nrm(ks[12], (L, POOL_GROUPS, POOL_GROUP_DIM, POOL_GROUP_DIM), POOL_GROUP_DIM ** -0.5),
        "pool_scale": gain(ks[13], (L, POOL_WIDTH)),
        "w_att_branch": nrm(ks[14], (L, V_WIDTH, D_MODEL), V_WIDTH ** -0.5),
        "w_pool_branch": nrm(ks[15], (L, POOL_WIDTH, D_MODEL), POOL_WIDTH ** -0.5),
        "w_out": nrm(ks[16], (L, D_MODEL, D_MODEL), D_MODEL ** -0.5),
        "g_post_mix": gain(ks[17], (L, D_MODEL)),
        "g_pre_ffn": gain(ks[18], (L, D_MODEL)),
        "w_router": nrm(ks[19], (L, D_MODEL, N_EXPERTS), D_MODEL ** -0.5),
        "b_router": nrm(ks[20], (L, N_EXPERTS), 0.01),
        "w_gate_up": nrm(ks[21], (L, N_EXPERTS, D_MODEL, 2 * D_FF), D_MODEL ** -0.5),
        "b_gate_up": nrm(ks[22], (L, N_EXPERTS, 2 * D_FF), 0.01),
        "w_down": nrm(ks[23], (L, N_EXPERTS, D_FF, D_MODEL), D_FF ** -0.5),
        "b_down": nrm(ks[24], (L, N_EXPERTS, D_MODEL), 0.01),
        "g_post_ffn": gain(ks[25], (L, D_MODEL)),
    }


def reference(x, c, positions, w_mod, b_mod, g_pre_mix, w_in, lambda_q1, lambda_k1,
              lambda_q2, lambda_k2, g_sub, w_pool_group, pool_scale, w_att_branch,
              w_pool_branch, w_out, g_post_mix, g_pre_ffn, w_router, b_router,
              w_gate_up, b_gate_up, w_down, b_down, g_post_ffn):
    B, S, D = x.shape
    cos, sin = rope_tables(positions, x.dtype)
    c_act = jax.nn.silu(c)
    splits = np.cumsum([QK_WIDTH, QK_WIDTH, V_WIDTH, POOL_WIDTH, D_MODEL]).tolist()
    for l in range(DEPTH):
        lambda_init = 0.8 - 0.6 * math.exp(-0.3 * l)
        mod = (c_act @ w_mod[l] + b_mod[l])[:, None, :]
        shift_m, scale_m, gate_m, shift_f, scale_f, gate_f = jnp.split(mod, N_MOD, axis=-1)

        u = rms_norm(x, g_pre_mix[l]) * (1.0 + scale_m) + shift_m
        z = u @ w_in[l]
        q, k, v, p, ga, gp = jnp.split(z, splits, axis=-1)
        q = apply_partial_rope(q.reshape(B, S, 2 * DA_HEADS, DA_HEAD_DIM), cos, sin)
        k = apply_partial_rope(k.reshape(B, S, 2 * DA_HEADS, DA_HEAD_DIM), cos, sin)
        v = v.reshape(B, S, DA_HEADS, DA_V_DIM)
        lam = (jnp.exp(jnp.sum(lambda_q1[l].astype(jnp.float32) * lambda_k1[l].astype(jnp.float32)))
               - jnp.exp(jnp.sum(lambda_q2[l].astype(jnp.float32) * lambda_k2[l].astype(jnp.float32)))
               + lambda_init)
        y_att = diff_attention(q, k, v, lam, g_sub[l], lambda_init) @ w_att_branch[l]
        y_pool = multiscale_pool(p, w_pool_group[l], pool_scale[l]) @ w_pool_branch[l]
        merged = jax.nn.sigmoid(ga) * y_att + jax.nn.sigmoid(gp) * y_pool
        mix_out = merged @ w_out[l]
        x = x + gate_m * rms_norm(mix_out, g_post_mix[l])

        u = rms_norm(x, g_pre_ffn[l]) * (1.0 + scale_f) + shift_f
        f = routed_experts(u, w_router[l], b_router[l], w_gate_up[l], b_gate_up[l],
                           w_down[l], b_down[l])
        x = x + gate_f * rms_norm(f, g_post_ffn[l])
    return x
```

```python
import functools

import jax
import jax.numpy as jnp
from jax import lax
from jax.experimental import pallas as pl
from jax.experimental.pallas import tpu as pltpu

F32 = jnp.float32
BF16 = jnp.bfloat16

NORM_EPS = 1e-6
CHUNK = 64
DA_HEADS = 8
DA_HEAD_DIM = 64
DA_V_DIM = 2 * DA_HEAD_DIM
ROPE_THETA = 500000.0
ROPE_DIM = DA_HEAD_DIM // 4
POOL_WINDOWS = (2, 4, 8, 16)
POOL_GROUP_DIM = 128
TOP_K = 4
SWIGLU_LIMIT = 7.0
SWIGLU_ALPHA = 1.702
N_MOD = 6
LAMBDA_INIT = 0.8 - 0.6 * 1.0

LANES = 128
SUBLANES = 8
VMEM_LIMIT = 56 * 1024 * 1024

TM_PROJ = 256
TQ = 256
TK = 256
TM_MIX = 256
TM_EXP = 256
TM_COMB = 256
POOL_HALO = 16
NEG_BIG = -1e30


def _cparams(sem):
    return pltpu.CompilerParams(dimension_semantics=sem, vmem_limit_bytes=VMEM_LIMIT)


def _split_dot(a, b):
    a_hi = a.astype(BF16)
    a_lo = (a - a_hi.astype(F32)).astype(BF16)
    b_hi = b.astype(BF16)
    b_lo = (b - b_hi.astype(F32)).astype(BF16)
    dot = functools.partial(jnp.dot, preferred_element_type=F32)
    return dot(a_hi, b_hi) + (dot(a_hi, b_lo) + dot(a_lo, b_hi))


def _rms(x):
    return x * lax.rsqrt(jnp.mean(x * x, axis=-1, keepdims=True) + NORM_EPS)


def _mod_kernel(c_ref, w_ref, b_ref, o_ref):
    c = c_ref[...]
    c_act = c * jax.nn.sigmoid(c)
    o_ref[...] = _split_dot(c_act, w_ref[...]) + b_ref[...]


def _mod(c, w_mod, b_mod):
    bsz, d = c.shape
    n = w_mod.shape[1]
    tn = 1024
    return pl.pallas_call(
        _mod_kernel,
        grid=(n // tn,),
        in_specs=[
            pl.BlockSpec((bsz, d), lambda j: (0, 0)),
            pl.BlockSpec((d, tn), lambda j: (0, j)),
            pl.BlockSpec((1, tn), lambda j: (0, j)),
        ],
        out_specs=pl.BlockSpec((bsz, tn), lambda j: (0, j)),
        out_shape=jax.ShapeDtypeStruct((bsz, n), F32),
        compiler_params=_cparams(("parallel",)),
        name="mod",
    )(c, w_mod, b_mod.reshape(1, n))


def _rope(t, cos_t, sin_a, sin_b):
    n = t.shape[1]
    up = pltpu.roll(t, n - ROPE_DIM // 2, axis=1)
    dn = pltpu.roll(t, ROPE_DIM // 2, axis=1)
    reps = n // LANES
    tile = lambda a: jnp.concatenate([a] * reps, axis=1)
    return t * tile(cos_t) + up * tile(sin_a) + dn * tile(sin_b)


def _in_proj_kernel(x_ref, mod_ref, g_ref, cos_ref, sa_ref, sb_ref,
                    wq_ref, wk_ref, wv_ref, wp_ref, wga_ref, wgp_ref,
                    q_ref, k_ref, v_ref, p_ref, sga_ref, sgp_ref):
    x = x_ref[...]
    shift = mod_ref[0, 0:1, :]
    scale = mod_ref[0, 1:2, :]
    u = (_rms(x) * g_ref[...]) * (1.0 + scale) + shift
    ub = u.astype(BF16)
    dot = functools.partial(jnp.dot, preferred_element_type=F32)
    cos_t, sin_a, sin_b = cos_ref[...], sa_ref[...], sb_ref[...]
    q = _rope(dot(ub, wq_ref[...]), cos_t, sin_a, sin_b)
    q_ref[...] = (q * (DA_HEAD_DIM ** -0.5)).astype(BF16)
    k = _rope(dot(ub, wk_ref[...]), cos_t, sin_a, sin_b)
    k_ref[...] = k.astype(BF16)
    v_ref[...] = dot(ub, wv_ref[...]).astype(BF16)
    p_ref[...] = dot(ub, wp_ref[...])
    sga_ref[...] = jax.nn.sigmoid(dot(ub, wga_ref[...])).astype(BF16)
    sgp_ref[...] = jax.nn.sigmoid(dot(ub, wgp_ref[...])).astype(BF16)


def _in_proj(x2, mod3, g_pre, cos_t, sin_a, sin_b, w_parts, seq):
    t, d = x2.shape
    tm = TM_PROJ
    steps_per_seq = seq // tm
    row = lambda w: pl.BlockSpec((tm, w), lambda i: (i, 0))
    full = lambda a: pl.BlockSpec(a.shape, lambda i: (0, 0))
    widths = [w.shape[1] for w in w_parts]
    out_dtypes = [BF16, BF16, BF16, F32, BF16, BF16]
    return pl.pallas_call(
        _in_proj_kernel,
        grid=(t // tm,),
        in_specs=[
            row(d),
            pl.BlockSpec((1, N_MOD, d), lambda i: (i // steps_per_seq, 0, 0)),
            pl.BlockSpec((1, d), lambda i: (0, 0)),
            row(LANES), row(LANES), row(LANES),
        ] + [full(w) for w in w_parts],
        out_specs=[row(w) for w in widths],
        out_shape=[jax.ShapeDtypeStruct((t, w), dt) for w, dt in zip(widths, out_dtypes)],
        compiler_params=_cparams(("parallel",)),
        name="in_proj",
    )(x2, mod3, g_pre, cos_t, sin_a, sin_b, *w_parts)


def _attn_kernel(lq1_ref, lk1_ref, lq2_ref, lk2_ref, g_ref, q_ref, k_ref, v_ref, o_ref):
    qi = pl.program_id(2)
    q = q_ref[...]
    lane = lax.broadcasted_iota(jnp.int32, q.shape, 1)
    zero = jnp.zeros_like(q)
    qq = jnp.concatenate([jnp.where(lane < DA_HEAD_DIM, q, zero),
                          jnp.where(lane >= DA_HEAD_DIM, q, zero)], axis=0)

    def scores(j):
        kt = k_ref[pl.ds(pl.multiple_of(j * TK, TK), TK), :]
        return lax.dot_general(qq, kt, (((1,), (1,)), ((), ())), preferred_element_type=F32)

    def update(j, s, carry):
        m, l, acc = carry
        m_new = jnp.maximum(m, jnp.max(s, axis=-1, keepdims=True))
        alpha = jnp.exp(m - m_new)
        p = jnp.exp(s - m_new)
        l_new = alpha * l + jnp.sum(p, axis=-1, keepdims=True)
        vt = v_ref[pl.ds(pl.multiple_of(j * TK, TK), TK), :]
        acc_new = alpha * acc + jnp.dot(p.astype(BF16), vt, preferred_element_type=F32)
        return m_new, l_new, acc_new

    def body(j, carry):
        return update(j, scores(j), carry)

    init = (jnp.full((2 * TQ, 1), NEG_BIG, F32), jnp.zeros((2 * TQ, 1), F32),
            jnp.zeros((2 * TQ, DA_V_DIM), F32))
    carry = lax.fori_loop(0, qi, body, init)
    s = scores(qi)
    r = lax.broadcasted_iota(jnp.int32, s.shape, 0)
    cidx = lax.broadcasted_iota(jnp.int32, s.shape, 1)
    q_chunk = jnp.where(r >= TQ, r - TQ, r) // CHUNK
    s = jnp.where(cidx // CHUNK <= q_chunk, s, NEG_BIG)
    m, l, acc = update(qi, s, carry)

    o = acc / l
    lam = (jnp.exp(jnp.sum(lq1_ref[...] * lk1_ref[...], axis=-1, keepdims=True))
           - jnp.exp(jnp.sum(lq2_ref[...] * lk2_ref[...], axis=-1, keepdims=True))
           + LAMBDA_INIT)
    a = o[:TQ] - lam * o[TQ:]
    o_ref[...] = ((_rms(a) * g_ref[...]) * (1.0 - LAMBDA_INIT)).astype(BF16)


def _attention(q, k, v, lq1, lk1, lq2, lk2, g_sub, bsz, seq):
    t = q.shape[0]
    nq = seq // TQ
    vec = lambda a: pl.BlockSpec(a.shape, lambda b, h, i: (0, 0))
    return pl.pallas_call(
        _attn_kernel,
        grid=(bsz, DA_HEADS, nq),
        in_specs=[
            vec(lq1), vec(lk1), vec(lq2), vec(lk2), vec(g_sub),
            pl.BlockSpec((TQ, LANES), lambda b, h, i: (b * nq + i, h)),
            pl.BlockSpec((seq, LANES), lambda b, h, i: (b, h)),
            pl.BlockSpec((seq, DA_V_DIM), lambda b, h, i: (b, h)),
        ],
        out_specs=pl.BlockSpec((TQ, DA_V_DIM), lambda b, h, i: (b * nq + i, h)),
        out_shape=jax.ShapeDtypeStruct((t, DA_HEADS * DA_V_DIM), BF16),
        compiler_params=_cparams(("parallel", "parallel", "arbitrary")),
        name="attention",
    )(lq1, lk1, lq2, lk2, g_sub, q, k, v)


def _post_mix_kernel(o_ref, p_ref, ph_ref, sga_ref, sgp_ref, x_ref, mod_ref,
                     watt_ref, wpg_ref, ps_ref, wpb_ref, wout_ref, gpm_ref, gpf_ref,
                     wr_ref, br_ref,
                     x1_ref, u2_ref, topi_ref, prob_ref, rank_ref, cnt_ref,
                     carry_ref, *, steps_per_seq, n_experts):
    i = pl.program_id(0)
    tm = x_ref.shape[0]
    dot = functools.partial(jnp.dot, preferred_element_type=F32)

    y_att = dot(o_ref[...], watt_ref[...])

    first = (i % steps_per_seq) == 0
    halo = jnp.where(first, 0.0, ph_ref[...])
    ext = jnp.concatenate([halo, p_ref[...]], axis=0)
    t_in_seq = (i % steps_per_seq) * tm + lax.broadcasted_iota(jnp.int32, (tm, 1), 0)
    pooled = []
    for g, w in enumerate(POOL_WINDOWS):
        e = ext[:, g * POOL_GROUP_DIM:(g + 1) * POOL_GROUP_DIM]
        acc, span = e, 1
        while span < w:
            acc = acc[span:] + acc[:-span]
            span *= 2
        win = acc[POOL_HALO - (w - 1):]
        cnt = jnp.minimum(t_in_seq + 1, w).astype(F32)
        mixed = win / cnt - e[POOL_HALO:]
        pooled.append(dot(mixed.astype(BF16), wpg_ref[g]))
    y_pool_in = jnp.concatenate(pooled, axis=1) * ps_ref[...]
    y_pool = dot(y_pool_in.astype(BF16), wpb_ref[...])

    merged = sga_ref[...].astype(F32) * y_att + sgp_ref[...].astype(F32) * y_pool
    mix_out = dot(merged.astype(BF16), wout_ref[...])
    gate_m = mod_ref[0, 2:3, :]
    shift_f = mod_ref[0, 3:4, :]
    scale_f = mod_ref[0, 4:5, :]
    x1 = x_ref[...] + gate_m * (_rms(mix_out) * gpm_ref[...])
    x1_ref[...] = x1
    u2 = (_rms(x1) * gpf_ref[...]) * (1.0 + scale_f) + shift_f
    n_sub = u2.shape[1] // LANES
    for s in range(n_sub):
        u2_ref[pl.ds(s, tm, stride=n_sub), :] = u2[:, s * LANES:(s + 1) * LANES]

    logits = _split_dot(u2, wr_ref[...]) + br_ref[...]
    lane = lax.broadcasted_iota(jnp.int32, logits.shape, 1)
    work = logits
    vals, idxs = [], []
    for _ in range(TOP_K):
        mx = jnp.max(work, axis=-1, keepdims=True)
        ix = jnp.min(jnp.where(work == mx, lane, n_experts), axis=-1, keepdims=True)
        vals.append(mx)
        idxs.append(ix)
        work = jnp.where(lane == ix, -jnp.inf, work)
    exps = [jnp.exp(vv - vals[0]) for vv in vals]
    denom = exps[0] + exps[1] + exps[2] + exps[3]
    prob_ref[...] = jnp.concatenate([e / denom for e in exps], axis=1)
    topi_ref[...] = jnp.concatenate(idxs, axis=1)

    @pl.when(i == 0)
    def _():
        carry_ref[...] = jnp.zeros_like(carry_ref)

    onehot = jnp.zeros(logits.shape, F32)
    for ix in idxs:
        onehot = onehot + (lane == ix).astype(F32)
    rr = lax.broadcasted_iota(jnp.int32, (tm, tm), 0)
    cc = lax.broadcasted_iota(jnp.int32, (tm, tm), 1)
    tri = (cc < rr).astype(BF16)
    before = dot(tri, onehot.astype(BF16)) + carry_ref[...]
    ranks = [jnp.sum(jnp.where(lane == ix, before, 0.0), axis=-1, keepdims=True) for ix in idxs]
    rank_ref[...] = jnp.concatenate(ranks, axis=1).astype(jnp.int32)
    carry_ref[...] = carry_ref[...] + jnp.sum(onehot, axis=0, keepdims=True)
    cnt_ref[...] = carry_ref[...].astype(jnp.int32)


def _post_mix(o, p, sga, sgp, x2, mod3, w_att, w_pg, pool_scale, w_pb, w_out,
              g_post_mix, g_pre_ffn, w_router, b_router, seq):
    t, d = x2.shape
    tm = TM_MIX
    steps_per_seq = seq // tm
    n_experts = w_router.shape[1]
    pw = p.shape[1]
    n_sub = d // LANES
    halo_blocks = tm // POOL_HALO
    row = lambda w: pl.BlockSpec((tm, w), lambda i: (i, 0))
    full2 = lambda a: pl.BlockSpec(a.shape, lambda i: (0, 0))
    kern = functools.partial(_post_mix_kernel, steps_per_seq=steps_per_seq, n_experts=n_experts)
    return pl.pallas_call(
        kern,
        grid=(t // tm,),
        in_specs=[
            row(d), row(pw),
            pl.BlockSpec((POOL_HALO, pw), lambda i: (jnp.maximum(i * halo_blocks - 1, 0), 0)),
            row(d), row(d), row(d),
            pl.BlockSpec((1, N_MOD, d), lambda i: (i // steps_per_seq, 0, 0)),
            full2(w_att),
            pl.BlockSpec(w_pg.shape, lambda i: (0, 0, 0)),
            full2(pool_scale), full2(w_pb), full2(w_out), full2(g_post_mix), full2(g_pre_ffn),
            full2(w_router), full2(b_router),
        ],
        out_specs=[
            row(d),
            pl.BlockSpec((tm * n_sub, LANES), lambda i: (i, 0)),
            row(TOP_K), row(TOP_K), row(TOP_K),
            pl.BlockSpec((1, n_experts), lambda i: (0, 0)),
        ],
        out_shape=[
            jax.ShapeDtypeStruct((t, d), F32),
            jax.ShapeDtypeStruct((t * n_sub, LANES), F32),
            jax.ShapeDtypeStruct((t, TOP_K), jnp.int32),
            jax.ShapeDtypeStruct((t, TOP_K), F32),
            jax.ShapeDtypeStruct((t, TOP_K), jnp.int32),
            jax.ShapeDtypeStruct((1, n_experts), jnp.int32),
        ],
        scratch_shapes=[pltpu.VMEM((1, n_experts), F32)],
        compiler_params=_cparams(("arbitrary",)),
        name="post_mix",
    )(o, p, p, sga, sgp, x2, mod3, w_att, w_pg, pool_scale, w_pb, w_out,
      g_post_mix, g_pre_ffn, w_router, b_router)


def _experts_kernel(te_ref, nt_ref, src_ref, u2_hbm, wgu_ref, bgu_ref, wd_ref, bd_ref,
                    y_ref, xbuf, wgu_bf, wd_bf, sem, *, n_sub):
    j = pl.program_id(0)
    tm = TM_EXP
    d_ff = wd_ref.shape[1]
    valid = j < nt_ref[0]

    @pl.when(valid)
    def _():
        def row_copy(r):
            tok = src_ref[j * tm + r]
            return pltpu.make_async_copy(
                u2_hbm.at[pl.ds(pl.multiple_of(tok * n_sub, n_sub), n_sub), :],
                xbuf.at[pl.ds(pl.multiple_of(r * n_sub, n_sub), n_sub), :], sem)

        def issue(r, c):
            row_copy(r).start()
            return c
        lax.fori_loop(0, tm, issue, 0)

        changed = jnp.logical_or(j == 0, te_ref[j] != te_ref[jnp.maximum(j - 1, 0)])

        @pl.when(changed)
        def _():
            wgu_bf[...] = wgu_ref[0].astype(BF16)
            wd_bf[...] = wd_ref[0].astype(BF16)

        pltpu.make_async_copy(u2_hbm.at[pl.ds(0, tm * n_sub), :], xbuf, sem).wait()

        xs = jnp.concatenate(
            [xbuf[pl.ds(s, tm, stride=n_sub), :] for s in range(n_sub)], axis=1).astype(BF16)
        gu = jnp.dot(xs, wgu_bf[...], preferred_element_type=F32) + bgu_ref[0]
        gate = jnp.minimum(gu[:, :d_ff], SWIGLU_LIMIT)
        up = jnp.clip(gu[:, d_ff:], -SWIGLU_LIMIT, SWIGLU_LIMIT)
        act = (up + 1.0) * (gate * jax.nn.sigmoid(SWIGLU_ALPHA * gate))
        y = jnp.dot(act.astype(BF16), wd_bf[...], preferred_element_type=F32) + bd_ref[0]
        for s in range(n_sub):
            y_ref[pl.ds(s, tm, stride=n_sub), :] = y[:, s * LANES:(s + 1) * LANES]

    @pl.when(jnp.logical_not(valid))
    def _():
        y_ref[...] = jnp.zeros_like(y_ref)


def _experts(tile_expert, n_tiles, row_src, u2_tiles, w_gate_up, b_gate_up, w_down, b_down):
    n_exp, d, d_gu = w_gate_up.shape
    d_ff = w_down.shape[1]
    n_sub = d // LANES
    tm = TM_EXP
    max_tiles = tile_expert.shape[0]
    grid_spec = pltpu.PrefetchScalarGridSpec(
        num_scalar_prefetch=3,
        grid=(max_tiles,),
        in_specs=[
            pl.BlockSpec(memory_space=pl.ANY),
            pl.BlockSpec((1, d, d_gu), lambda j, te, nt, src: (te[j], 0, 0)),
            pl.BlockSpec((1, 1, d_gu), lambda j, te, nt, src: (te[j], 0, 0)),
            pl.BlockSpec((1, d_ff, d), lambda j, te, nt, src: (te[j], 0, 0)),
            pl.BlockSpec((1, 1, d), lambda j, te, nt, src: (te[j], 0, 0)),
        ],
        out_specs=pl.BlockSpec((tm * n_sub, LANES), lambda j, te, nt, src: (j, 0)),
        scratch_shapes=[
            pltpu.VMEM((tm * n_sub, LANES), F32),
            pltpu.VMEM((d, d_gu), BF16),
            pltpu.VMEM((d_ff, d), BF16),
            pltpu.SemaphoreType.DMA,
        ],
    )
    return pl.pallas_call(
        functools.partial(_experts_kernel, n_sub=n_sub),
        grid_spec=grid_spec,
        out_shape=jax.ShapeDtypeStruct((max_tiles * tm * n_sub, LANES), F32),
        compiler_params=_cparams(("arbitrary",)),
        name="experts",
    )(tile_expert, n_tiles, row_src, u2_tiles,
      w_gate_up, b_gate_up.reshape(n_exp, 1, d_gu), w_down, b_down.reshape(n_exp, 1, d))


def _combine_kernel(pos_ref, y_hbm, prob_ref, x1_ref, mod_ref, g_ref, o_ref, ybuf, sem, *, n_sub):
    i = pl.program_id(0)
    tm = TM_COMB

    def issue(r, c):
        for k in range(TOP_K):
            slot = pos_ref[(i * tm + r) * TOP_K + k]
            pltpu.make_async_copy(
                y_hbm.at[pl.ds(pl.multiple_of(slot * n_sub, n_sub), n_sub), :],
                ybuf.at[k, pl.ds(pl.multiple_of(r * n_sub, n_sub), n_sub), :], sem).start()
        return c
    lax.fori_loop(0, tm, issue, 0)
    for k in range(TOP_K):
        pltpu.make_async_copy(y_hbm.at[pl.ds(0, tm * n_sub), :], ybuf.at[k], sem).wait()

    prob = prob_ref[...]
    f = None
    for k in range(TOP_K):
        yk = jnp.concatenate(
            [ybuf[k, pl.ds(s, tm, stride=n_sub), :] for s in range(n_sub)], axis=1)
        term = yk * prob[:, k:k + 1]
        f = term if f is None else f + term
    gate_f = mod_ref[0, 5:6, :]
    o_ref[...] = x1_ref[...] + gate_f * (_rms(f) * g_ref[...])


def _combine(pos_flat, y_tiles, probs, x1, mod3, g_post_ffn, seq):
    t, d = x1.shape
    tm = TM_COMB
    n_sub = d // LANES
    steps_per_seq = seq // tm
    grid_spec = pltpu.PrefetchScalarGridSpec(
        num_scalar_prefetch=1,
        grid=(t // tm,),
        in_specs=[
            pl.BlockSpec(memory_space=pl.ANY),
            pl.BlockSpec((tm, TOP_K), lambda i, pos: (i, 0)),
            pl.BlockSpec((tm, d), lambda i, pos: (i, 0)),
            pl.BlockSpec((1, N_MOD, d), lambda i, pos: (i // steps_per_seq, 0, 0)),
            pl.BlockSpec((1, d), lambda i, pos: (0, 0)),
        ],
        out_specs=pl.BlockSpec((tm, d), lambda i, pos: (i, 0)),
        scratch_shapes=[
            pltpu.VMEM((TOP_K, tm * n_sub, LANES), F32),
            pltpu.SemaphoreType.DMA,
        ],
    )
    return pl.pallas_call(
        functools.partial(_combine_kernel, n_sub=n_sub),
        grid_spec=grid_spec,
        out_shape=jax.ShapeDtypeStruct((t, d), F32),
        compiler_params=_cparams(("arbitrary",)),
        name="combine",
    )(pos_flat, y_tiles, probs, x1, mod3, g_post_ffn)


def _rope_tables(positions):
    half = ROPE_DIM // 2
    inv_freq = ROPE_THETA ** (-jnp.arange(0, ROPE_DIM, 2, dtype=F32) / ROPE_DIM)
    ang = positions.astype(F32).reshape(-1, 1) * inv_freq
    cos, sin = jnp.cos(ang), jnp.sin(ang)
    t = ang.shape[0]
    ones = jnp.ones((t, DA_HEAD_DIM - ROPE_DIM), F32)
    zeros = jnp.zeros((t, DA_HEAD_DIM - ROPE_DIM), F32)
    zh = jnp.zeros((t, half), F32)
    cos_h = jnp.concatenate([cos, cos, ones], axis=1)
    sa_h = jnp.concatenate([-sin, zh, zeros], axis=1)
    sb_h = jnp.concatenate([zh, sin, zeros], axis=1)
    dup = lambda a: jnp.concatenate([a] * (LANES // DA_HEAD_DIM), axis=1)
    return dup(cos_h), dup(sa_h), dup(sb_h)


def _routing_tables(counts, top_i, rank, n_tokens):
    tm = TM_EXP
    n_exp = counts.shape[0]
    max_tiles = (n_tokens * TOP_K) // tm + n_exp
    tiles = (counts + tm - 1) // tm
    tile_end = jnp.cumsum(tiles)
    n_tiles = tile_end[-1]
    row_off = (tile_end - tiles) * tm
    pos = row_off[top_i] + rank
    tok = jnp.broadcast_to(jnp.arange(n_tokens, dtype=jnp.int32)[:, None], pos.shape)
    row_src = jnp.zeros((max_tiles * tm,), jnp.int32).at[pos.reshape(-1)].set(tok.reshape(-1))
    jj = jnp.minimum(jnp.arange(max_tiles, dtype=jnp.int32), n_tiles - 1)
    tile_expert = jnp.searchsorted(tile_end, jj, side="right").astype(jnp.int32)
    return tile_expert, n_tiles.reshape(1).astype(jnp.int32), row_src, pos.reshape(-1).astype(jnp.int32)


def kernel(x, c, positions, w_mod, b_mod, g_pre_mix, w_in, lambda_q1, lambda_k1, lambda_q2, lambda_k2, g_sub, w_pool_group, pool_scale, w_att_branch, w_pool_branch, w_out, g_post_mix, g_pre_ffn, w_router, b_router, w_gate_up, b_gate_up, w_down, b_down, g_post_ffn):
    bsz, seq, d = x.shape
    assert w_mod.shape[0] == 1, "single layer"
    assert seq % TQ == 0 and seq % TM_PROJ == 0 and seq % TM_MIX == 0 and seq % TM_COMB == 0
    t = bsz * seq
    x2 = x.reshape(t, d)
    qk_w = 2 * DA_HEADS * DA_HEAD_DIM
    v_w = DA_HEADS * DA_V_DIM
    pool_w = len(POOL_WINDOWS) * POOL_GROUP_DIM

    mod3 = _mod(c, w_mod[0], b_mod[0]).reshape(bsz, N_MOD, d)

    w_in_b = w_in[0].astype(BF16)
    bounds = [0, qk_w, 2 * qk_w, 2 * qk_w + v_w, 2 * qk_w + v_w + pool_w,
              2 * qk_w + v_w + pool_w + d, 2 * qk_w + v_w + pool_w + 2 * d]
    w_parts = [w_in_b[:, a:b] for a, b in zip(bounds[:-1], bounds[1:])]
    cos_t, sin_a, sin_b = _rope_tables(positions)
    q, k, v, p, sga, sgp = _in_proj(x2, mod3, g_pre_mix, cos_t, sin_a, sin_b, w_parts, seq)

    row64 = lambda a: a.reshape(1, DA_HEAD_DIM)
    o = _attention(q, k, v, row64(lambda_q1[0]), row64(lambda_k1[0]), row64(lambda_q2[0]),
                   row64(lambda_k2[0]), g_sub, bsz, seq)

    x1, u2_tiles, top_i, probs, rank, counts = _post_mix(
        o, p, sga, sgp, x2, mod3, w_att_branch[0].astype(BF16), w_pool_group[0].astype(BF16),
        pool_scale, w_pool_branch[0].astype(BF16), w_out[0].astype(BF16),
        g_post_mix, g_pre_ffn, w_router[0], b_router, seq)

    tile_expert, n_tiles, row_src, pos_flat = _routing_tables(counts[0], top_i, rank, t)
    y_tiles = _experts(tile_expert, n_tiles, row_src, u2_tiles,
                       w_gate_up[0], b_gate_up[0], w_down[0], b_down[0])
    out = _combine(pos_flat, y_tiles, probs, x1, mod3, g_post_ffn, seq)
    return out.reshape(bsz, seq, d)
```

```python
import functools

import jax
import jax.numpy as jnp
from jax import lax
from jax.experimental import pallas as pl
from jax.experimental.pallas import tpu as pltpu

F32 = jnp.float32
BF16 = jnp.bfloat16

NORM_EPS = 1e-6
CHUNK = 64
DA_HEADS = 8
DA_HEAD_DIM = 64
DA_V_DIM = 2 * DA_HEAD_DIM
ROPE_THETA = 500000.0
ROPE_DIM = DA_HEAD_DIM // 4
POOL_WINDOWS = (2, 4, 8, 16)
POOL_GROUP_DIM = 128
TOP_K = 4
SWIGLU_LIMIT = 7.0
SWIGLU_ALPHA = 1.702
N_MOD = 6
LAMBDA_INIT = 0.8 - 0.6 * 1.0
Q_SCALE = (DA_HEAD_DIM ** -0.5) * 1.4426950408889634

LANES = 128
SUBLANES = 8
VMEM_LIMIT = 56 * 1024 * 1024

TM_PROJ = 256
TQ = 256
TK = 256
ATT_HEADS_PER_STEP = 2
TM_MIX = 256
TM_EXP = 256
TM_COMB = 256
POOL_HALO = 16
NEG_BIG = -1e30


def _cparams(sem):
    return pltpu.CompilerParams(dimension_semantics=sem, vmem_limit_bytes=VMEM_LIMIT)


def _split(a):
    hi = a.astype(BF16)
    return hi, (a - hi.astype(F32)).astype(BF16)


def _split_dot(a, b, dims=(((1,), (0,)), ((), ()))):
    a_hi, a_lo = _split(a)
    b_hi, b_lo = _split(b)
    dot = lambda x, y: lax.dot_general(x, y, dims, preferred_element_type=F32)
    return dot(a_hi, b_hi) + (dot(a_hi, b_lo) + dot(a_lo, b_hi))


def _rms(x):
    return x * lax.rsqrt(jnp.mean(x * x, axis=-1, keepdims=True) + NORM_EPS)


def _mod_kernel(c_ref, w_ref, b_ref, o_ref):
    c = c_ref[...]
    c_act = c * jax.nn.sigmoid(c)
    o_ref[...] = _split_dot(c_act, w_ref[...]) + b_ref[...]


def _mod(c, w_mod, b_mod):
    bsz, d = c.shape
    n = w_mod.shape[1]
    tn = 1024
    return pl.pallas_call(
        _mod_kernel,
        grid=(n // tn,),
        in_specs=[
            pl.BlockSpec((bsz, d), lambda j: (0, 0)),
            pl.BlockSpec((d, tn), lambda j: (0, j)),
            pl.BlockSpec((1, tn), lambda j: (0, j)),
        ],
        out_specs=pl.BlockSpec((bsz, tn), lambda j: (0, j)),
        out_shape=jax.ShapeDtypeStruct((bsz, n), F32),
        compiler_params=_cparams(("parallel",)),
        name="mod",
    )(c, w_mod, b_mod.reshape(1, n))


def _rope(t, cos_t, sin_a, sin_b):
    n = t.shape[1]
    up = pltpu.roll(t, n - ROPE_DIM // 2, axis=1)
    dn = pltpu.roll(t, ROPE_DIM // 2, axis=1)
    reps = n // LANES
    tile = lambda a: jnp.concatenate([a] * reps, axis=1)
    return t * tile(cos_t) + up * tile(sin_a) + dn * tile(sin_b)


def _in_proj_kernel(x_ref, mod_ref, g_ref, pos_ref, freq_ref, w_ref, wvt_ref,
                    q_ref, k_ref, v_ref, p_ref, sga_ref, sgp_ref, *, bounds):
    x = x_ref[...]
    shift = mod_ref[0, 0:1, :]
    scale = mod_ref[0, 1:2, :]
    u = (_rms(x) * g_ref[...]) * (1.0 + scale) + shift
    ub = u.astype(BF16)
    dot = functools.partial(jnp.dot, preferred_element_type=F32)
    part = lambda n: w_ref[:, bounds[n]:bounds[n + 1]]

    ang = pos_ref[...].astype(F32) * freq_ref[...]
    cos_t, sn = jnp.cos(ang), jnp.sin(ang)
    in_head = lax.broadcasted_iota(jnp.int32, ang.shape, 1) % DA_HEAD_DIM
    sin_a = jnp.where(in_head < ROPE_DIM // 2, -sn, 0.0)
    sin_b = jnp.where(in_head >= ROPE_DIM // 2, sn, 0.0)

    q = _rope(dot(ub, part(0)), cos_t, sin_a, sin_b)
    q_ref[...] = (q * Q_SCALE).astype(BF16)
    k = _rope(dot(ub, part(1)), cos_t, sin_a, sin_b)
    k_ref[...] = k.astype(BF16)
    vt = lax.dot_general(wvt_ref[...], ub, (((1,), (1,)), ((), ())), preferred_element_type=F32)
    v_ref[0] = vt.astype(BF16)
    p_ref[...] = dot(ub, part(3))
    sga_ref[...] = jax.nn.sigmoid(dot(ub, part(4))).astype(BF16)
    sgp_ref[...] = jax.nn.sigmoid(dot(ub, part(5))).astype(BF16)


def _in_proj(x2, mod3, g_pre, pos_col, freq_row, w_in_b, w_vt, bounds, seq):
    t, d = x2.shape
    tm = TM_PROJ
    assert tm == TK, "v is emitted as one transposed (channels, TK) slab per step"
    steps_per_seq = seq // tm
    widths = [b - a for a, b in zip(bounds[:-1], bounds[1:])]
    row = lambda w: pl.BlockSpec((tm, w), lambda i: (i, 0))
    full = lambda a: pl.BlockSpec(a.shape, lambda i: (0, 0))
    out_specs = [row(widths[0]), row(widths[1]),
                 pl.BlockSpec((1, widths[2], tm), lambda i: (i, 0, 0)),
                 row(widths[3]), row(widths[4]), row(widths[5])]
    out_shape = [jax.ShapeDtypeStruct((t, widths[0]), BF16),
                 jax.ShapeDtypeStruct((t, widths[1]), BF16),
                 jax.ShapeDtypeStruct((t // tm, widths[2], tm), BF16),
                 jax.ShapeDtypeStruct((t, widths[3]), F32),
                 jax.ShapeDtypeStruct((t, widths[4]), BF16),
                 jax.ShapeDtypeStruct((t, widths[5]), BF16)]
    return pl.pallas_call(
        functools.partial(_in_proj_kernel, bounds=tuple(bounds)),
        grid=(t // tm,),
        in_specs=[
            row(d),
            pl.BlockSpec((1, N_MOD, d), lambda i: (i // steps_per_seq, 0, 0)),
            full(g_pre), row(1), full(freq_row), full(w_in_b), full(w_vt),
        ],
        out_specs=out_specs,
        out_shape=out_shape,
        compiler_params=_cparams(("parallel",)),
        name="in_proj",
    )(x2, mod3, g_pre, pos_col, freq_row, w_in_b, w_vt)


def _attn_kernel(lq1_ref, lk1_ref, lq2_ref, lk2_ref, g_ref, q_ref, k_ref, vt_ref, o_ref,
                 s_a, s_b, p_a, p_b, acc_buf):
    assert TQ == TK
    seq = q_ref.shape[0]
    lam = (jnp.exp(jnp.sum(lq1_ref[...] * lk1_ref[...], axis=-1, keepdims=True))
           - jnp.exp(jnp.sum(lq2_ref[...] * lk2_ref[...], axis=-1, keepdims=True))
           + LAMBDA_INIT)
    dot = functools.partial(jnp.dot, preferred_element_type=F32)
    heads = range(ATT_HEADS_PER_STEP)
    lanes = lambda h: slice(h * LANES, (h + 1) * LANES)

    def q_tile(qi, c):
        q_rows = pl.ds(pl.multiple_of(qi * TQ, TQ), TQ)
        qqt = []
        for h in heads:
            qt = q_ref[q_rows, lanes(h)].astype(F32).T
            row = lax.broadcasted_iota(jnp.int32, qt.shape, 0)
            zero = jnp.zeros_like(qt)
            qqt.append(jnp.concatenate([jnp.where(row < DA_HEAD_DIM, qt, zero),
                                        jnp.where(row >= DA_HEAD_DIM, qt, zero)],
                                       axis=1).astype(BF16))

        def scores(j, h):
            return dot(k_ref[pl.ds(pl.multiple_of(j * TK, TK), TK), lanes(h)], qqt[h])

        def softmax_step(s, m, l):
            m_new = jnp.maximum(m, jnp.max(s, axis=0, keepdims=True))
            alpha = jnp.exp2(m - m_new)
            p = jnp.exp2(s - m_new)
            return m_new, alpha * l + jnp.sum(p, axis=0, keepdims=True), alpha, p.astype(BF16)

        def pv(j, h, p):
            return dot(vt_ref[j, lanes(h), :], p)

        n_kv = vt_ref.shape[0]
        kk = lax.broadcasted_iota(jnp.int32, (TK, 2 * TQ), 0)
        qq = lax.broadcasted_iota(jnp.int32, (TK, 2 * TQ), 1)
        rel_chunk = jnp.where(qq >= TQ, qq - TQ, qq) // CHUNK - kk // CHUNK

        def masked(s, j):
            return jnp.where(rel_chunk >= (j - qi) * (TK // CHUNK), s, NEG_BIG)

        def step(j, carries, s_cur, p_cur, s_nxt, p_prev, mask):
            pend = [pv(jnp.maximum(j - 1, 0), h, p_prev[h]) for h in heads]
            if s_nxt is not None:
                for h in heads:
                    s_nxt[h] = scores(jnp.minimum(j + 1, n_kv - 1), h)
            out = []
            for h in heads:
                m, l, alpha = carries[h]
                acc_buf[h] = alpha * acc_buf[h] + pend[h]
                s = s_cur[h]
                m, l, alpha, p = softmax_step(masked(s, j) if mask else s, m, l)
                p_cur[h] = p
                out.append((m, l, alpha))
            return tuple(out)

        def pair(i, carries):
            carries = step(2 * i, carries, s_a, p_a, s_b, p_b, False)
            return step(2 * i + 1, carries, s_b, p_b, s_a, p_a, False)

        for h in heads:
            s_a[h] = scores(0, h)
            p_b[h] = jnp.zeros((TK, 2 * TQ), BF16)
            acc_buf[h] = jnp.zeros((DA_V_DIM, 2 * TQ), F32)
        init = tuple((jnp.full((1, 2 * TQ), NEG_BIG, F32), jnp.zeros((1, 2 * TQ), F32),
                      jnp.ones((1, 2 * TQ), F32)) for _ in heads)
        n_pairs = qi // 2
        carries = lax.fori_loop(0, n_pairs, pair, init)
        ja = 2 * n_pairs
        carries = step(ja, carries, s_a, p_a, s_b, p_b, True)
        carries = step(ja + 1, carries, s_b, p_b, None, p_a, True)
        for h in heads:
            m, l, alpha = carries[h]
            acc = alpha * acc_buf[h] + pv(jnp.minimum(ja + 1, n_kv - 1), h, p_b[h])
            o = acc / l
            a = o[:, :TQ] - lam * o[:, TQ:]
            y = a * lax.rsqrt(jnp.mean(a * a, axis=0, keepdims=True) + NORM_EPS)
            y = (y * g_ref[...]) * (1.0 - LAMBDA_INIT)
            o_ref[q_rows, lanes(h)] = y.T.astype(BF16)
        return c

    lax.fori_loop(0, seq // TQ, q_tile, 0)


def _attention(q, k, vt, lq1, lk1, lq2, lk2, g_col, bsz, seq):
    t = q.shape[0]
    nkv = seq // TK
    hw = ATT_HEADS_PER_STEP * LANES
    vec = lambda a: pl.BlockSpec(a.shape, lambda b, h: (0, 0))
    return pl.pallas_call(
        _attn_kernel,
        grid=(bsz, DA_HEADS // ATT_HEADS_PER_STEP),
        in_specs=[
            vec(lq1), vec(lk1), vec(lq2), vec(lk2), vec(g_col),
            pl.BlockSpec((seq, hw), lambda b, h: (b, h)),
            pl.BlockSpec((seq, hw), lambda b, h: (b, h)),
            pl.BlockSpec((nkv, hw, TK), lambda b, h: (b, h, 0)),
        ],
        out_specs=pl.BlockSpec((seq, hw), lambda b, h: (b, h)),
        out_shape=jax.ShapeDtypeStruct((t, DA_HEADS * DA_V_DIM), BF16),
        scratch_shapes=[
            pltpu.VMEM((ATT_HEADS_PER_STEP, TK, 2 * TQ), F32),
            pltpu.VMEM((ATT_HEADS_PER_STEP, TK, 2 * TQ), F32),
            pltpu.VMEM((ATT_HEADS_PER_STEP, TK, 2 * TQ), BF16),
            pltpu.VMEM((ATT_HEADS_PER_STEP, TK, 2 * TQ), BF16),
            pltpu.VMEM((ATT_HEADS_PER_STEP, DA_V_DIM, 2 * TQ), F32),
        ],
        compiler_params=_cparams(("parallel", "parallel")),
        name="attention",
    )(lq1, lk1, lq2, lk2, g_col, q, k, vt)


def _post_mix_kernel(o_ref, p_ref, ph_ref, sga_ref, sgp_ref, x_ref, mod_ref,
                     watt_ref, wpg_ref, ps_ref, wpb_ref, wout_ref, gpm_ref, gpf_ref,
                     wrt_ref, br_ref,
                     x1_ref, prob_ref, pos_ref, cnt_ref, tab_ref, xs_hbm,
                     carry_ref, cur_ref, free_ref, u2t, zeros_v, pos_v, pos_s,
                     sem_rows, sem_pos, sem_zero,
                     *, steps_per_seq, n_blocks):
    i = pl.program_id(0)
    n_steps = pl.num_programs(0)
    tm = x_ref.shape[0]
    n_experts = wrt_ref.shape[0]
    n_sub = x_ref.shape[1] // LANES
    blk = TM_EXP * n_sub
    per_step = (tm * TOP_K) // TM_EXP
    slot = i % 2
    prev = 1 - slot
    dot = functools.partial(jnp.dot, preferred_element_type=F32)

    def pos_copy(s):
        return pltpu.make_async_copy(pos_v.at[s], pos_s.at[s], sem_pos.at[s])

    def row_copy(s, k, r):
        dst = pos_s[s, k, r]
        return pltpu.make_async_copy(
            u2t.at[s, pl.ds(r * n_sub, n_sub), :],
            xs_hbm.at[pl.ds(pl.multiple_of(dst * n_sub, n_sub), n_sub), :], sem_rows.at[s])

    def wait_rows(s):
        for _ in range(TOP_K):
            pltpu.make_async_copy(u2t.at[s], xs_hbm.at[pl.ds(0, tm * n_sub), :], sem_rows.at[s]).wait()

    def zero_copy(b):
        return pltpu.make_async_copy(zeros_v, xs_hbm.at[pl.ds(pl.multiple_of(b * blk, blk), blk), :],
                                     sem_zero)

    @pl.when(i == 0)
    def _():
        carry_ref[...] = jnp.zeros_like(carry_ref)
        cur_ref[...] = jnp.zeros_like(cur_ref)
        free_ref[...] = jnp.zeros_like(free_ref)
        tab_ref[...] = jnp.zeros_like(tab_ref)
        zeros_v[...] = jnp.zeros_like(zeros_v)
        u2t[1] = jnp.zeros(u2t.shape[1:], F32)
        spare = (n_blocks * TM_EXP
                 + lax.broadcasted_iota(jnp.int32, (TOP_K, tm), 0) * tm
                 + lax.broadcasted_iota(jnp.int32, (TOP_K, tm), 1))
        pos_v[1] = spare
        pos_copy(1).start()
        for b in range(n_experts):
            zero_copy(b).start()
        for b in range(n_experts):
            zero_copy(b).wait()

    for b in range(per_step):
        zero_copy(n_experts + i * per_step + b).start()

    pos_copy(prev).wait()
    for r in range(tm):
        for k in range(TOP_K):
            row_copy(prev, k, r).start()

    y_att = dot(o_ref[...], watt_ref[...])

    first = (i % steps_per_seq) == 0
    halo = jnp.where(first, 0.0, ph_ref[...])
    ext = jnp.concatenate([halo, p_ref[...]], axis=0)
    t_in_seq = (i % steps_per_seq) * tm + lax.broadcasted_iota(jnp.int32, (tm, 1), 0)
    pooled = []
    for g, w in enumerate(POOL_WINDOWS):
        e = ext[:, g * POOL_GROUP_DIM:(g + 1) * POOL_GROUP_DIM]
        acc, span = e, 1
        while span < w:
            acc = acc[span:] + acc[:-span]
            span *= 2
        win = acc[POOL_HALO - (w - 1):]
        cnt = jnp.minimum(t_in_seq + 1, w).astype(F32)
        mixed = win / cnt - e[POOL_HALO:]
        pooled.append(dot(mixed.astype(BF16), wpg_ref[g]))
    y_pool_in = jnp.concatenate(pooled, axis=1) * ps_ref[...]
    y_pool = dot(y_pool_in.astype(BF16), wpb_ref[...])

    merged = sga_ref[...].astype(F32) * y_att + sgp_ref[...].astype(F32) * y_pool
    mix_out = dot(merged.astype(BF16), wout_ref[...])
    gate_m = mod_ref[0, 2:3, :]
    shift_f = mod_ref[0, 3:4, :]
    scale_f = mod_ref[0, 4:5, :]
    x1 = x_ref[...] + gate_m * (_rms(mix_out) * gpm_ref[...])
    x1_ref[...] = x1
    u2 = (_rms(x1) * gpf_ref[...]) * (1.0 + scale_f) + shift_f

    logits = _split_dot(wrt_ref[...], u2, (((1,), (1,)), ((), ()))) + br_ref[...]
    erow = lax.broadcasted_iota(jnp.int32, logits.shape, 0)
    work = logits
    vals, idxs = [], []
    for _ in range(TOP_K):
        mx = jnp.max(work, axis=0, keepdims=True)
        ix = jnp.min(jnp.where(work == mx, erow, n_experts), axis=0, keepdims=True)
        vals.append(mx)
        idxs.append(ix)
        work = jnp.where(erow == ix, -jnp.inf, work)
    exps = [jnp.exp(vv - vals[0]) for vv in vals]
    denom = exps[0] + exps[1] + exps[2] + exps[3]
    prob_ref[...] = jnp.concatenate([e / denom for e in exps], axis=0)

    onehot = jnp.zeros(logits.shape, F32)
    for ix in idxs:
        onehot = onehot + (erow == ix).astype(F32)
    rr = lax.broadcasted_iota(jnp.int32, (tm, tm), 0)
    cc = lax.broadcasted_iota(jnp.int32, (tm, tm), 1)
    earlier = (rr < cc).astype(BF16)
    carry = carry_ref[...]
    before = dot(onehot.astype(BF16), earlier) + carry

    blk_rows = float(TM_EXP)
    total = carry + jnp.sum(onehot, axis=1, keepdims=True)
    blocks_old = jnp.floor((carry + (blk_rows - 1.0)) / blk_rows)
    opened = jnp.floor((total + (blk_rows - 1.0)) / blk_rows) - blocks_old
    e_r = lax.broadcasted_iota(jnp.int32, (n_experts, n_experts), 0)
    e_c = lax.broadcasted_iota(jnp.int32, (n_experts, n_experts), 1)
    lower = (e_c < e_r).astype(BF16)
    opened_before = dot(lower, jnp.broadcast_to(opened, (n_experts, LANES)).astype(BF16))[:, 0:1]
    new_blk = free_ref[...] + opened_before
    cur_blk = cur_ref[...]
    boundary = blocks_old * blk_rows
    blk_of = jnp.where(before < boundary, cur_blk, new_blk)
    pos_rows = []
    for ix in idxs:
        pick = erow == ix
        rank = jnp.sum(jnp.where(pick, before, 0.0), axis=0, keepdims=True)
        blk_id = jnp.sum(jnp.where(pick, blk_of, 0.0), axis=0, keepdims=True)
        within = rank - jnp.floor(rank / blk_rows) * blk_rows
        pos_rows.append((blk_id * blk_rows + within).astype(jnp.int32))
    pos = jnp.concatenate(pos_rows, axis=0)
    pos_ref[...] = pos
    tab_col = lax.broadcasted_iota(jnp.int32, tab_ref.shape, 1).astype(F32)
    tab_ref[...] = jnp.where(jnp.logical_and(tab_col == blocks_old, opened > 0.0),
                             new_blk.astype(jnp.int32), tab_ref[...])
    cur_ref[...] = jnp.where(opened > 0.0, new_blk, cur_blk)
    free_ref[...] = free_ref[...] + jnp.sum(opened, axis=0, keepdims=True)
    carry_ref[...] = total
    cnt_ref[...] = total.astype(jnp.int32)

    @pl.when(i > 0)
    def _():
        wait_rows(slot)

    for s in range(n_sub):
        u2t[slot, pl.ds(s, tm, stride=n_sub), :] = u2[:, s * LANES:(s + 1) * LANES]
    pos_v[slot] = pos
    pos_copy(slot).start()
    for b in range(per_step):
        zero_copy(n_experts + i * per_step + b).wait()

    @pl.when(i == n_steps - 1)
    def _():
        pos_copy(slot).wait()

        def issue(r, c):
            for k in range(TOP_K):
                row_copy(slot, k, r).start()
            return c
        lax.fori_loop(0, tm, issue, 0)
        wait_rows(prev)
        wait_rows(slot)


def _post_mix(o, p, sga, sgp, x2, mod3, w_att, w_pg, pool_scale, w_pb, w_out,
              g_post_mix, g_pre_ffn, w_router_t, b_router_col, seq):
    t, d = x2.shape
    tm = TM_MIX
    steps_per_seq = seq // tm
    n_experts = w_router_t.shape[0]
    pw = p.shape[1]
    n_sub = d // LANES
    halo_blocks = tm // POOL_HALO
    row = lambda w: pl.BlockSpec((tm, w), lambda i: (i, 0))
    col = lambda h: pl.BlockSpec((h, tm), lambda i: (0, i))
    full2 = lambda a: pl.BlockSpec(a.shape, lambda i: (0, 0))
    assert tm <= TM_EXP and (tm * TOP_K) % TM_EXP == 0 and t % TM_EXP == 0
    n_blocks = (t * TOP_K) // TM_EXP + n_experts
    tab_w = -(-(t // TM_EXP) // LANES) * LANES
    kern = functools.partial(_post_mix_kernel, steps_per_seq=steps_per_seq, n_blocks=n_blocks)
    xs_rows = n_blocks * TM_EXP + TOP_K * tm
    return pl.pallas_call(
        kern,
        grid=(t // tm,),
        in_specs=[
            row(d), row(pw),
            pl.BlockSpec((POOL_HALO, pw), lambda i: (jnp.maximum(i * halo_blocks - 1, 0), 0)),
            row(d), row(d), row(d),
            pl.BlockSpec((1, N_MOD, d), lambda i: (i // steps_per_seq, 0, 0)),
            full2(w_att),
            pl.BlockSpec(w_pg.shape, lambda i: (0, 0, 0)),
            full2(pool_scale), full2(w_pb), full2(w_out), full2(g_post_mix), full2(g_pre_ffn),
            full2(w_router_t), full2(b_router_col),
        ],
        out_specs=[
            row(d), col(TOP_K), col(TOP_K),
            pl.BlockSpec((n_experts, 1), lambda i: (0, 0)),
            pl.BlockSpec((n_experts, tab_w), lambda i: (0, 0)),
            pl.BlockSpec(memory_space=pl.ANY),
        ],
        out_shape=[
            jax.ShapeDtypeStruct((t, d), F32),
            jax.ShapeDtypeStruct((TOP_K, t), F32),
            jax.ShapeDtypeStruct((TOP_K, t), jnp.int32),
            jax.ShapeDtypeStruct((n_experts, 1), jnp.int32),
            jax.ShapeDtypeStruct((n_experts, tab_w), jnp.int32),
            jax.ShapeDtypeStruct((xs_rows * n_sub, LANES), F32),
        ],
        scratch_shapes=[
            pltpu.VMEM((n_experts, 1), F32),
            pltpu.VMEM((n_experts, 1), F32),
            pltpu.VMEM((1, 1), F32),
            pltpu.VMEM((2, tm * n_sub, LANES), F32),
            pltpu.VMEM((TM_EXP * n_sub, LANES), F32),
            pltpu.VMEM((2, TOP_K, tm), jnp.int32),
            pltpu.SMEM((2, TOP_K, tm), jnp.int32),
            pltpu.SemaphoreType.DMA((2,)),
            pltpu.SemaphoreType.DMA((2,)),
            pltpu.SemaphoreType.DMA,
        ],
        compiler_params=_cparams(("arbitrary",)),
        name="post_mix",
    )(o, p, p, sga, sgp, x2, mod3, w_att, w_pg, pool_scale, w_pb, w_out,
      g_post_mix, g_pre_ffn, w_router_t, b_router_col)


def _experts_kernel(te_ref, blk_ref, nt_ref, xs_ref, wgu_ref, bgu_ref, wd_ref, bd_ref,
                    y_ref, wgu_bf, wd_bf, *, n_sub):
    j = pl.program_id(0)
    tm = TM_EXP
    d_ff = wd_ref.shape[1]

    @pl.when(j < nt_ref[0])
    def _():
        changed = jnp.logical_or(j == 0, te_ref[j] != te_ref[jnp.maximum(j - 1, 0)])

        @pl.when(changed)
        def _():
            wgu_bf[...] = wgu_ref[0].astype(BF16)
            wd_bf[...] = wd_ref[0].astype(BF16)

        xs = jnp.concatenate(
            [xs_ref[pl.ds(s, tm, stride=n_sub), :] for s in range(n_sub)], axis=1).astype(BF16)
        gu = jnp.dot(xs, wgu_bf[...], preferred_element_type=F32) + bgu_ref[0]
        gate = jnp.minimum(gu[:, :d_ff], SWIGLU_LIMIT)
        up = jnp.clip(gu[:, d_ff:], -SWIGLU_LIMIT, SWIGLU_LIMIT)
        act = (up + 1.0) * (gate * jax.nn.sigmoid(SWIGLU_ALPHA * gate))
        y = jnp.dot(act.astype(BF16), wd_bf[...], preferred_element_type=F32) + bd_ref[0]
        for s in range(n_sub):
            y_ref[pl.ds(s, tm, stride=n_sub), :] = y[:, s * LANES:(s + 1) * LANES]

    @pl.when(j >= nt_ref[0])
    def _():
        y_ref[...] = jnp.zeros_like(y_ref)


def _experts(tile_expert, tile_block, n_tiles, xs_tiles, w_gate_up, b_gate_up, w_down, b_down):
    n_exp, d, d_gu = w_gate_up.shape
    d_ff = w_down.shape[1]
    n_sub = d // LANES
    tm = TM_EXP
    max_tiles = tile_expert.shape[0]
    by_expert = lambda j, te, blk, nt: (te[j], 0, 0)
    by_block = lambda j, te, blk, nt: (blk[j], 0)
    grid_spec = pltpu.PrefetchScalarGridSpec(
        num_scalar_prefetch=3,
        grid=(max_tiles,),
        in_specs=[
            pl.BlockSpec((tm * n_sub, LANES), by_block),
            pl.BlockSpec((1, d, d_gu), by_expert),
            pl.BlockSpec((1, 1, d_gu), by_expert),
            pl.BlockSpec((1, d_ff, d), by_expert),
            pl.BlockSpec((1, 1, d), by_expert),
        ],
        out_specs=pl.BlockSpec((tm * n_sub, LANES), by_block),
        scratch_shapes=[
            pltpu.VMEM((d, d_gu), BF16),
            pltpu.VMEM((d_ff, d), BF16),
        ],
    )
    return pl.pallas_call(
        functools.partial(_experts_kernel, n_sub=n_sub),
        grid_spec=grid_spec,
        out_shape=jax.ShapeDtypeStruct((max_tiles * tm * n_sub, LANES), F32),
        compiler_params=_cparams(("arbitrary",)),
        name="experts",
    )(tile_expert, tile_block, n_tiles, xs_tiles,
      w_gate_up, b_gate_up.reshape(n_exp, 1, d_gu), w_down, b_down.reshape(n_exp, 1, d))


def _combine_kernel(pos_ref, y_hbm, prob_ref, x1_ref, mod_ref, g_ref, o_ref, ybuf, sem,
                    *, n_sub, n_tokens):
    i = pl.program_id(0)
    tm = TM_COMB

    def issue(r, c):
        for k in range(TOP_K):
            row = pos_ref[k * n_tokens + i * tm + r]
            pltpu.make_async_copy(
                y_hbm.at[pl.ds(pl.multiple_of(row * n_sub, n_sub), n_sub), :],
                ybuf.at[k, pl.ds(pl.multiple_of(r * n_sub, n_sub), n_sub), :], sem).start()
        return c
    lax.fori_loop(0, tm, issue, 0)
    for k in range(TOP_K):
        pltpu.make_async_copy(y_hbm.at[pl.ds(0, tm * n_sub), :], ybuf.at[k], sem).wait()

    prob = prob_ref[...]
    f = None
    for k in range(TOP_K):
        yk = jnp.concatenate(
            [ybuf[k, pl.ds(s, tm, stride=n_sub), :] for s in range(n_sub)], axis=1)
        term = yk * prob[:, k:k + 1]
        f = term if f is None else f + term
    gate_f = mod_ref[0, 5:6, :]
    o_ref[...] = x1_ref[...] + gate_f * (_rms(f) * g_ref[...])


def _combine(pos_flat, y_tiles, probs, x1, mod3, g_post_ffn, seq):
    t, d = x1.shape
    tm = TM_COMB
    n_sub = d // LANES
    steps_per_seq = seq // tm
    grid_spec = pltpu.PrefetchScalarGridSpec(
        num_scalar_prefetch=1,
        grid=(t // tm,),
        in_specs=[
            pl.BlockSpec(memory_space=pl.ANY),
            pl.BlockSpec((tm, TOP_K), lambda i, pos: (i, 0)),
            pl.BlockSpec((tm, d), lambda i, pos: (i, 0)),
            pl.BlockSpec((1, N_MOD, d), lambda i, pos: (i // steps_per_seq, 0, 0)),
            pl.BlockSpec((1, d), lambda i, pos: (0, 0)),
        ],
        out_specs=pl.BlockSpec((tm, d), lambda i, pos: (i, 0)),
        scratch_shapes=[
            pltpu.VMEM((TOP_K, tm * n_sub, LANES), F32),
            pltpu.SemaphoreType.DMA,
        ],
    )
    return pl.pallas_call(
        functools.partial(_combine_kernel, n_sub=n_sub, n_tokens=t),
        grid_spec=grid_spec,
        out_shape=jax.ShapeDtypeStruct((t, d), F32),
        compiler_params=_cparams(("arbitrary",)),
        name="combine",
    )(pos_flat, y_tiles, probs, x1, mod3, g_post_ffn)


def _rope_freq_row():
    half = ROPE_DIM // 2
    inv_freq = ROPE_THETA ** (-jnp.arange(0, ROPE_DIM, 2, dtype=F32) / ROPE_DIM)
    head = jnp.concatenate([inv_freq, inv_freq, jnp.zeros((DA_HEAD_DIM - 2 * half,), F32)])
    return jnp.tile(head, LANES // DA_HEAD_DIM).reshape(1, LANES)


def _tile_tables(counts, block_tab, n_tokens):
    tm = TM_EXP
    n_exp = counts.shape[0]
    max_tiles = (n_tokens * TOP_K) // tm + n_exp
    tiles = (counts + tm - 1) // tm
    tile_end = jnp.cumsum(tiles)
    n_tiles = tile_end[-1]
    j = jnp.arange(max_tiles, dtype=jnp.int32)
    jj = jnp.minimum(j, n_tiles - 1)
    tile_expert = jnp.sum(jj[:, None] >= tile_end[None, :], axis=1).astype(jnp.int32)
    nth = jj - (tile_end - tiles)[tile_expert]
    tile_block = jnp.where(j < n_tiles, block_tab[tile_expert, nth], j)
    return tile_expert, tile_block.astype(jnp.int32), n_tiles.reshape(1).astype(jnp.int32)


def kernel(x, c, positions, w_mod, b_mod, g_pre_mix, w_in, lambda_q1, lambda_k1, lambda_q2, lambda_k2, g_sub, w_pool_group, pool_scale, w_att_branch, w_pool_branch, w_out, g_post_mix, g_pre_ffn, w_router, b_router, w_gate_up, b_gate_up, w_down, b_down, g_post_ffn):
    bsz, seq, d = x.shape
    assert w_mod.shape[0] == 1, "single layer"
    assert seq % TQ == 0 and seq % TM_PROJ == 0 and seq % TM_MIX == 0 and seq % TM_COMB == 0
    t = bsz * seq
    assert t % TM_EXP == 0
    x2 = x.reshape(t, d)
    qk_w = 2 * DA_HEADS * DA_HEAD_DIM
    v_w = DA_HEADS * DA_V_DIM
    pool_w = len(POOL_WINDOWS) * POOL_GROUP_DIM
    n_experts = w_router.shape[2]

    mod3 = _mod(c, w_mod[0], b_mod[0]).reshape(bsz, N_MOD, d)

    bounds = [0, qk_w, 2 * qk_w, 2 * qk_w + v_w, 2 * qk_w + v_w + pool_w,
              2 * qk_w + v_w + pool_w + d, 2 * qk_w + v_w + pool_w + 2 * d]
    w_in_b = w_in[0].astype(BF16)
    w_vt = w_in_b[:, bounds[2]:bounds[3]].T
    q, k, vt, p, sga, sgp = _in_proj(x2, mod3, g_pre_mix, positions.reshape(t, 1), _rope_freq_row(),
                                     w_in_b, w_vt, bounds, seq)

    row64 = lambda a: a.reshape(1, DA_HEAD_DIM)
    o = _attention(q, k, vt, row64(lambda_q1[0]), row64(lambda_k1[0]), row64(lambda_q2[0]),
                   row64(lambda_k2[0]), g_sub.reshape(DA_V_DIM, 1), bsz, seq)

    x1, probs_t, pos_t, counts, block_tab, xs_tiles = _post_mix(
        o, p, sga, sgp, x2, mod3, w_att_branch[0].astype(BF16), w_pool_group[0].astype(BF16),
        pool_scale, w_pool_branch[0].astype(BF16), w_out[0].astype(BF16),
        g_post_mix, g_pre_ffn, w_router[0].T, b_router.reshape(n_experts, 1), seq)

    tile_expert, tile_block, n_tiles = _tile_tables(counts[:, 0], block_tab, t)
    y_tiles = _experts(tile_expert, tile_block, n_tiles, xs_tiles,
                       w_gate_up[0], b_gate_up[0], w_down[0], b_down[0])
    out = _combine(pos_t.reshape(-1), y_tiles, probs_t.T, x1, mod3, g_post_ffn, seq)
    return out.reshape(bsz, seq, d)
```

```python
import functools

import jax
import jax.numpy as jnp
from jax import lax
from jax.experimental import pallas as pl
from jax.experimental.pallas import tpu as pltpu

F32 = jnp.float32
BF16 = jnp.bfloat16

NORM_EPS = 1e-6
CHUNK = 64
DA_HEADS = 8
DA_HEAD_DIM = 64
DA_V_DIM = 2 * DA_HEAD_DIM
ROPE_THETA = 500000.0
ROPE_DIM = DA_HEAD_DIM // 4
POOL_WINDOWS = (2, 4, 8, 16)
POOL_GROUP_DIM = 128
TOP_K = 4
SWIGLU_LIMIT = 7.0
SWIGLU_ALPHA = 1.702
N_MOD = 6
LAMBDA_INIT = 0.8 - 0.6 * 1.0
Q_SCALE = (DA_HEAD_DIM ** -0.5) * 1.4426950408889634

LANES = 128
SUBLANES = 8
VMEM_LIMIT = 56 * 1024 * 1024

TM_PROJ = 256
TQ = 512
TK = 256
ATT_HEADS_PER_STEP = 2
TM_MIX = 256
TM_EXP = 512
TM_COMB = 256
POOL_HALO = 16
NEG_BIG = -1e30


def _cparams(sem):
    return pltpu.CompilerParams(dimension_semantics=sem, vmem_limit_bytes=VMEM_LIMIT)


def _split(a):
    hi = a.astype(BF16)
    return hi, (a - hi.astype(F32)).astype(BF16)


def _split_dot(a, b, dims=(((1,), (0,)), ((), ()))):
    a_hi, a_lo = _split(a)
    b_hi, b_lo = _split(b)
    dot = lambda x, y: lax.dot_general(x, y, dims, preferred_element_type=F32)
    return dot(a_hi, b_hi) + (dot(a_hi, b_lo) + dot(a_lo, b_hi))


def _rms(x):
    return x * lax.rsqrt(jnp.mean(x * x, axis=-1, keepdims=True) + NORM_EPS)


def _mod_kernel(c_ref, w_ref, b_ref, o_ref):
    c = c_ref[...]
    c_act = c * jax.nn.sigmoid(c)
    o_ref[...] = _split_dot(c_act, w_ref[...]) + b_ref[...]


def _mod(c, w_mod, b_mod):
    bsz, d = c.shape
    n = w_mod.shape[1]
    tn = 1024
    return pl.pallas_call(
        _mod_kernel,
        grid=(n // tn,),
        in_specs=[
            pl.BlockSpec((bsz, d), lambda j: (0, 0)),
            pl.BlockSpec((d, tn), lambda j: (0, j)),
            pl.BlockSpec((1, tn), lambda j: (0, j)),
        ],
        out_specs=pl.BlockSpec((bsz, tn), lambda j: (0, j)),
        out_shape=jax.ShapeDtypeStruct((bsz, n), F32),
        compiler_params=_cparams(("parallel",)),
        name="mod",
    )(c, w_mod, b_mod.reshape(1, n))


def _rope(t, cos_t, sin_a, sin_b):
    n = t.shape[1]
    up = pltpu.roll(t, n - ROPE_DIM // 2, axis=1)
    dn = pltpu.roll(t, ROPE_DIM // 2, axis=1)
    reps = n // LANES
    tile = lambda a: jnp.concatenate([a] * reps, axis=1)
    return t * tile(cos_t) + up * tile(sin_a) + dn * tile(sin_b)


def _in_proj_kernel(x_ref, mod_ref, g_ref, pos_ref, freq_ref, w_ref, wvt_ref,
                    q_ref, k_ref, v_ref, p_ref, sga_ref, sgp_ref, *, bounds):
    x = x_ref[...]
    shift = mod_ref[0, 0:1, :]
    scale = mod_ref[0, 1:2, :]
    u = (_rms(x) * g_ref[...]) * (1.0 + scale) + shift
    ub = u.astype(BF16)
    dot = functools.partial(jnp.dot, preferred_element_type=F32)
    part = lambda n: w_ref[:, bounds[n]:bounds[n + 1]]

    ang = pos_ref[...].astype(F32) * freq_ref[...]
    cos_t, sn = jnp.cos(ang), jnp.sin(ang)
    in_head = lax.broadcasted_iota(jnp.int32, ang.shape, 1) % DA_HEAD_DIM
    sin_a = jnp.where(in_head < ROPE_DIM // 2, -sn, 0.0)
    sin_b = jnp.where(in_head >= ROPE_DIM // 2, sn, 0.0)

    q = _rope(dot(ub, part(0)), cos_t, sin_a, sin_b)
    q_ref[...] = (q * Q_SCALE).astype(BF16)
    k = _rope(dot(ub, part(1)), cos_t, sin_a, sin_b)
    k_ref[...] = k.astype(BF16)
    vt = lax.dot_general(wvt_ref[...], ub, (((1,), (1,)), ((), ())), preferred_element_type=F32)
    v_ref[0] = vt.astype(BF16)
    p_ref[...] = dot(ub, part(3))
    sga_ref[...] = jax.nn.sigmoid(dot(ub, part(4))).astype(BF16)
    sgp_ref[...] = jax.nn.sigmoid(dot(ub, part(5))).astype(BF16)


def _in_proj(x2, mod3, g_pre, pos_col, freq_row, w_in_b, w_vt, bounds, seq):
    t, d = x2.shape
    tm = TM_PROJ
    assert tm == TK, "v is emitted as one transposed (channels, TK) slab per step"
    steps_per_seq = seq // tm
    widths = [b - a for a, b in zip(bounds[:-1], bounds[1:])]
    row = lambda w: pl.BlockSpec((tm, w), lambda i: (i, 0))
    full = lambda a: pl.BlockSpec(a.shape, lambda i: (0, 0))
    out_specs = [row(widths[0]), row(widths[1]),
                 pl.BlockSpec((1, widths[2], tm), lambda i: (i, 0, 0)),
                 row(widths[3]), row(widths[4]), row(widths[5])]
    out_shape = [jax.ShapeDtypeStruct((t, widths[0]), BF16),
                 jax.ShapeDtypeStruct((t, widths[1]), BF16),
                 jax.ShapeDtypeStruct((t // tm, widths[2], tm), BF16),
                 jax.ShapeDtypeStruct((t, widths[3]), F32),
                 jax.ShapeDtypeStruct((t, widths[4]), BF16),
                 jax.ShapeDtypeStruct((t, widths[5]), BF16)]
    return pl.pallas_call(
        functools.partial(_in_proj_kernel, bounds=tuple(bounds)),
        grid=(t // tm,),
        in_specs=[
            row(d),
            pl.BlockSpec((1, N_MOD, d), lambda i: (i // steps_per_seq, 0, 0)),
            full(g_pre), row(1), full(freq_row), full(w_in_b), full(w_vt),
        ],
        out_specs=out_specs,
        out_shape=out_shape,
        compiler_params=_cparams(("parallel",)),
        name="in_proj",
    )(x2, mod3, g_pre, pos_col, freq_row, w_in_b, w_vt)


def _attn_kernel(lq1_ref, lk1_ref, lq2_ref, lk2_ref, g_ref, q_ref, k_ref, vt_ref, o_ref,
                 s_a, s_b, p_a, p_b, acc_buf):
    assert TQ % TK == 0 and TQ // TK in (1, 2)
    seq = q_ref.shape[0]
    lam = (jnp.exp(jnp.sum(lq1_ref[...] * lk1_ref[...], axis=-1, keepdims=True))
           - jnp.exp(jnp.sum(lq2_ref[...] * lk2_ref[...], axis=-1, keepdims=True))
           + LAMBDA_INIT)
    dot = functools.partial(jnp.dot, preferred_element_type=F32)
    heads = range(ATT_HEADS_PER_STEP)
    lanes = lambda h: slice(h * LANES, (h + 1) * LANES)

    def q_tile(qi, c):
        q_rows = pl.ds(pl.multiple_of(qi * TQ, TQ), TQ)
        qqt = []
        for h in heads:
            qt = q_ref[q_rows, lanes(h)].astype(F32).T
            row = lax.broadcasted_iota(jnp.int32, qt.shape, 0)
            zero = jnp.zeros_like(qt)
            qqt.append(jnp.concatenate([jnp.where(row < DA_HEAD_DIM, qt, zero),
                                        jnp.where(row >= DA_HEAD_DIM, qt, zero)],
                                       axis=1).astype(BF16))

        def scores(j, h):
            return dot(k_ref[pl.ds(pl.multiple_of(j * TK, TK), TK), lanes(h)], qqt[h])

        def softmax_step(s, m, l):
            m_new = jnp.maximum(m, jnp.max(s, axis=0, keepdims=True))
            alpha = jnp.exp2(m - m_new)
            p = jnp.exp2(s - m_new)
            return m_new, alpha * l + jnp.sum(p, axis=0, keepdims=True), alpha, p.astype(BF16)

        def pv(j, h, p):
            return dot(vt_ref[j, lanes(h), :], p)

        n_kv = vt_ref.shape[0]
        kk = lax.broadcasted_iota(jnp.int32, (TK, 2 * TQ), 0)
        qq = lax.broadcasted_iota(jnp.int32, (TK, 2 * TQ), 1)
        rel_chunk = jnp.where(qq >= TQ, qq - TQ, qq) // CHUNK - kk // CHUNK

        def masked(s, j):
            return jnp.where(rel_chunk >= j * (TK // CHUNK) - qi * (TQ // CHUNK), s, NEG_BIG)

        def step(j, carries, s_cur, p_cur, s_nxt, p_prev, mask):
            pend = [pv(jnp.maximum(j - 1, 0), h, p_prev[h]) for h in heads]
            if s_nxt is not None:
                for h in heads:
                    s_nxt[h] = scores(jnp.minimum(j + 1, n_kv - 1), h)
            out = []
            for h in heads:
                m, l, alpha = carries[h]
                acc_buf[h] = alpha * acc_buf[h] + pend[h]
                s = s_cur[h]
                m, l, alpha, p = softmax_step(masked(s, j) if mask else s, m, l)
                p_cur[h] = p
                out.append((m, l, alpha))
            return tuple(out)

        def pair(i, carries):
            carries = step(2 * i, carries, s_a, p_a, s_b, p_b, False)
            return step(2 * i + 1, carries, s_b, p_b, s_a, p_a, False)

        for h in heads:
            s_a[h] = scores(0, h)
            p_b[h] = jnp.zeros((TK, 2 * TQ), BF16)
            acc_buf[h] = jnp.zeros((DA_V_DIM, 2 * TQ), F32)
        init = tuple((jnp.full((1, 2 * TQ), NEG_BIG, F32), jnp.zeros((1, 2 * TQ), F32),
                      jnp.ones((1, 2 * TQ), F32)) for _ in heads)
        n_pairs = (qi * (TQ // TK)) // 2
        carries = lax.fori_loop(0, n_pairs, pair, init)
        ja = 2 * n_pairs
        carries = step(ja, carries, s_a, p_a, s_b, p_b, True)
        carries = step(ja + 1, carries, s_b, p_b, None, p_a, True)
        for h in heads:
            m, l, alpha = carries[h]
            acc = alpha * acc_buf[h] + pv(jnp.minimum(ja + 1, n_kv - 1), h, p_b[h])
            o = acc / l
            a = o[:, :TQ] - lam * o[:, TQ:]
            y = a * lax.rsqrt(jnp.mean(a * a, axis=0, keepdims=True) + NORM_EPS)
            y = (y * g_ref[...]) * (1.0 - LAMBDA_INIT)
            o_ref[q_rows, lanes(h)] = y.T.astype(BF16)
        return c

    lax.fori_loop(0, seq // TQ, q_tile, 0)


def _attention(q, k, vt, lq1, lk1, lq2, lk2, g_col, bsz, seq):
    t = q.shape[0]
    nkv = seq // TK
    hw = ATT_HEADS_PER_STEP * LANES
    vec = lambda a: pl.BlockSpec(a.shape, lambda b, h: (0, 0))
    return pl.pallas_call(
        _attn_kernel,
        grid=(bsz, DA_HEADS // ATT_HEADS_PER_STEP),
        in_specs=[
            vec(lq1), vec(lk1), vec(lq2), vec(lk2), vec(g_col),
            pl.BlockSpec((seq, hw), lambda b, h: (b, h)),
            pl.BlockSpec((seq, hw), lambda b, h: (b, h)),
            pl.BlockSpec((nkv, hw, TK), lambda b, h: (b, h, 0)),
        ],
        out_specs=pl.BlockSpec((seq, hw), lambda b, h: (b, h)),
        out_shape=jax.ShapeDtypeStruct((t, DA_HEADS * DA_V_DIM), BF16),
        scratch_shapes=[
            pltpu.VMEM((ATT_HEADS_PER_STEP, TK, 2 * TQ), F32),
            pltpu.VMEM((ATT_HEADS_PER_STEP, TK, 2 * TQ), F32),
            pltpu.VMEM((ATT_HEADS_PER_STEP, TK, 2 * TQ), BF16),
            pltpu.VMEM((ATT_HEADS_PER_STEP, TK, 2 * TQ), BF16),
            pltpu.VMEM((ATT_HEADS_PER_STEP, DA_V_DIM, 2 * TQ), F32),
        ],
        compiler_params=_cparams(("parallel", "parallel")),
        name="attention",
    )(lq1, lk1, lq2, lk2, g_col, q, k, vt)


def _post_mix_kernel(o_ref, p_ref, ph_ref, sga_ref, sgp_ref, x_ref, mod_ref,
                     watt_ref, wpg_ref, ps_ref, wpb_ref, wout_ref, gpm_ref, gpf_ref,
                     wrt_ref, br_ref,
                     x1_ref, prob_ref, pos_ref, cnt_ref, tab_ref, xs_hbm,
                     carry_ref, cur_ref, free_ref, u2t, zeros_v, pos_v, pos_s,
                     sem_rows, sem_pos, sem_zero,
                     *, steps_per_seq, n_blocks):
    i = pl.program_id(0)
    n_steps = pl.num_programs(0)
    tm = x_ref.shape[0]
    n_experts = wrt_ref.shape[0]
    n_sub = x_ref.shape[1] // LANES
    blk = TM_EXP * n_sub
    per_step = (tm * TOP_K) // TM_EXP
    slot = i % 2
    prev = 1 - slot
    dot = functools.partial(jnp.dot, preferred_element_type=F32)

    def pos_copy(s):
        return pltpu.make_async_copy(pos_v.at[s], pos_s.at[s], sem_pos.at[s])

    def row_copy(s, k, r):
        dst = pos_s[s, k, r]
        return pltpu.make_async_copy(
            u2t.at[s, pl.ds(r * n_sub, n_sub), :],
            xs_hbm.at[pl.ds(pl.multiple_of(dst * n_sub, n_sub), n_sub), :], sem_rows.at[s])

    def wait_rows(s):
        for _ in range(TOP_K):
            pltpu.make_async_copy(u2t.at[s], xs_hbm.at[pl.ds(0, tm * n_sub), :], sem_rows.at[s]).wait()

    def zero_copy(b):
        return pltpu.make_async_copy(zeros_v, xs_hbm.at[pl.ds(pl.multiple_of(b * blk, blk), blk), :],
                                     sem_zero)

    @pl.when(i == 0)
    def _():
        carry_ref[...] = jnp.zeros_like(carry_ref)
        cur_ref[...] = jnp.zeros_like(cur_ref)
        free_ref[...] = jnp.zeros_like(free_ref)
        tab_ref[...] = jnp.zeros_like(tab_ref)
        zeros_v[...] = jnp.zeros_like(zeros_v)
        u2t[1] = jnp.zeros(u2t.shape[1:], F32)
        spare = (n_blocks * TM_EXP
                 + lax.broadcasted_iota(jnp.int32, (TOP_K, tm), 0) * tm
                 + lax.broadcasted_iota(jnp.int32, (TOP_K, tm), 1))
        pos_v[1] = spare
        pos_copy(1).start()
        for b in range(n_experts):
            zero_copy(b).start()
        for b in range(n_experts):
            zero_copy(b).wait()

    for b in range(per_step):
        zero_copy(n_experts + i * per_step + b).start()

    pos_copy(prev).wait()
    for r in range(tm):
        for k in range(TOP_K):
            row_copy(prev, k, r).start()

    y_att = dot(o_ref[...], watt_ref[...])

    first = (i % steps_per_seq) == 0
    halo = jnp.where(first, 0.0, ph_ref[...])
    ext = jnp.concatenate([halo, p_ref[...]], axis=0)
    t_in_seq = (i % steps_per_seq) * tm + lax.broadcasted_iota(jnp.int32, (tm, 1), 0)
    pooled = []
    for g, w in enumerate(POOL_WINDOWS):
        e = ext[:, g * POOL_GROUP_DIM:(g + 1) * POOL_GROUP_DIM]
        acc, span = e, 1
        while span < w:
            acc = acc[span:] + acc[:-span]
            span *= 2
        win = acc[POOL_HALO - (w - 1):]
        cnt = jnp.minimum(t_in_seq + 1, w).astype(F32)
        mixed = win / cnt - e[POOL_HALO:]
        pooled.append(dot(mixed.astype(BF16), wpg_ref[g]))
    y_pool_in = jnp.concatenate(pooled, axis=1) * ps_ref[...]
    y_pool = dot(y_pool_in.astype(BF16), wpb_ref[...])

    merged = sga_ref[...].astype(F32) * y_att + sgp_ref[...].astype(F32) * y_pool
    mix_out = dot(merged.astype(BF16), wout_ref[...])
    gate_m = mod_ref[0, 2:3, :]
    shift_f = mod_ref[0, 3:4, :]
    scale_f = mod_ref[0, 4:5, :]
    x1 = x_ref[...] + gate_m * (_rms(mix_out) * gpm_ref[...])
    x1_ref[...] = x1
    u2 = (_rms(x1) * gpf_ref[...]) * (1.0 + scale_f) + shift_f

    logits = _split_dot(wrt_ref[...], u2, (((1,), (1,)), ((), ()))) + br_ref[...]
    erow = lax.broadcasted_iota(jnp.int32, logits.shape, 0)
    work = logits
    vals, idxs = [], []
    for _ in range(TOP_K):
        mx = jnp.max(work, axis=0, keepdims=True)
        ix = jnp.min(jnp.where(work == mx, erow, n_experts), axis=0, keepdims=True)
        vals.append(mx)
        idxs.append(ix)
        work = jnp.where(erow == ix, -jnp.inf, work)
    exps = [jnp.exp(vv - vals[0]) for vv in vals]
    denom = exps[0] + exps[1] + exps[2] + exps[3]
    prob_ref[...] = jnp.concatenate([e / denom for e in exps], axis=0)

    onehot = jnp.zeros(logits.shape, F32)
    for ix in idxs:
        onehot = onehot + (erow == ix).astype(F32)
    rr = lax.broadcasted_iota(jnp.int32, (tm, tm), 0)
    cc = lax.broadcasted_iota(jnp.int32, (tm, tm), 1)
    earlier = (rr < cc).astype(BF16)
    carry = carry_ref[...]
    before = dot(onehot.astype(BF16), earlier) + carry

    blk_rows = float(TM_EXP)
    total = carry + jnp.sum(onehot, axis=1, keepdims=True)
    blocks_old = jnp.floor((carry + (blk_rows - 1.0)) / blk_rows)
    opened = jnp.floor((total + (blk_rows - 1.0)) / blk_rows) - blocks_old
    e_r = lax.broadcasted_iota(jnp.int32, (n_experts, n_experts), 0)
    e_c = lax.broadcasted_iota(jnp.int32, (n_experts, n_experts), 1)
    lower = (e_c < e_r).astype(BF16)
    opened_before = dot(lower, jnp.broadcast_to(opened, (n_experts, LANES)).astype(BF16))[:, 0:1]
    new_blk = free_ref[...] + opened_before
    cur_blk = cur_ref[...]
    boundary = blocks_old * blk_rows
    blk_of = jnp.where(before < boundary, cur_blk, new_blk)
    pos_rows = []
    for ix in idxs:
        pick = erow == ix
        rank = jnp.sum(jnp.where(pick, before, 0.0), axis=0, keepdims=True)
        blk_id = jnp.sum(jnp.where(pick, blk_of, 0.0), axis=0, keepdims=True)
        within = rank - jnp.floor(rank / blk_rows) * blk_rows
        pos_rows.append((blk_id * blk_rows + within).astype(jnp.int32))
    pos = jnp.concatenate(pos_rows, axis=0)
    pos_ref[...] = pos
    tab_col = lax.broadcasted_iota(jnp.int32, tab_ref.shape, 1).astype(F32)
    tab_ref[...] = jnp.where(jnp.logical_and(tab_col == blocks_old, opened > 0.0),
                             new_blk.astype(jnp.int32), tab_ref[...])
    cur_ref[...] = jnp.where(opened > 0.0, new_blk, cur_blk)
    free_ref[...] = free_ref[...] + jnp.sum(opened, axis=0, keepdims=True)
    carry_ref[...] = total
    cnt_ref[...] = total.astype(jnp.int32)

    @pl.when(i > 0)
    def _():
        wait_rows(slot)

    for s in range(n_sub):
        u2t[slot, pl.ds(s, tm, stride=n_sub), :] = u2[:, s * LANES:(s + 1) * LANES]
    pos_v[slot] = pos
    pos_copy(slot).start()
    for b in range(per_step):
        zero_copy(n_experts + i * per_step + b).wait()

    @pl.when(i == n_steps - 1)
    def _():
        pos_copy(slot).wait()

        def issue(r, c):
            for k in range(TOP_K):
                row_copy(slot, k, r).start()
            return c
        lax.fori_loop(0, tm, issue, 0)
        wait_rows(prev)
        wait_rows(slot)


def _post_mix(o, p, sga, sgp, x2, mod3, w_att, w_pg, pool_scale, w_pb, w_out,
              g_post_mix, g_pre_ffn, w_router_t, b_router_col, seq):
    t, d = x2.shape
    tm = TM_MIX
    steps_per_seq = seq // tm
    n_experts = w_router_t.shape[0]
    pw = p.shape[1]
    n_sub = d // LANES
    halo_blocks = tm // POOL_HALO
    row = lambda w: pl.BlockSpec((tm, w), lambda i: (i, 0))
    col = lambda h: pl.BlockSpec((h, tm), lambda i: (0, i))
    full2 = lambda a: pl.BlockSpec(a.shape, lambda i: (0, 0))
    assert tm <= TM_EXP and (tm * TOP_K) % TM_EXP == 0 and t % TM_EXP == 0
    n_blocks = (t * TOP_K) // TM_EXP + n_experts
    tab_w = -(-(t // TM_EXP) // LANES) * LANES
    kern = functools.partial(_post_mix_kernel, steps_per_seq=steps_per_seq, n_blocks=n_blocks)
    xs_rows = n_blocks * TM_EXP + TOP_K * tm
    return pl.pallas_call(
        kern,
        grid=(t // tm,),
        in_specs=[
            row(d), row(pw),
            pl.BlockSpec((POOL_HALO, pw), lambda i: (jnp.maximum(i * halo_blocks - 1, 0), 0)),
            row(d), row(d), row(d),
            pl.BlockSpec((1, N_MOD, d), lambda i: (i // steps_per_seq, 0, 0)),
            full2(w_att),
            pl.BlockSpec(w_pg.shape, lambda i: (0, 0, 0)),
            full2(pool_scale), full2(w_pb), full2(w_out), full2(g_post_mix), full2(g_pre_ffn),
            full2(w_router_t), full2(b_router_col),
        ],
        out_specs=[
            row(d), col(TOP_K), col(TOP_K),
            pl.BlockSpec((n_experts, 1), lambda i: (0, 0)),
            pl.BlockSpec((n_experts, tab_w), lambda i: (0, 0)),
            pl.BlockSpec(memory_space=pl.ANY),
        ],
        out_shape=[
            jax.ShapeDtypeStruct((t, d), F32),
            jax.ShapeDtypeStruct((TOP_K, t), F32),
            jax.ShapeDtypeStruct((TOP_K, t), jnp.int32),
            jax.ShapeDtypeStruct((n_experts, 1), jnp.int32),
            jax.ShapeDtypeStruct((n_experts, tab_w), jnp.int32),
            jax.ShapeDtypeStruct((xs_rows * n_sub, LANES), F32),
        ],
        scratch_shapes=[
            pltpu.VMEM((n_experts, 1), F32),
            pltpu.VMEM((n_experts, 1), F32),
            pltpu.VMEM((1, 1), F32),
            pltpu.VMEM((2, tm * n_sub, LANES), F32),
            pltpu.VMEM((TM_EXP * n_sub, LANES), F32),
            pltpu.VMEM((2, TOP_K, tm), jnp.int32),
            pltpu.SMEM((2, TOP_K, tm), jnp.int32),
            pltpu.SemaphoreType.DMA((2,)),
            pltpu.SemaphoreType.DMA((2,)),
            pltpu.SemaphoreType.DMA,
        ],
        compiler_params=_cparams(("arbitrary",)),
        name="post_mix",
    )(o, p, p, sga, sgp, x2, mod3, w_att, w_pg, pool_scale, w_pb, w_out,
      g_post_mix, g_pre_ffn, w_router_t, b_router_col)


def _experts_kernel(te_ref, blk_ref, nt_ref, xs_ref, wgu_ref, bgu_ref, wd_ref, bd_ref,
                    y_ref, wgu_bf, wd_bf, *, n_sub):
    j = pl.program_id(0)
    tm = TM_EXP
    d_ff = wd_ref.shape[1]

    @pl.when(j < nt_ref[0])
    def _():
        changed = jnp.logical_or(j == 0, te_ref[j] != te_ref[jnp.maximum(j - 1, 0)])

        @pl.when(changed)
        def _():
            wgu_bf[...] = wgu_ref[0].astype(BF16)
            wd_bf[...] = wd_ref[0].astype(BF16)

        xs = jnp.concatenate(
            [xs_ref[pl.ds(s, tm, stride=n_sub), :] for s in range(n_sub)], axis=1).astype(BF16)
        gu = jnp.dot(xs, wgu_bf[...], preferred_element_type=F32) + bgu_ref[0]
        gate = jnp.minimum(gu[:, :d_ff], SWIGLU_LIMIT)
        up = jnp.clip(gu[:, d_ff:], -SWIGLU_LIMIT, SWIGLU_LIMIT)
        act = (up + 1.0) * (gate * jax.nn.sigmoid(SWIGLU_ALPHA * gate))
        y = jnp.dot(act.astype(BF16), wd_bf[...], preferred_element_type=F32) + bd_ref[0]
        for s in range(n_sub):
            y_ref[pl.ds(s, tm, stride=n_sub), :] = y[:, s * LANES:(s + 1) * LANES]

    @pl.when(j >= nt_ref[0])
    def _():
        y_ref[...] = jnp.zeros_like(y_ref)


def _experts(tile_expert, tile_block, n_tiles, xs_tiles, w_gate_up, b_gate_up, w_down, b_down):
    n_exp, d, d_gu = w_gate_up.shape
    d_ff = w_down.shape[1]
    n_sub = d // LANES
    tm = TM_EXP
    max_tiles = tile_expert.shape[0]
    by_expert = lambda j, te, blk, nt: (te[j], 0, 0)
    by_block = lambda j, te, blk, nt: (blk[j], 0)
    grid_spec = pltpu.PrefetchScalarGridSpec(
        num_scalar_prefetch=3,
        grid=(max_tiles,),
        in_specs=[
            pl.BlockSpec((tm * n_sub, LANES), by_block),
            pl.BlockSpec((1, d, d_gu), by_expert),
            pl.BlockSpec((1, 1, d_gu), by_expert),
            pl.BlockSpec((1, d_ff, d), by_expert),
            pl.BlockSpec((1, 1, d), by_expert),
        ],
        out_specs=pl.BlockSpec((tm * n_sub, LANES), by_block),
        scratch_shapes=[
            pltpu.VMEM((d, d_gu), BF16),
            pltpu.VMEM((d_ff, d), BF16),
        ],
    )
    return pl.pallas_call(
        functools.partial(_experts_kernel, n_sub=n_sub),
        grid_spec=grid_spec,
        out_shape=jax.ShapeDtypeStruct((max_tiles * tm * n_sub, LANES), F32),
        compiler_params=_cparams(("arbitrary",)),
        name="experts",
    )(tile_expert, tile_block, n_tiles, xs_tiles,
      w_gate_up, b_gate_up.reshape(n_exp, 1, d_gu), w_down, b_down.reshape(n_exp, 1, d))


def _combine_kernel(pos_ref, y_hbm, prob_ref, x1_ref, mod_ref, g_ref, o_ref, ybuf, sem,
                    *, n_sub, n_tokens):
    i = pl.program_id(0)
    n_steps = pl.num_programs(0)
    tm = TM_COMB
    slot = i % 2

    def gather(step, s):
        for r in range(tm):
            for k in range(TOP_K):
                row = pos_ref[k * n_tokens + step * tm + r]
                pltpu.make_async_copy(
                    y_hbm.at[pl.ds(pl.multiple_of(row * n_sub, n_sub), n_sub), :],
                    ybuf.at[s, k, pl.ds(r * n_sub, n_sub), :], sem.at[s]).start()

    @pl.when(i == 0)
    def _():
        gather(0, 0)

    @pl.when(i + 1 < n_steps)
    def _():
        gather(i + 1, 1 - slot)

    for k in range(TOP_K):
        pltpu.make_async_copy(y_hbm.at[pl.ds(0, tm * n_sub), :], ybuf.at[slot, k], sem.at[slot]).wait()

    prob = prob_ref[...]
    f = None
    for k in range(TOP_K):
        yk = jnp.concatenate(
            [ybuf[slot, k, pl.ds(s, tm, stride=n_sub), :] for s in range(n_sub)], axis=1)
        term = yk * prob[:, k:k + 1]
        f = term if f is None else f + term
    gate_f = mod_ref[0, 5:6, :]
    o_ref[...] = x1_ref[...] + gate_f * (_rms(f) * g_ref[...])


def _combine(pos_flat, y_tiles, probs, x1, mod3, g_post_ffn, seq):
    t, d = x1.shape
    tm = TM_COMB
    n_sub = d // LANES
    steps_per_seq = seq // tm
    grid_spec = pltpu.PrefetchScalarGridSpec(
        num_scalar_prefetch=1,
        grid=(t // tm,),
        in_specs=[
            pl.BlockSpec(memory_space=pl.ANY),
            pl.BlockSpec((tm, TOP_K), lambda i, pos: (i, 0)),
            pl.BlockSpec((tm, d), lambda i, pos: (i, 0)),
            pl.BlockSpec((1, N_MOD, d), lambda i, pos: (i // steps_per_seq, 0, 0)),
            pl.BlockSpec((1, d), lambda i, pos: (0, 0)),
        ],
        out_specs=pl.BlockSpec((tm, d), lambda i, pos: (i, 0)),
        scratch_shapes=[
            pltpu.VMEM((2, TOP_K, tm * n_sub, LANES), F32),
            pltpu.SemaphoreType.DMA((2,)),
        ],
    )
    return pl.pallas_call(
        functools.partial(_combine_kernel, n_sub=n_sub, n_tokens=t),
        grid_spec=grid_spec,
        out_shape=jax.ShapeDtypeStruct((t, d), F32),
        compiler_params=_cparams(("arbitrary",)),
        name="combine",
    )(pos_flat, y_tiles, probs, x1, mod3, g_post_ffn)


def _rope_freq_row():
    half = ROPE_DIM // 2
    inv_freq = ROPE_THETA ** (-jnp.arange(0, ROPE_DIM, 2, dtype=F32) / ROPE_DIM)
    head = jnp.concatenate([inv_freq, inv_freq, jnp.zeros((DA_HEAD_DIM - 2 * half,), F32)])
    return jnp.tile(head, LANES // DA_HEAD_DIM).reshape(1, LANES)


def _tile_tables(counts, block_tab, n_tokens):
    tm = TM_EXP
    n_exp = counts.shape[0]
    max_tiles = (n_tokens * TOP_K) // tm + n_exp
    tiles = (counts + tm - 1) // tm
    tile_end = jnp.cumsum(tiles)
    n_tiles = tile_end[-1]
    j = jnp.arange(max_tiles, dtype=jnp.int32)
    jj = jnp.minimum(j, n_tiles - 1)
    tile_expert = jnp.sum(jj[:, None] >= tile_end[None, :], axis=1).astype(jnp.int32)
    nth = jj - (tile_end - tiles)[tile_expert]
    tile_block = jnp.where(j < n_tiles, block_tab[tile_expert, nth], j)
    return tile_expert, tile_block.astype(jnp.int32), n_tiles.reshape(1).astype(jnp.int32)


def kernel(x, c, positions, w_mod, b_mod, g_pre_mix, w_in, lambda_q1, lambda_k1, lambda_q2, lambda_k2, g_sub, w_pool_group, pool_scale, w_att_branch, w_pool_branch, w_out, g_post_mix, g_pre_ffn, w_router, b_router, w_gate_up, b_gate_up, w_down, b_down, g_post_ffn):
    bsz, seq, d = x.shape
    assert w_mod.shape[0] == 1, "single layer"
    assert seq % TQ == 0 and seq % TM_PROJ == 0 and seq % TM_MIX == 0 and seq % TM_COMB == 0
    t = bsz * seq
    assert t % TM_EXP == 0
    x2 = x.reshape(t, d)
    qk_w = 2 * DA_HEADS * DA_HEAD_DIM
    v_w = DA_HEADS * DA_V_DIM
    pool_w = len(POOL_WINDOWS) * POOL_GROUP_DIM
    n_experts = w_router.shape[2]

    mod3 = _mod(c, w_mod[0], b_mod[0]).reshape(bsz, N_MOD, d)

    bounds = [0, qk_w, 2 * qk_w, 2 * qk_w + v_w, 2 * qk_w + v_w + pool_w,
              2 * qk_w + v_w + pool_w + d, 2 * qk_w + v_w + pool_w + 2 * d]
    w_in_b = w_in[0].astype(BF16)
    w_vt = w_in_b[:, bounds[2]:bounds[3]].T
    q, k, vt, p, sga, sgp = _in_proj(x2, mod3, g_pre_mix, positions.reshape(t, 1), _rope_freq_row(),
                                     w_in_b, w_vt, bounds, seq)

    row64 = lambda a: a.reshape(1, DA_HEAD_DIM)
    o = _attention(q, k, vt, row64(lambda_q1[0]), row64(lambda_k1[0]), row64(lambda_q2[0]),
                   row64(lambda_k2[0]), g_sub.reshape(DA_V_DIM, 1), bsz, seq)

    x1, probs_t, pos_t, counts, block_tab, xs_tiles = _post_mix(
        o, p, sga, sgp, x2, mod3, w_att_branch[0].astype(BF16), w_pool_group[0].astype(BF16),
        pool_scale, w_pool_branch[0].astype(BF16), w_out[0].astype(BF16),
        g_post_mix, g_pre_ffn, w_router[0].T, b_router.reshape(n_experts, 1), seq)

    tile_expert, tile_block, n_tiles = _tile_tables(counts[:, 0], block_tab, t)
    y_tiles = _experts(tile_expert, tile_block, n_tiles, xs_tiles,
                       w_gate_up[0], b_gate_up[0], w_down[0], b_down[0])
    out = _combine(pos_t.reshape(-1), y_tiles, probs_t.T, x1, mod3, g_post_ffn, seq)
    return out.reshape(bsz, seq, d)
```

```python
import functools

import jax
import jax.numpy as jnp
from jax import lax
from jax.experimental import pallas as pl
from jax.experimental.pallas import tpu as pltpu

F32 = jnp.float32
BF16 = jnp.bfloat16

NORM_EPS = 1e-6
CHUNK = 64
DA_HEADS = 8
DA_HEAD_DIM = 64
DA_V_DIM = 2 * DA_HEAD_DIM
ROPE_THETA = 500000.0
ROPE_DIM = DA_HEAD_DIM // 4
POOL_WINDOWS = (2, 4, 8, 16)
POOL_GROUP_DIM = 128
TOP_K = 4
SWIGLU_LIMIT = 7.0
SWIGLU_ALPHA = 1.702
N_MOD = 6
LAMBDA_INIT = 0.8 - 0.6 * 1.0
Q_SCALE = (DA_HEAD_DIM ** -0.5) * 1.4426950408889634

LANES = 128
SUBLANES = 8
VMEM_LIMIT = 56 * 1024 * 1024

TM_PROJ = 256
TQ = 512
TK = 256
ATT_HEADS_PER_STEP = 2
ATT_SUM_ROWS = 16
TM_MIX = 256
TM_EXP = 512
TM_COMB = 256
DISPATCH_GROUPS = 4
POOL_HALO = 16
NEG_BIG = -1e30


def _cparams(sem):
    return pltpu.CompilerParams(dimension_semantics=sem, vmem_limit_bytes=VMEM_LIMIT)


def _split(a):
    hi = a.astype(BF16)
    return hi, (a - hi.astype(F32)).astype(BF16)


def _split_dot(a, b, dims=(((1,), (0,)), ((), ()))):
    a_hi, a_lo = _split(a)
    b_hi, b_lo = _split(b)
    dot = lambda x, y: lax.dot_general(x, y, dims, preferred_element_type=F32)
    return dot(a_hi, b_hi) + (dot(a_hi, b_lo) + dot(a_lo, b_hi))


def _rms(x):
    return x * lax.rsqrt(jnp.mean(x * x, axis=-1, keepdims=True) + NORM_EPS)


def _mod_kernel(c_ref, w_ref, b_ref, o_ref):
    c = c_ref[...]
    c_act = c * jax.nn.sigmoid(c)
    o_ref[...] = _split_dot(c_act, w_ref[...]) + b_ref[...]


def _mod(c, w_mod, b_mod):
    bsz, d = c.shape
    n = w_mod.shape[1]
    tn = 1024
    return pl.pallas_call(
        _mod_kernel,
        grid=(n // tn,),
        in_specs=[
            pl.BlockSpec((bsz, d), lambda j: (0, 0)),
            pl.BlockSpec((d, tn), lambda j: (0, j)),
            pl.BlockSpec((1, tn), lambda j: (0, j)),
        ],
        out_specs=pl.BlockSpec((bsz, tn), lambda j: (0, j)),
        out_shape=jax.ShapeDtypeStruct((bsz, n), F32),
        compiler_params=_cparams(("parallel",)),
        name="mod",
    )(c, w_mod, b_mod.reshape(1, n))


def _rope(t, cos_t, sin_a, sin_b):
    n = t.shape[1]
    up = pltpu.roll(t, n - ROPE_DIM // 2, axis=1)
    dn = pltpu.roll(t, ROPE_DIM // 2, axis=1)
    reps = n // LANES
    tile = lambda a: jnp.concatenate([a] * reps, axis=1)
    return t * tile(cos_t) + up * tile(sin_a) + dn * tile(sin_b)


def _in_proj_kernel(x_ref, mod_ref, g_ref, pos_ref, freq_ref, w_ref, wvt_ref,
                    q_ref, k_ref, v_ref, p_ref, sga_ref, sgp_ref, *, bounds):
    x = x_ref[...]
    shift = mod_ref[0, 0:1, :]
    scale = mod_ref[0, 1:2, :]
    u = (_rms(x) * g_ref[...]) * (1.0 + scale) + shift
    ub = u.astype(BF16)
    dot = functools.partial(jnp.dot, preferred_element_type=F32)
    part = lambda n: w_ref[:, bounds[n]:bounds[n + 1]]

    ang = pos_ref[...].astype(F32) * freq_ref[...]
    cos_t, sn = jnp.cos(ang), jnp.sin(ang)
    in_head = lax.broadcasted_iota(jnp.int32, ang.shape, 1) % DA_HEAD_DIM
    sin_a = jnp.where(in_head < ROPE_DIM // 2, -sn, 0.0)
    sin_b = jnp.where(in_head >= ROPE_DIM // 2, sn, 0.0)

    q = _rope(dot(ub, part(0)), cos_t, sin_a, sin_b)
    q_ref[...] = (q * Q_SCALE).astype(BF16)
    k = _rope(dot(ub, part(1)), cos_t, sin_a, sin_b)
    k_ref[...] = k.astype(BF16)
    vt = lax.dot_general(wvt_ref[...], ub, (((1,), (1,)), ((), ())), preferred_element_type=F32)
    v_ref[0] = vt.astype(BF16)
    p_ref[...] = dot(ub, part(3))
    sga_ref[...] = jax.nn.sigmoid(dot(ub, part(4))).astype(BF16)
    sgp_ref[...] = jax.nn.sigmoid(dot(ub, part(5))).astype(BF16)


def _in_proj(x2, mod3, g_pre, pos_col, freq_row, w_in_b, w_vt, bounds, seq):
    t, d = x2.shape
    tm = TM_PROJ
    assert tm == TK, "v is emitted as one transposed (channels, TK) slab per step"
    steps_per_seq = seq // tm
    widths = [b - a for a, b in zip(bounds[:-1], bounds[1:])]
    row = lambda w: pl.BlockSpec((tm, w), lambda i: (i, 0))
    full = lambda a: pl.BlockSpec(a.shape, lambda i: (0, 0))
    out_specs = [row(widths[0]), row(widths[1]),
                 pl.BlockSpec((1, widths[2], tm), lambda i: (i, 0, 0)),
                 row(widths[3]), row(widths[4]), row(widths[5])]
    out_shape = [jax.ShapeDtypeStruct((t, widths[0]), BF16),
                 jax.ShapeDtypeStruct((t, widths[1]), BF16),
                 jax.ShapeDtypeStruct((t // tm, widths[2], tm), BF16),
                 jax.ShapeDtypeStruct((t, widths[3]), F32),
                 jax.ShapeDtypeStruct((t, widths[4]), BF16),
                 jax.ShapeDtypeStruct((t, widths[5]), BF16)]
    return pl.pallas_call(
        functools.partial(_in_proj_kernel, bounds=tuple(bounds)),
        grid=(t // tm,),
        in_specs=[
            row(d),
            pl.BlockSpec((1, N_MOD, d), lambda i: (i // steps_per_seq, 0, 0)),
            full(g_pre), row(1), full(freq_row), full(w_in_b), full(w_vt),
        ],
        out_specs=out_specs,
        out_shape=out_shape,
        compiler_params=_cparams(("parallel",)),
        name="in_proj",
    )(x2, mod3, g_pre, pos_col, freq_row, w_in_b, w_vt)


def _attn_kernel(lq1_ref, lk1_ref, lq2_ref, lk2_ref, g_ref, q_ref, k_ref, vt_ref, o_ref,
                 s_a, s_b, p_a, p_b, acc_buf):
    assert TQ % TK == 0 and TQ // TK in (1, 2)
    seq = q_ref.shape[0]
    lam = (jnp.exp(jnp.sum(lq1_ref[...] * lk1_ref[...], axis=-1, keepdims=True))
           - jnp.exp(jnp.sum(lq2_ref[...] * lk2_ref[...], axis=-1, keepdims=True))
           + LAMBDA_INIT)
    dot = functools.partial(jnp.dot, preferred_element_type=F32)
    heads = range(ATT_HEADS_PER_STEP)
    lanes = lambda h: slice(h * LANES, (h + 1) * LANES)

    def q_tile(qi, c):
        q_rows = pl.ds(pl.multiple_of(qi * TQ, TQ), TQ)
        qqt = []
        for h in heads:
            qt = q_ref[q_rows, lanes(h)].astype(F32).T
            row = lax.broadcasted_iota(jnp.int32, qt.shape, 0)
            zero = jnp.zeros_like(qt)
            qqt.append(jnp.concatenate([jnp.where(row < DA_HEAD_DIM, qt, zero),
                                        jnp.where(row >= DA_HEAD_DIM, qt, zero)],
                                       axis=1).astype(BF16))

        def scores(j, h):
            return dot(k_ref[pl.ds(pl.multiple_of(j * TK, TK), TK), lanes(h)], qqt[h])

        def softmax_step(s, m):
            m_new = jnp.maximum(m, jnp.max(s, axis=0, keepdims=True))
            return m_new, jnp.exp2(m - m_new), jnp.exp2(s - m_new).astype(BF16)

        ones_rows = jnp.ones((ATT_SUM_ROWS, TK), BF16)

        def pv(j, h, p):
            return dot(jnp.concatenate([vt_ref[j, lanes(h), :], ones_rows], axis=0), p)

        n_kv = vt_ref.shape[0]
        kk = lax.broadcasted_iota(jnp.int32, (TK, 2 * TQ), 0)
        qq = lax.broadcasted_iota(jnp.int32, (TK, 2 * TQ), 1)
        rel_chunk = jnp.where(qq >= TQ, qq - TQ, qq) // CHUNK - kk // CHUNK

        def masked(s, j):
            return jnp.where(rel_chunk >= j * (TK // CHUNK) - qi * (TQ // CHUNK), s, NEG_BIG)

        def step(j, carries, s_cur, p_cur, s_nxt, p_prev, mask):
            pend = [pv(jnp.maximum(j - 1, 0), h, p_prev[h]) for h in heads]
            if s_nxt is not None:
                for h in heads:
                    s_nxt[h] = scores(jnp.minimum(j + 1, n_kv - 1), h)
            out = []
            for h in heads:
                m, alpha = carries[h]
                acc_buf[h] = alpha * acc_buf[h] + pend[h]
                s = s_cur[h]
                m, alpha, p = softmax_step(masked(s, j) if mask else s, m)
                p_cur[h] = p
                out.append((m, alpha))
            return tuple(out)

        def pair(i, carries):
            carries = step(2 * i, carries, s_a, p_a, s_b, p_b, False)
            return step(2 * i + 1, carries, s_b, p_b, s_a, p_a, False)

        for h in heads:
            s_a[h] = scores(0, h)
            p_b[h] = jnp.zeros((TK, 2 * TQ), BF16)
            acc_buf[h] = jnp.zeros(acc_buf.shape[1:], F32)
        init = tuple((jnp.full((1, 2 * TQ), NEG_BIG, F32), jnp.ones((1, 2 * TQ), F32))
                     for _ in heads)
        n_pairs = (qi * (TQ // TK)) // 2
        carries = lax.fori_loop(0, n_pairs, pair, init)
        ja = 2 * n_pairs
        carries = step(ja, carries, s_a, p_a, s_b, p_b, True)
        carries = step(ja + 1, carries, s_b, p_b, None, p_a, True)
        for h in heads:
            _, alpha = carries[h]
            acc = alpha * acc_buf[h] + pv(jnp.minimum(ja + 1, n_kv - 1), h, p_b[h])
            o = acc[:DA_V_DIM] / acc[DA_V_DIM:DA_V_DIM + 1]
            a = o[:, :TQ] - lam * o[:, TQ:]
            y = a * lax.rsqrt(jnp.mean(a * a, axis=0, keepdims=True) + NORM_EPS)
            y = (y * g_ref[...]) * (1.0 - LAMBDA_INIT)
            o_ref[q_rows, lanes(h)] = y.T.astype(BF16)
        return c

    lax.fori_loop(0, seq // TQ, q_tile, 0)


def _attention(q, k, vt, lq1, lk1, lq2, lk2, g_col, bsz, seq):
    t = q.shape[0]
    nkv = seq // TK
    hw = ATT_HEADS_PER_STEP * LANES
    vec = lambda a: pl.BlockSpec(a.shape, lambda b, h: (0, 0))
    return pl.pallas_call(
        _attn_kernel,
        grid=(bsz, DA_HEADS // ATT_HEADS_PER_STEP),
        in_specs=[
            vec(lq1), vec(lk1), vec(lq2), vec(lk2), vec(g_col),
            pl.BlockSpec((seq, hw), lambda b, h: (b, h)),
            pl.BlockSpec((seq, hw), lambda b, h: (b, h)),
            pl.BlockSpec((nkv, hw, TK), lambda b, h: (b, h, 0)),
        ],
        out_specs=pl.BlockSpec((seq, hw), lambda b, h: (b, h)),
        out_shape=jax.ShapeDtypeStruct((t, DA_HEADS * DA_V_DIM), BF16),
        scratch_shapes=[
            pltpu.VMEM((ATT_HEADS_PER_STEP, TK, 2 * TQ), F32),
            pltpu.VMEM((ATT_HEADS_PER_STEP, TK, 2 * TQ), F32),
            pltpu.VMEM((ATT_HEADS_PER_STEP, TK, 2 * TQ), BF16),
            pltpu.VMEM((ATT_HEADS_PER_STEP, TK, 2 * TQ), BF16),
            pltpu.VMEM((ATT_HEADS_PER_STEP, DA_V_DIM + ATT_SUM_ROWS, 2 * TQ), F32),
        ],
        compiler_params=_cparams(("parallel", "parallel")),
        name="attention",
    )(lq1, lk1, lq2, lk2, g_col, q, k, vt)


def _post_mix_kernel(o_ref, p_ref, ph_ref, sga_ref, sgp_ref, x_ref, mod_ref,
                     watt_ref, wpg_ref, ps_ref, wpb_ref, wout_ref, gpm_ref, gpf_ref,
                     wrt_ref, br_ref,
                     x1_ref, prob_ref, pos_ref, cnt_ref, tab_ref, xs_hbm,
                     carry_ref, cur_ref, free_ref, u2t, zeros_v, pos_v, pos_s,
                     sem_rows, sem_pos, sem_zero,
                     *, steps_per_seq, n_blocks):
    i = pl.program_id(0)
    n_steps = pl.num_programs(0)
    tm = x_ref.shape[0]
    n_experts = wrt_ref.shape[0]
    n_sub = x_ref.shape[1] // LANES
    blk = TM_EXP * n_sub
    per_step = (tm * TOP_K) // TM_EXP
    slot = i % 2
    prev = 1 - slot
    dot = functools.partial(jnp.dot, preferred_element_type=F32)

    def pos_copy(s):
        return pltpu.make_async_copy(pos_v.at[s], pos_s.at[s], sem_pos.at[s])

    def row_copy(s, k, r):
        dst = pos_s[s, k, r]
        return pltpu.make_async_copy(
            u2t.at[s, pl.ds(r * n_sub, n_sub), :],
            xs_hbm.at[pl.ds(pl.multiple_of(dst * n_sub, n_sub), n_sub), :], sem_rows.at[s])

    def wait_rows(s):
        for _ in range(TOP_K):
            pltpu.make_async_copy(u2t.at[s], xs_hbm.at[pl.ds(0, tm * n_sub), :], sem_rows.at[s]).wait()

    def zero_copy(b):
        return pltpu.make_async_copy(zeros_v, xs_hbm.at[pl.ds(pl.multiple_of(b * blk, blk), blk), :],
                                     sem_zero)

    @pl.when(i == 0)
    def _():
        carry_ref[...] = jnp.zeros_like(carry_ref)
        cur_ref[...] = jnp.zeros_like(cur_ref)
        free_ref[...] = jnp.zeros_like(free_ref)
        tab_ref[...] = jnp.zeros_like(tab_ref)
        zeros_v[...] = jnp.zeros_like(zeros_v)
        u2t[1] = jnp.zeros(u2t.shape[1:], F32)
        spare = (n_blocks * TM_EXP
                 + lax.broadcasted_iota(jnp.int32, (TOP_K, tm), 0) * tm
                 + lax.broadcasted_iota(jnp.int32, (TOP_K, tm), 1))
        pos_v[1] = spare
        pos_copy(1).start()
        for b in range(n_experts):
            zero_copy(b).start()
        for b in range(n_experts):
            zero_copy(b).wait()

    for b in range(per_step):
        zero_copy(n_experts + i * per_step + b).start()

    pos_copy(prev).wait()

    stays = pos_s[prev, 0, 0] >= 0

    def dispatch_group(g):
        rows = tm // DISPATCH_GROUPS

        @pl.when(stays)
        def _():
            for r in range(g * rows, (g + 1) * rows):
                for k in range(TOP_K):
                    row_copy(prev, k, r).start()

    dispatch_group(0)
    y_att = dot(o_ref[...], watt_ref[...])
    dispatch_group(1)

    first = (i % steps_per_seq) == 0
    halo = jnp.where(first, 0.0, ph_ref[...])
    ext = jnp.concatenate([halo, p_ref[...]], axis=0)
    t_in_seq = (i % steps_per_seq) * tm + lax.broadcasted_iota(jnp.int32, (tm, 1), 0)
    pooled = []
    for g, w in enumerate(POOL_WINDOWS):
        e = ext[:, g * POOL_GROUP_DIM:(g + 1) * POOL_GROUP_DIM]
        acc, span = e, 1
        while span < w:
            acc = acc[span:] + acc[:-span]
            span *= 2
        win = acc[POOL_HALO - (w - 1):]
        cnt = jnp.minimum(t_in_seq + 1, w).astype(F32)
        mixed = win / cnt - e[POOL_HALO:]
        pooled.append(dot(mixed.astype(BF16), wpg_ref[g]))
    y_pool_in = jnp.concatenate(pooled, axis=1) * ps_ref[...]
    y_pool = dot(y_pool_in.astype(BF16), wpb_ref[...])
    dispatch_group(2)

    merged = sga_ref[...].astype(F32) * y_att + sgp_ref[...].astype(F32) * y_pool
    mix_out = dot(merged.astype(BF16), wout_ref[...])
    gate_m = mod_ref[0, 2:3, :]
    shift_f = mod_ref[0, 3:4, :]
    scale_f = mod_ref[0, 4:5, :]
    x1 = x_ref[...] + gate_m * (_rms(mix_out) * gpm_ref[...])
    x1_ref[...] = x1
    u2 = (_rms(x1) * gpf_ref[...]) * (1.0 + scale_f) + shift_f
    dispatch_group(3)

    logits = _split_dot(wrt_ref[...], u2, (((1,), (1,)), ((), ()))) + br_ref[...]
    erow = lax.broadcasted_iota(jnp.int32, logits.shape, 0)
    work = logits
    vals, idxs = [], []
    for _ in range(TOP_K):
        mx = jnp.max(work, axis=0, keepdims=True)
        ix = jnp.min(jnp.where(work == mx, erow, n_experts), axis=0, keepdims=True)
        vals.append(mx)
        idxs.append(ix)
        work = jnp.where(erow == ix, -jnp.inf, work)
    exps = [jnp.exp(vv - vals[0]) for vv in vals]
    denom = exps[0] + exps[1] + exps[2] + exps[3]
    prob_ref[...] = jnp.concatenate([e / denom for e in exps], axis=0)

    onehot = jnp.zeros(logits.shape, F32)
    for ix in idxs:
        onehot = onehot + (erow == ix).astype(F32)
    rr = lax.broadcasted_iota(jnp.int32, (tm, tm), 0)
    cc = lax.broadcasted_iota(jnp.int32, (tm, tm), 1)
    earlier = (rr < cc).astype(BF16)
    carry = carry_ref[...]
    before = dot(onehot.astype(BF16), earlier) + carry

    blk_rows = float(TM_EXP)
    total = carry + jnp.sum(onehot, axis=1, keepdims=True)
    blocks_old = jnp.floor((carry + (blk_rows - 1.0)) / blk_rows)
    opened = jnp.floor((total + (blk_rows - 1.0)) / blk_rows) - blocks_old
    e_r = lax.broadcasted_iota(jnp.int32, (n_experts, n_experts), 0)
    e_c = lax.broadcasted_iota(jnp.int32, (n_experts, n_experts), 1)
    lower = (e_c < e_r).astype(BF16)
    opened_before = dot(lower, jnp.broadcast_to(opened, (n_experts, LANES)).astype(BF16))[:, 0:1]
    new_blk = free_ref[...] + opened_before
    cur_blk = cur_ref[...]
    boundary = blocks_old * blk_rows
    blk_of = jnp.where(before < boundary, cur_blk, new_blk)
    pos_rows = []
    for ix in idxs:
        pick = erow == ix
        rank = jnp.sum(jnp.where(pick, before, 0.0), axis=0, keepdims=True)
        blk_id = jnp.sum(jnp.where(pick, blk_of, 0.0), axis=0, keepdims=True)
        within = rank - jnp.floor(rank / blk_rows) * blk_rows
        pos_rows.append((blk_id * blk_rows + within).astype(jnp.int32))
    pos = jnp.concatenate(pos_rows, axis=0)
    pos_ref[...] = pos
    tab_col = lax.broadcasted_iota(jnp.int32, tab_ref.shape, 1).astype(F32)
    tab_ref[...] = jnp.where(jnp.logical_and(tab_col == blocks_old, opened > 0.0),
                             new_blk.astype(jnp.int32), tab_ref[...])
    cur_ref[...] = jnp.where(opened > 0.0, new_blk, cur_blk)
    free_ref[...] = free_ref[...] + jnp.sum(opened, axis=0, keepdims=True)
    carry_ref[...] = total
    cnt_ref[...] = total.astype(jnp.int32)

    @pl.when(i > 0)
    def _():
        wait_rows(slot)

    for s in range(n_sub):
        u2t[slot, pl.ds(s, tm, stride=n_sub), :] = u2[:, s * LANES:(s + 1) * LANES]
    pos_v[slot] = pos
    pos_copy(slot).start()
    for b in range(per_step):
        zero_copy(n_experts + i * per_step + b).wait()

    @pl.when(i == n_steps - 1)
    def _():
        pos_copy(slot).wait()

        def issue(r, c):
            for k in range(TOP_K):
                row_copy(slot, k, r).start()
            return c
        lax.fori_loop(0, tm, issue, 0)
        wait_rows(prev)
        wait_rows(slot)


def _post_mix(o, p, sga, sgp, x2, mod3, w_att, w_pg, pool_scale, w_pb, w_out,
              g_post_mix, g_pre_ffn, w_router_t, b_router_col, seq):
    t, d = x2.shape
    tm = TM_MIX
    steps_per_seq = seq // tm
    n_experts = w_router_t.shape[0]
    pw = p.shape[1]
    n_sub = d // LANES
    halo_blocks = tm // POOL_HALO
    row = lambda w: pl.BlockSpec((tm, w), lambda i: (i, 0))
    col = lambda h: pl.BlockSpec((h, tm), lambda i: (0, i))
    full2 = lambda a: pl.BlockSpec(a.shape, lambda i: (0, 0))
    assert tm <= TM_EXP and (tm * TOP_K) % TM_EXP == 0 and t % TM_EXP == 0
    n_blocks = (t * TOP_K) // TM_EXP + n_experts
    tab_w = -(-(t // TM_EXP) // LANES) * LANES
    kern = functools.partial(_post_mix_kernel, steps_per_seq=steps_per_seq, n_blocks=n_blocks)
    xs_rows = n_blocks * TM_EXP + TOP_K * tm
    return pl.pallas_call(
        kern,
        grid=(t // tm,),
        in_specs=[
            row(d), row(pw),
            pl.BlockSpec((POOL_HALO, pw), lambda i: (jnp.maximum(i * halo_blocks - 1, 0), 0)),
            row(d), row(d), row(d),
            pl.BlockSpec((1, N_MOD, d), lambda i: (i // steps_per_seq, 0, 0)),
            full2(w_att),
            pl.BlockSpec(w_pg.shape, lambda i: (0, 0, 0)),
            full2(pool_scale), full2(w_pb), full2(w_out), full2(g_post_mix), full2(g_pre_ffn),
            full2(w_router_t), full2(b_router_col),
        ],
        out_specs=[
            row(d), col(TOP_K), col(TOP_K),
            pl.BlockSpec((n_experts, 1), lambda i: (0, 0)),
            pl.BlockSpec((n_experts, tab_w), lambda i: (0, 0)),
            pl.BlockSpec(memory_space=pl.ANY),
        ],
        out_shape=[
            jax.ShapeDtypeStruct((t, d), F32),
            jax.ShapeDtypeStruct((TOP_K, t), F32),
            jax.ShapeDtypeStruct((TOP_K, t), jnp.int32),
            jax.ShapeDtypeStruct((n_experts, 1), jnp.int32),
            jax.ShapeDtypeStruct((n_experts, tab_w), jnp.int32),
            jax.ShapeDtypeStruct((xs_rows * n_sub, LANES), F32),
        ],
        scratch_shapes=[
            pltpu.VMEM((n_experts, 1), F32),
            pltpu.VMEM((n_experts, 1), F32),
            pltpu.VMEM((1, 1), F32),
            pltpu.VMEM((2, tm * n_sub, LANES), F32),
            pltpu.VMEM((TM_EXP * n_sub, LANES), F32),
            pltpu.VMEM((2, TOP_K, tm), jnp.int32),
            pltpu.SMEM((2, TOP_K, tm), jnp.int32),
            pltpu.SemaphoreType.DMA((2,)),
            pltpu.SemaphoreType.DMA((2,)),
            pltpu.SemaphoreType.DMA,
        ],
        compiler_params=_cparams(("arbitrary",)),
        name="post_mix",
    )(o, p, p, sga, sgp, x2, mod3, w_att, w_pg, pool_scale, w_pb, w_out,
      g_post_mix, g_pre_ffn, w_router_t, b_router_col)


def _experts_kernel(te_ref, blk_ref, nt_ref, xs_ref, wgu_ref, bgu_ref, wd_ref, bd_ref,
                    y_ref, wgu_bf, wd_bf, *, n_sub):
    j = pl.program_id(0)
    tm = TM_EXP
    d_ff = wd_ref.shape[1]

    @pl.when(j < nt_ref[0])
    def _():
        changed = jnp.logical_or(j == 0, te_ref[j] != te_ref[jnp.maximum(j - 1, 0)])

        @pl.when(changed)
        def _():
            wgu_bf[...] = wgu_ref[0].astype(BF16)
            wd_bf[...] = wd_ref[0].astype(BF16)

        xs = jnp.concatenate(
            [xs_ref[pl.ds(s, tm, stride=n_sub), :] for s in range(n_sub)], axis=1).astype(BF16)
        gu = jnp.dot(xs, wgu_bf[...], preferred_element_type=F32) + bgu_ref[0]
        gate = jnp.minimum(gu[:, :d_ff], SWIGLU_LIMIT)
        up = jnp.clip(gu[:, d_ff:], -SWIGLU_LIMIT, SWIGLU_LIMIT)
        act = (up + 1.0) * (gate * jax.nn.sigmoid(SWIGLU_ALPHA * gate))
        y = jnp.dot(act.astype(BF16), wd_bf[...], preferred_element_type=F32) + bd_ref[0]
        for s in range(n_sub):
            y_ref[pl.ds(s, tm, stride=n_sub), :] = y[:, s * LANES:(s + 1) * LANES]

    @pl.when(j >= nt_ref[0])
    def _():
        y_ref[...] = jnp.zeros_like(y_ref)


def _experts(tile_expert, tile_block, n_tiles, xs_tiles, w_gate_up, b_gate_up, w_down, b_down):
    n_exp, d, d_gu = w_gate_up.shape
    d_ff = w_down.shape[1]
    n_sub = d // LANES
    tm = TM_EXP
    max_tiles = tile_expert.shape[0]
    by_expert = lambda j, te, blk, nt: (te[j], 0, 0)
    by_block = lambda j, te, blk, nt: (blk[j], 0)
    grid_spec = pltpu.PrefetchScalarGridSpec(
        num_scalar_prefetch=3,
        grid=(max_tiles,),
        in_specs=[
            pl.BlockSpec((tm * n_sub, LANES), by_block),
            pl.BlockSpec((1, d, d_gu), by_expert),
            pl.BlockSpec((1, 1, d_gu), by_expert),
            pl.BlockSpec((1, d_ff, d), by_expert),
            pl.BlockSpec((1, 1, d), by_expert),
        ],
        out_specs=pl.BlockSpec((tm * n_sub, LANES), by_block),
        scratch_shapes=[
            pltpu.VMEM((d, d_gu), BF16),
            pltpu.VMEM((d_ff, d), BF16),
        ],
    )
    return pl.pallas_call(
        functools.partial(_experts_kernel, n_sub=n_sub),
        grid_spec=grid_spec,
        out_shape=jax.ShapeDtypeStruct((max_tiles * tm * n_sub, LANES), F32),
        compiler_params=_cparams(("arbitrary",)),
        name="experts",
    )(tile_expert, tile_block, n_tiles, xs_tiles,
      w_gate_up, b_gate_up.reshape(n_exp, 1, d_gu), w_down, b_down.reshape(n_exp, 1, d))


def _combine_kernel(pos_ref, y_hbm, prob_ref, x1_ref, mod_ref, g_ref, o_ref, ybuf, sem,
                    *, n_sub, n_tokens):
    i = pl.program_id(0)
    n_steps = pl.num_programs(0)
    tm = TM_COMB
    slot = i % 2

    def gather(step, s):
        for r in range(tm):
            for k in range(TOP_K):
                row = pos_ref[k * n_tokens + step * tm + r]
                pltpu.make_async_copy(
                    y_hbm.at[pl.ds(pl.multiple_of(row * n_sub, n_sub), n_sub), :],
                    ybuf.at[s, k, pl.ds(r * n_sub, n_sub), :], sem.at[s]).start()

    @pl.when(i == 0)
    def _():
        gather(0, 0)

    @pl.when(i + 1 < n_steps)
    def _():
        gather(i + 1, 1 - slot)

    for k in range(TOP_K):
        pltpu.make_async_copy(y_hbm.at[pl.ds(0, tm * n_sub), :], ybuf.at[slot, k], sem.at[slot]).wait()

    prob = prob_ref[...]
    f = None
    for k in range(TOP_K):
        yk = jnp.concatenate(
            [ybuf[slot, k, pl.ds(s, tm, stride=n_sub), :] for s in range(n_sub)], axis=1)
        term = yk * prob[:, k:k + 1]
        f = term if f is None else f + term
    gate_f = mod_ref[0, 5:6, :]
    o_ref[...] = x1_ref[...] + gate_f * (_rms(f) * g_ref[...])


def _combine(pos_flat, y_tiles, probs, x1, mod3, g_post_ffn, seq):
    t, d = x1.shape
    tm = TM_COMB
    n_sub = d // LANES
    steps_per_seq = seq // tm
    grid_spec = pltpu.PrefetchScalarGridSpec(
        num_scalar_prefetch=1,
        grid=(t // tm,),
        in_specs=[
            pl.BlockSpec(memory_space=pl.ANY),
            pl.BlockSpec((tm, TOP_K), lambda i, pos: (i, 0)),
            pl.BlockSpec((tm, d), lambda i, pos: (i, 0)),
            pl.BlockSpec((1, N_MOD, d), lambda i, pos: (i // steps_per_seq, 0, 0)),
            pl.BlockSpec((1, d), lambda i, pos: (0, 0)),
        ],
        out_specs=pl.BlockSpec((tm, d), lambda i, pos: (i, 0)),
        scratch_shapes=[
            pltpu.VMEM((2, TOP_K, tm * n_sub, LANES), F32),
            pltpu.SemaphoreType.DMA((2,)),
        ],
    )
    return pl.pallas_call(
        functools.partial(_combine_kernel, n_sub=n_sub, n_tokens=t),
        grid_spec=grid_spec,
        out_shape=jax.ShapeDtypeStruct((t, d), F32),
        compiler_params=_cparams(("arbitrary",)),
        name="combine",
    )(pos_flat, y_tiles, probs, x1, mod3, g_post_ffn)


def _rope_freq_row():
    half = ROPE_DIM // 2
    inv_freq = ROPE_THETA ** (-jnp.arange(0, ROPE_DIM, 2, dtype=F32) / ROPE_DIM)
    head = jnp.concatenate([inv_freq, inv_freq, jnp.zeros((DA_HEAD_DIM - 2 * half,), F32)])
    return jnp.tile(head, LANES // DA_HEAD_DIM).reshape(1, LANES)


def _tile_tables(counts, block_tab, n_tokens):
    tm = TM_EXP
    n_exp = counts.shape[0]
    max_tiles = (n_tokens * TOP_K) // tm + n_exp
    tiles = (counts + tm - 1) // tm
    tile_end = jnp.cumsum(tiles)
    n_tiles = tile_end[-1]
    j = jnp.arange(max_tiles, dtype=jnp.int32)
    jj = jnp.minimum(j, n_tiles - 1)
    tile_expert = jnp.sum(jj[:, None] >= tile_end[None, :], axis=1).astype(jnp.int32)
    nth = jj - (tile_end - tiles)[tile_expert]
    tile_block = jnp.where(j < n_tiles, block_tab[tile_expert, nth], j)
    return tile_expert, tile_block.astype(jnp.int32), n_tiles.reshape(1).astype(jnp.int32)


def kernel(x, c, positions, w_mod, b_mod, g_pre_mix, w_in, lambda_q1, lambda_k1, lambda_q2, lambda_k2, g_sub, w_pool_group, pool_scale, w_att_branch, w_pool_branch, w_out, g_post_mix, g_pre_ffn, w_router, b_router, w_gate_up, b_gate_up, w_down, b_down, g_post_ffn):
    bsz, seq, d = x.shape
    assert w_mod.shape[0] == 1, "single layer"
    assert seq % TQ == 0 and seq % TM_PROJ == 0 and seq % TM_MIX == 0 and seq % TM_COMB == 0
    t = bsz * seq
    assert t % TM_EXP == 0
    x2 = x.reshape(t, d)
    qk_w = 2 * DA_HEADS * DA_HEAD_DIM
    v_w = DA_HEADS * DA_V_DIM
    pool_w = len(POOL_WINDOWS) * POOL_GROUP_DIM
    n_experts = w_router.shape[2]

    mod3 = _mod(c, w_mod[0], b_mod[0]).reshape(bsz, N_MOD, d)

    bounds = [0, qk_w, 2 * qk_w, 2 * qk_w + v_w, 2 * qk_w + v_w + pool_w,
              2 * qk_w + v_w + pool_w + d, 2 * qk_w + v_w + pool_w + 2 * d]
    w_in_b = w_in[0].astype(BF16)
    w_vt = w_in_b[:, bounds[2]:bounds[3]].T
    q, k, vt, p, sga, sgp = _in_proj(x2, mod3, g_pre_mix, positions.reshape(t, 1), _rope_freq_row(),
                                     w_in_b, w_vt, bounds, seq)

    row64 = lambda a: a.reshape(1, DA_HEAD_DIM)
    o = _attention(q, k, vt, row64(lambda_q1[0]), row64(lambda_k1[0]), row64(lambda_q2[0]),
                   row64(lambda_k2[0]), g_sub.reshape(DA_V_DIM, 1), bsz, seq)

    x1, probs_t, pos_t, counts, block_tab, xs_tiles = _post_mix(
        o, p, sga, sgp, x2, mod3, w_att_branch[0].astype(BF16), w_pool_group[0].astype(BF16),
        pool_scale, w_pool_branch[0].astype(BF16), w_out[0].astype(BF16),
        g_post_mix, g_pre_ffn, w_router[0].T, b_router.reshape(n_experts, 1), seq)

    tile_expert, tile_block, n_tiles = _tile_tables(counts[:, 0], block_tab, t)
    y_tiles = _experts(tile_expert, tile_block, n_tiles, xs_tiles,
                       w_gate_up[0], b_gate_up[0], w_down[0], b_down[0])
    out = _combine(pos_t.reshape(-1), y_tiles, probs_t.T, x1, mod3, g_post_ffn, seq)
    return out.reshape(bsz, seq, d)
```

```python
import functools

import jax
import jax.numpy as jnp
from jax import lax
from jax.experimental import pallas as pl
from jax.experimental.pallas import tpu as pltpu

F32 = jnp.float32
BF16 = jnp.bfloat16

NORM_EPS = 1e-6
CHUNK = 64
DA_HEADS = 8
DA_HEAD_DIM = 64
DA_V_DIM = 2 * DA_HEAD_DIM
ROPE_THETA = 500000.0
ROPE_DIM = DA_HEAD_DIM // 4
POOL_WINDOWS = (2, 4, 8, 16)
POOL_GROUP_DIM = 128
TOP_K = 4
SWIGLU_LIMIT = 7.0
SWIGLU_ALPHA = 1.702
N_MOD = 6
LAMBDA_INIT = 0.8 - 0.6 * 1.0
Q_SCALE = (DA_HEAD_DIM ** -0.5) * 1.4426950408889634

LANES = 128
SUBLANES = 8
VMEM_LIMIT = 56 * 1024 * 1024

TM_PROJ = 256
TQ = 512
TK = 256
ATT_HEADS_PER_STEP = 2
ATT_SUM_ROWS = 16
TM_MIX = 256
TM_EXP = 512
TM_COMB = 256
DISPATCH_GROUPS = 1
WEIGHT_CHUNK = 512
POOL_HALO = 16
NEG_BIG = -1e30


def _cparams(sem):
    return pltpu.CompilerParams(dimension_semantics=sem, vmem_limit_bytes=VMEM_LIMIT)


def _split(a):
    hi = a.astype(BF16)
    return hi, (a - hi.astype(F32)).astype(BF16)


def _split_dot(a, b, dims=(((1,), (0,)), ((), ()))):
    a_hi, a_lo = _split(a)
    b_hi, b_lo = _split(b)
    dot = lambda x, y: lax.dot_general(x, y, dims, preferred_element_type=F32)
    return dot(a_hi, b_hi) + (dot(a_hi, b_lo) + dot(a_lo, b_hi))


def _rms(x):
    return x * lax.rsqrt(jnp.mean(x * x, axis=-1, keepdims=True) + NORM_EPS)


def _mod_kernel(c_ref, w_ref, b_ref, o_ref):
    c = c_ref[...]
    c_act = c * jax.nn.sigmoid(c)
    o_ref[...] = _split_dot(c_act, w_ref[...]) + b_ref[...]


def _mod(c, w_mod, b_mod):
    bsz, d = c.shape
    n = w_mod.shape[1]
    tn = 1024
    return pl.pallas_call(
        _mod_kernel,
        grid=(n // tn,),
        in_specs=[
            pl.BlockSpec((bsz, d), lambda j: (0, 0)),
            pl.BlockSpec((d, tn), lambda j: (0, j)),
            pl.BlockSpec((1, tn), lambda j: (0, j)),
        ],
        out_specs=pl.BlockSpec((bsz, tn), lambda j: (0, j)),
        out_shape=jax.ShapeDtypeStruct((bsz, n), F32),
        compiler_params=_cparams(("parallel",)),
        name="mod",
    )(c, w_mod, b_mod.reshape(1, n))


def _rope(t, cos_t, sin_a, sin_b):
    n = t.shape[1]
    up = pltpu.roll(t, n - ROPE_DIM // 2, axis=1)
    dn = pltpu.roll(t, ROPE_DIM // 2, axis=1)
    reps = n // LANES
    tile = lambda a: jnp.concatenate([a] * reps, axis=1)
    return t * tile(cos_t) + up * tile(sin_a) + dn * tile(sin_b)


def _stream_windows(src_hbm, windows, stage, sem, sink):
    def copy(n):
        r0, nr, c0, nc = windows[n]
        return pltpu.make_async_copy(src_hbm.at[pl.ds(r0, nr), pl.ds(c0, nc)],
                                     stage.at[n % 2, pl.ds(0, nr), pl.ds(0, nc)], sem.at[n % 2])
    copy(0).start()
    for n, (_, nr, _, nc) in enumerate(windows):
        if n + 1 < len(windows):
            copy(n + 1).start()
        copy(n).wait()
        sink(n, stage[n % 2, 0:nr, 0:nc])


def _in_proj_kernel(x_ref, mod_ref, g_ref, pos_ref, freq_ref, w_hbm,
                    q_ref, k_ref, v_ref, p_ref, sga_ref, sgp_ref,
                    w_ref, wvt_ref, stage, sem, *, bounds):
    @pl.when(pl.program_id(0) == 0)
    def _():
        d_in, width = w_hbm.shape
        windows = [(0, d_in, c0, WEIGHT_CHUNK) for c0 in range(0, width, WEIGHT_CHUNK)]

        def sink(n, tile):
            c0 = windows[n][2]
            w_ref[:, c0:c0 + WEIGHT_CHUNK] = tile.astype(BF16)
            if bounds[2] <= c0 < bounds[3]:
                wvt_ref[c0 - bounds[2]:c0 - bounds[2] + WEIGHT_CHUNK, :] = tile.T.astype(BF16)
        _stream_windows(w_hbm, windows, stage, sem, sink)

    x = x_ref[...]
    shift = mod_ref[0, 0:1, :]
    scale = mod_ref[0, 1:2, :]
    u = (_rms(x) * g_ref[...]) * (1.0 + scale) + shift
    ub = u.astype(BF16)
    dot = functools.partial(jnp.dot, preferred_element_type=F32)
    part = lambda n: w_ref[:, bounds[n]:bounds[n + 1]]

    ang = pos_ref[...].astype(F32) * freq_ref[...]
    cos_t, sn = jnp.cos(ang), jnp.sin(ang)
    in_head = lax.broadcasted_iota(jnp.int32, ang.shape, 1) % DA_HEAD_DIM
    sin_a = jnp.where(in_head < ROPE_DIM // 2, -sn, 0.0)
    sin_b = jnp.where(in_head >= ROPE_DIM // 2, sn, 0.0)

    q = _rope(dot(ub, part(0)), cos_t, sin_a, sin_b)
    q_ref[...] = (q * Q_SCALE).astype(BF16)
    k = _rope(dot(ub, part(1)), cos_t, sin_a, sin_b)
    k_ref[...] = k.astype(BF16)
    vt = lax.dot_general(wvt_ref[...], ub, (((1,), (1,)), ((), ())), preferred_element_type=F32)
    v_ref[0] = vt.astype(BF16)
    p_ref[...] = dot(ub, part(3))
    sga_ref[...] = jax.nn.sigmoid(dot(ub, part(4))).astype(BF16)
    sgp_ref[...] = jax.nn.sigmoid(dot(ub, part(5))).astype(BF16)


def _in_proj(x2, mod3, g_pre, pos_col, freq_row, w_in, bounds, seq):
    t, d = x2.shape
    width = w_in.shape[1]
    assert width % WEIGHT_CHUNK == 0 and all(b % WEIGHT_CHUNK == 0 for b in bounds[2:4])
    tm = TM_PROJ
    assert tm == TK, "v is emitted as one transposed (channels, TK) slab per step"
    steps_per_seq = seq // tm
    widths = [b - a for a, b in zip(bounds[:-1], bounds[1:])]
    row = lambda w: pl.BlockSpec((tm, w), lambda i: (i, 0))
    full = lambda a: pl.BlockSpec(a.shape, lambda i: (0, 0))
    out_specs = [row(widths[0]), row(widths[1]),
                 pl.BlockSpec((1, widths[2], tm), lambda i: (i, 0, 0)),
                 row(widths[3]), row(widths[4]), row(widths[5])]
    out_shape = [jax.ShapeDtypeStruct((t, widths[0]), BF16),
                 jax.ShapeDtypeStruct((t, widths[1]), BF16),
                 jax.ShapeDtypeStruct((t // tm, widths[2], tm), BF16),
                 jax.ShapeDtypeStruct((t, widths[3]), F32),
                 jax.ShapeDtypeStruct((t, widths[4]), BF16),
                 jax.ShapeDtypeStruct((t, widths[5]), BF16)]
    return pl.pallas_call(
        functools.partial(_in_proj_kernel, bounds=tuple(bounds)),
        grid=(t // tm,),
        in_specs=[
            row(d),
            pl.BlockSpec((1, N_MOD, d), lambda i: (i // steps_per_seq, 0, 0)),
            full(g_pre), row(1), full(freq_row), pl.BlockSpec(memory_space=pl.ANY),
        ],
        out_specs=out_specs,
        out_shape=out_shape,
        scratch_shapes=[
            pltpu.VMEM((d, width), BF16),
            pltpu.VMEM((widths[2], d), BF16),
            pltpu.VMEM((2, d, WEIGHT_CHUNK), F32),
            pltpu.SemaphoreType.DMA((2,)),
        ],
        compiler_params=_cparams(("arbitrary",)),
        name="in_proj",
    )(x2, mod3, g_pre, pos_col, freq_row, w_in)


def _attn_kernel(lq1_ref, lk1_ref, lq2_ref, lk2_ref, g_ref, q_ref, k_ref, vt_ref, o_ref,
                 s_a, s_b, p_a, p_b, acc_buf):
    assert TQ % TK == 0 and TQ // TK in (1, 2)
    seq = q_ref.shape[0]
    lam = (jnp.exp(jnp.sum(lq1_ref[...] * lk1_ref[...], axis=-1, keepdims=True))
           - jnp.exp(jnp.sum(lq2_ref[...] * lk2_ref[...], axis=-1, keepdims=True))
           + LAMBDA_INIT)
    dot = functools.partial(jnp.dot, preferred_element_type=F32)
    heads = range(ATT_HEADS_PER_STEP)
    lanes = lambda h: slice(h * LANES, (h + 1) * LANES)

    def q_tile(qi, c):
        q_rows = pl.ds(pl.multiple_of(qi * TQ, TQ), TQ)
        qqt = []
        for h in heads:
            qt = q_ref[q_rows, lanes(h)].astype(F32).T
            row = lax.broadcasted_iota(jnp.int32, qt.shape, 0)
            zero = jnp.zeros_like(qt)
            qqt.append(jnp.concatenate([jnp.where(row < DA_HEAD_DIM, qt, zero),
                                        jnp.where(row >= DA_HEAD_DIM, qt, zero)],
                                       axis=1).astype(BF16))

        def scores(j, h):
            return dot(k_ref[pl.ds(pl.multiple_of(j * TK, TK), TK), lanes(h)], qqt[h])

        def softmax_step(s, m):
            m_new = jnp.maximum(m, jnp.max(s, axis=0, keepdims=True))
            return m_new, jnp.exp2(m - m_new), jnp.exp2(s - m_new).astype(BF16)

        ones_rows = jnp.ones((ATT_SUM_ROWS, TK), BF16)

        def pv(j, h, p):
            return dot(jnp.concatenate([vt_ref[j, lanes(h), :], ones_rows], axis=0), p)

        n_kv = vt_ref.shape[0]
        kk = lax.broadcasted_iota(jnp.int32, (TK, 2 * TQ), 0)
        qq = lax.broadcasted_iota(jnp.int32, (TK, 2 * TQ), 1)
        rel_chunk = jnp.where(qq >= TQ, qq - TQ, qq) // CHUNK - kk // CHUNK

        def masked(s, j):
            return jnp.where(rel_chunk >= j * (TK // CHUNK) - qi * (TQ // CHUNK), s, NEG_BIG)

        def step(j, carries, s_cur, p_cur, s_nxt, p_prev, mask):
            pend = [pv(jnp.maximum(j - 1, 0), h, p_prev[h]) for h in heads]
            if s_nxt is not None:
                for h in heads:
                    s_nxt[h] = scores(jnp.minimum(j + 1, n_kv - 1), h)
            out = []
            for h in heads:
                m, alpha = carries[h]
                acc_buf[h] = alpha * acc_buf[h] + pend[h]
                s = s_cur[h]
                m, alpha, p = softmax_step(masked(s, j) if mask else s, m)
                p_cur[h] = p
                out.append((m, alpha))
            return tuple(out)

        def pair(i, carries):
            carries = step(2 * i, carries, s_a, p_a, s_b, p_b, False)
            return step(2 * i + 1, carries, s_b, p_b, s_a, p_a, False)

        for h in heads:
            s_a[h] = scores(0, h)
            p_b[h] = jnp.zeros((TK, 2 * TQ), BF16)
            acc_buf[h] = jnp.zeros(acc_buf.shape[1:], F32)
        init = tuple((jnp.full((1, 2 * TQ), NEG_BIG, F32), jnp.ones((1, 2 * TQ), F32))
                     for _ in heads)
        n_pairs = (qi * (TQ // TK)) // 2
        carries = lax.fori_loop(0, n_pairs, pair, init)
        ja = 2 * n_pairs
        carries = step(ja, carries, s_a, p_a, s_b, p_b, True)
        carries = step(ja + 1, carries, s_b, p_b, None, p_a, True)
        for h in heads:
            _, alpha = carries[h]
            acc = alpha * acc_buf[h] + pv(jnp.minimum(ja + 1, n_kv - 1), h, p_b[h])
            o = acc[:DA_V_DIM] / acc[DA_V_DIM:DA_V_DIM + 1]
            a = o[:, :TQ] - lam * o[:, TQ:]
            y = a * lax.rsqrt(jnp.mean(a * a, axis=0, keepdims=True) + NORM_EPS)
            y = (y * g_ref[...]) * (1.0 - LAMBDA_INIT)
            o_ref[q_rows, lanes(h)] = y.T.astype(BF16)
        return c

    lax.fori_loop(0, seq // TQ, q_tile, 0)


def _attention(q, k, vt, lq1, lk1, lq2, lk2, g_col, bsz, seq):
    t = q.shape[0]
    nkv = seq // TK
    hw = ATT_HEADS_PER_STEP * LANES
    vec = lambda a: pl.BlockSpec(a.shape, lambda b, h: (0, 0))
    return pl.pallas_call(
        _attn_kernel,
        grid=(bsz, DA_HEADS // ATT_HEADS_PER_STEP),
        in_specs=[
            vec(lq1), vec(lk1), vec(lq2), vec(lk2), vec(g_col),
            pl.BlockSpec((seq, hw), lambda b, h: (b, h)),
            pl.BlockSpec((seq, hw), lambda b, h: (b, h)),
            pl.BlockSpec((nkv, hw, TK), lambda b, h: (b, h, 0)),
        ],
        out_specs=pl.BlockSpec((seq, hw), lambda b, h: (b, h)),
        out_shape=jax.ShapeDtypeStruct((t, DA_HEADS * DA_V_DIM), BF16),
        scratch_shapes=[
            pltpu.VMEM((ATT_HEADS_PER_STEP, TK, 2 * TQ), F32),
            pltpu.VMEM((ATT_HEADS_PER_STEP, TK, 2 * TQ), F32),
            pltpu.VMEM((ATT_HEADS_PER_STEP, TK, 2 * TQ), BF16),
            pltpu.VMEM((ATT_HEADS_PER_STEP, TK, 2 * TQ), BF16),
            pltpu.VMEM((ATT_HEADS_PER_STEP, DA_V_DIM + ATT_SUM_ROWS, 2 * TQ), F32),
        ],
        compiler_params=_cparams(("parallel", "parallel")),
        name="attention",
    )(lq1, lk1, lq2, lk2, g_col, q, k, vt)


def _post_mix_kernel(o_ref, p_ref, ph_ref, sga_ref, sgp_ref, x_ref, mod_ref,
                     watt_hbm, wpg_hbm, ps_ref, wpb_hbm, wout_hbm, gpm_ref, gpf_ref,
                     wrt_ref, br_ref,
                     x1_ref, prob_ref, pos_ref, cnt_ref, tab_ref, xs_hbm,
                     carry_ref, cur_ref, free_ref, u2t, zeros_v, pos_v, pos_s,
                     watt_ref, wpg_ref, wpb_ref, wout_ref, stage, sem_w,
                     sem_rows, sem_pos, sem_zero,
                     *, steps_per_seq, n_blocks):
    i = pl.program_id(0)
    n_steps = pl.num_programs(0)
    tm = x_ref.shape[0]
    n_experts = wrt_ref.shape[0]
    n_sub = x_ref.shape[1] // LANES
    blk = TM_EXP * n_sub
    per_step = (tm * TOP_K) // TM_EXP
    slot = i % 2
    prev = 1 - slot
    dot = functools.partial(jnp.dot, preferred_element_type=F32)

    def pos_copy(s):
        return pltpu.make_async_copy(pos_v.at[s], pos_s.at[s], sem_pos.at[s])

    def row_copy(s, k, r):
        dst = pos_s[s, k, r]
        return pltpu.make_async_copy(
            u2t.at[s, pl.ds(r * n_sub, n_sub), :],
            xs_hbm.at[pl.ds(pl.multiple_of(dst * n_sub, n_sub), n_sub), :], sem_rows.at[s])

    def wait_rows(s):
        for _ in range(TOP_K):
            pltpu.make_async_copy(u2t.at[s], xs_hbm.at[pl.ds(0, tm * n_sub), :], sem_rows.at[s]).wait()

    def zero_copy(b):
        return pltpu.make_async_copy(zeros_v, xs_hbm.at[pl.ds(pl.multiple_of(b * blk, blk), blk), :],
                                     sem_zero)

    @pl.when(i == 0)
    def _():
        carry_ref[...] = jnp.zeros_like(carry_ref)
        cur_ref[...] = jnp.zeros_like(cur_ref)
        free_ref[...] = jnp.zeros_like(free_ref)
        tab_ref[...] = jnp.zeros_like(tab_ref)
        zeros_v[...] = jnp.zeros_like(zeros_v)
        for src, dst in ((watt_hbm, watt_ref), (wpg_hbm, wpg_ref), (wpb_hbm, wpb_ref),
                         (wout_hbm, wout_ref)):
            rows, cols = src.shape
            step_c = min(cols, WEIGHT_CHUNK)
            windows = [(0, rows, c0, step_c) for c0 in range(0, cols, step_c)]

            def sink(n, tile, dst=dst, windows=windows):
                c0, nc = windows[n][2], windows[n][3]
                dst[:, c0:c0 + nc] = tile.astype(BF16)
            _stream_windows(src, windows, stage, sem_w, sink)
        u2t[1] = jnp.zeros(u2t.shape[1:], F32)
        spare = (n_blocks * TM_EXP
                 + lax.broadcasted_iota(jnp.int32, (TOP_K, tm), 0) * tm
                 + lax.broadcasted_iota(jnp.int32, (TOP_K, tm), 1))
        pos_v[1] = spare
        pos_copy(1).start()
        for b in range(n_experts):
            zero_copy(b).start()
        for b in range(n_experts):
            zero_copy(b).wait()

    for b in range(per_step):
        zero_copy(n_experts + i * per_step + b).start()

    pos_copy(prev).wait()

    stays = pos_s[prev, 0, 0] >= 0

    def dispatch_group(g):
        if g >= DISPATCH_GROUPS:
            return
        rows = tm // DISPATCH_GROUPS

        @pl.when(stays)
        def _():
            for r in range(g * rows, (g + 1) * rows):
                for k in range(TOP_K):
                    row_copy(prev, k, r).start()

    dispatch_group(0)
    y_att = dot(o_ref[...], watt_ref[...])
    dispatch_group(1)

    first = (i % steps_per_seq) == 0
    halo = jnp.where(first, 0.0, ph_ref[...])
    ext = jnp.concatenate([halo, p_ref[...]], axis=0)
    t_in_seq = (i % steps_per_seq) * tm + lax.broadcasted_iota(jnp.int32, (tm, 1), 0)
    pooled = []
    for g, w in enumerate(POOL_WINDOWS):
        e = ext[:, g * POOL_GROUP_DIM:(g + 1) * POOL_GROUP_DIM]
        acc, span = e, 1
        while span < w:
            acc = acc[span:] + acc[:-span]
            span *= 2
        win = acc[POOL_HALO - (w - 1):]
        cnt = jnp.minimum(t_in_seq + 1, w).astype(F32)
        mixed = win / cnt - e[POOL_HALO:]
        pooled.append(dot(mixed.astype(BF16), wpg_ref[g * POOL_GROUP_DIM:(g + 1) * POOL_GROUP_DIM, :]))
    y_pool_in = jnp.concatenate(pooled, axis=1) * ps_ref[...]
    y_pool = dot(y_pool_in.astype(BF16), wpb_ref[...])
    dispatch_group(2)

    merged = sga_ref[...].astype(F32) * y_att + sgp_ref[...].astype(F32) * y_pool
    mix_out = dot(merged.astype(BF16), wout_ref[...])
    gate_m = mod_ref[0, 2:3, :]
    shift_f = mod_ref[0, 3:4, :]
    scale_f = mod_ref[0, 4:5, :]
    x1 = x_ref[...] + gate_m * (_rms(mix_out) * gpm_ref[...])
    x1_ref[...] = x1
    u2 = (_rms(x1) * gpf_ref[...]) * (1.0 + scale_f) + shift_f
    dispatch_group(3)

    logits = _split_dot(wrt_ref[...], u2, (((1,), (1,)), ((), ()))) + br_ref[...]
    erow = lax.broadcasted_iota(jnp.int32, logits.shape, 0)
    work = logits
    vals, idxs = [], []
    for _ in range(TOP_K):
        mx = jnp.max(work, axis=0, keepdims=True)
        ix = jnp.min(jnp.where(work == mx, erow, n_experts), axis=0, keepdims=True)
        vals.append(mx)
        idxs.append(ix)
        work = jnp.where(erow == ix, -jnp.inf, work)
    exps = [jnp.exp(vv - vals[0]) for vv in vals]
    denom = exps[0] + exps[1] + exps[2] + exps[3]
    prob_ref[...] = jnp.concatenate([e / denom for e in exps], axis=0)

    onehot = jnp.zeros(logits.shape, F32)
    for ix in idxs:
        onehot = onehot + (erow == ix).astype(F32)
    rr = lax.broadcasted_iota(jnp.int32, (tm, tm), 0)
    cc = lax.broadcasted_iota(jnp.int32, (tm, tm), 1)
    earlier = (rr < cc).astype(BF16)
    carry = carry_ref[...]
    before = dot(onehot.astype(BF16), earlier) + carry

    blk_rows = float(TM_EXP)
    total = carry + jnp.sum(onehot, axis=1, keepdims=True)
    blocks_old = jnp.floor((carry + (blk_rows - 1.0)) / blk_rows)
    opened = jnp.floor((total + (blk_rows - 1.0)) / blk_rows) - blocks_old
    e_r = lax.broadcasted_iota(jnp.int32, (n_experts, n_experts), 0)
    e_c = lax.broadcasted_iota(jnp.int32, (n_experts, n_experts), 1)
    lower = (e_c < e_r).astype(BF16)
    opened_before = dot(lower, jnp.broadcast_to(opened, (n_experts, LANES)).astype(BF16))[:, 0:1]
    new_blk = free_ref[...] + opened_before
    cur_blk = cur_ref[...]
    boundary = blocks_old * blk_rows
    blk_of = jnp.where(before < boundary, cur_blk, new_blk)
    pos_rows = []
    for ix in idxs:
        pick = erow == ix
        rank = jnp.sum(jnp.where(pick, before, 0.0), axis=0, keepdims=True)
        blk_id = jnp.sum(jnp.where(pick, blk_of, 0.0), axis=0, keepdims=True)
        within = rank - jnp.floor(rank / blk_rows) * blk_rows
        pos_rows.append((blk_id * blk_rows + within).astype(jnp.int32))
    pos = jnp.concatenate(pos_rows, axis=0)
    pos_ref[...] = pos
    tab_col = lax.broadcasted_iota(jnp.int32, tab_ref.shape, 1).astype(F32)
    tab_ref[...] = jnp.where(jnp.logical_and(tab_col == blocks_old, opened > 0.0),
                             new_blk.astype(jnp.int32), tab_ref[...])
    cur_ref[...] = jnp.where(opened > 0.0, new_blk, cur_blk)
    free_ref[...] = free_ref[...] + jnp.sum(opened, axis=0, keepdims=True)
    carry_ref[...] = total
    cnt_ref[...] = total.astype(jnp.int32)

    @pl.when(i > 0)
    def _():
        wait_rows(slot)

    for s in range(n_sub):
        u2t[slot, pl.ds(s, tm, stride=n_sub), :] = u2[:, s * LANES:(s + 1) * LANES]
    pos_v[slot] = pos
    pos_copy(slot).start()
    for b in range(per_step):
        zero_copy(n_experts + i * per_step + b).wait()

    @pl.when(i == n_steps - 1)
    def _():
        pos_copy(slot).wait()

        def issue(r, c):
            for k in range(TOP_K):
                row_copy(slot, k, r).start()
            return c
        lax.fori_loop(0, tm, issue, 0)
        wait_rows(prev)
        wait_rows(slot)


def _post_mix(o, p, sga, sgp, x2, mod3, w_att, w_pg, pool_scale, w_pb, w_out,
              g_post_mix, g_pre_ffn, w_router_t, b_router_col, seq):
    t, d = x2.shape
    tm = TM_MIX
    steps_per_seq = seq // tm
    n_experts = w_router_t.shape[0]
    pw = p.shape[1]
    n_sub = d // LANES
    halo_blocks = tm // POOL_HALO
    row = lambda w: pl.BlockSpec((tm, w), lambda i: (i, 0))
    col = lambda h: pl.BlockSpec((h, tm), lambda i: (0, i))
    full2 = lambda a: pl.BlockSpec(a.shape, lambda i: (0, 0))
    hbm = pl.BlockSpec(memory_space=pl.ANY)
    assert tm <= TM_EXP and (tm * TOP_K) % TM_EXP == 0 and t % TM_EXP == 0
    n_blocks = (t * TOP_K) // TM_EXP + n_experts
    tab_w = -(-(t // TM_EXP) // LANES) * LANES
    kern = functools.partial(_post_mix_kernel, steps_per_seq=steps_per_seq, n_blocks=n_blocks)
    xs_rows = n_blocks * TM_EXP + TOP_K * tm
    return pl.pallas_call(
        kern,
        grid=(t // tm,),
        in_specs=[
            row(d), row(pw),
            pl.BlockSpec((POOL_HALO, pw), lambda i: (jnp.maximum(i * halo_blocks - 1, 0), 0)),
            row(d), row(d), row(d),
            pl.BlockSpec((1, N_MOD, d), lambda i: (i // steps_per_seq, 0, 0)),
            hbm, hbm, full2(pool_scale), hbm, hbm, full2(g_post_mix), full2(g_pre_ffn),
            full2(w_router_t), full2(b_router_col),
        ],
        out_specs=[
            row(d), col(TOP_K), col(TOP_K),
            pl.BlockSpec((n_experts, 1), lambda i: (0, 0)),
            pl.BlockSpec((n_experts, tab_w), lambda i: (0, 0)),
            pl.BlockSpec(memory_space=pl.ANY),
        ],
        out_shape=[
            jax.ShapeDtypeStruct((t, d), F32),
            jax.ShapeDtypeStruct((TOP_K, t), F32),
            jax.ShapeDtypeStruct((TOP_K, t), jnp.int32),
            jax.ShapeDtypeStruct((n_experts, 1), jnp.int32),
            jax.ShapeDtypeStruct((n_experts, tab_w), jnp.int32),
            jax.ShapeDtypeStruct((xs_rows * n_sub, LANES), F32),
        ],
        scratch_shapes=[
            pltpu.VMEM((n_experts, 1), F32),
            pltpu.VMEM((n_experts, 1), F32),
            pltpu.VMEM((1, 1), F32),
            pltpu.VMEM((2, tm * n_sub, LANES), F32),
            pltpu.VMEM((TM_EXP * n_sub, LANES), F32),
            pltpu.VMEM((2, TOP_K, tm), jnp.int32),
            pltpu.SMEM((2, TOP_K, tm), jnp.int32),
            pltpu.VMEM(w_att.shape, BF16),
            pltpu.VMEM(w_pg.shape, BF16),
            pltpu.VMEM(w_pb.shape, BF16),
            pltpu.VMEM(w_out.shape, BF16),
            pltpu.VMEM((2, max(w_att.shape[0], w_out.shape[0]), WEIGHT_CHUNK), F32),
            pltpu.SemaphoreType.DMA((2,)),
            pltpu.SemaphoreType.DMA((2,)),
            pltpu.SemaphoreType.DMA((2,)),
            pltpu.SemaphoreType.DMA,
        ],
        compiler_params=_cparams(("arbitrary",)),
        name="post_mix",
    )(o, p, p, sga, sgp, x2, mod3, w_att, w_pg, pool_scale, w_pb, w_out,
      g_post_mix, g_pre_ffn, w_router_t, b_router_col)


def _experts_kernel(te_ref, blk_ref, nt_ref, xs_ref, wgu_ref, bgu_ref, wd_ref, bd_ref,
                    y_ref, wgu_bf, wd_bf, *, n_sub):
    j = pl.program_id(0)
    tm = TM_EXP
    d_ff = wd_ref.shape[1]

    @pl.when(j < nt_ref[0])
    def _():
        changed = jnp.logical_or(j == 0, te_ref[j] != te_ref[jnp.maximum(j - 1, 0)])

        @pl.when(changed)
        def _():
            wgu_bf[...] = wgu_ref[0].astype(BF16)
            wd_bf[...] = wd_ref[0].astype(BF16)

        xs = jnp.concatenate(
            [xs_ref[pl.ds(s, tm, stride=n_sub), :] for s in range(n_sub)], axis=1).astype(BF16)
        gu = jnp.dot(xs, wgu_bf[...], preferred_element_type=F32) + bgu_ref[0]
        gate = jnp.minimum(gu[:, :d_ff], SWIGLU_LIMIT)
        up = jnp.clip(gu[:, d_ff:], -SWIGLU_LIMIT, SWIGLU_LIMIT)
        act = (up + 1.0) * (gate * jax.nn.sigmoid(SWIGLU_ALPHA * gate))
        y = jnp.dot(act.astype(BF16), wd_bf[...], preferred_element_type=F32) + bd_ref[0]
        for s in range(n_sub):
            y_ref[pl.ds(s, tm, stride=n_sub), :] = y[:, s * LANES:(s + 1) * LANES]

    @pl.when(j >= nt_ref[0])
    def _():
        y_ref[...] = jnp.zeros_like(y_ref)


def _experts(tile_expert, tile_block, n_tiles, xs_tiles, w_gate_up, b_gate_up, w_down, b_down):
    n_exp, d, d_gu = w_gate_up.shape
    d_ff = w_down.shape[1]
    n_sub = d // LANES
    tm = TM_EXP
    max_tiles = tile_expert.shape[0]
    by_expert = lambda j, te, blk, nt: (te[j], 0, 0)
    by_block = lambda j, te, blk, nt: (blk[j], 0)
    grid_spec = pltpu.PrefetchScalarGridSpec(
        num_scalar_prefetch=3,
        grid=(max_tiles,),
        in_specs=[
            pl.BlockSpec((tm * n_sub, LANES), by_block),
            pl.BlockSpec((1, d, d_gu), by_expert),
            pl.BlockSpec((1, 1, d_gu), by_expert),
            pl.BlockSpec((1, d_ff, d), by_expert),
            pl.BlockSpec((1, 1, d), by_expert),
        ],
        out_specs=pl.BlockSpec((tm * n_sub, LANES), by_block),
        scratch_shapes=[
            pltpu.VMEM((d, d_gu), BF16),
            pltpu.VMEM((d_ff, d), BF16),
        ],
    )
    return pl.pallas_call(
        functools.partial(_experts_kernel, n_sub=n_sub),
        grid_spec=grid_spec,
        out_shape=jax.ShapeDtypeStruct((max_tiles * tm * n_sub, LANES), F32),
        compiler_params=_cparams(("arbitrary",)),
        name="experts",
    )(tile_expert, tile_block, n_tiles, xs_tiles,
      w_gate_up, b_gate_up.reshape(n_exp, 1, d_gu), w_down, b_down.reshape(n_exp, 1, d))


def _combine_kernel(pos_ref, y_hbm, prob_ref, x1_ref, mod_ref, g_ref, o_ref, ybuf, sem,
                    *, n_sub, n_tokens):
    i = pl.program_id(0)
    n_steps = pl.num_programs(0)
    tm = TM_COMB
    slot = i % 2

    def gather(step, s):
        for r in range(tm):
            for k in range(TOP_K):
                row = pos_ref[k * n_tokens + step * tm + r]
                pltpu.make_async_copy(
                    y_hbm.at[pl.ds(pl.multiple_of(row * n_sub, n_sub), n_sub), :],
                    ybuf.at[s, k, pl.ds(r * n_sub, n_sub), :], sem.at[s]).start()

    @pl.when(i == 0)
    def _():
        gather(0, 0)

    @pl.when(i + 1 < n_steps)
    def _():
        gather(i + 1, 1 - slot)

    for k in range(TOP_K):
        pltpu.make_async_copy(y_hbm.at[pl.ds(0, tm * n_sub), :], ybuf.at[slot, k], sem.at[slot]).wait()

    prob = prob_ref[...]
    f = None
    for k in range(TOP_K):
        yk = jnp.concatenate(
            [ybuf[slot, k, pl.ds(s, tm, stride=n_sub), :] for s in range(n_sub)], axis=1)
        term = yk * prob[:, k:k + 1]
        f = term if f is None else f + term
    gate_f = mod_ref[0, 5:6, :]
    o_ref[...] = x1_ref[...] + gate_f * (_rms(f) * g_ref[...])


def _combine(pos_flat, y_tiles, probs, x1, mod3, g_post_ffn, seq):
    t, d = x1.shape
    tm = TM_COMB
    n_sub = d // LANES
    steps_per_seq = seq // tm
    grid_spec = pltpu.PrefetchScalarGridSpec(
        num_scalar_prefetch=1,
        grid=(t // tm,),
        in_specs=[
            pl.BlockSpec(memory_space=pl.ANY),
            pl.BlockSpec((tm, TOP_K), lambda i, pos: (i, 0)),
            pl.BlockSpec((tm, d), lambda i, pos: (i, 0)),
            pl.BlockSpec((1, N_MOD, d), lambda i, pos: (i // steps_per_seq, 0, 0)),
            pl.BlockSpec((1, d), lambda i, pos: (0, 0)),
        ],
        out_specs=pl.BlockSpec((tm, d), lambda i, pos: (i, 0)),
        scratch_shapes=[
            pltpu.VMEM((2, TOP_K, tm * n_sub, LANES), F32),
            pltpu.SemaphoreType.DMA((2,)),
        ],
    )
    return pl.pallas_call(
        functools.partial(_combine_kernel, n_sub=n_sub, n_tokens=t),
        grid_spec=grid_spec,
        out_shape=jax.ShapeDtypeStruct((t, d), F32),
        compiler_params=_cparams(("arbitrary",)),
        name="combine",
    )(pos_flat, y_tiles, probs, x1, mod3, g_post_ffn)


def _rope_freq_row():
    half = ROPE_DIM // 2
    inv_freq = ROPE_THETA ** (-jnp.arange(0, ROPE_DIM, 2, dtype=F32) / ROPE_DIM)
    head = jnp.concatenate([inv_freq, inv_freq, jnp.zeros((DA_HEAD_DIM - 2 * half,), F32)])
    return jnp.tile(head, LANES // DA_HEAD_DIM).reshape(1, LANES)


def _tile_tables(counts, block_tab, n_tokens):
    tm = TM_EXP
    n_exp = counts.shape[0]
    max_tiles = (n_tokens * TOP_K) // tm + n_exp
    tiles = (counts + tm - 1) // tm
    tile_end = jnp.cumsum(tiles)
    n_tiles = tile_end[-1]
    j = jnp.arange(max_tiles, dtype=jnp.int32)
    jj = jnp.minimum(j, n_tiles - 1)
    tile_expert = jnp.sum(jj[:, None] >= tile_end[None, :], axis=1).astype(jnp.int32)
    nth = jj - (tile_end - tiles)[tile_expert]
    tile_block = jnp.where(j < n_tiles, block_tab[tile_expert, nth], j)
    return tile_expert, tile_block.astype(jnp.int32), n_tiles.reshape(1).astype(jnp.int32)


def kernel(x, c, positions, w_mod, b_mod, g_pre_mix, w_in, lambda_q1, lambda_k1, lambda_q2, lambda_k2, g_sub, w_pool_group, pool_scale, w_att_branch, w_pool_branch, w_out, g_post_mix, g_pre_ffn, w_router, b_router, w_gate_up, b_gate_up, w_down, b_down, g_post_ffn):
    bsz, seq, d = x.shape
    assert w_mod.shape[0] == 1, "single layer"
    assert seq % TQ == 0 and seq % TM_PROJ == 0 and seq % TM_MIX == 0 and seq % TM_COMB == 0
    t = bsz * seq
    assert t % TM_EXP == 0
    x2 = x.reshape(t, d)
    qk_w = 2 * DA_HEADS * DA_HEAD_DIM
    v_w = DA_HEADS * DA_V_DIM
    pool_w = len(POOL_WINDOWS) * POOL_GROUP_DIM
    n_experts = w_router.shape[2]

    mod3 = _mod(c, w_mod[0], b_mod[0]).reshape(bsz, N_MOD, d)

    bounds = [0, qk_w, 2 * qk_w, 2 * qk_w + v_w, 2 * qk_w + v_w + pool_w,
              2 * qk_w + v_w + pool_w + d, 2 * qk_w + v_w + pool_w + 2 * d]
    q, k, vt, p, sga, sgp = _in_proj(x2, mod3, g_pre_mix, positions.reshape(t, 1), _rope_freq_row(),
                                     w_in[0], bounds, seq)

    row64 = lambda a: a.reshape(1, DA_HEAD_DIM)
    o = _attention(q, k, vt, row64(lambda_q1[0]), row64(lambda_k1[0]), row64(lambda_q2[0]),
                   row64(lambda_k2[0]), g_sub.reshape(DA_V_DIM, 1), bsz, seq)

    x1, probs_t, pos_t, counts, block_tab, xs_tiles = _post_mix(
        o, p, sga, sgp, x2, mod3, w_att_branch[0], w_pool_group[0].reshape(pool_w, POOL_GROUP_DIM),
        pool_scale, w_pool_branch[0], w_out[0],
        g_post_mix, g_pre_ffn, w_router[0].T, b_router.reshape(n_experts, 1), seq)

    tile_expert, tile_block, n_tiles = _tile_tables(counts[:, 0], block_tab, t)
    y_tiles = _experts(tile_expert, tile_block, n_tiles, xs_tiles,
                       w_gate_up[0], b_gate_up[0], w_down[0], b_down[0])
    out = _combine(pos_t.reshape(-1), y_tiles, probs_t.T, x1, mod3, g_post_ffn, seq)
    return out.reshape(bsz, seq, d)
```

```python
import functools

import jax
import jax.numpy as jnp
from jax import lax
from jax.experimental import pallas as pl
from jax.experimental.pallas import tpu as pltpu

F32 = jnp.float32
BF16 = jnp.bfloat16

NORM_EPS = 1e-6
CHUNK = 64
DA_HEADS = 8
DA_HEAD_DIM = 64
DA_V_DIM = 2 * DA_HEAD_DIM
ROPE_THETA = 500000.0
ROPE_DIM = DA_HEAD_DIM // 4
POOL_WINDOWS = (2, 4, 8, 16)
POOL_GROUP_DIM = 128
TOP_K = 4
SWIGLU_LIMIT = 7.0
SWIGLU_ALPHA = 1.702
N_MOD = 6
LAMBDA_INIT = 0.8 - 0.6 * 1.0
Q_SCALE = (DA_HEAD_DIM ** -0.5) * 1.4426950408889634

LANES = 128
SUBLANES = 8
VMEM_LIMIT = 56 * 1024 * 1024

TM_PROJ = 256
TQ = 512
TK = 256
ATT_HEADS_PER_STEP = 1
ATT_SUM_ROWS = 16
TM_MIX = 256
TM_EXP = 512
TM_COMB = 256
WEIGHT_CHUNK = 512
POOL_HALO = 16
NEG_BIG = -1e30


def _cparams(sem):
    return pltpu.CompilerParams(dimension_semantics=sem, vmem_limit_bytes=VMEM_LIMIT)


def _split(a):
    hi = a.astype(BF16)
    return hi, (a - hi.astype(F32)).astype(BF16)


def _split_dot(a, b, dims=(((1,), (0,)), ((), ()))):
    a_hi, a_lo = _split(a)
    b_hi, b_lo = _split(b)
    dot = lambda x, y: lax.dot_general(x, y, dims, preferred_element_type=F32)
    return dot(a_hi, b_hi) + (dot(a_hi, b_lo) + dot(a_lo, b_hi))


def _rms(x):
    return x * lax.rsqrt(jnp.mean(x * x, axis=-1, keepdims=True) + NORM_EPS)


def _mod_kernel(c_ref, w_ref, b_ref, o_ref):
    c = c_ref[...]
    c_act = c * jax.nn.sigmoid(c)
    o_ref[...] = _split_dot(c_act, w_ref[...]) + b_ref[...]


def _mod(c, w_mod, b_mod):
    bsz, d = c.shape
    n = w_mod.shape[1]
    tn = 1024
    return pl.pallas_call(
        _mod_kernel,
        grid=(n // tn,),
        in_specs=[
            pl.BlockSpec((bsz, d), lambda j: (0, 0)),
            pl.BlockSpec((d, tn), lambda j: (0, j)),
            pl.BlockSpec((1, tn), lambda j: (0, j)),
        ],
        out_specs=pl.BlockSpec((bsz, tn), lambda j: (0, j)),
        out_shape=jax.ShapeDtypeStruct((bsz, n), F32),
        compiler_params=_cparams(("parallel",)),
        name="mod",
    )(c, w_mod, b_mod.reshape(1, n))


def _rope(t, cos_t, sin_a, sin_b):
    n = t.shape[1]
    up = pltpu.roll(t, n - ROPE_DIM // 2, axis=1)
    dn = pltpu.roll(t, ROPE_DIM // 2, axis=1)
    reps = n // LANES
    tile = lambda a: jnp.concatenate([a] * reps, axis=1)
    return t * tile(cos_t) + up * tile(sin_a) + dn * tile(sin_b)


def _stream_windows(src_hbm, windows, stage, sem, sink):
    def copy(n):
        r0, nr, c0, nc = windows[n]
        return pltpu.make_async_copy(src_hbm.at[pl.ds(r0, nr), pl.ds(c0, nc)],
                                     stage.at[n % 2, pl.ds(0, nr), pl.ds(0, nc)], sem.at[n % 2])
    copy(0).start()
    for n, (_, nr, _, nc) in enumerate(windows):
        if n + 1 < len(windows):
            copy(n + 1).start()
        copy(n).wait()
        sink(n, stage[n % 2, 0:nr, 0:nc])


def _in_proj_kernel(x_ref, mod_ref, g_ref, pos_ref, freq_ref, w_hbm,
                    q_ref, k_ref, v_ref, p_ref, sga_ref, sgp_ref,
                    w_ref, wvt_ref, stage, sem, *, bounds):
    @pl.when(pl.program_id(0) == 0)
    def _():
        d_in, width = w_hbm.shape
        windows = [(0, d_in, c0, WEIGHT_CHUNK) for c0 in range(0, width, WEIGHT_CHUNK)]

        def sink(n, tile):
            c0 = windows[n][2]
            w_ref[:, c0:c0 + WEIGHT_CHUNK] = tile.astype(BF16)
            if bounds[2] <= c0 < bounds[3]:
                wvt_ref[c0 - bounds[2]:c0 - bounds[2] + WEIGHT_CHUNK, :] = tile.T.astype(BF16)
        _stream_windows(w_hbm, windows, stage, sem, sink)

    x = x_ref[...]
    shift = mod_ref[0, 0:1, :]
    scale = mod_ref[0, 1:2, :]
    u = (_rms(x) * g_ref[...]) * (1.0 + scale) + shift
    ub = u.astype(BF16)
    dot = functools.partial(jnp.dot, preferred_element_type=F32)
    part = lambda n: w_ref[:, bounds[n]:bounds[n + 1]]

    ang = pos_ref[...].astype(F32) * freq_ref[...]
    cos_t, sn = jnp.cos(ang), jnp.sin(ang)
    in_head = lax.broadcasted_iota(jnp.int32, ang.shape, 1) % DA_HEAD_DIM
    sin_a = jnp.where(in_head < ROPE_DIM // 2, -sn, 0.0)
    sin_b = jnp.where(in_head >= ROPE_DIM // 2, sn, 0.0)

    q = _rope(dot(ub, part(0)), cos_t, sin_a, sin_b)
    q_ref[...] = (q * Q_SCALE).astype(BF16)
    k = _rope(dot(ub, part(1)), cos_t, sin_a, sin_b)
    k_ref[...] = k.astype(BF16)
    vt = lax.dot_general(wvt_ref[...], ub, (((1,), (1,)), ((), ())), preferred_element_type=F32)
    v_ref[0] = vt.astype(BF16)
    p_ref[...] = dot(ub, part(3))
    sga_ref[...] = jax.nn.sigmoid(dot(ub, part(4))).astype(BF16)
    sgp_ref[...] = jax.nn.sigmoid(dot(ub, part(5))).astype(BF16)


def _in_proj(x2, mod3, g_pre, pos_col, freq_row, w_in, bounds, seq):
    t, d = x2.shape
    width = w_in.shape[1]
    assert width % WEIGHT_CHUNK == 0 and all(b % WEIGHT_CHUNK == 0 for b in bounds[2:4])
    tm = TM_PROJ
    assert tm == TK, "v is emitted as one transposed (channels, TK) slab per step"
    steps_per_seq = seq // tm
    widths = [b - a for a, b in zip(bounds[:-1], bounds[1:])]
    row = lambda w: pl.BlockSpec((tm, w), lambda i: (i, 0))
    full = lambda a: pl.BlockSpec(a.shape, lambda i: (0, 0))
    out_specs = [row(widths[0]), row(widths[1]),
                 pl.BlockSpec((1, widths[2], tm), lambda i: (i, 0, 0)),
                 row(widths[3]), row(widths[4]), row(widths[5])]
    out_shape = [jax.ShapeDtypeStruct((t, widths[0]), BF16),
                 jax.ShapeDtypeStruct((t, widths[1]), BF16),
                 jax.ShapeDtypeStruct((t // tm, widths[2], tm), BF16),
                 jax.ShapeDtypeStruct((t, widths[3]), F32),
                 jax.ShapeDtypeStruct((t, widths[4]), BF16),
                 jax.ShapeDtypeStruct((t, widths[5]), BF16)]
    return pl.pallas_call(
        functools.partial(_in_proj_kernel, bounds=tuple(bounds)),
        grid=(t // tm,),
        in_specs=[
            row(d),
            pl.BlockSpec((1, N_MOD, d), lambda i: (i // steps_per_seq, 0, 0)),
            full(g_pre), row(1), full(freq_row), pl.BlockSpec(memory_space=pl.ANY),
        ],
        out_specs=out_specs,
        out_shape=out_shape,
        scratch_shapes=[
            pltpu.VMEM((d, width), BF16),
            pltpu.VMEM((widths[2], d), BF16),
            pltpu.VMEM((2, d, WEIGHT_CHUNK), F32),
            pltpu.SemaphoreType.DMA((2,)),
        ],
        compiler_params=_cparams(("arbitrary",)),
        name="in_proj",
    )(x2, mod3, g_pre, pos_col, freq_row, w_in)


def _attn_kernel(lq1_ref, lk1_ref, lq2_ref, lk2_ref, g_ref, q_ref, k_ref, vt_ref, o_ref,
                 s_a, s_b, p_a, p_b, acc_buf):
    assert TQ % TK == 0 and TQ // TK in (1, 2)
    seq = q_ref.shape[0]
    lam = (jnp.exp(jnp.sum(lq1_ref[...] * lk1_ref[...], axis=-1, keepdims=True))
           - jnp.exp(jnp.sum(lq2_ref[...] * lk2_ref[...], axis=-1, keepdims=True))
           + LAMBDA_INIT)
    dot = functools.partial(jnp.dot, preferred_element_type=F32)
    heads = range(ATT_HEADS_PER_STEP)
    lanes = lambda h: slice(h * LANES, (h + 1) * LANES)

    def q_tile(qi, c):
        q_rows = pl.ds(pl.multiple_of(qi * TQ, TQ), TQ)
        qqt = []
        for h in heads:
            qt = q_ref[q_rows, lanes(h)].astype(F32).T
            row = lax.broadcasted_iota(jnp.int32, qt.shape, 0)
            zero = jnp.zeros_like(qt)
            qqt.append(jnp.concatenate([jnp.where(row < DA_HEAD_DIM, qt, zero),
                                        jnp.where(row >= DA_HEAD_DIM, qt, zero)],
                                       axis=1).astype(BF16))

        def scores(j, h):
            return dot(k_ref[pl.ds(pl.multiple_of(j * TK, TK), TK), lanes(h)], qqt[h])

        def softmax_step(s, m):
            m_new = jnp.maximum(m, jnp.max(s, axis=0, keepdims=True))
            return m_new, jnp.exp2(m - m_new), jnp.exp2(s - m_new).astype(BF16)

        ones_rows = jnp.ones((ATT_SUM_ROWS, TK), BF16)

        def pv(j, h, p):
            return dot(jnp.concatenate([vt_ref[j, lanes(h), :], ones_rows], axis=0), p)

        n_kv = vt_ref.shape[0]
        kk = lax.broadcasted_iota(jnp.int32, (TK, 2 * TQ), 0)
        qq = lax.broadcasted_iota(jnp.int32, (TK, 2 * TQ), 1)
        rel_chunk = jnp.where(qq >= TQ, qq - TQ, qq) // CHUNK - kk // CHUNK

        def masked(s, j):
            return jnp.where(rel_chunk >= j * (TK // CHUNK) - qi * (TQ // CHUNK), s, NEG_BIG)

        def step(j, carries, s_cur, p_cur, s_nxt, p_prev, mask):
            pend = [pv(jnp.maximum(j - 1, 0), h, p_prev[h]) for h in heads]
            if s_nxt is not None:
                for h in heads:
                    s_nxt[h] = scores(jnp.minimum(j + 1, n_kv - 1), h)
            out = []
            for h in heads:
                m, alpha = carries[h]
                acc_buf[h] = alpha * acc_buf[h] + pend[h]
                s = s_cur[h]
                m, alpha, p = softmax_step(masked(s, j) if mask else s, m)
                p_cur[h] = p
                out.append((m, alpha))
            return tuple(out)

        def pair(i, carries):
            carries = step(2 * i, carries, s_a, p_a, s_b, p_b, False)
            return step(2 * i + 1, carries, s_b, p_b, s_a, p_a, False)

        for h in heads:
            s_a[h] = scores(0, h)
            p_b[h] = jnp.zeros((TK, 2 * TQ), BF16)
            acc_buf[h] = jnp.zeros(acc_buf.shape[1:], F32)
        init = tuple((jnp.full((1, 2 * TQ), NEG_BIG, F32), jnp.ones((1, 2 * TQ), F32))
                     for _ in heads)
        n_pairs = (qi * (TQ // TK)) // 2
        carries = lax.fori_loop(0, n_pairs, pair, init)
        ja = 2 * n_pairs
        carries = step(ja, carries, s_a, p_a, s_b, p_b, True)
        carries = step(ja + 1, carries, s_b, p_b, None, p_a, True)
        for h in heads:
            _, alpha = carries[h]
            acc = alpha * acc_buf[h] + pv(jnp.minimum(ja + 1, n_kv - 1), h, p_b[h])
            o = acc[:DA_V_DIM] / acc[DA_V_DIM:DA_V_DIM + 1]
            a = o[:, :TQ] - lam * o[:, TQ:]
            y = a * lax.rsqrt(jnp.mean(a * a, axis=0, keepdims=True) + NORM_EPS)
            y = (y * g_ref[...]) * (1.0 - LAMBDA_INIT)
            o_ref[q_rows, lanes(h)] = y.T.astype(BF16)
        return c

    lax.fori_loop(0, seq // TQ, q_tile, 0)


def _attention(q, k, vt, lq1, lk1, lq2, lk2, g_col, bsz, seq):
    t = q.shape[0]
    nkv = seq // TK
    hw = ATT_HEADS_PER_STEP * LANES
    vec = lambda a: pl.BlockSpec(a.shape, lambda b, h: (0, 0))
    return pl.pallas_call(
        _attn_kernel,
        grid=(bsz, DA_HEADS // ATT_HEADS_PER_STEP),
        in_specs=[
            vec(lq1), vec(lk1), vec(lq2), vec(lk2), vec(g_col),
            pl.BlockSpec((seq, hw), lambda b, h: (b, h)),
            pl.BlockSpec((seq, hw), lambda b, h: (b, h)),
            pl.BlockSpec((nkv, hw, TK), lambda b, h: (b, h, 0)),
        ],
        out_specs=pl.BlockSpec((seq, hw), lambda b, h: (b, h)),
        out_shape=jax.ShapeDtypeStruct((t, DA_HEADS * DA_V_DIM), BF16),
        scratch_shapes=[
            pltpu.VMEM((ATT_HEADS_PER_STEP, TK, 2 * TQ), F32),
            pltpu.VMEM((ATT_HEADS_PER_STEP, TK, 2 * TQ), F32),
            pltpu.VMEM((ATT_HEADS_PER_STEP, TK, 2 * TQ), BF16),
            pltpu.VMEM((ATT_HEADS_PER_STEP, TK, 2 * TQ), BF16),
            pltpu.VMEM((ATT_HEADS_PER_STEP, DA_V_DIM + ATT_SUM_ROWS, 2 * TQ), F32),
        ],
        compiler_params=_cparams(("parallel", "parallel")),
        name="attention",
    )(lq1, lk1, lq2, lk2, g_col, q, k, vt)


def _post_mix_kernel(o_ref, p_ref, ph_ref, sga_ref, sgp_ref, x_ref, mod_ref,
                     watt_hbm, wpg_hbm, ps_ref, wpb_hbm, wout_hbm, gpm_ref, gpf_ref,
                     wrt_ref, br_ref,
                     x1_ref, prob_ref, pos_ref, cnt_ref, tab_ref, xs_hbm,
                     carry_ref, cur_ref, free_ref, u2t, zeros_v, pos_v, pos_s,
                     watt_ref, wpg_ref, wpb_ref, wout_ref, stage, sem_w,
                     sem_rows, sem_pos, sem_zero,
                     *, steps_per_seq, n_blocks):
    i = pl.program_id(0)
    n_steps = pl.num_programs(0)
    tm = x_ref.shape[0]
    n_experts = wrt_ref.shape[0]
    n_sub = x_ref.shape[1] // LANES
    blk = TM_EXP * n_sub
    per_step = (tm * TOP_K) // TM_EXP
    slot = i % 2
    prev = 1 - slot
    dot = functools.partial(jnp.dot, preferred_element_type=F32)

    def pos_copy(s):
        return pltpu.make_async_copy(pos_v.at[s], pos_s.at[s], sem_pos.at[s])

    def row_copy(s, k, r):
        dst = pos_s[s, k, r]
        return pltpu.make_async_copy(
            u2t.at[s, pl.ds(r * n_sub, n_sub), :],
            xs_hbm.at[pl.ds(pl.multiple_of(dst * n_sub, n_sub), n_sub), :], sem_rows.at[s])

    def wait_rows(s):
        for _ in range(TOP_K):
            pltpu.make_async_copy(u2t.at[s], xs_hbm.at[pl.ds(0, tm * n_sub), :], sem_rows.at[s]).wait()

    def zero_copy(b):
        return pltpu.make_async_copy(zeros_v, xs_hbm.at[pl.ds(pl.multiple_of(b * blk, blk), blk), :],
                                     sem_zero)

    @pl.when(i == 0)
    def _():
        carry_ref[...] = jnp.zeros_like(carry_ref)
        cur_ref[...] = jnp.zeros_like(cur_ref)
        free_ref[...] = jnp.zeros_like(free_ref)
        tab_ref[...] = jnp.zeros_like(tab_ref)
        zeros_v[...] = jnp.zeros_like(zeros_v)
        for src, dst in ((watt_hbm, watt_ref), (wpg_hbm, wpg_ref), (wpb_hbm, wpb_ref),
                         (wout_hbm, wout_ref)):
            rows, cols = src.shape
            step_c = min(cols, WEIGHT_CHUNK)
            windows = [(0, rows, c0, step_c) for c0 in range(0, cols, step_c)]

            def sink(n, tile, dst=dst, windows=windows):
                c0, nc = windows[n][2], windows[n][3]
                dst[:, c0:c0 + nc] = tile.astype(BF16)
            _stream_windows(src, windows, stage, sem_w, sink)
        u2t[1] = jnp.zeros(u2t.shape[1:], F32)
        spare = (n_blocks * TM_EXP
                 + lax.broadcasted_iota(jnp.int32, (TOP_K, tm), 0) * tm
                 + lax.broadcasted_iota(jnp.int32, (TOP_K, tm), 1))
        pos_v[1] = spare
        pos_copy(1).start()
        for b in range(n_experts):
            zero_copy(b).start()
        for b in range(n_experts):
            zero_copy(b).wait()

    for b in range(per_step):
        zero_copy(n_experts + i * per_step + b).start()

    pos_copy(prev).wait()

    for r in range(tm):
        for k in range(TOP_K):
            row_copy(prev, k, r).start()

    y_att = dot(o_ref[...], watt_ref[...])

    first = (i % steps_per_seq) == 0
    halo = jnp.where(first, 0.0, ph_ref[...])
    ext = jnp.concatenate([halo, p_ref[...]], axis=0)
    t_in_seq = (i % steps_per_seq) * tm + lax.broadcasted_iota(jnp.int32, (tm, 1), 0)
    pooled = []
    for g, w in enumerate(POOL_WINDOWS):
        e = ext[:, g * POOL_GROUP_DIM:(g + 1) * POOL_GROUP_DIM]
        acc, span = e, 1
        while span < w:
            acc = acc[span:] + acc[:-span]
            span *= 2
        win = acc[POOL_HALO - (w - 1):]
        cnt = jnp.minimum(t_in_seq + 1, w).astype(F32)
        mixed = win / cnt - e[POOL_HALO:]
        pooled.append(dot(mixed.astype(BF16), wpg_ref[g * POOL_GROUP_DIM:(g + 1) * POOL_GROUP_DIM, :]))
    y_pool_in = jnp.concatenate(pooled, axis=1) * ps_ref[...]
    y_pool = dot(y_pool_in.astype(BF16), wpb_ref[...])

    merged = sga_ref[...].astype(F32) * y_att + sgp_ref[...].astype(F32) * y_pool
    mix_out = dot(merged.astype(BF16), wout_ref[...])
    gate_m = mod_ref[0, 2:3, :]
    shift_f = mod_ref[0, 3:4, :]
    scale_f = mod_ref[0, 4:5, :]
    x1 = x_ref[...] + gate_m * (_rms(mix_out) * gpm_ref[...])
    x1_ref[...] = x1
    u2 = (_rms(x1) * gpf_ref[...]) * (1.0 + scale_f) + shift_f

    logits = _split_dot(wrt_ref[...], u2, (((1,), (1,)), ((), ()))) + br_ref[...]
    erow = lax.broadcasted_iota(jnp.int32, logits.shape, 0)
    work = logits
    vals, idxs = [], []
    for _ in range(TOP_K):
        mx = jnp.max(work, axis=0, keepdims=True)
        ix = jnp.min(jnp.where(work == mx, erow, n_experts), axis=0, keepdims=True)
        vals.append(mx)
        idxs.append(ix)
        work = jnp.where(erow == ix, -jnp.inf, work)
    exps = [jnp.exp(vv - vals[0]) for vv in vals]
    denom = exps[0] + exps[1] + exps[2] + exps[3]
    prob_ref[...] = jnp.concatenate([e / denom for e in exps], axis=0)

    onehot = jnp.zeros(logits.shape, F32)
    for ix in idxs:
        onehot = onehot + (erow == ix).astype(F32)
    rr = lax.broadcasted_iota(jnp.int32, (tm, tm), 0)
    cc = lax.broadcasted_iota(jnp.int32, (tm, tm), 1)
    earlier = (rr < cc).astype(BF16)
    carry = carry_ref[...]
    before = dot(onehot.astype(BF16), earlier) + carry

    blk_rows = float(TM_EXP)
    total = carry + jnp.sum(onehot, axis=1, keepdims=True)
    blocks_old = jnp.floor((carry + (blk_rows - 1.0)) / blk_rows)
    opened = jnp.floor((total + (blk_rows - 1.0)) / blk_rows) - blocks_old
    e_r = lax.broadcasted_iota(jnp.int32, (n_experts, n_experts), 0)
    e_c = lax.broadcasted_iota(jnp.int32, (n_experts, n_experts), 1)
    lower = (e_c < e_r).astype(BF16)
    opened_before = dot(lower, jnp.broadcast_to(opened, (n_experts, LANES)).astype(BF16))[:, 0:1]
    new_blk = free_ref[...] + opened_before
    cur_blk = cur_ref[...]
    boundary = blocks_old * blk_rows
    blk_of = jnp.where(before < boundary, cur_blk, new_blk)
    pos_rows = []
    for ix in idxs:
        pick = erow == ix
        rank = jnp.sum(jnp.where(pick, before, 0.0), axis=0, keepdims=True)
        blk_id = jnp.sum(jnp.where(pick, blk_of, 0.0), axis=0, keepdims=True)
        within = rank - jnp.floor(rank / blk_rows) * blk_rows
        pos_rows.append((blk_id * blk_rows + within).astype(jnp.int32))
    pos = jnp.concatenate(pos_rows, axis=0)
    pos_ref[...] = pos
    tab_col = lax.broadcasted_iota(jnp.int32, tab_ref.shape, 1).astype(F32)
    tab_ref[...] = jnp.where(jnp.logical_and(tab_col == blocks_old, opened > 0.0),
                             new_blk.astype(jnp.int32), tab_ref[...])
    cur_ref[...] = jnp.where(opened > 0.0, new_blk, cur_blk)
    free_ref[...] = free_ref[...] + jnp.sum(opened, axis=0, keepdims=True)
    carry_ref[...] = total
    cnt_ref[...] = total.astype(jnp.int32)

    @pl.when(i > 0)
    def _():
        wait_rows(slot)

    for s in range(n_sub):
        u2t[slot, pl.ds(s, tm, stride=n_sub), :] = u2[:, s * LANES:(s + 1) * LANES]
    pos_v[slot] = pos
    pos_copy(slot).start()
    for b in range(per_step):
        zero_copy(n_experts + i * per_step + b).wait()

    @pl.when(i == n_steps - 1)
    def _():
        pos_copy(slot).wait()

        def issue(r, c):
            for k in range(TOP_K):
                row_copy(slot, k, r).start()
            return c
        lax.fori_loop(0, tm, issue, 0)
        wait_rows(prev)
        wait_rows(slot)


def _post_mix(o, p, sga, sgp, x2, mod3, w_att, w_pg, pool_scale, w_pb, w_out,
              g_post_mix, g_pre_ffn, w_router_t, b_router_col, seq):
    t, d = x2.shape
    tm = TM_MIX
    steps_per_seq = seq // tm
    n_experts = w_router_t.shape[0]
    pw = p.shape[1]
    n_sub = d // LANES
    halo_blocks = tm // POOL_HALO
    row = lambda w: pl.BlockSpec((tm, w), lambda i: (i, 0))
    col = lambda h: pl.BlockSpec((h, tm), lambda i: (0, i))
    full2 = lambda a: pl.BlockSpec(a.shape, lambda i: (0, 0))
    hbm = pl.BlockSpec(memory_space=pl.ANY)
    assert tm <= TM_EXP and (tm * TOP_K) % TM_EXP == 0 and t % TM_EXP == 0
    n_blocks = (t * TOP_K) // TM_EXP + n_experts
    tab_w = -(-(t // TM_EXP) // LANES) * LANES
    kern = functools.partial(_post_mix_kernel, steps_per_seq=steps_per_seq, n_blocks=n_blocks)
    xs_rows = n_blocks * TM_EXP + TOP_K * tm
    return pl.pallas_call(
        kern,
        grid=(t // tm,),
        in_specs=[
            row(d), row(pw),
            pl.BlockSpec((POOL_HALO, pw), lambda i: (jnp.maximum(i * halo_blocks - 1, 0), 0)),
            row(d), row(d), row(d),
            pl.BlockSpec((1, N_MOD, d), lambda i: (i // steps_per_seq, 0, 0)),
            hbm, hbm, full2(pool_scale), hbm, hbm, full2(g_post_mix), full2(g_pre_ffn),
            full2(w_router_t), full2(b_router_col),
        ],
        out_specs=[
            row(d), col(TOP_K), col(TOP_K),
            pl.BlockSpec((n_experts, 1), lambda i: (0, 0)),
            pl.BlockSpec((n_experts, tab_w), lambda i: (0, 0)),
            pl.BlockSpec(memory_space=pl.ANY),
        ],
        out_shape=[
            jax.ShapeDtypeStruct((t, d), F32),
            jax.ShapeDtypeStruct((TOP_K, t), F32),
            jax.ShapeDtypeStruct((TOP_K, t), jnp.int32),
            jax.ShapeDtypeStruct((n_experts, 1), jnp.int32),
            jax.ShapeDtypeStruct((n_experts, tab_w), jnp.int32),
            jax.ShapeDtypeStruct((xs_rows * n_sub, LANES), F32),
        ],
        scratch_shapes=[
            pltpu.VMEM((n_experts, 1), F32),
            pltpu.VMEM((n_experts, 1), F32),
            pltpu.VMEM((1, 1), F32),
            pltpu.VMEM((2, tm * n_sub, LANES), F32),
            pltpu.VMEM((TM_EXP * n_sub, LANES), F32),
            pltpu.VMEM((2, TOP_K, tm), jnp.int32),
            pltpu.SMEM((2, TOP_K, tm), jnp.int32),
            pltpu.VMEM(w_att.shape, BF16),
            pltpu.VMEM(w_pg.shape, BF16),
            pltpu.VMEM(w_pb.shape, BF16),
            pltpu.VMEM(w_out.shape, BF16),
            pltpu.VMEM((2, max(w_att.shape[0], w_out.shape[0]), WEIGHT_CHUNK), F32),
            pltpu.SemaphoreType.DMA((2,)),
            pltpu.SemaphoreType.DMA((2,)),
            pltpu.SemaphoreType.DMA((2,)),
            pltpu.SemaphoreType.DMA,
        ],
        compiler_params=_cparams(("arbitrary",)),
        name="post_mix",
    )(o, p, p, sga, sgp, x2, mod3, w_att, w_pg, pool_scale, w_pb, w_out,
      g_post_mix, g_pre_ffn, w_router_t, b_router_col)


def _experts_kernel(te_ref, blk_ref, nt_ref, xs_ref, wgu_ref, bgu_ref, wd_ref, bd_ref,
                    y_ref, wgu_bf, wd_bf, *, n_sub):
    j = pl.program_id(0)
    tm = TM_EXP
    d_ff = wd_ref.shape[1]

    @pl.when(j < nt_ref[0])
    def _():
        changed = jnp.logical_or(j == 0, te_ref[j] != te_ref[jnp.maximum(j - 1, 0)])

        @pl.when(changed)
        def _():
            wgu_bf[...] = wgu_ref[0].astype(BF16)
            wd_bf[...] = wd_ref[0].astype(BF16)

        xs = jnp.concatenate(
            [xs_ref[pl.ds(s, tm, stride=n_sub), :] for s in range(n_sub)], axis=1).astype(BF16)
        gu = jnp.dot(xs, wgu_bf[...], preferred_element_type=F32) + bgu_ref[0]
        gate = jnp.minimum(gu[:, :d_ff], SWIGLU_LIMIT)
        up = jnp.clip(gu[:, d_ff:], -SWIGLU_LIMIT, SWIGLU_LIMIT)
        act = (up + 1.0) * (gate * jax.nn.sigmoid(SWIGLU_ALPHA * gate))
        y = jnp.dot(act.astype(BF16), wd_bf[...], preferred_element_type=F32) + bd_ref[0]
        for s in range(n_sub):
            y_ref[pl.ds(s, tm, stride=n_sub), :] = y[:, s * LANES:(s + 1) * LANES]

    @pl.when(j >= nt_ref[0])
    def _():
        y_ref[...] = jnp.zeros_like(y_ref)


def _experts(tile_expert, tile_block, n_tiles, xs_tiles, w_gate_up, b_gate_up, w_down, b_down):
    n_exp, d, d_gu = w_gate_up.shape
    d_ff = w_down.shape[1]
    n_sub = d // LANES
    tm = TM_EXP
    max_tiles = tile_expert.shape[0]
    by_expert = lambda j, te, blk, nt: (te[j], 0, 0)
    by_block = lambda j, te, blk, nt: (blk[j], 0)
    grid_spec = pltpu.PrefetchScalarGridSpec(
        num_scalar_prefetch=3,
        grid=(max_tiles,),
        in_specs=[
            pl.BlockSpec((tm * n_sub, LANES), by_block),
            pl.BlockSpec((1, d, d_gu), by_expert),
            pl.BlockSpec((1, 1, d_gu), by_expert),
            pl.BlockSpec((1, d_ff, d), by_expert),
            pl.BlockSpec((1, 1, d), by_expert),
        ],
        out_specs=pl.BlockSpec((tm * n_sub, LANES), by_block),
        scratch_shapes=[
            pltpu.VMEM((d, d_gu), BF16),
            pltpu.VMEM((d_ff, d), BF16),
        ],
    )
    return pl.pallas_call(
        functools.partial(_experts_kernel, n_sub=n_sub),
        grid_spec=grid_spec,
        out_shape=jax.ShapeDtypeStruct((max_tiles * tm * n_sub, LANES), F32),
        compiler_params=_cparams(("arbitrary",)),
        name="experts",
    )(tile_expert, tile_block, n_tiles, xs_tiles,
      w_gate_up, b_gate_up.reshape(n_exp, 1, d_gu), w_down, b_down.reshape(n_exp, 1, d))


def _combine_kernel(pos_ref, y_hbm, prob_ref, x1_ref, mod_ref, g_ref, o_ref, ybuf, sem,
                    *, n_sub, n_tokens):
    i = pl.program_id(0)
    n_steps = pl.num_programs(0)
    tm = TM_COMB
    slot = i % 2

    def gather(step, s):
        for r in range(tm):
            for k in range(TOP_K):
                row = pos_ref[k * n_tokens + step * tm + r]
                pltpu.make_async_copy(
                    y_hbm.at[pl.ds(pl.multiple_of(row * n_sub, n_sub), n_sub), :],
                    ybuf.at[s, k, pl.ds(r * n_sub, n_sub), :], sem.at[s]).start()

    @pl.when(i == 0)
    def _():
        gather(0, 0)

    @pl.when(i + 1 < n_steps)
    def _():
        gather(i + 1, 1 - slot)

    for k in range(TOP_K):
        pltpu.make_async_copy(y_hbm.at[pl.ds(0, tm * n_sub), :], ybuf.at[slot, k], sem.at[slot]).wait()

    prob = prob_ref[...]
    f = None
    for k in range(TOP_K):
        yk = jnp.concatenate(
            [ybuf[slot, k, pl.ds(s, tm, stride=n_sub), :] for s in range(n_sub)], axis=1)
        term = yk * prob[:, k:k + 1]
        f = term if f is None else f + term
    gate_f = mod_ref[0, 5:6, :]
    o_ref[...] = x1_ref[...] + gate_f * (_rms(f) * g_ref[...])


def _combine(pos_flat, y_tiles, probs, x1, mod3, g_post_ffn, seq):
    t, d = x1.shape
    tm = TM_COMB
    n_sub = d // LANES
    steps_per_seq = seq // tm
    grid_spec = pltpu.PrefetchScalarGridSpec(
        num_scalar_prefetch=1,
        grid=(t // tm,),
        in_specs=[
            pl.BlockSpec(memory_space=pl.ANY),
            pl.BlockSpec((tm, TOP_K), lambda i, pos: (i, 0)),
            pl.BlockSpec((tm, d), lambda i, pos: (i, 0)),
            pl.BlockSpec((1, N_MOD, d), lambda i, pos: (i // steps_per_seq, 0, 0)),
            pl.BlockSpec((1, d), lambda i, pos: (0, 0)),
        ],
        out_specs=pl.BlockSpec((tm, d), lambda i, pos: (i, 0)),
        scratch_shapes=[
            pltpu.VMEM((2, TOP_K, tm * n_sub, LANES), F32),
            pltpu.SemaphoreType.DMA((2,)),
        ],
    )
    return pl.pallas_call(
        functools.partial(_combine_kernel, n_sub=n_sub, n_tokens=t),
        grid_spec=grid_spec,
        out_shape=jax.ShapeDtypeStruct((t, d), F32),
        compiler_params=_cparams(("arbitrary",)),
        name="combine",
    )(pos_flat, y_tiles, probs, x1, mod3, g_post_ffn)


def _rope_freq_row():
    half = ROPE_DIM // 2
    inv_freq = ROPE_THETA ** (-jnp.arange(0, ROPE_DIM, 2, dtype=F32) / ROPE_DIM)
    head = jnp.concatenate([inv_freq, inv_freq, jnp.zeros((DA_HEAD_DIM - 2 * half,), F32)])
    return jnp.tile(head, LANES // DA_HEAD_DIM).reshape(1, LANES)


def _tile_tables(counts, block_tab, n_tokens):
    tm = TM_EXP
    n_exp = counts.shape[0]
    max_tiles = (n_tokens * TOP_K) // tm + n_exp
    tiles = (counts + tm - 1) // tm
    tile_end = jnp.cumsum(tiles)
    n_tiles = tile_end[-1]
    j = jnp.arange(max_tiles, dtype=jnp.int32)
    jj = jnp.minimum(j, n_tiles - 1)
    tile_expert = jnp.sum(jj[:, None] >= tile_end[None, :], axis=1).astype(jnp.int32)
    nth = jj - (tile_end - tiles)[tile_expert]
    tile_block = jnp.where(j < n_tiles, block_tab[tile_expert, nth], j)
    return tile_expert, tile_block.astype(jnp.int32), n_tiles.reshape(1).astype(jnp.int32)


def kernel(x, c, positions, w_mod, b_mod, g_pre_mix, w_in, lambda_q1, lambda_k1, lambda_q2, lambda_k2, g_sub, w_pool_group, pool_scale, w_att_branch, w_pool_branch, w_out, g_post_mix, g_pre_ffn, w_router, b_router, w_gate_up, b_gate_up, w_down, b_down, g_post_ffn):
    bsz, seq, d = x.shape
    assert w_mod.shape[0] == 1, "single layer"
    assert seq % TQ == 0 and seq % TM_PROJ == 0 and seq % TM_MIX == 0 and seq % TM_COMB == 0
    t = bsz * seq
    assert t % TM_EXP == 0
    x2 = x.reshape(t, d)
    qk_w = 2 * DA_HEADS * DA_HEAD_DIM
    v_w = DA_HEADS * DA_V_DIM
    pool_w = len(POOL_WINDOWS) * POOL_GROUP_DIM
    n_experts = w_router.shape[2]

    mod3 = _mod(c, w_mod[0], b_mod[0]).reshape(bsz, N_MOD, d)

    bounds = [0, qk_w, 2 * qk_w, 2 * qk_w + v_w, 2 * qk_w + v_w + pool_w,
              2 * qk_w + v_w + pool_w + d, 2 * qk_w + v_w + pool_w + 2 * d]
    q, k, vt, p, sga, sgp = _in_proj(x2, mod3, g_pre_mix, positions.reshape(t, 1), _rope_freq_row(),
                                     w_in[0], bounds, seq)

    row64 = lambda a: a.reshape(1, DA_HEAD_DIM)
    o = _attention(q, k, vt, row64(lambda_q1[0]), row64(lambda_k1[0]), row64(lambda_q2[0]),
                   row64(lambda_k2[0]), g_sub.reshape(DA_V_DIM, 1), bsz, seq)

    x1, probs_t, pos_t, counts, block_tab, xs_tiles = _post_mix(
        o, p, sga, sgp, x2, mod3, w_att_branch[0], w_pool_group[0].reshape(pool_w, POOL_GROUP_DIM),
        pool_scale, w_pool_branch[0], w_out[0],
        g_post_mix, g_pre_ffn, w_router[0].T, b_router.reshape(n_experts, 1), seq)

    tile_expert, tile_block, n_tiles = _tile_tables(counts[:, 0], block_tab, t)
    y_tiles = _experts(tile_expert, tile_block, n_tiles, xs_tiles,
                       w_gate_up[0], b_gate_up[0], w_down[0], b_down[0])
    out = _combine(pos_t.reshape(-1), y_tiles, probs_t.T, x1, mod3, g_post_ffn, seq)
    return out.reshape(bsz, seq, d)
```

```python
import functools

import jax
import jax.numpy as jnp
from jax import lax
from jax.experimental import pallas as pl
from jax.experimental.pallas import tpu as pltpu

F32 = jnp.float32
BF16 = jnp.bfloat16

NORM_EPS = 1e-6
CHUNK = 64
DA_HEADS = 8
DA_HEAD_DIM = 64
DA_V_DIM = 2 * DA_HEAD_DIM
ROPE_THETA = 500000.0
ROPE_DIM = DA_HEAD_DIM // 4
POOL_WINDOWS = (2, 4, 8, 16)
POOL_GROUP_DIM = 128
TOP_K = 4
SWIGLU_LIMIT = 7.0
SWIGLU_ALPHA = 1.702
N_MOD = 6
LAMBDA_INIT = 0.8 - 0.6 * 1.0
Q_SCALE = (DA_HEAD_DIM ** -0.5) * 1.4426950408889634

LANES = 128
SUBLANES = 8
VMEM_LIMIT = 56 * 1024 * 1024

TM_PROJ = 512
TQ = 512
TK = 256
ATT_HEADS_PER_STEP = 2
ATT_SUM_ROWS = 16
TM_MIX = 256
TM_EXP = 512
TM_COMB = 256
WEIGHT_CHUNK = 512
POOL_HALO = 16
NEG_BIG = -1e30


def _cparams(sem):
    return pltpu.CompilerParams(dimension_semantics=sem, vmem_limit_bytes=VMEM_LIMIT)


def _split(a):
    hi = a.astype(BF16)
    return hi, (a - hi.astype(F32)).astype(BF16)


def _split_dot(a, b, dims=(((1,), (0,)), ((), ()))):
    a_hi, a_lo = _split(a)
    b_hi, b_lo = _split(b)
    dot = lambda x, y: lax.dot_general(x, y, dims, preferred_element_type=F32)
    return dot(a_hi, b_hi) + (dot(a_hi, b_lo) + dot(a_lo, b_hi))


def _rms(x):
    return x * lax.rsqrt(jnp.mean(x * x, axis=-1, keepdims=True) + NORM_EPS)


def _mod_kernel(c_ref, w_ref, b_ref, o_ref):
    c = c_ref[...]
    c_act = c * jax.nn.sigmoid(c)
    o_ref[...] = _split_dot(c_act, w_ref[...]) + b_ref[...]


def _mod(c, w_mod, b_mod):
    bsz, d = c.shape
    n = w_mod.shape[1]
    tn = 1024
    return pl.pallas_call(
        _mod_kernel,
        grid=(n // tn,),
        in_specs=[
            pl.BlockSpec((bsz, d), lambda j: (0, 0)),
            pl.BlockSpec((d, tn), lambda j: (0, j)),
            pl.BlockSpec((1, tn), lambda j: (0, j)),
        ],
        out_specs=pl.BlockSpec((bsz, tn), lambda j: (0, j)),
        out_shape=jax.ShapeDtypeStruct((bsz, n), F32),
        compiler_params=_cparams(("parallel",)),
        name="mod",
    )(c, w_mod, b_mod.reshape(1, n))


def _rope(t, cos_t, sin_a, sin_b):
    n = t.shape[1]
    up = pltpu.roll(t, n - ROPE_DIM // 2, axis=1)
    dn = pltpu.roll(t, ROPE_DIM // 2, axis=1)
    reps = n // LANES
    tile = lambda a: jnp.concatenate([a] * reps, axis=1)
    return t * tile(cos_t) + up * tile(sin_a) + dn * tile(sin_b)


def _stream_windows(src_hbm, windows, stage, sem, sink):
    def copy(n):
        r0, nr, c0, nc = windows[n]
        return pltpu.make_async_copy(src_hbm.at[pl.ds(r0, nr), pl.ds(c0, nc)],
                                     stage.at[n % 2, pl.ds(0, nr), pl.ds(0, nc)], sem.at[n % 2])
    copy(0).start()
    for n, (_, nr, _, nc) in enumerate(windows):
        if n + 1 < len(windows):
            copy(n + 1).start()
        copy(n).wait()
        sink(n, stage[n % 2, 0:nr, 0:nc])


def _in_proj_kernel(x_ref, mod_ref, g_ref, pos_ref, freq_ref, w_hbm,
                    q_ref, k_ref, v_ref, p_ref, sga_ref, sgp_ref,
                    w_ref, wvt_ref, stage, sem, *, bounds):
    @pl.when(pl.program_id(0) == 0)
    def _():
        d_in, width = w_hbm.shape
        windows = [(0, d_in, c0, WEIGHT_CHUNK) for c0 in range(0, width, WEIGHT_CHUNK)]

        def sink(n, tile):
            c0 = windows[n][2]
            w_ref[:, c0:c0 + WEIGHT_CHUNK] = tile.astype(BF16)
            if bounds[2] <= c0 < bounds[3]:
                wvt_ref[c0 - bounds[2]:c0 - bounds[2] + WEIGHT_CHUNK, :] = tile.T.astype(BF16)
        _stream_windows(w_hbm, windows, stage, sem, sink)

    x = x_ref[...]
    shift = mod_ref[0, 0:1, :]
    scale = mod_ref[0, 1:2, :]
    u = (_rms(x) * g_ref[...]) * (1.0 + scale) + shift
    ub = u.astype(BF16)
    dot = functools.partial(jnp.dot, preferred_element_type=F32)
    part = lambda n: w_ref[:, bounds[n]:bounds[n + 1]]

    ang = pos_ref[...].astype(F32) * freq_ref[...]
    cos_t, sn = jnp.cos(ang), jnp.sin(ang)
    in_head = lax.broadcasted_iota(jnp.int32, ang.shape, 1) % DA_HEAD_DIM
    sin_a = jnp.where(in_head < ROPE_DIM // 2, -sn, 0.0)
    sin_b = jnp.where(in_head >= ROPE_DIM // 2, sn, 0.0)

    q = _rope(dot(ub, part(0)), cos_t, sin_a, sin_b)
    q_ref[...] = (q * Q_SCALE).astype(BF16)
    k = _rope(dot(ub, part(1)), cos_t, sin_a, sin_b)
    k_ref[...] = k.astype(BF16)
    vt = lax.dot_general(wvt_ref[...], ub, (((1,), (1,)), ((), ())), preferred_element_type=F32)
    for n in range(v_ref.shape[0]):
        v_ref[n] = vt[:, n * TK:(n + 1) * TK].astype(BF16)
    p_ref[...] = dot(ub, part(3))
    sga_ref[...] = jax.nn.sigmoid(dot(ub, part(4))).astype(BF16)
    sgp_ref[...] = jax.nn.sigmoid(dot(ub, part(5))).astype(BF16)


def _in_proj(x2, mod3, g_pre, pos_col, freq_row, w_in, bounds, seq):
    t, d = x2.shape
    width = w_in.shape[1]
    assert width % WEIGHT_CHUNK == 0 and all(b % WEIGHT_CHUNK == 0 for b in bounds[2:4])
    tm = TM_PROJ
    assert tm % TK == 0, "v is emitted as transposed (channels, TK) slabs"
    steps_per_seq = seq // tm
    widths = [b - a for a, b in zip(bounds[:-1], bounds[1:])]
    row = lambda w: pl.BlockSpec((tm, w), lambda i: (i, 0))
    full = lambda a: pl.BlockSpec(a.shape, lambda i: (0, 0))
    out_specs = [row(widths[0]), row(widths[1]),
                 pl.BlockSpec((tm // TK, widths[2], TK), lambda i: (i, 0, 0)),
                 row(widths[3]), row(widths[4]), row(widths[5])]
    out_shape = [jax.ShapeDtypeStruct((t, widths[0]), BF16),
                 jax.ShapeDtypeStruct((t, widths[1]), BF16),
                 jax.ShapeDtypeStruct((t // TK, widths[2], TK), BF16),
                 jax.ShapeDtypeStruct((t, widths[3]), F32),
                 jax.ShapeDtypeStruct((t, widths[4]), BF16),
                 jax.ShapeDtypeStruct((t, widths[5]), BF16)]
    return pl.pallas_call(
        functools.partial(_in_proj_kernel, bounds=tuple(bounds)),
        grid=(t // tm,),
        in_specs=[
            row(d),
            pl.BlockSpec((1, N_MOD, d), lambda i: (i // steps_per_seq, 0, 0)),
            full(g_pre), row(1), full(freq_row), pl.BlockSpec(memory_space=pl.ANY),
        ],
        out_specs=out_specs,
        out_shape=out_shape,
        scratch_shapes=[
            pltpu.VMEM((d, width), BF16),
            pltpu.VMEM((widths[2], d), BF16),
            pltpu.VMEM((2, d, WEIGHT_CHUNK), F32),
            pltpu.SemaphoreType.DMA((2,)),
        ],
        compiler_params=_cparams(("arbitrary",)),
        name="in_proj",
    )(x2, mod3, g_pre, pos_col, freq_row, w_in)


def _attn_kernel(lq1_ref, lk1_ref, lq2_ref, lk2_ref, g_ref, q_ref, k_ref, vt_ref, o_ref,
                 s_a, s_b, p_a, p_b, acc_buf):
    assert TQ == 2 * TK and TK % CHUNK == 0
    seq = q_ref.shape[0]
    lam = (jnp.exp(jnp.sum(lq1_ref[...] * lk1_ref[...], axis=-1, keepdims=True))
           - jnp.exp(jnp.sum(lq2_ref[...] * lk2_ref[...], axis=-1, keepdims=True))
           + LAMBDA_INIT)
    dot = functools.partial(jnp.dot, preferred_element_type=F32)
    heads = range(ATT_HEADS_PER_STEP)
    lanes = lambda h: slice(h * LANES, (h + 1) * LANES)

    def q_tile(qi, c):
        q_rows = pl.ds(pl.multiple_of(qi * TQ, TQ), TQ)
        qqt = []
        for h in heads:
            qt = q_ref[q_rows, lanes(h)].astype(F32).T
            row = lax.broadcasted_iota(jnp.int32, qt.shape, 0)
            zero = jnp.zeros_like(qt)
            qqt.append(jnp.concatenate([jnp.where(row < DA_HEAD_DIM, qt, zero),
                                        jnp.where(row >= DA_HEAD_DIM, qt, zero)],
                                       axis=1).astype(BF16))

        def scores(j, h):
            return dot(k_ref[pl.ds(pl.multiple_of(j * TK, TK), TK), lanes(h)], qqt[h])

        def softmax_step(s, m):
            m_new = jnp.maximum(m, jnp.max(s, axis=0, keepdims=True))
            return m_new, jnp.exp2(m - m_new), jnp.exp2(s - m_new).astype(BF16)

        ones_rows = jnp.ones((ATT_SUM_ROWS, TK), BF16)

        def pv(j, h, p):
            return dot(jnp.concatenate([vt_ref[j, lanes(h), :], ones_rows], axis=0), p)

        kk = lax.broadcasted_iota(jnp.int32, (TK, 2 * TQ), 0)
        qq = lax.broadcasted_iota(jnp.int32, (TK, 2 * TQ), 1)
        rel_chunk = jnp.where(qq >= TQ, qq - TQ, qq) // CHUNK - kk // CHUNK

        def masked(s, j):
            return jnp.where(rel_chunk >= j * (TK // CHUNK) - qi * (TQ // CHUNK), s, NEG_BIG)

        def step(j, carries, s_cur, p_cur, s_nxt, p_prev):
            pend = [pv(jnp.maximum(j - 1, 0), h, p_prev[h]) for h in heads]
            for h in heads:
                s_nxt[h] = scores(j + 1, h)
            out = []
            for h in heads:
                m, alpha = carries[h]
                acc_buf[h] = alpha * acc_buf[h] + pend[h]
                m, alpha, p = softmax_step(s_cur[h], m)
                p_cur[h] = p
                out.append((m, alpha))
            return tuple(out)

        def pair(i, carries):
            carries = step(2 * i, carries, s_a, p_a, s_b, p_b)
            return step(2 * i + 1, carries, s_b, p_b, s_a, p_a)

        for h in heads:
            s_a[h] = scores(0, h)
            p_b[h] = jnp.zeros((TK, 2 * TQ), BF16)
            acc_buf[h] = jnp.zeros(acc_buf.shape[1:], F32)
        init = tuple((jnp.full((1, 2 * TQ), NEG_BIG, F32), jnp.ones((1, 2 * TQ), F32))
                     for _ in heads)
        carries = lax.fori_loop(0, qi, pair, init)
        ja = 2 * qi
        late = lambda a: jnp.concatenate([a[:, TK:TQ], a[:, TQ + TK:]], axis=1)
        pend = [pv(jnp.maximum(ja - 1, 0), h, p_b[h]) for h in heads]
        for h in heads:
            k_last = k_ref[pl.ds(pl.multiple_of((ja + 1) * TK, TK), TK), lanes(h)]
            s_b[h, :, 0:TQ] = dot(k_last, late(qqt[h]))
        mid = []
        for h in heads:
            m, alpha = carries[h]
            acc_buf[h] = alpha * acc_buf[h] + pend[h]
            m, alpha, p = softmax_step(masked(s_a[h], ja), m)
            p_a[h] = p
            mid.append((m, alpha))
        pend = [pv(ja, h, p_a[h]) for h in heads]
        k_in = lax.broadcasted_iota(jnp.int32, (TK, TQ), 0)
        q_in = lax.broadcasted_iota(jnp.int32, (TK, TQ), 1) % TK
        diagonal = k_in // CHUNK <= q_in // CHUNK
        for h in heads:
            m, alpha = mid[h]
            acc = alpha * acc_buf[h] + pend[h]
            m_l, alpha_l, p_l = softmax_step(jnp.where(diagonal, s_b[h, :, 0:TQ], NEG_BIG), late(m))
            acc_l = alpha_l * late(acc) + pv(ja + 1, h, p_l)
            acc = jnp.concatenate([acc[:, :TK], acc_l[:, :TK], acc[:, TQ:TQ + TK], acc_l[:, TK:]], axis=1)
            o = acc[:DA_V_DIM] / acc[DA_V_DIM:DA_V_DIM + 1]
            a = o[:, :TQ] - lam * o[:, TQ:]
            y = a * lax.rsqrt(jnp.mean(a * a, axis=0, keepdims=True) + NORM_EPS)
            y = (y * g_ref[...]) * (1.0 - LAMBDA_INIT)
            o_ref[q_rows, lanes(h)] = y.T.astype(BF16)
        return c

    lax.fori_loop(0, seq // TQ, q_tile, 0)


def _attention(q, k, vt, lq1, lk1, lq2, lk2, g_col, bsz, seq):
    t = q.shape[0]
    nkv = seq // TK
    hw = ATT_HEADS_PER_STEP * LANES
    vec = lambda a: pl.BlockSpec(a.shape, lambda b, h: (0, 0))
    return pl.pallas_call(
        _attn_kernel,
        grid=(bsz, DA_HEADS // ATT_HEADS_PER_STEP),
        in_specs=[
            vec(lq1), vec(lk1), vec(lq2), vec(lk2), vec(g_col),
            pl.BlockSpec((seq, hw), lambda b, h: (b, h)),
            pl.BlockSpec((seq, hw), lambda b, h: (b, h)),
            pl.BlockSpec((nkv, hw, TK), lambda b, h: (b, h, 0)),
        ],
        out_specs=pl.BlockSpec((seq, hw), lambda b, h: (b, h)),
        out_shape=jax.ShapeDtypeStruct((t, DA_HEADS * DA_V_DIM), BF16),
        scratch_shapes=[
            pltpu.VMEM((ATT_HEADS_PER_STEP, TK, 2 * TQ), F32),
            pltpu.VMEM((ATT_HEADS_PER_STEP, TK, 2 * TQ), F32),
            pltpu.VMEM((ATT_HEADS_PER_STEP, TK, 2 * TQ), BF16),
            pltpu.VMEM((ATT_HEADS_PER_STEP, TK, 2 * TQ), BF16),
            pltpu.VMEM((ATT_HEADS_PER_STEP, DA_V_DIM + ATT_SUM_ROWS, 2 * TQ), F32),
        ],
        compiler_params=_cparams(("parallel", "parallel")),
        name="attention",
    )(lq1, lk1, lq2, lk2, g_col, q, k, vt)


def _post_mix_kernel(o_ref, p_ref, ph_ref, sga_ref, sgp_ref, x_ref, mod_ref,
                     watt_hbm, wpg_hbm, ps_ref, wpb_hbm, wout_hbm, gpm_ref, gpf_ref,
                     wrt_ref, br_ref,
                     x1_ref, prob_ref, pos_ref, cnt_ref, tab_ref, xs_hbm,
                     carry_ref, cur_ref, free_ref, u2t, zeros_v, pos_v, pos_s,
                     watt_ref, wpg_ref, wpb_ref, wout_ref, stage, sem_w,
                     sem_rows, sem_pos, sem_zero,
                     *, steps_per_seq, n_blocks):
    i = pl.program_id(0)
    n_steps = pl.num_programs(0)
    tm = x_ref.shape[0]
    n_experts = wrt_ref.shape[0]
    n_sub = x_ref.shape[1] // LANES
    blk = TM_EXP * n_sub
    per_step = (tm * TOP_K) // TM_EXP
    slot = i % 2
    prev = 1 - slot
    dot = functools.partial(jnp.dot, preferred_element_type=F32)

    def pos_copy(s):
        return pltpu.make_async_copy(pos_v.at[s], pos_s.at[s], sem_pos.at[s])

    def row_copy(s, k, r):
        dst = pos_s[s, k, r]
        return pltpu.make_async_copy(
            u2t.at[s, pl.ds(r * n_sub, n_sub), :],
            xs_hbm.at[pl.ds(pl.multiple_of(dst * n_sub, n_sub), n_sub), :], sem_rows.at[s])

    def wait_rows(s):
        for _ in range(TOP_K):
            pltpu.make_async_copy(u2t.at[s], xs_hbm.at[pl.ds(0, tm * n_sub), :], sem_rows.at[s]).wait()

    def zero_copy(b):
        return pltpu.make_async_copy(zeros_v, xs_hbm.at[pl.ds(pl.multiple_of(b * blk, blk), blk), :],
                                     sem_zero)

    @pl.when(i == 0)
    def _():
        carry_ref[...] = jnp.zeros_like(carry_ref)
        cur_ref[...] = jnp.zeros_like(cur_ref)
        free_ref[...] = jnp.zeros_like(free_ref)
        tab_ref[...] = jnp.zeros_like(tab_ref)
        zeros_v[...] = jnp.zeros_like(zeros_v)
        for src, dst in ((watt_hbm, watt_ref), (wpg_hbm, wpg_ref), (wpb_hbm, wpb_ref),
                         (wout_hbm, wout_ref)):
            rows, cols = src.shape
            step_c = min(cols, WEIGHT_CHUNK)
            windows = [(0, rows, c0, step_c) for c0 in range(0, cols, step_c)]

            def sink(n, tile, dst=dst, windows=windows):
                c0, nc = windows[n][2], windows[n][3]
                dst[:, c0:c0 + nc] = tile.astype(BF16)
            _stream_windows(src, windows, stage, sem_w, sink)
        u2t[1] = jnp.zeros(u2t.shape[1:], F32)
        spare = (n_blocks * TM_EXP
                 + lax.broadcasted_iota(jnp.int32, (TOP_K, tm), 0) * tm
                 + lax.broadcasted_iota(jnp.int32, (TOP_K, tm), 1))
        pos_v[1] = spare
        pos_copy(1).start()
        for b in range(n_experts):
            zero_copy(b).start()
        for b in range(n_experts):
            zero_copy(b).wait()

    for b in range(per_step):
        zero_copy(n_experts + i * per_step + b).start()

    pos_copy(prev).wait()

    for r in range(tm):
        for k in range(TOP_K):
            row_copy(prev, k, r).start()

    y_att = dot(o_ref[...], watt_ref[...])

    first = (i % steps_per_seq) == 0
    halo = jnp.where(first, 0.0, ph_ref[...])
    ext = jnp.concatenate([halo, p_ref[...]], axis=0)
    t_in_seq = (i % steps_per_seq) * tm + lax.broadcasted_iota(jnp.int32, (tm, 1), 0)
    pooled = []
    for g, w in enumerate(POOL_WINDOWS):
        e = ext[:, g * POOL_GROUP_DIM:(g + 1) * POOL_GROUP_DIM]
        acc, span = e, 1
        while span < w:
            acc = acc[span:] + acc[:-span]
            span *= 2
        win = acc[POOL_HALO - (w - 1):]
        cnt = jnp.minimum(t_in_seq + 1, w).astype(F32)
        mixed = win / cnt - e[POOL_HALO:]
        pooled.append(dot(mixed.astype(BF16), wpg_ref[g * POOL_GROUP_DIM:(g + 1) * POOL_GROUP_DIM, :]))
    y_pool_in = jnp.concatenate(pooled, axis=1) * ps_ref[...]
    y_pool = dot(y_pool_in.astype(BF16), wpb_ref[...])

    merged = sga_ref[...].astype(F32) * y_att + sgp_ref[...].astype(F32) * y_pool
    mix_out = dot(merged.astype(BF16), wout_ref[...])
    gate_m = mod_ref[0, 2:3, :]
    shift_f = mod_ref[0, 3:4, :]
    scale_f = mod_ref[0, 4:5, :]
    x1 = x_ref[...] + gate_m * (_rms(mix_out) * gpm_ref[...])
    x1_ref[...] = x1
    u2 = (_rms(x1) * gpf_ref[...]) * (1.0 + scale_f) + shift_f

    logits = _split_dot(wrt_ref[...], u2, (((1,), (1,)), ((), ()))) + br_ref[...]
    erow = lax.broadcasted_iota(jnp.int32, logits.shape, 0)
    work = logits
    vals, idxs = [], []
    for _ in range(TOP_K):
        mx = jnp.max(work, axis=0, keepdims=True)
        ix = jnp.min(jnp.where(work == mx, erow, n_experts), axis=0, keepdims=True)
        vals.append(mx)
        idxs.append(ix)
        work = jnp.where(erow == ix, -jnp.inf, work)
    exps = [jnp.exp(vv - vals[0]) for vv in vals]
    denom = exps[0] + exps[1] + exps[2] + exps[3]
    prob_ref[...] = jnp.concatenate([e / denom for e in exps], axis=0)

    onehot = jnp.zeros(logits.shape, F32)
    for ix in idxs:
        onehot = onehot + (erow == ix).astype(F32)
    rr = lax.broadcasted_iota(jnp.int32, (tm, tm), 0)
    cc = lax.broadcasted_iota(jnp.int32, (tm, tm), 1)
    earlier = (rr < cc).astype(BF16)
    carry = carry_ref[...]
    before = dot(onehot.astype(BF16), earlier) + carry

    blk_rows = float(TM_EXP)
    total = carry + jnp.sum(onehot, axis=1, keepdims=True)
    blocks_old = jnp.floor((carry + (blk_rows - 1.0)) / blk_rows)
    opened = jnp.floor((total + (blk_rows - 1.0)) / blk_rows) - blocks_old
    e_r = lax.broadcasted_iota(jnp.int32, (n_experts, n_experts), 0)
    e_c = lax.broadcasted_iota(jnp.int32, (n_experts, n_experts), 1)
    lower = (e_c < e_r).astype(BF16)
    opened_before = dot(lower, jnp.broadcast_to(opened, (n_experts, LANES)).astype(BF16))[:, 0:1]
    new_blk = free_ref[...] + opened_before
    cur_blk = cur_ref[...]
    boundary = blocks_old * blk_rows
    blk_of = jnp.where(before < boundary, cur_blk, new_blk)
    pos_rows = []
    for ix in idxs:
        pick = erow == ix
        rank = jnp.sum(jnp.where(pick, before, 0.0), axis=0, keepdims=True)
        blk_id = jnp.sum(jnp.where(pick, blk_of, 0.0), axis=0, keepdims=True)
        within = rank - jnp.floor(rank / blk_rows) * blk_rows
        pos_rows.append((blk_id * blk_rows + within).astype(jnp.int32))
    pos = jnp.concatenate(pos_rows, axis=0)
    pos_ref[...] = pos
    tab_col = lax.broadcasted_iota(jnp.int32, tab_ref.shape, 1).astype(F32)
    tab_ref[...] = jnp.where(jnp.logical_and(tab_col == blocks_old, opened > 0.0),
                             new_blk.astype(jnp.int32), tab_ref[...])
    cur_ref[...] = jnp.where(opened > 0.0, new_blk, cur_blk)
    free_ref[...] = free_ref[...] + jnp.sum(opened, axis=0, keepdims=True)
    carry_ref[...] = total
    cnt_ref[...] = total.astype(jnp.int32)

    @pl.when(i > 0)
    def _():
        wait_rows(slot)

    for s in range(n_sub):
        u2t[slot, pl.ds(s, tm, stride=n_sub), :] = u2[:, s * LANES:(s + 1) * LANES]
    pos_v[slot] = pos
    pos_copy(slot).start()
    for b in range(per_step):
        zero_copy(n_experts + i * per_step + b).wait()

    @pl.when(i == n_steps - 1)
    def _():
        pos_copy(slot).wait()

        def issue(r, c):
            for k in range(TOP_K):
                row_copy(slot, k, r).start()
            return c
        lax.fori_loop(0, tm, issue, 0)
        wait_rows(prev)
        wait_rows(slot)


def _post_mix(o, p, sga, sgp, x2, mod3, w_att, w_pg, pool_scale, w_pb, w_out,
              g_post_mix, g_pre_ffn, w_router_t, b_router_col, seq):
    t, d = x2.shape
    tm = TM_MIX
    steps_per_seq = seq // tm
    n_experts = w_router_t.shape[0]
    pw = p.shape[1]
    n_sub = d // LANES
    halo_blocks = tm // POOL_HALO
    row = lambda w: pl.BlockSpec((tm, w), lambda i: (i, 0))
    col = lambda h: pl.BlockSpec((h, tm), lambda i: (0, i))
    full2 = lambda a: pl.BlockSpec(a.shape, lambda i: (0, 0))
    hbm = pl.BlockSpec(memory_space=pl.ANY)
    assert tm <= TM_EXP and (tm * TOP_K) % TM_EXP == 0 and t % TM_EXP == 0
    n_blocks = (t * TOP_K) // TM_EXP + n_experts
    tab_w = -(-(t // TM_EXP) // LANES) * LANES
    kern = functools.partial(_post_mix_kernel, steps_per_seq=steps_per_seq, n_blocks=n_blocks)
    xs_rows = n_blocks * TM_EXP + TOP_K * tm
    return pl.pallas_call(
        kern,
        grid=(t // tm,),
        in_specs=[
            row(d), row(pw),
            pl.BlockSpec((POOL_HALO, pw), lambda i: (jnp.maximum(i * halo_blocks - 1, 0), 0)),
            row(d), row(d), row(d),
            pl.BlockSpec((1, N_MOD, d), lambda i: (i // steps_per_seq, 0, 0)),
            hbm, hbm, full2(pool_scale), hbm, hbm, full2(g_post_mix), full2(g_pre_ffn),
            full2(w_router_t), full2(b_router_col),
        ],
        out_specs=[
            row(d), col(TOP_K), col(TOP_K),
            pl.BlockSpec((n_experts, 1), lambda i: (0, 0)),
            pl.BlockSpec((n_experts, tab_w), lambda i: (0, 0)),
            pl.BlockSpec(memory_space=pl.ANY),
        ],
        out_shape=[
            jax.ShapeDtypeStruct((t, d), F32),
            jax.ShapeDtypeStruct((TOP_K, t), F32),
            jax.ShapeDtypeStruct((TOP_K, t), jnp.int32),
            jax.ShapeDtypeStruct((n_experts, 1), jnp.int32),
            jax.ShapeDtypeStruct((n_experts, tab_w), jnp.int32),
            jax.ShapeDtypeStruct((xs_rows * n_sub, LANES), F32),
        ],
        scratch_shapes=[
            pltpu.VMEM((n_experts, 1), F32),
            pltpu.VMEM((n_experts, 1), F32),
            pltpu.VMEM((1, 1), F32),
            pltpu.VMEM((2, tm * n_sub, LANES), F32),
            pltpu.VMEM((TM_EXP * n_sub, LANES), F32),
            pltpu.VMEM((2, TOP_K, tm), jnp.int32),
            pltpu.SMEM((2, TOP_K, tm), jnp.int32),
            pltpu.VMEM(w_att.shape, BF16),
            pltpu.VMEM(w_pg.shape, BF16),
            pltpu.VMEM(w_pb.shape, BF16),
            pltpu.VMEM(w_out.shape, BF16),
            pltpu.VMEM((2, max(w_att.shape[0], w_out.shape[0]), WEIGHT_CHUNK), F32),
            pltpu.SemaphoreType.DMA((2,)),
            pltpu.SemaphoreType.DMA((2,)),
            pltpu.SemaphoreType.DMA((2,)),
            pltpu.SemaphoreType.DMA,
        ],
        compiler_params=_cparams(("arbitrary",)),
        name="post_mix",
    )(o, p, p, sga, sgp, x2, mod3, w_att, w_pg, pool_scale, w_pb, w_out,
      g_post_mix, g_pre_ffn, w_router_t, b_router_col)


def _experts_kernel(te_ref, blk_ref, nt_ref, xs_ref, wgu_ref, bgu_ref, wd_ref, bd_ref,
                    y_ref, wgu_bf, wd_bf, *, n_sub):
    j = pl.program_id(0)
    tm = TM_EXP
    d_ff = wd_ref.shape[1]

    @pl.when(j < nt_ref[0])
    def _():
        changed = jnp.logical_or(j == 0, te_ref[j] != te_ref[jnp.maximum(j - 1, 0)])

        @pl.when(changed)
        def _():
            wgu_bf[...] = wgu_ref[0].astype(BF16)
            wd_bf[...] = wd_ref[0].astype(BF16)

        xs = jnp.concatenate(
            [xs_ref[pl.ds(s, tm, stride=n_sub), :] for s in range(n_sub)], axis=1).astype(BF16)
        gu = jnp.dot(xs, wgu_bf[...], preferred_element_type=F32) + bgu_ref[0]
        gate = jnp.minimum(gu[:, :d_ff], SWIGLU_LIMIT)
        up = jnp.clip(gu[:, d_ff:], -SWIGLU_LIMIT, SWIGLU_LIMIT)
        act = (up + 1.0) * (gate * jax.nn.sigmoid(SWIGLU_ALPHA * gate))
        y = jnp.dot(act.astype(BF16), wd_bf[...], preferred_element_type=F32) + bd_ref[0]
        for s in range(n_sub):
            y_ref[pl.ds(s, tm, stride=n_sub), :] = y[:, s * LANES:(s + 1) * LANES]

    @pl.when(j >= nt_ref[0])
    def _():
        y_ref[...] = jnp.zeros_like(y_ref)


def _experts(tile_expert, tile_block, n_tiles, xs_tiles, w_gate_up, b_gate_up, w_down, b_down):
    n_exp, d, d_gu = w_gate_up.shape
    d_ff = w_down.shape[1]
    n_sub = d // LANES
    tm = TM_EXP
    max_tiles = tile_expert.shape[0]
    by_expert = lambda j, te, blk, nt: (te[j], 0, 0)
    by_block = lambda j, te, blk, nt: (blk[j], 0)
    grid_spec = pltpu.PrefetchScalarGridSpec(
        num_scalar_prefetch=3,
        grid=(max_tiles,),
        in_specs=[
            pl.BlockSpec((tm * n_sub, LANES), by_block),
            pl.BlockSpec((1, d, d_gu), by_expert),
            pl.BlockSpec((1, 1, d_gu), by_expert),
            pl.BlockSpec((1, d_ff, d), by_expert),
            pl.BlockSpec((1, 1, d), by_expert),
        ],
        out_specs=pl.BlockSpec((tm * n_sub, LANES), by_block),
        scratch_shapes=[
            pltpu.VMEM((d, d_gu), BF16),
            pltpu.VMEM((d_ff, d), BF16),
        ],
    )
    return pl.pallas_call(
        functools.partial(_experts_kernel, n_sub=n_sub),
        grid_spec=grid_spec,
        out_shape=jax.ShapeDtypeStruct((max_tiles * tm * n_sub, LANES), F32),
        compiler_params=_cparams(("arbitrary",)),
        name="experts",
    )(tile_expert, tile_block, n_tiles, xs_tiles,
      w_gate_up, b_gate_up.reshape(n_exp, 1, d_gu), w_down, b_down.reshape(n_exp, 1, d))


def _combine_kernel(pos_ref, y_hbm, prob_ref, x1_ref, mod_ref, g_ref, o_ref, ybuf, sem,
                    *, n_sub, n_tokens):
    i = pl.program_id(0)
    n_steps = pl.num_programs(0)
    tm = TM_COMB
    slot = i % 2

    def gather(step, s):
        for r in range(tm):
            for k in range(TOP_K):
                row = pos_ref[k * n_tokens + step * tm + r]
                pltpu.make_async_copy(
                    y_hbm.at[pl.ds(pl.multiple_of(row * n_sub, n_sub), n_sub), :],
                    ybuf.at[s, k, pl.ds(r * n_sub, n_sub), :], sem.at[s]).start()

    @pl.when(i == 0)
    def _():
        gather(0, 0)

    @pl.when(i + 1 < n_steps)
    def _():
        gather(i + 1, 1 - slot)

    for k in range(TOP_K):
        pltpu.make_async_copy(y_hbm.at[pl.ds(0, tm * n_sub), :], ybuf.at[slot, k], sem.at[slot]).wait()

    prob = prob_ref[...]
    f = None
    for k in range(TOP_K):
        yk = jnp.concatenate(
            [ybuf[slot, k, pl.ds(s, tm, stride=n_sub), :] for s in range(n_sub)], axis=1)
        term = yk * prob[:, k:k + 1]
        f = term if f is None else f + term
    gate_f = mod_ref[0, 5:6, :]
    o_ref[...] = x1_ref[...] + gate_f * (_rms(f) * g_ref[...])


def _combine(pos_flat, y_tiles, probs, x1, mod3, g_post_ffn, seq):
    t, d = x1.shape
    tm = TM_COMB
    n_sub = d // LANES
    steps_per_seq = seq // tm
    grid_spec = pltpu.PrefetchScalarGridSpec(
        num_scalar_prefetch=1,
        grid=(t // tm,),
        in_specs=[
            pl.BlockSpec(memory_space=pl.ANY),
            pl.BlockSpec((tm, TOP_K), lambda i, pos: (i, 0)),
            pl.BlockSpec((tm, d), lambda i, pos: (i, 0)),
            pl.BlockSpec((1, N_MOD, d), lambda i, pos: (i // steps_per_seq, 0, 0)),
            pl.BlockSpec((1, d), lambda i, pos: (0, 0)),
        ],
        out_specs=pl.BlockSpec((tm, d), lambda i, pos: (i, 0)),
        scratch_shapes=[
            pltpu.VMEM((2, TOP_K, tm * n_sub, LANES), F32),
            pltpu.SemaphoreType.DMA((2,)),
        ],
    )
    return pl.pallas_call(
        functools.partial(_combine_kernel, n_sub=n_sub, n_tokens=t),
        grid_spec=grid_spec,
        out_shape=jax.ShapeDtypeStruct((t, d), F32),
        compiler_params=_cparams(("arbitrary",)),
        name="combine",
    )(pos_flat, y_tiles, probs, x1, mod3, g_post_ffn)


def _rope_freq_row():
    half = ROPE_DIM // 2
    inv_freq = ROPE_THETA ** (-jnp.arange(0, ROPE_DIM, 2, dtype=F32) / ROPE_DIM)
    head = jnp.concatenate([inv_freq, inv_freq, jnp.zeros((DA_HEAD_DIM - 2 * half,), F32)])
    return jnp.tile(head, LANES // DA_HEAD_DIM).reshape(1, LANES)


def _tile_tables(counts, block_tab, n_tokens):
    tm = TM_EXP
    n_exp = counts.shape[0]
    max_tiles = (n_tokens * TOP_K) // tm + n_exp
    tiles = (counts + tm - 1) // tm
    tile_end = jnp.cumsum(tiles)
    n_tiles = tile_end[-1]
    j = jnp.arange(max_tiles, dtype=jnp.int32)
    jj = jnp.minimum(j, n_tiles - 1)
    tile_expert = jnp.sum(jj[:, None] >= tile_end[None, :], axis=1).astype(jnp.int32)
    nth = jj - (tile_end - tiles)[tile_expert]
    tile_block = jnp.where(j < n_tiles, block_tab[tile_expert, nth], j)
    return tile_expert, tile_block.astype(jnp.int32), n_tiles.reshape(1).astype(jnp.int32)


def kernel(x, c, positions, w_mod, b_mod, g_pre_mix, w_in, lambda_q1, lambda_k1, lambda_q2, lambda_k2, g_sub, w_pool_group, pool_scale, w_att_branch, w_pool_branch, w_out, g_post_mix, g_pre_ffn, w_router, b_router, w_gate_up, b_gate_up, w_down, b_down, g_post_ffn):
    bsz, seq, d = x.shape
    assert w_mod.shape[0] == 1, "single layer"
    assert seq % TQ == 0 and seq % TM_PROJ == 0 and seq % TM_MIX == 0 and seq % TM_COMB == 0
    t = bsz * seq
    assert t % TM_EXP == 0
    x2 = x.reshape(t, d)
    qk_w = 2 * DA_HEADS * DA_HEAD_DIM
    v_w = DA_HEADS * DA_V_DIM
    pool_w = len(POOL_WINDOWS) * POOL_GROUP_DIM
    n_experts = w_router.shape[2]

    mod3 = _mod(c, w_mod[0], b_mod[0]).reshape(bsz, N_MOD, d)

    bounds = [0, qk_w, 2 * qk_w, 2 * qk_w + v_w, 2 * qk_w + v_w + pool_w,
              2 * qk_w + v_w + pool_w + d, 2 * qk_w + v_w + pool_w + 2 * d]
    q, k, vt, p, sga, sgp = _in_proj(x2, mod3, g_pre_mix, positions.reshape(t, 1), _rope_freq_row(),
                                     w_in[0], bounds, seq)

    row64 = lambda a: a.reshape(1, DA_HEAD_DIM)
    o = _attention(q, k, vt, row64(lambda_q1[0]), row64(lambda_k1[0]), row64(lambda_q2[0]),
                   row64(lambda_k2[0]), g_sub.reshape(DA_V_DIM, 1), bsz, seq)

    x1, probs_t, pos_t, counts, block_tab, xs_tiles = _post_mix(
        o, p, sga, sgp, x2, mod3, w_att_branch[0], w_pool_group[0].reshape(pool_w, POOL_GROUP_DIM),
        pool_scale, w_pool_branch[0], w_out[0],
        g_post_mix, g_pre_ffn, w_router[0].T, b_router.reshape(n_experts, 1), seq)

    tile_expert, tile_block, n_tiles = _tile_tables(counts[:, 0], block_tab, t)
    y_tiles = _experts(tile_expert, tile_block, n_tiles, xs_tiles,
                       w_gate_up[0], b_gate_up[0], w_down[0], b_down[0])
    out = _combine(pos_t.reshape(-1), y_tiles, probs_t.T, x1, mod3, g_post_ffn, seq)
    return out.reshape(bsz, seq, d)
```

```python
import functools

import jax
import jax.numpy as jnp
from jax import lax
from jax.experimental import pallas as pl
from jax.experimental.pallas import tpu as pltpu

F32 = jnp.float32
BF16 = jnp.bfloat16

NORM_EPS = 1e-6
CHUNK = 64
DA_HEADS = 8
DA_HEAD_DIM = 64
DA_V_DIM = 2 * DA_HEAD_DIM
ROPE_THETA = 500000.0
ROPE_DIM = DA_HEAD_DIM // 4
POOL_WINDOWS = (2, 4, 8, 16)
POOL_GROUP_DIM = 128
TOP_K = 4
SWIGLU_LIMIT = 7.0
SWIGLU_ALPHA = 1.702
N_MOD = 6
LAMBDA_INIT = 0.8 - 0.6 * 1.0
Q_SCALE = (DA_HEAD_DIM ** -0.5) * 1.4426950408889634

LANES = 128
SUBLANES = 8
VMEM_LIMIT = 56 * 1024 * 1024

TM_PROJ = 512
TQ = 512
TK = 256
ATT_HEADS_PER_STEP = 2
ATT_SUM_ROWS = 16
TM_MIX = 256
TM_EXP = 512
TM_COMB = 256
WEIGHT_CHUNK = 512
POOL_HALO = 16
NEG_BIG = -1e30


def _cparams(sem):
    return pltpu.CompilerParams(dimension_semantics=sem, vmem_limit_bytes=VMEM_LIMIT)


def _split(a):
    hi = a.astype(BF16)
    return hi, (a - hi.astype(F32)).astype(BF16)


def _split_dot(a, b, dims=(((1,), (0,)), ((), ()))):
    a_hi, a_lo = _split(a)
    b_hi, b_lo = _split(b)
    dot = lambda x, y: lax.dot_general(x, y, dims, preferred_element_type=F32)
    return dot(a_hi, b_hi) + (dot(a_hi, b_lo) + dot(a_lo, b_hi))


def _rms(x):
    return x * lax.rsqrt(jnp.mean(x * x, axis=-1, keepdims=True) + NORM_EPS)


def _mod_kernel(c_ref, w_ref, b_ref, o_ref):
    c = c_ref[...]
    c_act = c * jax.nn.sigmoid(c)
    o_ref[...] = _split_dot(c_act, w_ref[...]) + b_ref[...]


def _mod(c, w_mod, b_mod):
    bsz, d = c.shape
    n = w_mod.shape[1]
    tn = 1024
    return pl.pallas_call(
        _mod_kernel,
        grid=(n // tn,),
        in_specs=[
            pl.BlockSpec((bsz, d), lambda j: (0, 0)),
            pl.BlockSpec((d, tn), lambda j: (0, j)),
            pl.BlockSpec((1, tn), lambda j: (0, j)),
        ],
        out_specs=pl.BlockSpec((bsz, tn), lambda j: (0, j)),
        out_shape=jax.ShapeDtypeStruct((bsz, n), F32),
        compiler_params=_cparams(("parallel",)),
        name="mod",
    )(c, w_mod, b_mod.reshape(1, n))


def _rope(t, cos_t, sin_a, sin_b):
    n = t.shape[1]
    up = pltpu.roll(t, n - ROPE_DIM // 2, axis=1)
    dn = pltpu.roll(t, ROPE_DIM // 2, axis=1)
    reps = n // LANES
    tile = lambda a: jnp.concatenate([a] * reps, axis=1)
    return t * tile(cos_t) + up * tile(sin_a) + dn * tile(sin_b)


def _stream_windows(src_hbm, windows, stage, sem, sink):
    def copy(n):
        r0, nr, c0, nc = windows[n]
        return pltpu.make_async_copy(src_hbm.at[pl.ds(r0, nr), pl.ds(c0, nc)],
                                     stage.at[n % 2, pl.ds(0, nr), pl.ds(0, nc)], sem.at[n % 2])
    copy(0).start()
    for n, (_, nr, _, nc) in enumerate(windows):
        if n + 1 < len(windows):
            copy(n + 1).start()
        copy(n).wait()
        sink(n, stage[n % 2, 0:nr, 0:nc])


def _in_proj_kernel(x_ref, mod_ref, g_ref, pos_ref, freq_ref, w_hbm,
                    q_ref, k_ref, v_ref, p_ref, sga_ref, sgp_ref,
                    w_ref, wvt_ref, stage, sem, *, bounds):
    @pl.when(pl.program_id(0) == 0)
    def _():
        d_in, width = w_hbm.shape
        windows = [(0, d_in, c0, WEIGHT_CHUNK) for c0 in range(0, width, WEIGHT_CHUNK)]

        def sink(n, tile):
            c0 = windows[n][2]
            w_ref[:, c0:c0 + WEIGHT_CHUNK] = tile.astype(BF16)
            if bounds[2] <= c0 < bounds[3]:
                wvt_ref[c0 - bounds[2]:c0 - bounds[2] + WEIGHT_CHUNK, :] = tile.T.astype(BF16)
        _stream_windows(w_hbm, windows, stage, sem, sink)

    x = x_ref[...]
    shift = mod_ref[0, 0:1, :]
    scale = mod_ref[0, 1:2, :]
    u = (_rms(x) * g_ref[...]) * (1.0 + scale) + shift
    ub = u.astype(BF16)
    dot = functools.partial(jnp.dot, preferred_element_type=F32)
    part = lambda n: w_ref[:, bounds[n]:bounds[n + 1]]

    ang = pos_ref[...].astype(F32) * freq_ref[...]
    cos_t, sn = jnp.cos(ang), jnp.sin(ang)
    in_head = lax.broadcasted_iota(jnp.int32, ang.shape, 1) % DA_HEAD_DIM
    sin_a = jnp.where(in_head < ROPE_DIM // 2, -sn, 0.0)
    sin_b = jnp.where(in_head >= ROPE_DIM // 2, sn, 0.0)

    q = _rope(dot(ub, part(0)), cos_t, sin_a, sin_b)
    q_ref[...] = (q * Q_SCALE).astype(BF16)
    k = _rope(dot(ub, part(1)), cos_t, sin_a, sin_b)
    k_ref[...] = k.astype(BF16)
    vt = lax.dot_general(wvt_ref[...], ub, (((1,), (1,)), ((), ())), preferred_element_type=F32)
    for n in range(v_ref.shape[0]):
        v_ref[n] = vt[:, n * TK:(n + 1) * TK].astype(BF16)
    p_ref[...] = dot(ub, part(3))
    sga_ref[...] = jax.nn.sigmoid(dot(ub, part(4))).astype(BF16)
    sgp_ref[...] = jax.nn.sigmoid(dot(ub, part(5))).astype(BF16)


def _in_proj(x2, mod3, g_pre, pos_col, freq_row, w_in, bounds, seq):
    t, d = x2.shape
    width = w_in.shape[1]
    assert width % WEIGHT_CHUNK == 0 and all(b % WEIGHT_CHUNK == 0 for b in bounds[2:4])
    tm = TM_PROJ
    assert tm % TK == 0, "v is emitted as transposed (channels, TK) slabs"
    steps_per_seq = seq // tm
    widths = [b - a for a, b in zip(bounds[:-1], bounds[1:])]
    row = lambda w: pl.BlockSpec((tm, w), lambda i: (i, 0))
    full = lambda a: pl.BlockSpec(a.shape, lambda i: (0, 0))
    out_specs = [row(widths[0]), row(widths[1]),
                 pl.BlockSpec((tm // TK, widths[2], TK), lambda i: (i, 0, 0)),
                 row(widths[3]), row(widths[4]), row(widths[5])]
    out_shape = [jax.ShapeDtypeStruct((t, widths[0]), BF16),
                 jax.ShapeDtypeStruct((t, widths[1]), BF16),
                 jax.ShapeDtypeStruct((t // TK, widths[2], TK), BF16),
                 jax.ShapeDtypeStruct((t, widths[3]), F32),
                 jax.ShapeDtypeStruct((t, widths[4]), BF16),
                 jax.ShapeDtypeStruct((t, widths[5]), BF16)]
    return pl.pallas_call(
        functools.partial(_in_proj_kernel, bounds=tuple(bounds)),
        grid=(t // tm,),
        in_specs=[
            row(d),
            pl.BlockSpec((1, N_MOD, d), lambda i: (i // steps_per_seq, 0, 0)),
            full(g_pre), row(1), full(freq_row), pl.BlockSpec(memory_space=pl.ANY),
        ],
        out_specs=out_specs,
        out_shape=out_shape,
        scratch_shapes=[
            pltpu.VMEM((d, width), BF16),
            pltpu.VMEM((widths[2], d), BF16),
            pltpu.VMEM((2, d, WEIGHT_CHUNK), F32),
            pltpu.SemaphoreType.DMA((2,)),
        ],
        compiler_params=_cparams(("arbitrary",)),
        name="in_proj",
    )(x2, mod3, g_pre, pos_col, freq_row, w_in)


def _attn_kernel(lq1_ref, lk1_ref, lq2_ref, lk2_ref, g_ref, q_ref, k_ref, vt_ref, o_ref,
                 s_a, s_b, p_a, p_b, acc_buf):
    assert TQ == 2 * TK and TK % CHUNK == 0
    seq = q_ref.shape[0]
    lam = (jnp.exp(jnp.sum(lq1_ref[...] * lk1_ref[...], axis=-1, keepdims=True))
           - jnp.exp(jnp.sum(lq2_ref[...] * lk2_ref[...], axis=-1, keepdims=True))
           + LAMBDA_INIT)
    dot = functools.partial(jnp.dot, preferred_element_type=F32)
    heads = range(ATT_HEADS_PER_STEP)
    lanes = lambda h: slice(h * LANES, (h + 1) * LANES)

    def q_tile(qi, c):
        q_rows = pl.ds(pl.multiple_of(qi * TQ, TQ), TQ)
        qqt = []
        for h in heads:
            qt = q_ref[q_rows, lanes(h)].astype(F32).T
            row = lax.broadcasted_iota(jnp.int32, qt.shape, 0)
            zero = jnp.zeros_like(qt)
            qqt.append(jnp.concatenate([jnp.where(row < DA_HEAD_DIM, qt, zero),
                                        jnp.where(row >= DA_HEAD_DIM, qt, zero)],
                                       axis=1).astype(BF16))

        def scores(j, h):
            return dot(k_ref[pl.ds(pl.multiple_of(j * TK, TK), TK), lanes(h)], qqt[h])

        def softmax_step(s, m):
            m_new = jnp.maximum(m, jnp.max(s, axis=0, keepdims=True))
            return m_new, jnp.exp2(m - m_new), jnp.exp2(s - m_new).astype(BF16)

        ones_rows = jnp.ones((ATT_SUM_ROWS, TK), BF16)

        def pv(j, h, p):
            return dot(jnp.concatenate([vt_ref[j, lanes(h), :], ones_rows], axis=0), p)

        kk = lax.broadcasted_iota(jnp.int32, (TK, 2 * TQ), 0)
        qq = lax.broadcasted_iota(jnp.int32, (TK, 2 * TQ), 1)
        rel_chunk = jnp.where(qq >= TQ, qq - TQ, qq) // CHUNK - kk // CHUNK

        def masked(s, j):
            return jnp.where(rel_chunk >= j * (TK // CHUNK) - qi * (TQ // CHUNK), s, NEG_BIG)

        def step(j, carries, s_cur, p_cur, s_nxt, p_prev):
            pend = [pv(jnp.maximum(j - 1, 0), h, p_prev[h]) for h in heads]
            for h in heads:
                s_nxt[h] = scores(j + 1, h)
            out = []
            for h in heads:
                m, alpha = carries[h]
                acc_buf[h] = alpha * acc_buf[h] + pend[h]
                m, alpha, p = softmax_step(s_cur[h], m)
                p_cur[h] = p
                out.append((m, alpha))
            return tuple(out)

        def pair(i, carries):
            carries = step(2 * i, carries, s_a, p_a, s_b, p_b)
            return step(2 * i + 1, carries, s_b, p_b, s_a, p_a)

        for h in heads:
            s_a[h] = scores(0, h)
            p_b[h] = jnp.zeros((TK, 2 * TQ), BF16)
            acc_buf[h] = jnp.zeros(acc_buf.shape[1:], F32)
        init = tuple((jnp.full((1, 2 * TQ), NEG_BIG, F32), jnp.ones((1, 2 * TQ), F32))
                     for _ in heads)
        carries = lax.fori_loop(0, qi, pair, init)
        ja = 2 * qi
        late = lambda a: jnp.concatenate([a[:, TK:TQ], a[:, TQ + TK:]], axis=1)
        pend = [pv(jnp.maximum(ja - 1, 0), h, p_b[h]) for h in heads]
        for h in heads:
            k_last = k_ref[pl.ds(pl.multiple_of((ja + 1) * TK, TK), TK), lanes(h)]
            s_b[h, :, 0:TQ] = dot(k_last, late(qqt[h]))
        mid = []
        for h in heads:
            m, alpha = carries[h]
            acc_buf[h] = alpha * acc_buf[h] + pend[h]
            m, alpha, p = softmax_step(masked(s_a[h], ja), m)
            p_a[h] = p
            mid.append((m, alpha))
        pend = [pv(ja, h, p_a[h]) for h in heads]
        k_in = lax.broadcasted_iota(jnp.int32, (TK, TQ), 0)
        q_in = lax.broadcasted_iota(jnp.int32, (TK, TQ), 1) % TK
        diagonal = k_in // CHUNK <= q_in // CHUNK
        for h in heads:
            m, alpha = mid[h]
            acc = alpha * acc_buf[h] + pend[h]
            m_l, alpha_l, p_l = softmax_step(jnp.where(diagonal, s_b[h, :, 0:TQ], NEG_BIG), late(m))
            acc_l = alpha_l * late(acc) + pv(ja + 1, h, p_l)
            acc = jnp.concatenate([acc[:, :TK], acc_l[:, :TK], acc[:, TQ:TQ + TK], acc_l[:, TK:]], axis=1)
            o = acc[:DA_V_DIM] / acc[DA_V_DIM:DA_V_DIM + 1]
            a = o[:, :TQ] - lam * o[:, TQ:]
            y = a * lax.rsqrt(jnp.mean(a * a, axis=0, keepdims=True) + NORM_EPS)
            y = (y * g_ref[...]) * (1.0 - LAMBDA_INIT)
            o_ref[q_rows, lanes(h)] = y.T.astype(BF16)
        return c

    lax.fori_loop(0, seq // TQ, q_tile, 0)


def _attention(q, k, vt, lq1, lk1, lq2, lk2, g_col, bsz, seq):
    t = q.shape[0]
    nkv = seq // TK
    hw = ATT_HEADS_PER_STEP * LANES
    vec = lambda a: pl.BlockSpec(a.shape, lambda b, h: (0, 0))
    return pl.pallas_call(
        _attn_kernel,
        grid=(bsz, DA_HEADS // ATT_HEADS_PER_STEP),
        in_specs=[
            vec(lq1), vec(lk1), vec(lq2), vec(lk2), vec(g_col),
            pl.BlockSpec((seq, hw), lambda b, h: (b, h)),
            pl.BlockSpec((seq, hw), lambda b, h: (b, h)),
            pl.BlockSpec((nkv, hw, TK), lambda b, h: (b, h, 0)),
        ],
        out_specs=pl.BlockSpec((seq, hw), lambda b, h: (b, h)),
        out_shape=jax.ShapeDtypeStruct((t, DA_HEADS * DA_V_DIM), BF16),
        scratch_shapes=[
            pltpu.VMEM((ATT_HEADS_PER_STEP, TK, 2 * TQ), F32),
            pltpu.VMEM((ATT_HEADS_PER_STEP, TK, 2 * TQ), F32),
            pltpu.VMEM((ATT_HEADS_PER_STEP, TK, 2 * TQ), BF16),
            pltpu.VMEM((ATT_HEADS_PER_STEP, TK, 2 * TQ), BF16),
            pltpu.VMEM((ATT_HEADS_PER_STEP, DA_V_DIM + ATT_SUM_ROWS, 2 * TQ), F32),
        ],
        compiler_params=_cparams(("parallel", "parallel")),
        name="attention",
    )(lq1, lk1, lq2, lk2, g_col, q, k, vt)


def _post_mix_kernel(o_ref, p_ref, ph_ref, sga_ref, sgp_ref, x_ref, mod_ref,
                     watt_hbm, wpg_hbm, ps_ref, wpb_hbm, wout_hbm, gpm_ref, gpf_ref,
                     wrt_ref, br_ref,
                     x1_ref, prob_ref, pos_ref, cnt_ref, tab_ref, xs_hbm,
                     carry_ref, cur_ref, free_ref, u2t, zeros_v, pos_v, pos_s, meta_v, meta_s,
                     watt_ref, wpg_ref, wpb_ref, wout_ref, stage, sem_w,
                     sem_rows, sem_pos, sem_zero,
                     *, steps_per_seq, n_blocks):
    i = pl.program_id(0)
    n_steps = pl.num_programs(0)
    tm = x_ref.shape[0]
    n_experts = wrt_ref.shape[0]
    n_sub = x_ref.shape[1] // LANES
    slot = i % 2
    prev = 1 - slot
    dot = functools.partial(jnp.dot, preferred_element_type=F32)

    def pos_copy(s):
        return pltpu.make_async_copy(pos_v.at[s], pos_s.at[s], sem_pos.at[s])

    def row_copy(s, k, r):
        dst = pos_s[s, k, r]
        return pltpu.make_async_copy(
            u2t.at[s, pl.ds(r * n_sub, n_sub), :],
            xs_hbm.at[pl.ds(pl.multiple_of(dst * n_sub, n_sub), n_sub), :], sem_rows.at[s])

    def wait_rows(s):
        for _ in range(TOP_K):
            pltpu.make_async_copy(u2t.at[s], xs_hbm.at[pl.ds(0, tm * n_sub), :], sem_rows.at[s]).wait()

    @pl.when(i == 0)
    def _():
        carry_ref[...] = jnp.zeros_like(carry_ref)
        cur_ref[...] = jnp.zeros_like(cur_ref)
        free_ref[...] = jnp.zeros_like(free_ref)
        tab_ref[...] = jnp.zeros_like(tab_ref)
        zeros_v[...] = jnp.zeros_like(zeros_v)
        for src, dst in ((watt_hbm, watt_ref), (wpg_hbm, wpg_ref), (wpb_hbm, wpb_ref),
                         (wout_hbm, wout_ref)):
            rows, cols = src.shape
            step_c = min(cols, WEIGHT_CHUNK)
            windows = [(0, rows, c0, step_c) for c0 in range(0, cols, step_c)]

            def sink(n, tile, dst=dst, windows=windows):
                c0, nc = windows[n][2], windows[n][3]
                dst[:, c0:c0 + nc] = tile.astype(BF16)
            _stream_windows(src, windows, stage, sem_w, sink)
        u2t[1] = jnp.zeros(u2t.shape[1:], F32)
        spare = (n_blocks * TM_EXP
                 + lax.broadcasted_iota(jnp.int32, (TOP_K, tm), 0) * tm
                 + lax.broadcasted_iota(jnp.int32, (TOP_K, tm), 1))
        pos_v[1] = spare
        pos_copy(1).start()

    pos_copy(prev).wait()

    for r in range(tm):
        for k in range(TOP_K):
            row_copy(prev, k, r).start()

    y_att = dot(o_ref[...], watt_ref[...])

    first = (i % steps_per_seq) == 0
    halo = jnp.where(first, 0.0, ph_ref[...])
    ext = jnp.concatenate([halo, p_ref[...]], axis=0)
    t_in_seq = (i % steps_per_seq) * tm + lax.broadcasted_iota(jnp.int32, (tm, 1), 0)
    pooled = []
    for g, w in enumerate(POOL_WINDOWS):
        e = ext[:, g * POOL_GROUP_DIM:(g + 1) * POOL_GROUP_DIM]
        acc, span = e, 1
        while span < w:
            acc = acc[span:] + acc[:-span]
            span *= 2
        win = acc[POOL_HALO - (w - 1):]
        cnt = jnp.minimum(t_in_seq + 1, w).astype(F32)
        mixed = win / cnt - e[POOL_HALO:]
        pooled.append(dot(mixed.astype(BF16), wpg_ref[g * POOL_GROUP_DIM:(g + 1) * POOL_GROUP_DIM, :]))
    y_pool_in = jnp.concatenate(pooled, axis=1) * ps_ref[...]
    y_pool = dot(y_pool_in.astype(BF16), wpb_ref[...])

    merged = sga_ref[...].astype(F32) * y_att + sgp_ref[...].astype(F32) * y_pool
    mix_out = dot(merged.astype(BF16), wout_ref[...])
    gate_m = mod_ref[0, 2:3, :]
    shift_f = mod_ref[0, 3:4, :]
    scale_f = mod_ref[0, 4:5, :]
    x1 = x_ref[...] + gate_m * (_rms(mix_out) * gpm_ref[...])
    x1_ref[...] = x1
    u2 = (_rms(x1) * gpf_ref[...]) * (1.0 + scale_f) + shift_f

    logits = _split_dot(wrt_ref[...], u2, (((1,), (1,)), ((), ()))) + br_ref[...]
    erow = lax.broadcasted_iota(jnp.int32, logits.shape, 0)
    work = logits
    vals, idxs = [], []
    for _ in range(TOP_K):
        mx = jnp.max(work, axis=0, keepdims=True)
        ix = jnp.min(jnp.where(work == mx, erow, n_experts), axis=0, keepdims=True)
        vals.append(mx)
        idxs.append(ix)
        work = jnp.where(erow == ix, -jnp.inf, work)
    exps = [jnp.exp(vv - vals[0]) for vv in vals]
    denom = exps[0] + exps[1] + exps[2] + exps[3]
    prob_ref[...] = jnp.concatenate([e / denom for e in exps], axis=0)

    onehot = jnp.zeros(logits.shape, F32)
    for ix in idxs:
        onehot = onehot + (erow == ix).astype(F32)
    rr = lax.broadcasted_iota(jnp.int32, (tm, tm), 0)
    cc = lax.broadcasted_iota(jnp.int32, (tm, tm), 1)
    earlier = (rr < cc).astype(BF16)
    carry = carry_ref[...]
    before = dot(onehot.astype(BF16), earlier) + carry

    blk_rows = float(TM_EXP)
    total = carry + jnp.sum(onehot, axis=1, keepdims=True)
    blocks_old = jnp.floor((carry + (blk_rows - 1.0)) / blk_rows)
    opened = jnp.floor((total + (blk_rows - 1.0)) / blk_rows) - blocks_old
    e_r = lax.broadcasted_iota(jnp.int32, (n_experts, n_experts), 0)
    e_c = lax.broadcasted_iota(jnp.int32, (n_experts, n_experts), 1)
    lower = (e_c < e_r).astype(BF16)
    opened_before = dot(lower, jnp.broadcast_to(opened, (n_experts, LANES)).astype(BF16))[:, 0:1]
    new_blk = free_ref[...] + opened_before
    cur_blk = cur_ref[...]
    boundary = blocks_old * blk_rows
    blk_of = jnp.where(before < boundary, cur_blk, new_blk)
    pos_rows = []
    for ix in idxs:
        pick = erow == ix
        rank = jnp.sum(jnp.where(pick, before, 0.0), axis=0, keepdims=True)
        blk_id = jnp.sum(jnp.where(pick, blk_of, 0.0), axis=0, keepdims=True)
        within = rank - jnp.floor(rank / blk_rows) * blk_rows
        pos_rows.append((blk_id * blk_rows + within).astype(jnp.int32))
    pos = jnp.concatenate(pos_rows, axis=0)
    pos_ref[...] = pos
    tab_col = lax.broadcasted_iota(jnp.int32, tab_ref.shape, 1).astype(F32)
    tab_ref[...] = jnp.where(jnp.logical_and(tab_col == blocks_old, opened > 0.0),
                             new_blk.astype(jnp.int32), tab_ref[...])
    cur_ref[...] = jnp.where(opened > 0.0, new_blk, cur_blk)
    free_ref[...] = free_ref[...] + jnp.sum(opened, axis=0, keepdims=True)
    carry_ref[...] = total
    cnt_ref[...] = total.astype(jnp.int32)

    @pl.when(i > 0)
    def _():
        wait_rows(slot)

    for s in range(n_sub):
        u2t[slot, pl.ds(s, tm, stride=n_sub), :] = u2[:, s * LANES:(s + 1) * LANES]
    pos_v[slot] = pos
    pos_copy(slot).start()

    @pl.when(i == n_steps - 1)
    def _():
        pos_copy(slot).wait()

        def issue(r, c):
            for k in range(TOP_K):
                row_copy(slot, k, r).start()
            return c
        lax.fori_loop(0, tm, issue, 0)

        eye = e_r == e_c
        as_row = lambda col: jnp.sum(jnp.where(eye, col, 0.0), axis=0, keepdims=True).astype(jnp.int32)
        used = total - jnp.floor(total / blk_rows) * blk_rows
        meta = jnp.concatenate([as_row(used), as_row(cur_ref[...]),
                                jnp.broadcast_to(free_ref[...].astype(jnp.int32), (1, n_experts))], axis=0)
        meta_v[...] = jnp.zeros_like(meta_v)
        meta_v[0:3, 0:n_experts] = meta
        meta_copy = pltpu.make_async_copy(meta_v, meta_s, sem_zero)
        meta_copy.start()
        meta_copy.wait()

        sizes = [TM_EXP >> (b + 1) for b in range(TM_EXP.bit_length() - 1)]
        zero_run = lambda row0, n: pltpu.make_async_copy(
            zeros_v.at[pl.ds(0, n * n_sub), :],
            xs_hbm.at[pl.ds(pl.multiple_of(row0 * n_sub, n_sub), n * n_sub), :], sem_zero)
        issued = [jnp.int32(0) for _ in sizes]
        for e in range(n_experts):
            first = meta_s[0, e]
            row0 = meta_s[1, e] * TM_EXP + first
            pad = jnp.where(first > 0, TM_EXP - first, 0)
            for b, n in enumerate(sizes):
                take = (pad & n) != 0

                @pl.when(take)
                def _(row0=row0, n=n):
                    zero_run(row0, n).start()
                issued[b] = issued[b] + take.astype(jnp.int32)
                row0 = row0 + jnp.where(take, n, 0)
        n_open = meta_s[2, 0]

        def whole(b, c):
            zero_run(b * TM_EXP, TM_EXP).start()
            return c
        lax.fori_loop(n_open, n_blocks, whole, 0)

        for b, n in enumerate(sizes):
            lax.fori_loop(0, issued[b], lambda _, c, n=n: (zero_run(0, n).wait(), c)[1], 0)
        lax.fori_loop(n_open, n_blocks, lambda _, c: (zero_run(0, TM_EXP).wait(), c)[1], 0)
        wait_rows(prev)
        wait_rows(slot)


def _post_mix(o, p, sga, sgp, x2, mod3, w_att, w_pg, pool_scale, w_pb, w_out,
              g_post_mix, g_pre_ffn, w_router_t, b_router_col, seq):
    t, d = x2.shape
    tm = TM_MIX
    steps_per_seq = seq // tm
    n_experts = w_router_t.shape[0]
    pw = p.shape[1]
    n_sub = d // LANES
    halo_blocks = tm // POOL_HALO
    row = lambda w: pl.BlockSpec((tm, w), lambda i: (i, 0))
    col = lambda h: pl.BlockSpec((h, tm), lambda i: (0, i))
    full2 = lambda a: pl.BlockSpec(a.shape, lambda i: (0, 0))
    hbm = pl.BlockSpec(memory_space=pl.ANY)
    assert tm <= TM_EXP and (tm * TOP_K) % TM_EXP == 0 and t % TM_EXP == 0
    n_blocks = (t * TOP_K) // TM_EXP + n_experts
    tab_w = -(-(t // TM_EXP) // LANES) * LANES
    kern = functools.partial(_post_mix_kernel, steps_per_seq=steps_per_seq, n_blocks=n_blocks)
    xs_rows = n_blocks * TM_EXP + TOP_K * tm
    return pl.pallas_call(
        kern,
        grid=(t // tm,),
        in_specs=[
            row(d), row(pw),
            pl.BlockSpec((POOL_HALO, pw), lambda i: (jnp.maximum(i * halo_blocks - 1, 0), 0)),
            row(d), row(d), row(d),
            pl.BlockSpec((1, N_MOD, d), lambda i: (i // steps_per_seq, 0, 0)),
            hbm, hbm, full2(pool_scale), hbm, hbm, full2(g_post_mix), full2(g_pre_ffn),
            full2(w_router_t), full2(b_router_col),
        ],
        out_specs=[
            row(d), col(TOP_K), col(TOP_K),
            pl.BlockSpec((n_experts, 1), lambda i: (0, 0)),
            pl.BlockSpec((n_experts, tab_w), lambda i: (0, 0)),
            pl.BlockSpec(memory_space=pl.ANY),
        ],
        out_shape=[
            jax.ShapeDtypeStruct((t, d), F32),
            jax.ShapeDtypeStruct((TOP_K, t), F32),
            jax.ShapeDtypeStruct((TOP_K, t), jnp.int32),
            jax.ShapeDtypeStruct((n_experts, 1), jnp.int32),
            jax.ShapeDtypeStruct((n_experts, tab_w), jnp.int32),
            jax.ShapeDtypeStruct((xs_rows * n_sub, LANES), F32),
        ],
        scratch_shapes=[
            pltpu.VMEM((n_experts, 1), F32),
            pltpu.VMEM((n_experts, 1), F32),
            pltpu.VMEM((1, 1), F32),
            pltpu.VMEM((2, tm * n_sub, LANES), F32),
            pltpu.VMEM((TM_EXP * n_sub, LANES), F32),
            pltpu.VMEM((2, TOP_K, tm), jnp.int32),
            pltpu.SMEM((2, TOP_K, tm), jnp.int32),
            pltpu.VMEM((SUBLANES, LANES), jnp.int32),
            pltpu.SMEM((SUBLANES, LANES), jnp.int32),
            pltpu.VMEM(w_att.shape, BF16),
            pltpu.VMEM(w_pg.shape, BF16),
            pltpu.VMEM(w_pb.shape, BF16),
            pltpu.VMEM(w_out.shape, BF16),
            pltpu.VMEM((2, max(w_att.shape[0], w_out.shape[0]), WEIGHT_CHUNK), F32),
            pltpu.SemaphoreType.DMA((2,)),
            pltpu.SemaphoreType.DMA((2,)),
            pltpu.SemaphoreType.DMA((2,)),
            pltpu.SemaphoreType.DMA,
        ],
        compiler_params=_cparams(("arbitrary",)),
        name="post_mix",
    )(o, p, p, sga, sgp, x2, mod3, w_att, w_pg, pool_scale, w_pb, w_out,
      g_post_mix, g_pre_ffn, w_router_t, b_router_col)


def _experts_kernel(te_ref, blk_ref, nt_ref, xs_ref, wgu_ref, bgu_ref, wd_ref, bd_ref,
                    y_ref, wgu_bf, wd_bf, *, n_sub):
    j = pl.program_id(0)
    tm = TM_EXP
    d_ff = wd_ref.shape[1]

    @pl.when(j < nt_ref[0])
    def _():
        changed = jnp.logical_or(j == 0, te_ref[j] != te_ref[jnp.maximum(j - 1, 0)])

        @pl.when(changed)
        def _():
            wgu_bf[...] = wgu_ref[0].astype(BF16)
            wd_bf[...] = wd_ref[0].astype(BF16)

        xs = jnp.concatenate(
            [xs_ref[pl.ds(s, tm, stride=n_sub), :] for s in range(n_sub)], axis=1).astype(BF16)
        gu = jnp.dot(xs, wgu_bf[...], preferred_element_type=F32) + bgu_ref[0]
        gate = jnp.minimum(gu[:, :d_ff], SWIGLU_LIMIT)
        up = jnp.clip(gu[:, d_ff:], -SWIGLU_LIMIT, SWIGLU_LIMIT)
        act = (up + 1.0) * (gate * jax.nn.sigmoid(SWIGLU_ALPHA * gate))
        y = jnp.dot(act.astype(BF16), wd_bf[...], preferred_element_type=F32) + bd_ref[0]
        for s in range(n_sub):
            y_ref[pl.ds(s, tm, stride=n_sub), :] = y[:, s * LANES:(s + 1) * LANES]

    @pl.when(j >= nt_ref[0])
    def _():
        y_ref[...] = jnp.zeros_like(y_ref)


def _experts(tile_expert, tile_block, n_tiles, xs_tiles, w_gate_up, b_gate_up, w_down, b_down):
    n_exp, d, d_gu = w_gate_up.shape
    d_ff = w_down.shape[1]
    n_sub = d // LANES
    tm = TM_EXP
    max_tiles = tile_expert.shape[0]
    by_expert = lambda j, te, blk, nt: (te[j], 0, 0)
    by_block = lambda j, te, blk, nt: (blk[j], 0)
    grid_spec = pltpu.PrefetchScalarGridSpec(
        num_scalar_prefetch=3,
        grid=(max_tiles,),
        in_specs=[
            pl.BlockSpec((tm * n_sub, LANES), by_block),
            pl.BlockSpec((1, d, d_gu), by_expert),
            pl.BlockSpec((1, 1, d_gu), by_expert),
            pl.BlockSpec((1, d_ff, d), by_expert),
            pl.BlockSpec((1, 1, d), by_expert),
        ],
        out_specs=pl.BlockSpec((tm * n_sub, LANES), by_block),
        scratch_shapes=[
            pltpu.VMEM((d, d_gu), BF16),
            pltpu.VMEM((d_ff, d), BF16),
        ],
    )
    return pl.pallas_call(
        functools.partial(_experts_kernel, n_sub=n_sub),
        grid_spec=grid_spec,
        out_shape=jax.ShapeDtypeStruct((max_tiles * tm * n_sub, LANES), F32),
        compiler_params=_cparams(("arbitrary",)),
        name="experts",
    )(tile_expert, tile_block, n_tiles, xs_tiles,
      w_gate_up, b_gate_up.reshape(n_exp, 1, d_gu), w_down, b_down.reshape(n_exp, 1, d))


def _combine_kernel(pos_ref, y_hbm, prob_ref, x1_ref, mod_ref, g_ref, o_ref, ybuf, sem,
                    *, n_sub, n_tokens):
    i = pl.program_id(0)
    n_steps = pl.num_programs(0)
    tm = TM_COMB
    slot = i % 2

    def gather(step, s):
        for r in range(tm):
            for k in range(TOP_K):
                row = pos_ref[k * n_tokens + step * tm + r]
                pltpu.make_async_copy(
                    y_hbm.at[pl.ds(pl.multiple_of(row * n_sub, n_sub), n_sub), :],
                    ybuf.at[s, k, pl.ds(r * n_sub, n_sub), :], sem.at[s]).start()

    @pl.when(i == 0)
    def _():
        gather(0, 0)

    @pl.when(i + 1 < n_steps)
    def _():
        gather(i + 1, 1 - slot)

    for k in range(TOP_K):
        pltpu.make_async_copy(y_hbm.at[pl.ds(0, tm * n_sub), :], ybuf.at[slot, k], sem.at[slot]).wait()

    prob = prob_ref[...]
    f = None
    for k in range(TOP_K):
        yk = jnp.concatenate(
            [ybuf[slot, k, pl.ds(s, tm, stride=n_sub), :] for s in range(n_sub)], axis=1)
        term = yk * prob[:, k:k + 1]
        f = term if f is None else f + term
    gate_f = mod_ref[0, 5:6, :]
    o_ref[...] = x1_ref[...] + gate_f * (_rms(f) * g_ref[...])


def _combine(pos_flat, y_tiles, probs, x1, mod3, g_post_ffn, seq):
    t, d = x1.shape
    tm = TM_COMB
    n_sub = d // LANES
    steps_per_seq = seq // tm
    grid_spec = pltpu.PrefetchScalarGridSpec(
        num_scalar_prefetch=1,
        grid=(t // tm,),
        in_specs=[
            pl.BlockSpec(memory_space=pl.ANY),
            pl.BlockSpec((tm, TOP_K), lambda i, pos: (i, 0)),
            pl.BlockSpec((tm, d), lambda i, pos: (i, 0)),
            pl.BlockSpec((1, N_MOD, d), lambda i, pos: (i // steps_per_seq, 0, 0)),
            pl.BlockSpec((1, d), lambda i, pos: (0, 0)),
        ],
        out_specs=pl.BlockSpec((tm, d), lambda i, pos: (i, 0)),
        scratch_shapes=[
            pltpu.VMEM((2, TOP_K, tm * n_sub, LANES), F32),
            pltpu.SemaphoreType.DMA((2,)),
        ],
    )
    return pl.pallas_call(
        functools.partial(_combine_kernel, n_sub=n_sub, n_tokens=t),
        grid_spec=grid_spec,
        out_shape=jax.ShapeDtypeStruct((t, d), F32),
        compiler_params=_cparams(("arbitrary",)),
        name="combine",
    )(pos_flat, y_tiles, probs, x1, mod3, g_post_ffn)


def _rope_freq_row():
    half = ROPE_DIM // 2
    inv_freq = ROPE_THETA ** (-jnp.arange(0, ROPE_DIM, 2, dtype=F32) / ROPE_DIM)
    head = jnp.concatenate([inv_freq, inv_freq, jnp.zeros((DA_HEAD_DIM - 2 * half,), F32)])
    return jnp.tile(head, LANES // DA_HEAD_DIM).reshape(1, LANES)


def _tile_tables(counts, block_tab, n_tokens):
    tm = TM_EXP
    n_exp = counts.shape[0]
    max_tiles = (n_tokens * TOP_K) // tm + n_exp
    tiles = (counts + tm - 1) // tm
    tile_end = jnp.cumsum(tiles)
    n_tiles = tile_end[-1]
    j = jnp.arange(max_tiles, dtype=jnp.int32)
    jj = jnp.minimum(j, n_tiles - 1)
    past = (jj[:, None] >= tile_end[None, :]).astype(jnp.int32)
    tile_expert = jnp.sum(past, axis=1)
    nth = jj - jnp.sum(past * tiles[None, :], axis=1)
    tile_block = jnp.where(j < n_tiles, block_tab[tile_expert, nth], j)
    return tile_expert, tile_block.astype(jnp.int32), n_tiles.reshape(1).astype(jnp.int32)


def kernel(x, c, positions, w_mod, b_mod, g_pre_mix, w_in, lambda_q1, lambda_k1, lambda_q2, lambda_k2, g_sub, w_pool_group, pool_scale, w_att_branch, w_pool_branch, w_out, g_post_mix, g_pre_ffn, w_router, b_router, w_gate_up, b_gate_up, w_down, b_down, g_post_ffn):
    bsz, seq, d = x.shape
    assert w_mod.shape[0] == 1, "single layer"
    assert seq % TQ == 0 and seq % TM_PROJ == 0 and seq % TM_MIX == 0 and seq % TM_COMB == 0
    t = bsz * seq
    assert t % TM_EXP == 0
    x2 = x.reshape(t, d)
    qk_w = 2 * DA_HEADS * DA_HEAD_DIM
    v_w = DA_HEADS * DA_V_DIM
    pool_w = len(POOL_WINDOWS) * POOL_GROUP_DIM
    n_experts = w_router.shape[2]

    mod3 = _mod(c, w_mod[0], b_mod[0]).reshape(bsz, N_MOD, d)

    bounds = [0, qk_w, 2 * qk_w, 2 * qk_w + v_w, 2 * qk_w + v_w + pool_w,
              2 * qk_w + v_w + pool_w + d, 2 * qk_w + v_w + pool_w + 2 * d]
    q, k, vt, p, sga, sgp = _in_proj(x2, mod3, g_pre_mix, positions.reshape(t, 1), _rope_freq_row(),
                                     w_in[0], bounds, seq)

    row64 = lambda a: a.reshape(1, DA_HEAD_DIM)
    o = _attention(q, k, vt, row64(lambda_q1[0]), row64(lambda_k1[0]), row64(lambda_q2[0]),
                   row64(lambda_k2[0]), g_sub.reshape(DA_V_DIM, 1), bsz, seq)

    x1, probs_t, pos_t, counts, block_tab, xs_tiles = _post_mix(
        o, p, sga, sgp, x2, mod3, w_att_branch[0], w_pool_group[0].reshape(pool_w, POOL_GROUP_DIM),
        pool_scale, w_pool_branch[0], w_out[0],
        g_post_mix, g_pre_ffn, w_router[0].T, b_router.reshape(n_experts, 1), seq)

    tile_expert, tile_block, n_tiles = _tile_tables(counts[:, 0], block_tab, t)
    y_tiles = _experts(tile_expert, tile_block, n_tiles, xs_tiles,
                       w_gate_up[0], b_gate_up[0], w_down[0], b_down[0])
    out = _combine(pos_t.reshape(-1), y_tiles, probs_t.T, x1, mod3, g_post_ffn, seq)
    return out.reshape(bsz, seq, d)
```

```python
import functools

import jax
import jax.numpy as jnp
from jax import lax
from jax.experimental import pallas as pl
from jax.experimental.pallas import tpu as pltpu

F32 = jnp.float32
BF16 = jnp.bfloat16

NORM_EPS = 1e-6
CHUNK = 64
DA_HEADS = 8
DA_HEAD_DIM = 64
DA_V_DIM = 2 * DA_HEAD_DIM
ROPE_THETA = 500000.0
ROPE_DIM = DA_HEAD_DIM // 4
POOL_WINDOWS = (2, 4, 8, 16)
POOL_GROUP_DIM = 128
TOP_K = 4
SWIGLU_LIMIT = 7.0
SWIGLU_ALPHA = 1.702
N_MOD = 6
LAMBDA_INIT = 0.8 - 0.6 * 1.0
Q_SCALE = (DA_HEAD_DIM ** -0.5) * 1.4426950408889634

LANES = 128
SUBLANES = 8
VMEM_LIMIT = 56 * 1024 * 1024

TM_PROJ = 512
TQ = 512
TK = 256
ATT_HEADS_PER_STEP = 2
ATT_SUM_ROWS = 16
TM_MIX = 512
TM_EXP = 512
TM_COMB = 256
WEIGHT_CHUNK = 512
POOL_HALO = 16
NEG_BIG = -1e30


def _cparams(sem):
    return pltpu.CompilerParams(dimension_semantics=sem, vmem_limit_bytes=VMEM_LIMIT)


def _split(a):
    hi = a.astype(BF16)
    return hi, (a - hi.astype(F32)).astype(BF16)


def _split_dot(a, b, dims=(((1,), (0,)), ((), ()))):
    a_hi, a_lo = _split(a)
    b_hi, b_lo = _split(b)
    dot = lambda x, y: lax.dot_general(x, y, dims, preferred_element_type=F32)
    return dot(a_hi, b_hi) + (dot(a_hi, b_lo) + dot(a_lo, b_hi))


def _rms(x):
    return x * lax.rsqrt(jnp.mean(x * x, axis=-1, keepdims=True) + NORM_EPS)


def _mod_kernel(c_ref, w_ref, b_ref, o_ref):
    c = c_ref[...]
    c_act = c * jax.nn.sigmoid(c)
    o_ref[...] = _split_dot(c_act, w_ref[...]) + b_ref[...]


def _mod(c, w_mod, b_mod):
    bsz, d = c.shape
    n = w_mod.shape[1]
    tn = 1024
    return pl.pallas_call(
        _mod_kernel,
        grid=(n // tn,),
        in_specs=[
            pl.BlockSpec((bsz, d), lambda j: (0, 0)),
            pl.BlockSpec((d, tn), lambda j: (0, j)),
            pl.BlockSpec((1, tn), lambda j: (0, j)),
        ],
        out_specs=pl.BlockSpec((bsz, tn), lambda j: (0, j)),
        out_shape=jax.ShapeDtypeStruct((bsz, n), F32),
        compiler_params=_cparams(("parallel",)),
        name="mod",
    )(c, w_mod, b_mod.reshape(1, n))


def _rope(t, cos_t, sin_a, sin_b):
    n = t.shape[1]
    up = pltpu.roll(t, n - ROPE_DIM // 2, axis=1)
    dn = pltpu.roll(t, ROPE_DIM // 2, axis=1)
    reps = n // LANES
    tile = lambda a: jnp.concatenate([a] * reps, axis=1)
    return t * tile(cos_t) + up * tile(sin_a) + dn * tile(sin_b)


def _stream_windows(src_hbm, windows, stage, sem, sink):
    def copy(n):
        r0, nr, c0, nc = windows[n]
        return pltpu.make_async_copy(src_hbm.at[pl.ds(r0, nr), pl.ds(c0, nc)],
                                     stage.at[n % 2, pl.ds(0, nr), pl.ds(0, nc)], sem.at[n % 2])
    copy(0).start()
    for n, (_, nr, _, nc) in enumerate(windows):
        if n + 1 < len(windows):
            copy(n + 1).start()
        copy(n).wait()
        sink(n, stage[n % 2, 0:nr, 0:nc])


def _in_proj_kernel(x_ref, mod_ref, g_ref, pos_ref, freq_ref, w_hbm,
                    q_ref, k_ref, v_ref, p_ref, sga_ref, sgp_ref,
                    w_ref, wvt_ref, stage, sem, *, bounds):
    @pl.when(pl.program_id(0) == 0)
    def _():
        d_in, width = w_hbm.shape
        windows = [(0, d_in, c0, WEIGHT_CHUNK) for c0 in range(0, width, WEIGHT_CHUNK)]

        def sink(n, tile):
            c0 = windows[n][2]
            w_ref[:, c0:c0 + WEIGHT_CHUNK] = tile.astype(BF16)
            if bounds[2] <= c0 < bounds[3]:
                wvt_ref[c0 - bounds[2]:c0 - bounds[2] + WEIGHT_CHUNK, :] = tile.T.astype(BF16)
        _stream_windows(w_hbm, windows, stage, sem, sink)

    x = x_ref[...]
    shift = mod_ref[0, 0:1, :]
    scale = mod_ref[0, 1:2, :]
    u = (_rms(x) * g_ref[...]) * (1.0 + scale) + shift
    ub = u.astype(BF16)
    dot = functools.partial(jnp.dot, preferred_element_type=F32)
    part = lambda n: w_ref[:, bounds[n]:bounds[n + 1]]

    ang = pos_ref[...].astype(F32) * freq_ref[...]
    cos_t, sn = jnp.cos(ang), jnp.sin(ang)
    in_head = lax.broadcasted_iota(jnp.int32, ang.shape, 1) % DA_HEAD_DIM
    sin_a = jnp.where(in_head < ROPE_DIM // 2, -sn, 0.0)
    sin_b = jnp.where(in_head >= ROPE_DIM // 2, sn, 0.0)

    q = _rope(dot(ub, part(0)), cos_t, sin_a, sin_b)
    q_ref[...] = (q * Q_SCALE).astype(BF16)
    k = _rope(dot(ub, part(1)), cos_t, sin_a, sin_b)
    k_ref[...] = k.astype(BF16)
    vt = lax.dot_general(wvt_ref[...], ub, (((1,), (1,)), ((), ())), preferred_element_type=F32)
    for n in range(v_ref.shape[0]):
        v_ref[n] = vt[:, n * TK:(n + 1) * TK].astype(BF16)
    p_ref[...] = dot(ub, part(3))
    sga_ref[...] = jax.nn.sigmoid(dot(ub, part(4))).astype(BF16)
    sgp_ref[...] = jax.nn.sigmoid(dot(ub, part(5))).astype(BF16)


def _in_proj(x2, mod3, g_pre, pos_col, freq_row, w_in, bounds, seq):
    t, d = x2.shape
    width = w_in.shape[1]
    assert width % WEIGHT_CHUNK == 0 and all(b % WEIGHT_CHUNK == 0 for b in bounds[2:4])
    tm = TM_PROJ
    assert tm % TK == 0, "v is emitted as transposed (channels, TK) slabs"
    steps_per_seq = seq // tm
    widths = [b - a for a, b in zip(bounds[:-1], bounds[1:])]
    row = lambda w: pl.BlockSpec((tm, w), lambda i: (i, 0))
    full = lambda a: pl.BlockSpec(a.shape, lambda i: (0, 0))
    out_specs = [row(widths[0]), row(widths[1]),
                 pl.BlockSpec((tm // TK, widths[2], TK), lambda i: (i, 0, 0)),
                 row(widths[3]), row(widths[4]), row(widths[5])]
    out_shape = [jax.ShapeDtypeStruct((t, widths[0]), BF16),
                 jax.ShapeDtypeStruct((t, widths[1]), BF16),
                 jax.ShapeDtypeStruct((t // TK, widths[2], TK), BF16),
                 jax.ShapeDtypeStruct((t, widths[3]), F32),
                 jax.ShapeDtypeStruct((t, widths[4]), BF16),
                 jax.ShapeDtypeStruct((t, widths[5]), BF16)]
    return pl.pallas_call(
        functools.partial(_in_proj_kernel, bounds=tuple(bounds)),
        grid=(t // tm,),
        in_specs=[
            row(d),
            pl.BlockSpec((1, N_MOD, d), lambda i: (i // steps_per_seq, 0, 0)),
            full(g_pre), row(1), full(freq_row), pl.BlockSpec(memory_space=pl.ANY),
        ],
        out_specs=out_specs,
        out_shape=out_shape,
        scratch_shapes=[
            pltpu.VMEM((d, width), BF16),
            pltpu.VMEM((widths[2], d), BF16),
            pltpu.VMEM((2, d, WEIGHT_CHUNK), F32),
            pltpu.SemaphoreType.DMA((2,)),
        ],
        compiler_params=_cparams(("arbitrary",)),
        name="in_proj",
    )(x2, mod3, g_pre, pos_col, freq_row, w_in)


def _attn_kernel(lq1_ref, lk1_ref, lq2_ref, lk2_ref, g_ref, q_ref, k_ref, vt_ref, o_ref,
                 s_a, s_b, p_a, p_b, acc_buf):
    assert TQ == 2 * TK and TK % CHUNK == 0
    seq = q_ref.shape[0]
    lam = (jnp.exp(jnp.sum(lq1_ref[...] * lk1_ref[...], axis=-1, keepdims=True))
           - jnp.exp(jnp.sum(lq2_ref[...] * lk2_ref[...], axis=-1, keepdims=True))
           + LAMBDA_INIT)
    dot = functools.partial(jnp.dot, preferred_element_type=F32)
    heads = range(ATT_HEADS_PER_STEP)
    lanes = lambda h: slice(h * LANES, (h + 1) * LANES)

    def q_tile(qi, c):
        q_rows = pl.ds(pl.multiple_of(qi * TQ, TQ), TQ)
        qqt = []
        for h in heads:
            qt = q_ref[q_rows, lanes(h)].astype(F32).T
            row = lax.broadcasted_iota(jnp.int32, qt.shape, 0)
            zero = jnp.zeros_like(qt)
            qqt.append(jnp.concatenate([jnp.where(row < DA_HEAD_DIM, qt, zero),
                                        jnp.where(row >= DA_HEAD_DIM, qt, zero)],
                                       axis=1).astype(BF16))

        def scores(j, h):
            return dot(k_ref[pl.ds(pl.multiple_of(j * TK, TK), TK), lanes(h)], qqt[h])

        def softmax_step(s, m):
            m_new = jnp.maximum(m, jnp.max(s, axis=0, keepdims=True))
            return m_new, jnp.exp2(m - m_new), jnp.exp2(s - m_new).astype(BF16)

        ones_rows = jnp.ones((ATT_SUM_ROWS, TK), BF16)

        def pv(j, h, p):
            return dot(jnp.concatenate([vt_ref[j, lanes(h), :], ones_rows], axis=0), p)

        kk = lax.broadcasted_iota(jnp.int32, (TK, 2 * TQ), 0)
        qq = lax.broadcasted_iota(jnp.int32, (TK, 2 * TQ), 1)
        rel_chunk = jnp.where(qq >= TQ, qq - TQ, qq) // CHUNK - kk // CHUNK

        def masked(s, j):
            return jnp.where(rel_chunk >= j * (TK // CHUNK) - qi * (TQ // CHUNK), s, NEG_BIG)

        def step(j, carries, s_cur, p_cur, s_nxt, p_prev):
            pend = [pv(jnp.maximum(j - 1, 0), h, p_prev[h]) for h in heads]
            for h in heads:
                s_nxt[h] = scores(j + 1, h)
            out = []
            for h in heads:
                m, alpha = carries[h]
                acc_buf[h] = alpha * acc_buf[h] + pend[h]
                m, alpha, p = softmax_step(s_cur[h], m)
                p_cur[h] = p
                out.append((m, alpha))
            return tuple(out)

        def pair(i, carries):
            carries = step(2 * i, carries, s_a, p_a, s_b, p_b)
            return step(2 * i + 1, carries, s_b, p_b, s_a, p_a)

        for h in heads:
            s_a[h] = scores(0, h)
            p_b[h] = jnp.zeros((TK, 2 * TQ), BF16)
            acc_buf[h] = jnp.zeros(acc_buf.shape[1:], F32)
        init = tuple((jnp.full((1, 2 * TQ), NEG_BIG, F32), jnp.ones((1, 2 * TQ), F32))
                     for _ in heads)
        carries = lax.fori_loop(0, qi, pair, init)
        ja = 2 * qi
        late = lambda a: jnp.concatenate([a[:, TK:TQ], a[:, TQ + TK:]], axis=1)
        pend = [pv(jnp.maximum(ja - 1, 0), h, p_b[h]) for h in heads]
        for h in heads:
            k_last = k_ref[pl.ds(pl.multiple_of((ja + 1) * TK, TK), TK), lanes(h)]
            s_b[h, :, 0:TQ] = dot(k_last, late(qqt[h]))
        mid = []
        for h in heads:
            m, alpha = carries[h]
            acc_buf[h] = alpha * acc_buf[h] + pend[h]
            m, alpha, p = softmax_step(masked(s_a[h], ja), m)
            p_a[h] = p
            mid.append((m, alpha))
        pend = [pv(ja, h, p_a[h]) for h in heads]
        k_in = lax.broadcasted_iota(jnp.int32, (TK, TQ), 0)
        q_in = lax.broadcasted_iota(jnp.int32, (TK, TQ), 1) % TK
        diagonal = k_in // CHUNK <= q_in // CHUNK
        for h in heads:
            m, alpha = mid[h]
            acc = alpha * acc_buf[h] + pend[h]
            m_l, alpha_l, p_l = softmax_step(jnp.where(diagonal, s_b[h, :, 0:TQ], NEG_BIG), late(m))
            acc_l = alpha_l * late(acc) + pv(ja + 1, h, p_l)
            acc = jnp.concatenate([acc[:, :TK], acc_l[:, :TK], acc[:, TQ:TQ + TK], acc_l[:, TK:]], axis=1)
            o = acc[:DA_V_DIM] / acc[DA_V_DIM:DA_V_DIM + 1]
            a = o[:, :TQ] - lam * o[:, TQ:]
            y = a * lax.rsqrt(jnp.mean(a * a, axis=0, keepdims=True) + NORM_EPS)
            y = (y * g_ref[...]) * (1.0 - LAMBDA_INIT)
            o_ref[q_rows, lanes(h)] = y.T.astype(BF16)
        return c

    lax.fori_loop(0, seq // TQ, q_tile, 0)


def _post_mix_kernel(o_ref, p_ref, ph_ref, sga_ref, sgp_ref, x_ref, mod_ref,
                     watt_hbm, wpg_hbm, ps_ref, wpb_hbm, wout_hbm, gpm_ref, gpf_ref,
                     wrt_ref, br_ref,
                     x1_ref, prob_ref, pos_ref, cnt_ref, tab_ref, xs_hbm,
                     carry_ref, cur_ref, free_ref, u2t, zeros_v, pos_v, pos_s, meta_v, meta_s,
                     watt_ref, wpg_ref, wpb_ref, wout_ref, stage, sem_w,
                     sem_rows, sem_pos, sem_zero,
                     *, i, n_steps, steps_per_seq, n_blocks):
    tm = x_ref.shape[0]
    n_experts = wrt_ref.shape[0]
    n_sub = x_ref.shape[1] // LANES
    slot = i % 2
    prev = 1 - slot
    dot = functools.partial(jnp.dot, preferred_element_type=F32)

    def pos_copy(s):
        return pltpu.make_async_copy(pos_v.at[s], pos_s.at[s], sem_pos.at[s])

    def row_copy(s, k, r):
        dst = pos_s[s, k, r]
        return pltpu.make_async_copy(
            u2t.at[s, pl.ds(r * n_sub, n_sub), :],
            xs_hbm.at[pl.ds(pl.multiple_of(dst * n_sub, n_sub), n_sub), :], sem_rows.at[s])

    def wait_rows(s):
        for _ in range(TOP_K):
            pltpu.make_async_copy(u2t.at[s], xs_hbm.at[pl.ds(0, tm * n_sub), :], sem_rows.at[s]).wait()

    @pl.when(i == 0)
    def _():
        carry_ref[...] = jnp.zeros_like(carry_ref)
        cur_ref[...] = jnp.zeros_like(cur_ref)
        free_ref[...] = jnp.zeros_like(free_ref)
        tab_ref[...] = jnp.zeros_like(tab_ref)
        zeros_v[...] = jnp.zeros_like(zeros_v)
        for src, dst in ((watt_hbm, watt_ref), (wpg_hbm, wpg_ref), (wpb_hbm, wpb_ref),
                         (wout_hbm, wout_ref)):
            rows, cols = src.shape
            step_c = min(cols, WEIGHT_CHUNK)
            windows = [(0, rows, c0, step_c) for c0 in range(0, cols, step_c)]

            def sink(n, tile, dst=dst, windows=windows):
                c0, nc = windows[n][2], windows[n][3]
                dst[:, c0:c0 + nc] = tile.astype(BF16)
            _stream_windows(src, windows, stage, sem_w, sink)
        u2t[1] = jnp.zeros(u2t.shape[1:], F32)
        spare = (n_blocks * TM_EXP
                 + lax.broadcasted_iota(jnp.int32, (TOP_K, tm), 0) * tm
                 + lax.broadcasted_iota(jnp.int32, (TOP_K, tm), 1))
        pos_v[1] = spare
        pos_copy(1).start()

    pos_copy(prev).wait()

    for r in range(tm):
        for k in range(TOP_K):
            row_copy(prev, k, r).start()

    y_att = dot(o_ref(), watt_ref[...])

    first = (i % steps_per_seq) == 0
    halo = jnp.where(first, 0.0, ph_ref[...])
    ext = jnp.concatenate([halo, p_ref[...]], axis=0)
    t_in_seq = (i % steps_per_seq) * tm + lax.broadcasted_iota(jnp.int32, (tm, 1), 0)
    pooled = []
    for g, w in enumerate(POOL_WINDOWS):
        e = ext[:, g * POOL_GROUP_DIM:(g + 1) * POOL_GROUP_DIM]
        acc, span = e, 1
        while span < w:
            acc = acc[span:] + acc[:-span]
            span *= 2
        win = acc[POOL_HALO - (w - 1):]
        cnt = jnp.minimum(t_in_seq + 1, w).astype(F32)
        mixed = win / cnt - e[POOL_HALO:]
        pooled.append(dot(mixed.astype(BF16), wpg_ref[g * POOL_GROUP_DIM:(g + 1) * POOL_GROUP_DIM, :]))
    y_pool_in = jnp.concatenate(pooled, axis=1) * ps_ref[...]
    y_pool = dot(y_pool_in.astype(BF16), wpb_ref[...])

    merged = sga_ref[...].astype(F32) * y_att + sgp_ref[...].astype(F32) * y_pool
    mix_out = dot(merged.astype(BF16), wout_ref[...])
    gate_m = mod_ref[0, 2:3, :]
    shift_f = mod_ref[0, 3:4, :]
    scale_f = mod_ref[0, 4:5, :]
    x1 = x_ref[...] + gate_m * (_rms(mix_out) * gpm_ref[...])
    x1_ref[...] = x1
    u2 = (_rms(x1) * gpf_ref[...]) * (1.0 + scale_f) + shift_f

    logits = _split_dot(wrt_ref[...], u2, (((1,), (1,)), ((), ()))) + br_ref[...]
    erow = lax.broadcasted_iota(jnp.int32, logits.shape, 0)
    work = logits
    vals, idxs = [], []
    for _ in range(TOP_K):
        mx = jnp.max(work, axis=0, keepdims=True)
        ix = jnp.min(jnp.where(work == mx, erow, n_experts), axis=0, keepdims=True)
        vals.append(mx)
        idxs.append(ix)
        work = jnp.where(erow == ix, -jnp.inf, work)
    exps = [jnp.exp(vv - vals[0]) for vv in vals]
    denom = exps[0] + exps[1] + exps[2] + exps[3]
    prob_ref[...] = jnp.concatenate([e / denom for e in exps], axis=0)

    onehot = jnp.zeros(logits.shape, F32)
    for ix in idxs:
        onehot = onehot + (erow == ix).astype(F32)
    rr = lax.broadcasted_iota(jnp.int32, (tm, tm), 0)
    cc = lax.broadcasted_iota(jnp.int32, (tm, tm), 1)
    earlier = (rr < cc).astype(BF16)
    carry = carry_ref[...]
    before = dot(onehot.astype(BF16), earlier) + carry

    blk_rows = float(TM_EXP)
    total = carry + jnp.sum(onehot, axis=1, keepdims=True)
    blocks_old = jnp.floor((carry + (blk_rows - 1.0)) / blk_rows)
    opened = jnp.floor((total + (blk_rows - 1.0)) / blk_rows) - blocks_old
    e_r = lax.broadcasted_iota(jnp.int32, (n_experts, n_experts), 0)
    e_c = lax.broadcasted_iota(jnp.int32, (n_experts, n_experts), 1)
    lower = (e_c < e_r).astype(BF16)
    opened_before = dot(lower, jnp.broadcast_to(opened, (n_experts, LANES)).astype(BF16))[:, 0:1]
    new_blk = free_ref[...] + opened_before
    cur_blk = cur_ref[...]
    boundary = blocks_old * blk_rows
    blk_of = jnp.where(before < boundary, cur_blk, new_blk)
    pos_rows = []
    for ix in idxs:
        pick = erow == ix
        rank = jnp.sum(jnp.where(pick, before, 0.0), axis=0, keepdims=True)
        blk_id = jnp.sum(jnp.where(pick, blk_of, 0.0), axis=0, keepdims=True)
        within = rank - jnp.floor(rank / blk_rows) * blk_rows
        pos_rows.append((blk_id * blk_rows + within).astype(jnp.int32))
    pos = jnp.concatenate(pos_rows, axis=0)
    pos_ref[...] = pos
    tab_col = lax.broadcasted_iota(jnp.int32, tab_ref.shape, 1).astype(F32)
    tab_ref[...] = jnp.where(jnp.logical_and(tab_col == blocks_old, opened > 0.0),
                             new_blk.astype(jnp.int32), tab_ref[...])
    cur_ref[...] = jnp.where(opened > 0.0, new_blk, cur_blk)
    free_ref[...] = free_ref[...] + jnp.sum(opened, axis=0, keepdims=True)
    carry_ref[...] = total
    cnt_ref[...] = total.astype(jnp.int32)

    @pl.when(i > 0)
    def _():
        wait_rows(slot)

    for s in range(n_sub):
        u2t[slot, pl.ds(s, tm, stride=n_sub), :] = u2[:, s * LANES:(s + 1) * LANES]
    pos_v[slot] = pos
    pos_copy(slot).start()

    @pl.when(i == n_steps - 1)
    def _():
        pos_copy(slot).wait()

        def issue(r, c):
            for k in range(TOP_K):
                row_copy(slot, k, r).start()
            return c
        lax.fori_loop(0, tm, issue, 0)

        eye = e_r == e_c
        as_row = lambda col: jnp.sum(jnp.where(eye, col, 0.0), axis=0, keepdims=True).astype(jnp.int32)
        used = total - jnp.floor(total / blk_rows) * blk_rows
        meta = jnp.concatenate([as_row(used), as_row(cur_ref[...]),
                                jnp.broadcast_to(free_ref[...].astype(jnp.int32), (1, n_experts))], axis=0)
        meta_v[...] = jnp.zeros_like(meta_v)
        meta_v[0:3, 0:n_experts] = meta
        meta_copy = pltpu.make_async_copy(meta_v, meta_s, sem_zero)
        meta_copy.start()
        meta_copy.wait()

        sizes = [TM_EXP >> (b + 1) for b in range(TM_EXP.bit_length() - 1)]
        zero_run = lambda row0, n: pltpu.make_async_copy(
            zeros_v.at[pl.ds(0, n * n_sub), :],
            xs_hbm.at[pl.ds(pl.multiple_of(row0 * n_sub, n_sub), n * n_sub), :], sem_zero)
        issued = [jnp.int32(0) for _ in sizes]
        for e in range(n_experts):
            first = meta_s[0, e]
            row0 = meta_s[1, e] * TM_EXP + first
            pad = jnp.where(first > 0, TM_EXP - first, 0)
            for b, n in enumerate(sizes):
                take = (pad & n) != 0

                @pl.when(take)
                def _(row0=row0, n=n):
                    zero_run(row0, n).start()
                issued[b] = issued[b] + take.astype(jnp.int32)
                row0 = row0 + jnp.where(take, n, 0)
        n_open = meta_s[2, 0]

        def whole(b, c):
            zero_run(b * TM_EXP, TM_EXP).start()
            return c
        lax.fori_loop(n_open, n_blocks, whole, 0)

        for b, n in enumerate(sizes):
            lax.fori_loop(0, issued[b], lambda _, c, n=n: (zero_run(0, n).wait(), c)[1], 0)
        lax.fori_loop(n_open, n_blocks, lambda _, c: (zero_run(0, TM_EXP).wait(), c)[1], 0)
        wait_rows(prev)
        wait_rows(slot)


def _attn_mix_kernel(lq1_ref, lk1_ref, lq2_ref, lk2_ref, gsub_ref, q_ref, k_ref, vt_ref,
                     p_ref, ph_ref, sga_ref, sgp_ref, x_ref, mod_ref,
                     watt_hbm, wpg_hbm, ps_ref, wpb_hbm, wout_hbm, gpm_ref, gpf_ref, wrt_ref, br_ref,
                     x1_ref, prob_ref, pos_ref, cnt_ref, tab_ref, xs_hbm,
                     s_a, s_b, p_a, p_b, acc_buf, o_buf, *mix_scratch,
                     n_batches, steps_per_seq, n_blocks):
    b = pl.program_id(0)
    g = pl.program_id(1)
    tm = x_ref.shape[0]

    @pl.when(b < n_batches)
    def _():
        _attn_kernel(lq1_ref, lk1_ref, lq2_ref, lk2_ref, gsub_ref, q_ref, k_ref, vt_ref,
                     o_buf.at[b % 2, g], s_a, s_b, p_a, p_b, acc_buf)

    @pl.when(b >= 1)
    def _():
        def o_rows():
            rows = pl.ds(pl.multiple_of(g * tm, tm), tm)
            return jnp.concatenate([o_buf[(b - 1) % 2, hg, rows, :] for hg in range(o_buf.shape[1])],
                                   axis=1)
        _post_mix_kernel(o_rows, p_ref, ph_ref, sga_ref, sgp_ref, x_ref, mod_ref,
                         watt_hbm, wpg_hbm, ps_ref, wpb_hbm, wout_hbm, gpm_ref, gpf_ref,
                         wrt_ref, br_ref, x1_ref, prob_ref, pos_ref, cnt_ref, tab_ref, xs_hbm,
                         *mix_scratch,
                         i=(b - 1) * steps_per_seq + g, n_steps=n_batches * steps_per_seq,
                         steps_per_seq=steps_per_seq, n_blocks=n_blocks)


def _attn_mix(q, k, vt, lq1, lk1, lq2, lk2, g_col, p, sga, sgp, x2, mod3, w_att, w_pg, pool_scale,
              w_pb, w_out, g_post_mix, g_pre_ffn, w_router_t, b_router_col, bsz, seq):
    t, d = x2.shape
    tm = TM_MIX
    steps_per_seq = seq // tm
    n_groups = DA_HEADS // ATT_HEADS_PER_STEP
    assert steps_per_seq == n_groups, "one post-mix tile rides on every attention step"
    n_experts = w_router_t.shape[0]
    pw = p.shape[1]
    n_sub = d // LANES
    nkv = seq // TK
    hw = ATT_HEADS_PER_STEP * LANES
    halo_blocks = tm // POOL_HALO
    last = bsz - 1
    tile = lambda b, g: jnp.where(b == 0, 0, (b - 1) * steps_per_seq + g)
    row = lambda w: pl.BlockSpec((tm, w), lambda b, g: (tile(b, g), 0))
    col = lambda h: pl.BlockSpec((h, tm), lambda b, g: (0, tile(b, g)))
    full2 = lambda a: pl.BlockSpec(a.shape, lambda b, g: (0, 0))
    hbm = pl.BlockSpec(memory_space=pl.ANY)
    assert tm <= TM_EXP and (tm * TOP_K) % TM_EXP == 0 and t % TM_EXP == 0
    n_blocks = (t * TOP_K) // TM_EXP + n_experts
    tab_w = -(-(t // TM_EXP) // LANES) * LANES
    kern = functools.partial(_attn_mix_kernel, n_batches=bsz, steps_per_seq=steps_per_seq,
                             n_blocks=n_blocks)
    xs_rows = n_blocks * TM_EXP + TOP_K * tm
    return pl.pallas_call(
        kern,
        grid=(bsz + 1, n_groups),
        in_specs=[
            full2(lq1), full2(lk1), full2(lq2), full2(lk2), full2(g_col),
            pl.BlockSpec((seq, hw), lambda b, g: (jnp.minimum(b, last), g)),
            pl.BlockSpec((seq, hw), lambda b, g: (jnp.minimum(b, last), g)),
            pl.BlockSpec((nkv, hw, TK), lambda b, g: (jnp.minimum(b, last), g, 0)),
            row(pw),
            pl.BlockSpec((POOL_HALO, pw),
                         lambda b, g: (jnp.maximum(tile(b, g) * halo_blocks - 1, 0), 0)),
            row(d), row(d), row(d),
            pl.BlockSpec((1, N_MOD, d), lambda b, g: (jnp.maximum(b - 1, 0), 0, 0)),
            hbm, hbm, full2(pool_scale), hbm, hbm, full2(g_post_mix), full2(g_pre_ffn),
            full2(w_router_t), full2(b_router_col),
        ],
        out_specs=[
            row(d), col(TOP_K), col(TOP_K),
            pl.BlockSpec((n_experts, 1), lambda b, g: (0, 0)),
            pl.BlockSpec((n_experts, tab_w), lambda b, g: (0, 0)),
            pl.BlockSpec(memory_space=pl.ANY),
        ],
        out_shape=[
            jax.ShapeDtypeStruct((t, d), F32),
            jax.ShapeDtypeStruct((TOP_K, t), F32),
            jax.ShapeDtypeStruct((TOP_K, t), jnp.int32),
            jax.ShapeDtypeStruct((n_experts, 1), jnp.int32),
            jax.ShapeDtypeStruct((n_experts, tab_w), jnp.int32),
            jax.ShapeDtypeStruct((xs_rows * n_sub, LANES), F32),
        ],
        scratch_shapes=[
            pltpu.VMEM((ATT_HEADS_PER_STEP, TK, 2 * TQ), F32),
            pltpu.VMEM((ATT_HEADS_PER_STEP, TK, 2 * TQ), F32),
            pltpu.VMEM((ATT_HEADS_PER_STEP, TK, 2 * TQ), BF16),
            pltpu.VMEM((ATT_HEADS_PER_STEP, TK, 2 * TQ), BF16),
            pltpu.VMEM((ATT_HEADS_PER_STEP, DA_V_DIM + ATT_SUM_ROWS, 2 * TQ), F32),
            pltpu.VMEM((2, n_groups, seq, hw), BF16),
            pltpu.VMEM((n_experts, 1), F32),
            pltpu.VMEM((n_experts, 1), F32),
            pltpu.VMEM((1, 1), F32),
            pltpu.VMEM((2, tm * n_sub, LANES), F32),
            pltpu.VMEM((TM_EXP * n_sub, LANES), F32),
            pltpu.VMEM((2, TOP_K, tm), jnp.int32),
            pltpu.SMEM((2, TOP_K, tm), jnp.int32),
            pltpu.VMEM((SUBLANES, LANES), jnp.int32),
            pltpu.SMEM((SUBLANES, LANES), jnp.int32),
            pltpu.VMEM(w_att.shape, BF16),
            pltpu.VMEM(w_pg.shape, BF16),
            pltpu.VMEM(w_pb.shape, BF16),
            pltpu.VMEM(w_out.shape, BF16),
            pltpu.VMEM((2, max(w_att.shape[0], w_out.shape[0]), WEIGHT_CHUNK), F32),
            pltpu.SemaphoreType.DMA((2,)),
            pltpu.SemaphoreType.DMA((2,)),
            pltpu.SemaphoreType.DMA((2,)),
            pltpu.SemaphoreType.DMA,
        ],
        compiler_params=_cparams(("arbitrary", "arbitrary")),
        name="attn_mix",
    )(lq1, lk1, lq2, lk2, g_col, q, k, vt, p, p, sga, sgp, x2, mod3, w_att, w_pg, pool_scale,
      w_pb, w_out, g_post_mix, g_pre_ffn, w_router_t, b_router_col)


def _experts_kernel(te_ref, blk_ref, nt_ref, xs_ref, wgu_ref, bgu_ref, wd_ref, bd_ref,
                    y_ref, wgu_bf, wd_bf, *, n_sub):
    j = pl.program_id(0)
    tm = TM_EXP
    d_ff = wd_ref.shape[1]

    @pl.when(j < nt_ref[0])
    def _():
        changed = jnp.logical_or(j == 0, te_ref[j] != te_ref[jnp.maximum(j - 1, 0)])

        @pl.when(changed)
        def _():
            wgu_bf[...] = wgu_ref[0].astype(BF16)
            wd_bf[...] = wd_ref[0].astype(BF16)

        xs = jnp.concatenate(
            [xs_ref[pl.ds(s, tm, stride=n_sub), :] for s in range(n_sub)], axis=1).astype(BF16)
        gu = jnp.dot(xs, wgu_bf[...], preferred_element_type=F32) + bgu_ref[0]
        gate = jnp.minimum(gu[:, :d_ff], SWIGLU_LIMIT)
        up = jnp.clip(gu[:, d_ff:], -SWIGLU_LIMIT, SWIGLU_LIMIT)
        act = (up + 1.0) * (gate * jax.nn.sigmoid(SWIGLU_ALPHA * gate))
        y = jnp.dot(act.astype(BF16), wd_bf[...], preferred_element_type=F32) + bd_ref[0]
        for s in range(n_sub):
            y_ref[pl.ds(s, tm, stride=n_sub), :] = y[:, s * LANES:(s + 1) * LANES]

    @pl.when(j >= nt_ref[0])
    def _():
        y_ref[...] = jnp.zeros_like(y_ref)


def _experts(tile_expert, tile_block, n_tiles, xs_tiles, w_gate_up, b_gate_up, w_down, b_down):
    n_exp, d, d_gu = w_gate_up.shape
    d_ff = w_down.shape[1]
    n_sub = d // LANES
    tm = TM_EXP
    max_tiles = tile_expert.shape[0]
    by_expert = lambda j, te, blk, nt: (te[j], 0, 0)
    by_block = lambda j, te, blk, nt: (blk[j], 0)
    grid_spec = pltpu.PrefetchScalarGridSpec(
        num_scalar_prefetch=3,
        grid=(max_tiles,),
        in_specs=[
            pl.BlockSpec((tm * n_sub, LANES), by_block),
            pl.BlockSpec((1, d, d_gu), by_expert),
            pl.BlockSpec((1, 1, d_gu), by_expert),
            pl.BlockSpec((1, d_ff, d), by_expert),
            pl.BlockSpec((1, 1, d), by_expert),
        ],
        out_specs=pl.BlockSpec((tm * n_sub, LANES), by_block),
        scratch_shapes=[
            pltpu.VMEM((d, d_gu), BF16),
            pltpu.VMEM((d_ff, d), BF16),
        ],
    )
    return pl.pallas_call(
        functools.partial(_experts_kernel, n_sub=n_sub),
        grid_spec=grid_spec,
        out_shape=jax.ShapeDtypeStruct((max_tiles * tm * n_sub, LANES), F32),
        compiler_params=_cparams(("arbitrary",)),
        name="experts",
    )(tile_expert, tile_block, n_tiles, xs_tiles,
      w_gate_up, b_gate_up.reshape(n_exp, 1, d_gu), w_down, b_down.reshape(n_exp, 1, d))


def _combine_kernel(pos_ref, y_hbm, prob_ref, x1_ref, mod_ref, g_ref, o_ref, ybuf, sem,
                    *, n_sub, n_tokens):
    i = pl.program_id(0)
    n_steps = pl.num_programs(0)
    tm = TM_COMB
    slot = i % 2

    def gather(step, s):
        for r in range(tm):
            for k in range(TOP_K):
                row = pos_ref[k * n_tokens + step * tm + r]
                pltpu.make_async_copy(
                    y_hbm.at[pl.ds(pl.multiple_of(row * n_sub, n_sub), n_sub), :],
                    ybuf.at[s, k, pl.ds(r * n_sub, n_sub), :], sem.at[s]).start()

    @pl.when(i == 0)
    def _():
        gather(0, 0)

    @pl.when(i + 1 < n_steps)
    def _():
        gather(i + 1, 1 - slot)

    for k in range(TOP_K):
        pltpu.make_async_copy(y_hbm.at[pl.ds(0, tm * n_sub), :], ybuf.at[slot, k], sem.at[slot]).wait()

    prob = prob_ref[...]
    f = None
    for k in range(TOP_K):
        yk = jnp.concatenate(
            [ybuf[slot, k, pl.ds(s, tm, stride=n_sub), :] for s in range(n_sub)], axis=1)
        term = yk * prob[:, k:k + 1]
        f = term if f is None else f + term
    gate_f = mod_ref[0, 5:6, :]
    o_ref[...] = x1_ref[...] + gate_f * (_rms(f) * g_ref[...])


def _combine(pos_flat, y_tiles, probs, x1, mod3, g_post_ffn, seq):
    t, d = x1.shape
    tm = TM_COMB
    n_sub = d // LANES
    steps_per_seq = seq // tm
    grid_spec = pltpu.PrefetchScalarGridSpec(
        num_scalar_prefetch=1,
        grid=(t // tm,),
        in_specs=[
            pl.BlockSpec(memory_space=pl.ANY),
            pl.BlockSpec((tm, TOP_K), lambda i, pos: (i, 0)),
            pl.BlockSpec((tm, d), lambda i, pos: (i, 0)),
            pl.BlockSpec((1, N_MOD, d), lambda i, pos: (i // steps_per_seq, 0, 0)),
            pl.BlockSpec((1, d), lambda i, pos: (0, 0)),
        ],
        out_specs=pl.BlockSpec((tm, d), lambda i, pos: (i, 0)),
        scratch_shapes=[
            pltpu.VMEM((2, TOP_K, tm * n_sub, LANES), F32),
            pltpu.SemaphoreType.DMA((2,)),
        ],
    )
    return pl.pallas_call(
        functools.partial(_combine_kernel, n_sub=n_sub, n_tokens=t),
        grid_spec=grid_spec,
        out_shape=jax.ShapeDtypeStruct((t, d), F32),
        compiler_params=_cparams(("arbitrary",)),
        name="combine",
    )(pos_flat, y_tiles, probs, x1, mod3, g_post_ffn)


def _rope_freq_row():
    half = ROPE_DIM // 2
    inv_freq = ROPE_THETA ** (-jnp.arange(0, ROPE_DIM, 2, dtype=F32) / ROPE_DIM)
    head = jnp.concatenate([inv_freq, inv_freq, jnp.zeros((DA_HEAD_DIM - 2 * half,), F32)])
    return jnp.tile(head, LANES // DA_HEAD_DIM).reshape(1, LANES)


def _tile_tables(counts, block_tab, n_tokens):
    tm = TM_EXP
    n_exp = counts.shape[0]
    max_tiles = (n_tokens * TOP_K) // tm + n_exp
    tiles = (counts + tm - 1) // tm
    tile_end = jnp.cumsum(tiles)
    n_tiles = tile_end[-1]
    j = jnp.arange(max_tiles, dtype=jnp.int32)
    jj = jnp.minimum(j, n_tiles - 1)
    past = (jj[:, None] >= tile_end[None, :]).astype(jnp.int32)
    tile_expert = jnp.sum(past, axis=1)
    nth = jj - jnp.sum(past * tiles[None, :], axis=1)
    tile_block = jnp.where(j < n_tiles, block_tab[tile_expert, nth], j)
    return tile_expert, tile_block.astype(jnp.int32), n_tiles.reshape(1).astype(jnp.int32)


def kernel(x, c, positions, w_mod, b_mod, g_pre_mix, w_in, lambda_q1, lambda_k1, lambda_q2, lambda_k2, g_sub, w_pool_group, pool_scale, w_att_branch, w_pool_branch, w_out, g_post_mix, g_pre_ffn, w_router, b_router, w_gate_up, b_gate_up, w_down, b_down, g_post_ffn):
    bsz, seq, d = x.shape
    assert w_mod.shape[0] == 1, "single layer"
    assert seq % TQ == 0 and seq % TM_PROJ == 0 and seq % TM_MIX == 0 and seq % TM_COMB == 0
    t = bsz * seq
    assert t % TM_EXP == 0
    x2 = x.reshape(t, d)
    qk_w = 2 * DA_HEADS * DA_HEAD_DIM
    v_w = DA_HEADS * DA_V_DIM
    pool_w = len(POOL_WINDOWS) * POOL_GROUP_DIM
    n_experts = w_router.shape[2]

    mod3 = _mod(c, w_mod[0], b_mod[0]).reshape(bsz, N_MOD, d)

    bounds = [0, qk_w, 2 * qk_w, 2 * qk_w + v_w, 2 * qk_w + v_w + pool_w,
              2 * qk_w + v_w + pool_w + d, 2 * qk_w + v_w + pool_w + 2 * d]
    q, k, vt, p, sga, sgp = _in_proj(x2, mod3, g_pre_mix, positions.reshape(t, 1), _rope_freq_row(),
                                     w_in[0], bounds, seq)

    row64 = lambda a: a.reshape(1, DA_HEAD_DIM)
    x1, probs_t, pos_t, counts, block_tab, xs_tiles = _attn_mix(
        q, k, vt, row64(lambda_q1[0]), row64(lambda_k1[0]), row64(lambda_q2[0]), row64(lambda_k2[0]),
        g_sub.reshape(DA_V_DIM, 1),
        p, sga, sgp, x2, mod3, w_att_branch[0], w_pool_group[0].reshape(pool_w, POOL_GROUP_DIM),
        pool_scale, w_pool_branch[0], w_out[0],
        g_post_mix, g_pre_ffn, w_router[0].T, b_router.reshape(n_experts, 1), bsz, seq)

    tile_expert, tile_block, n_tiles = _tile_tables(counts[:, 0], block_tab, t)
    y_tiles = _experts(tile_expert, tile_block, n_tiles, xs_tiles,
                       w_gate_up[0], b_gate_up[0], w_down[0], b_down[0])
    out = _combine(pos_t.reshape(-1), y_tiles, probs_t.T, x1, mod3, g_post_ffn, seq)
    return out.reshape(bsz, seq, d)
```

```python
import functools

import jax
import jax.numpy as jnp
from jax import lax
from jax.experimental import pallas as pl
from jax.experimental.pallas import tpu as pltpu

F32 = jnp.float32
BF16 = jnp.bfloat16

NORM_EPS = 1e-6
CHUNK = 64
DA_HEADS = 8
DA_HEAD_DIM = 64
DA_V_DIM = 2 * DA_HEAD_DIM
ROPE_THETA = 500000.0
ROPE_DIM = DA_HEAD_DIM // 4
POOL_WINDOWS = (2, 4, 8, 16)
POOL_GROUP_DIM = 128
TOP_K = 4
SWIGLU_LIMIT = 7.0
SWIGLU_ALPHA = 1.702
N_MOD = 6
LAMBDA_INIT = 0.8 - 0.6 * 1.0
Q_SCALE = (DA_HEAD_DIM ** -0.5) * 1.4426950408889634

LANES = 128
SUBLANES = 8
VMEM_LIMIT = 56 * 1024 * 1024

TM_PROJ = 512
TQ = 512
TK = 256
ATT_HEADS_PER_STEP = 2
ATT_SUM_ROWS = 16
TM_MIX = 256
TM_EXP = 512
TM_COMB = 256
WEIGHT_CHUNK = 512
POOL_HALO = 16
NEG_BIG = -1e30


def _cparams(sem):
    return pltpu.CompilerParams(dimension_semantics=sem, vmem_limit_bytes=VMEM_LIMIT)


def _split(a):
    hi = a.astype(BF16)
    return hi, (a - hi.astype(F32)).astype(BF16)


def _split_dot(a, b, dims=(((1,), (0,)), ((), ()))):
    a_hi, a_lo = _split(a)
    b_hi, b_lo = _split(b)
    dot = lambda x, y: lax.dot_general(x, y, dims, preferred_element_type=F32)
    return dot(a_hi, b_hi) + (dot(a_hi, b_lo) + dot(a_lo, b_hi))


def _rms(x):
    return x * lax.rsqrt(jnp.mean(x * x, axis=-1, keepdims=True) + NORM_EPS)


def _mod_kernel(c_ref, w_ref, b_ref, o_ref):
    c = c_ref[...]
    c_act = c * jax.nn.sigmoid(c)
    o_ref[...] = _split_dot(c_act, w_ref[...]) + b_ref[...]


def _mod(c, w_mod, b_mod):
    bsz, d = c.shape
    n = w_mod.shape[1]
    tn = 1024
    return pl.pallas_call(
        _mod_kernel,
        grid=(n // tn,),
        in_specs=[
            pl.BlockSpec((bsz, d), lambda j: (0, 0)),
            pl.BlockSpec((d, tn), lambda j: (0, j)),
            pl.BlockSpec((1, tn), lambda j: (0, j)),
        ],
        out_specs=pl.BlockSpec((bsz, tn), lambda j: (0, j)),
        out_shape=jax.ShapeDtypeStruct((bsz, n), F32),
        compiler_params=_cparams(("parallel",)),
        name="mod",
    )(c, w_mod, b_mod.reshape(1, n))


def _rope(t, cos_t, sin_a, sin_b):
    n = t.shape[1]
    up = pltpu.roll(t, n - ROPE_DIM // 2, axis=1)
    dn = pltpu.roll(t, ROPE_DIM // 2, axis=1)
    reps = n // LANES
    tile = lambda a: jnp.concatenate([a] * reps, axis=1)
    return t * tile(cos_t) + up * tile(sin_a) + dn * tile(sin_b)


def _stream_windows(src_hbm, windows, stage, sem, sink):
    def copy(n):
        r0, nr, c0, nc = windows[n]
        return pltpu.make_async_copy(src_hbm.at[pl.ds(r0, nr), pl.ds(c0, nc)],
                                     stage.at[n % 2, pl.ds(0, nr), pl.ds(0, nc)], sem.at[n % 2])
    copy(0).start()
    for n, (_, nr, _, nc) in enumerate(windows):
        if n + 1 < len(windows):
            copy(n + 1).start()
        copy(n).wait()
        sink(n, stage[n % 2, 0:nr, 0:nc])


def _in_proj_kernel(x_ref, mod_ref, g_ref, pos_ref, freq_ref, w_hbm,
                    q_ref, k_ref, v_ref, p_ref, sga_ref, sgp_ref,
                    w_ref, wvt_ref, stage, sem, *, bounds):
    @pl.when(pl.program_id(0) == 0)
    def _():
        d_in, width = w_hbm.shape
        windows = [(0, d_in, c0, WEIGHT_CHUNK) for c0 in range(0, width, WEIGHT_CHUNK)]

        def sink(n, tile):
            c0 = windows[n][2]
            w_ref[:, c0:c0 + WEIGHT_CHUNK] = tile.astype(BF16)
            if bounds[2] <= c0 < bounds[3]:
                wvt_ref[c0 - bounds[2]:c0 - bounds[2] + WEIGHT_CHUNK, :] = tile.T.astype(BF16)
        _stream_windows(w_hbm, windows, stage, sem, sink)

    x = x_ref[...]
    shift = mod_ref[0, 0:1, :]
    scale = mod_ref[0, 1:2, :]
    u = (_rms(x) * g_ref[...]) * (1.0 + scale) + shift
    ub = u.astype(BF16)
    dot = functools.partial(jnp.dot, preferred_element_type=F32)
    part = lambda n: w_ref[:, bounds[n]:bounds[n + 1]]

    ang = pos_ref[...].astype(F32) * freq_ref[...]
    cos_t, sn = jnp.cos(ang), jnp.sin(ang)
    in_head = lax.broadcasted_iota(jnp.int32, ang.shape, 1) % DA_HEAD_DIM
    sin_a = jnp.where(in_head < ROPE_DIM // 2, -sn, 0.0)
    sin_b = jnp.where(in_head >= ROPE_DIM // 2, sn, 0.0)

    q = _rope(dot(ub, part(0)), cos_t, sin_a, sin_b)
    q_ref[...] = (q * Q_SCALE).astype(BF16)
    k = _rope(dot(ub, part(1)), cos_t, sin_a, sin_b)
    k_ref[...] = k.astype(BF16)
    vt = lax.dot_general(wvt_ref[...], ub, (((1,), (1,)), ((), ())), preferred_element_type=F32)
    for n in range(v_ref.shape[0]):
        v_ref[n] = vt[:, n * TK:(n + 1) * TK].astype(BF16)
    p_ref[...] = dot(ub, part(3))
    sga_ref[...] = jax.nn.sigmoid(dot(ub, part(4))).astype(BF16)
    sgp_ref[...] = jax.nn.sigmoid(dot(ub, part(5))).astype(BF16)


def _in_proj(x2, mod3, g_pre, pos_col, freq_row, w_in, bounds, seq):
    t, d = x2.shape
    width = w_in.shape[1]
    assert width % WEIGHT_CHUNK == 0 and all(b % WEIGHT_CHUNK == 0 for b in bounds[2:4])
    tm = TM_PROJ
    assert tm % TK == 0, "v is emitted as transposed (channels, TK) slabs"
    steps_per_seq = seq // tm
    widths = [b - a for a, b in zip(bounds[:-1], bounds[1:])]
    row = lambda w: pl.BlockSpec((tm, w), lambda i: (i, 0))
    full = lambda a: pl.BlockSpec(a.shape, lambda i: (0, 0))
    out_specs = [row(widths[0]), row(widths[1]),
                 pl.BlockSpec((tm // TK, widths[2], TK), lambda i: (i, 0, 0)),
                 row(widths[3]), row(widths[4]), row(widths[5])]
    out_shape = [jax.ShapeDtypeStruct((t, widths[0]), BF16),
                 jax.ShapeDtypeStruct((t, widths[1]), BF16),
                 jax.ShapeDtypeStruct((t // TK, widths[2], TK), BF16),
                 jax.ShapeDtypeStruct((t, widths[3]), F32),
                 jax.ShapeDtypeStruct((t, widths[4]), BF16),
                 jax.ShapeDtypeStruct((t, widths[5]), BF16)]
    return pl.pallas_call(
        functools.partial(_in_proj_kernel, bounds=tuple(bounds)),
        grid=(t // tm,),
        in_specs=[
            row(d),
            pl.BlockSpec((1, N_MOD, d), lambda i: (i // steps_per_seq, 0, 0)),
            full(g_pre), row(1), full(freq_row), pl.BlockSpec(memory_space=pl.ANY),
        ],
        out_specs=out_specs,
        out_shape=out_shape,
        scratch_shapes=[
            pltpu.VMEM((d, width), BF16),
            pltpu.VMEM((widths[2], d), BF16),
            pltpu.VMEM((2, d, WEIGHT_CHUNK), F32),
            pltpu.SemaphoreType.DMA((2,)),
        ],
        compiler_params=_cparams(("arbitrary",)),
        name="in_proj",
    )(x2, mod3, g_pre, pos_col, freq_row, w_in)


def _attn_kernel(lq1_ref, lk1_ref, lq2_ref, lk2_ref, g_ref, q_ref, k_ref, vt_ref, o_ref,
                 s_a, s_b, p_a, p_b, acc_buf):
    assert TQ == 2 * TK and TK % CHUNK == 0
    seq = q_ref.shape[0]
    lam = (jnp.exp(jnp.sum(lq1_ref[...] * lk1_ref[...], axis=-1, keepdims=True))
           - jnp.exp(jnp.sum(lq2_ref[...] * lk2_ref[...], axis=-1, keepdims=True))
           + LAMBDA_INIT)
    dot = functools.partial(jnp.dot, preferred_element_type=F32)
    heads = range(ATT_HEADS_PER_STEP)
    lanes = lambda h: slice(h * LANES, (h + 1) * LANES)

    def q_tile(qi, c):
        q_rows = pl.ds(pl.multiple_of(qi * TQ, TQ), TQ)
        qqt = []
        for h in heads:
            qt = q_ref[q_rows, lanes(h)].astype(F32).T
            row = lax.broadcasted_iota(jnp.int32, qt.shape, 0)
            zero = jnp.zeros_like(qt)
            qqt.append(jnp.concatenate([jnp.where(row < DA_HEAD_DIM, qt, zero),
                                        jnp.where(row >= DA_HEAD_DIM, qt, zero)],
                                       axis=1).astype(BF16))

        def scores(j, h):
            return dot(k_ref[pl.ds(pl.multiple_of(j * TK, TK), TK), lanes(h)], qqt[h])

        def softmax_step(s, m):
            m_new = jnp.maximum(m, jnp.max(s, axis=0, keepdims=True))
            return m_new, jnp.exp2(m - m_new), jnp.exp2(s - m_new).astype(BF16)

        ones_rows = jnp.ones((ATT_SUM_ROWS, TK), BF16)

        def pv(j, h, p):
            return dot(jnp.concatenate([vt_ref[j, lanes(h), :], ones_rows], axis=0), p)

        kk = lax.broadcasted_iota(jnp.int32, (TK, 2 * TQ), 0)
        qq = lax.broadcasted_iota(jnp.int32, (TK, 2 * TQ), 1)
        rel_chunk = jnp.where(qq >= TQ, qq - TQ, qq) // CHUNK - kk // CHUNK

        def masked(s, j):
            return jnp.where(rel_chunk >= j * (TK // CHUNK) - qi * (TQ // CHUNK), s, NEG_BIG)

        def step(j, carries, s_cur, p_cur, s_nxt, p_prev):
            pend = [pv(jnp.maximum(j - 1, 0), h, p_prev[h]) for h in heads]
            for h in heads:
                s_nxt[h] = scores(j + 1, h)
            out = []
            for h in heads:
                m, alpha = carries[h]
                acc_buf[h] = alpha * acc_buf[h] + pend[h]
                m, alpha, p = softmax_step(s_cur[h], m)
                p_cur[h] = p
                out.append((m, alpha))
            return tuple(out)

        def pair(i, carries):
            carries = step(2 * i, carries, s_a, p_a, s_b, p_b)
            return step(2 * i + 1, carries, s_b, p_b, s_a, p_a)

        for h in heads:
            s_a[h] = scores(0, h)
            p_b[h] = jnp.zeros((TK, 2 * TQ), BF16)
            acc_buf[h] = jnp.zeros(acc_buf.shape[1:], F32)
        init = tuple((jnp.full((1, 2 * TQ), NEG_BIG, F32), jnp.ones((1, 2 * TQ), F32))
                     for _ in heads)
        carries = lax.fori_loop(0, qi, pair, init)
        ja = 2 * qi
        late = lambda a: jnp.concatenate([a[:, TK:TQ], a[:, TQ + TK:]], axis=1)
        pend = [pv(jnp.maximum(ja - 1, 0), h, p_b[h]) for h in heads]
        for h in heads:
            k_last = k_ref[pl.ds(pl.multiple_of((ja + 1) * TK, TK), TK), lanes(h)]
            s_b[h, :, 0:TQ] = dot(k_last, late(qqt[h]))
        mid = []
        for h in heads:
            m, alpha = carries[h]
            acc_buf[h] = alpha * acc_buf[h] + pend[h]
            m, alpha, p = softmax_step(masked(s_a[h], ja), m)
            p_a[h] = p
            mid.append((m, alpha))
        pend = [pv(ja, h, p_a[h]) for h in heads]
        k_in = lax.broadcasted_iota(jnp.int32, (TK, TQ), 0)
        q_in = lax.broadcasted_iota(jnp.int32, (TK, TQ), 1) % TK
        diagonal = k_in // CHUNK <= q_in // CHUNK
        for h in heads:
            m, alpha = mid[h]
            acc = alpha * acc_buf[h] + pend[h]
            m_l, alpha_l, p_l = softmax_step(jnp.where(diagonal, s_b[h, :, 0:TQ], NEG_BIG), late(m))
            acc_l = alpha_l * late(acc) + pv(ja + 1, h, p_l)
            acc = jnp.concatenate([acc[:, :TK], acc_l[:, :TK], acc[:, TQ:TQ + TK], acc_l[:, TK:]], axis=1)
            o = acc[:DA_V_DIM] / acc[DA_V_DIM:DA_V_DIM + 1]
            a = o[:, :TQ] - lam * o[:, TQ:]
            y = a * lax.rsqrt(jnp.mean(a * a, axis=0, keepdims=True) + NORM_EPS)
            y = (y * g_ref[...]) * (1.0 - LAMBDA_INIT)
            o_ref[q_rows, lanes(h)] = y.T.astype(BF16)
        return c

    lax.fori_loop(0, seq // TQ, q_tile, 0)


def _attention(q, k, vt, lq1, lk1, lq2, lk2, g_col, bsz, seq):
    t = q.shape[0]
    nkv = seq // TK
    hw = ATT_HEADS_PER_STEP * LANES
    vec = lambda a: pl.BlockSpec(a.shape, lambda b, h: (0, 0))
    return pl.pallas_call(
        _attn_kernel,
        grid=(bsz, DA_HEADS // ATT_HEADS_PER_STEP),
        in_specs=[
            vec(lq1), vec(lk1), vec(lq2), vec(lk2), vec(g_col),
            pl.BlockSpec((seq, hw), lambda b, h: (b, h)),
            pl.BlockSpec((seq, hw), lambda b, h: (b, h)),
            pl.BlockSpec((nkv, hw, TK), lambda b, h: (b, h, 0)),
        ],
        out_specs=pl.BlockSpec((seq, hw), lambda b, h: (b, h)),
        out_shape=jax.ShapeDtypeStruct((t, DA_HEADS * DA_V_DIM), BF16),
        scratch_shapes=[
            pltpu.VMEM((ATT_HEADS_PER_STEP, TK, 2 * TQ), F32),
            pltpu.VMEM((ATT_HEADS_PER_STEP, TK, 2 * TQ), F32),
            pltpu.VMEM((ATT_HEADS_PER_STEP, TK, 2 * TQ), BF16),
            pltpu.VMEM((ATT_HEADS_PER_STEP, TK, 2 * TQ), BF16),
            pltpu.VMEM((ATT_HEADS_PER_STEP, DA_V_DIM + ATT_SUM_ROWS, 2 * TQ), F32),
        ],
        compiler_params=_cparams(("parallel", "parallel")),
        name="attention",
    )(lq1, lk1, lq2, lk2, g_col, q, k, vt)


def _post_mix_kernel(o_ref, p_ref, ph_ref, sga_ref, sgp_ref, x_ref, mod_ref,
                     watt_hbm, wpg_hbm, ps_ref, wpb_hbm, wout_hbm, gpm_ref, gpf_ref,
                     wrt_ref, br_ref,
                     x1_ref, prob_ref, pos_ref, cnt_ref, tab_ref, xs_hbm,
                     carry_ref, cur_ref, free_ref, u2t, zeros_v, pos_v, pos_s, meta_v, meta_s,
                     watt_ref, wpg_ref, wpb_ref, wout_ref, stage, sem_w,
                     sem_rows, sem_pos, sem_zero,
                     *, steps_per_seq, n_blocks):
    i = pl.program_id(0)
    n_steps = pl.num_programs(0)
    tm = x_ref.shape[0]
    n_experts = wrt_ref.shape[0]
    n_sub = x_ref.shape[1] // LANES
    slot = i % 2
    prev = 1 - slot
    dot = functools.partial(jnp.dot, preferred_element_type=F32)

    def pos_copy(s):
        return pltpu.make_async_copy(pos_v.at[s], pos_s.at[s], sem_pos.at[s])

    def row_copy(s, k, r):
        dst = pos_s[s, k, r]
        return pltpu.make_async_copy(
            u2t.at[s, pl.ds(r * n_sub, n_sub), :],
            xs_hbm.at[pl.ds(pl.multiple_of(dst * n_sub, n_sub), n_sub), :], sem_rows.at[s])

    def wait_rows(s):
        for _ in range(TOP_K):
            pltpu.make_async_copy(u2t.at[s], xs_hbm.at[pl.ds(0, tm * n_sub), :], sem_rows.at[s]).wait()

    @pl.when(i == 0)
    def _():
        carry_ref[...] = jnp.zeros_like(carry_ref)
        cur_ref[...] = jnp.zeros_like(cur_ref)
        free_ref[...] = jnp.zeros_like(free_ref)
        tab_ref[...] = jnp.zeros_like(tab_ref)
        zeros_v[...] = jnp.zeros_like(zeros_v)
        for src, dst in ((watt_hbm, watt_ref), (wpg_hbm, wpg_ref), (wpb_hbm, wpb_ref),
                         (wout_hbm, wout_ref)):
            rows, cols = src.shape
            step_c = min(cols, WEIGHT_CHUNK)
            windows = [(0, rows, c0, step_c) for c0 in range(0, cols, step_c)]

            def sink(n, tile, dst=dst, windows=windows):
                c0, nc = windows[n][2], windows[n][3]
                dst[:, c0:c0 + nc] = tile.astype(BF16)
            _stream_windows(src, windows, stage, sem_w, sink)
        u2t[1] = jnp.zeros(u2t.shape[1:], F32)
        spare = (n_blocks * TM_EXP
                 + lax.broadcasted_iota(jnp.int32, (TOP_K, tm), 0) * tm
                 + lax.broadcasted_iota(jnp.int32, (TOP_K, tm), 1))
        pos_v[1] = spare
        pos_copy(1).start()

    pos_copy(prev).wait()

    for r in range(tm):
        for k in range(TOP_K):
            row_copy(prev, k, r).start()

    y_att = dot(o_ref[...], watt_ref[...])

    first = (i % steps_per_seq) == 0
    halo = jnp.where(first, 0.0, ph_ref[...])
    ext = jnp.concatenate([halo, p_ref[...]], axis=0)
    t_in_seq = (i % steps_per_seq) * tm + lax.broadcasted_iota(jnp.int32, (tm, 1), 0)
    pooled = []
    for g, w in enumerate(POOL_WINDOWS):
        e = ext[:, g * POOL_GROUP_DIM:(g + 1) * POOL_GROUP_DIM]
        acc, span = e, 1
        while span < w:
            acc = acc[span:] + acc[:-span]
            span *= 2
        win = acc[POOL_HALO - (w - 1):]
        cnt = jnp.minimum(t_in_seq + 1, w).astype(F32)
        mixed = win / cnt - e[POOL_HALO:]
        pooled.append(dot(mixed.astype(BF16), wpg_ref[g * POOL_GROUP_DIM:(g + 1) * POOL_GROUP_DIM, :]))
    y_pool_in = jnp.concatenate(pooled, axis=1) * ps_ref[...]
    y_pool = dot(y_pool_in.astype(BF16), wpb_ref[...])

    merged = sga_ref[...].astype(F32) * y_att + sgp_ref[...].astype(F32) * y_pool
    mix_out = dot(merged.astype(BF16), wout_ref[...])
    gate_m = mod_ref[0, 2:3, :]
    shift_f = mod_ref[0, 3:4, :]
    scale_f = mod_ref[0, 4:5, :]
    x1 = x_ref[...] + gate_m * (_rms(mix_out) * gpm_ref[...])
    x1_ref[...] = x1
    u2 = (_rms(x1) * gpf_ref[...]) * (1.0 + scale_f) + shift_f

    logits = _split_dot(wrt_ref[...], u2, (((1,), (1,)), ((), ()))) + br_ref[...]
    erow = lax.broadcasted_iota(jnp.int32, logits.shape, 0)
    work = logits
    vals, idxs = [], []
    for _ in range(TOP_K):
        mx = jnp.max(work, axis=0, keepdims=True)
        ix = jnp.min(jnp.where(work == mx, erow, n_experts), axis=0, keepdims=True)
        vals.append(mx)
        idxs.append(ix)
        work = jnp.where(erow == ix, -jnp.inf, work)
    exps = [jnp.exp(vv - vals[0]) for vv in vals]
    denom = exps[0] + exps[1] + exps[2] + exps[3]
    prob_ref[...] = jnp.concatenate([e / denom for e in exps], axis=0)

    onehot = jnp.zeros(logits.shape, F32)
    for ix in idxs:
        onehot = onehot + (erow == ix).astype(F32)
    rr = lax.broadcasted_iota(jnp.int32, (tm, tm), 0)
    cc = lax.broadcasted_iota(jnp.int32, (tm, tm), 1)
    earlier = (rr < cc).astype(BF16)
    carry = carry_ref[...]
    before = dot(onehot.astype(BF16), earlier) + carry

    blk_rows = float(TM_EXP)
    total = carry + jnp.sum(onehot, axis=1, keepdims=True)
    blocks_old = jnp.floor((carry + (blk_rows - 1.0)) / blk_rows)
    opened = jnp.floor((total + (blk_rows - 1.0)) / blk_rows) - blocks_old
    e_r = lax.broadcasted_iota(jnp.int32, (n_experts, n_experts), 0)
    e_c = lax.broadcasted_iota(jnp.int32, (n_experts, n_experts), 1)
    lower = (e_c < e_r).astype(BF16)
    opened_before = dot(lower, jnp.broadcast_to(opened, (n_experts, LANES)).astype(BF16))[:, 0:1]
    new_blk = free_ref[...] + opened_before
    cur_blk = cur_ref[...]
    boundary = blocks_old * blk_rows
    blk_of = jnp.where(before < boundary, cur_blk, new_blk)
    pos_rows = []
    for ix in idxs:
        pick = erow == ix
        rank = jnp.sum(jnp.where(pick, before, 0.0), axis=0, keepdims=True)
        blk_id = jnp.sum(jnp.where(pick, blk_of, 0.0), axis=0, keepdims=True)
        within = rank - jnp.floor(rank / blk_rows) * blk_rows
        pos_rows.append((blk_id * blk_rows + within).astype(jnp.int32))
    pos = jnp.concatenate(pos_rows, axis=0)
    pos_ref[...] = pos
    tab_col = lax.broadcasted_iota(jnp.int32, tab_ref.shape, 1).astype(F32)
    tab_ref[...] = jnp.where(jnp.logical_and(tab_col == blocks_old, opened > 0.0),
                             new_blk.astype(jnp.int32), tab_ref[...])
    cur_ref[...] = jnp.where(opened > 0.0, new_blk, cur_blk)
    free_ref[...] = free_ref[...] + jnp.sum(opened, axis=0, keepdims=True)
    carry_ref[...] = total
    cnt_ref[...] = total.astype(jnp.int32)

    @pl.when(i > 0)
    def _():
        wait_rows(slot)

    for s in range(n_sub):
        u2t[slot, pl.ds(s, tm, stride=n_sub), :] = u2[:, s * LANES:(s + 1) * LANES]
    pos_v[slot] = pos
    pos_copy(slot).start()

    @pl.when(i == n_steps - 1)
    def _():
        pos_copy(slot).wait()

        def issue(r, c):
            for k in range(TOP_K):
                row_copy(slot, k, r).start()
            return c
        lax.fori_loop(0, tm, issue, 0)

        eye = e_r == e_c
        as_row = lambda col: jnp.sum(jnp.where(eye, col, 0.0), axis=0, keepdims=True).astype(jnp.int32)
        used = total - jnp.floor(total / blk_rows) * blk_rows
        meta = jnp.concatenate([as_row(used), as_row(cur_ref[...]),
                                jnp.broadcast_to(free_ref[...].astype(jnp.int32), (1, n_experts))], axis=0)
        meta_v[...] = jnp.zeros_like(meta_v)
        meta_v[0:3, 0:n_experts] = meta
        meta_copy = pltpu.make_async_copy(meta_v, meta_s, sem_zero)
        meta_copy.start()
        meta_copy.wait()

        sizes = [TM_EXP >> (b + 1) for b in range(TM_EXP.bit_length() - 1)]
        zero_run = lambda row0, n: pltpu.make_async_copy(
            zeros_v.at[pl.ds(0, n * n_sub), :],
            xs_hbm.at[pl.ds(pl.multiple_of(row0 * n_sub, n_sub), n * n_sub), :], sem_zero)
        issued = [jnp.int32(0) for _ in sizes]
        for e in range(n_experts):
            first = meta_s[0, e]
            row0 = meta_s[1, e] * TM_EXP + first
            pad = jnp.where(first > 0, TM_EXP - first, 0)
            for b, n in enumerate(sizes):
                take = (pad & n) != 0

                @pl.when(take)
                def _(row0=row0, n=n):
                    zero_run(row0, n).start()
                issued[b] = issued[b] + take.astype(jnp.int32)
                row0 = row0 + jnp.where(take, n, 0)
        n_open = meta_s[2, 0]

        def whole(b, c):
            zero_run(b * TM_EXP, TM_EXP).start()
            return c
        lax.fori_loop(n_open, n_blocks, whole, 0)

        for b, n in enumerate(sizes):
            lax.fori_loop(0, issued[b], lambda _, c, n=n: (zero_run(0, n).wait(), c)[1], 0)
        lax.fori_loop(n_open, n_blocks, lambda _, c: (zero_run(0, TM_EXP).wait(), c)[1], 0)
        wait_rows(prev)
        wait_rows(slot)


def _post_mix(o, p, sga, sgp, x2, mod3, w_att, w_pg, pool_scale, w_pb, w_out,
              g_post_mix, g_pre_ffn, w_router_t, b_router_col, seq):
    t, d = x2.shape
    tm = TM_MIX
    steps_per_seq = seq // tm
    n_experts = w_router_t.shape[0]
    pw = p.shape[1]
    n_sub = d // LANES
    halo_blocks = tm // POOL_HALO
    row = lambda w: pl.BlockSpec((tm, w), lambda i: (i, 0))
    col = lambda h: pl.BlockSpec((h, tm), lambda i: (0, i))
    full2 = lambda a: pl.BlockSpec(a.shape, lambda i: (0, 0))
    hbm = pl.BlockSpec(memory_space=pl.ANY)
    assert tm <= TM_EXP and (tm * TOP_K) % TM_EXP == 0 and t % TM_EXP == 0
    n_blocks = (t * TOP_K) // TM_EXP + n_experts
    tab_w = -(-(t // TM_EXP) // LANES) * LANES
    kern = functools.partial(_post_mix_kernel, steps_per_seq=steps_per_seq, n_blocks=n_blocks)
    xs_rows = n_blocks * TM_EXP + TOP_K * tm
    return pl.pallas_call(
        kern,
        grid=(t // tm,),
        in_specs=[
            row(d), row(pw),
            pl.BlockSpec((POOL_HALO, pw), lambda i: (jnp.maximum(i * halo_blocks - 1, 0), 0)),
            row(d), row(d), row(d),
            pl.BlockSpec((1, N_MOD, d), lambda i: (i // steps_per_seq, 0, 0)),
            hbm, hbm, full2(pool_scale), hbm, hbm, full2(g_post_mix), full2(g_pre_ffn),
            full2(w_router_t), full2(b_router_col),
        ],
        out_specs=[
            row(d), col(TOP_K), col(TOP_K),
            pl.BlockSpec((n_experts, 1), lambda i: (0, 0)),
            pl.BlockSpec((n_experts, tab_w), lambda i: (0, 0)),
            pl.BlockSpec(memory_space=pl.ANY),
        ],
        out_shape=[
            jax.ShapeDtypeStruct((t, d), F32),
            jax.ShapeDtypeStruct((TOP_K, t), F32),
            jax.ShapeDtypeStruct((TOP_K, t), jnp.int32),
            jax.ShapeDtypeStruct((n_experts, 1), jnp.int32),
            jax.ShapeDtypeStruct((n_experts, tab_w), jnp.int32),
            jax.ShapeDtypeStruct((xs_rows * n_sub, LANES), F32),
        ],
        scratch_shapes=[
            pltpu.VMEM((n_experts, 1), F32),
            pltpu.VMEM((n_experts, 1), F32),
            pltpu.VMEM((1, 1), F32),
            pltpu.VMEM((2, tm * n_sub, LANES), F32),
            pltpu.VMEM((TM_EXP * n_sub, LANES), F32),
            pltpu.VMEM((2, TOP_K, tm), jnp.int32),
            pltpu.SMEM((2, TOP_K, tm), jnp.int32),
            pltpu.VMEM((SUBLANES, LANES), jnp.int32),
            pltpu.SMEM((SUBLANES, LANES), jnp.int32),
            pltpu.VMEM(w_att.shape, BF16),
            pltpu.VMEM(w_pg.shape, BF16),
            pltpu.VMEM(w_pb.shape, BF16),
            pltpu.VMEM(w_out.shape, BF16),
            pltpu.VMEM((2, max(w_att.shape[0], w_out.shape[0]), WEIGHT_CHUNK), F32),
            pltpu.SemaphoreType.DMA((2,)),
            pltpu.SemaphoreType.DMA((2,)),
            pltpu.SemaphoreType.DMA((2,)),
            pltpu.SemaphoreType.DMA,
        ],
        compiler_params=_cparams(("arbitrary",)),
        name="post_mix",
    )(o, p, p, sga, sgp, x2, mod3, w_att, w_pg, pool_scale, w_pb, w_out,
      g_post_mix, g_pre_ffn, w_router_t, b_router_col)


def _experts_kernel(te_ref, xb_ref, yb_ref, nt_ref, nv_ref, xs_ref, wgu_ref, bgu_ref, wd_ref, bd_ref,
                    y_ref, wgu_bf, wd_bf, *, n_sub):
    j = pl.program_id(0)
    tm = TM_EXP
    half = tm // 2
    d_ff = wd_ref.shape[1]
    open_tile = j < nt_ref[0]

    @pl.when(open_tile)
    def _():
        changed = jnp.logical_or(j == 0, te_ref[j] != te_ref[jnp.maximum(j - 1, 0)])

        @pl.when(changed)
        def _():
            wgu_bf[...] = wgu_ref[0].astype(BF16)
            wd_bf[...] = wd_ref[0].astype(BF16)

    def mlp(rows):
        xs = jnp.concatenate(
            [xs_ref[pl.ds(s, rows, stride=n_sub), :] for s in range(n_sub)], axis=1).astype(BF16)
        gu = jnp.dot(xs, wgu_bf[...], preferred_element_type=F32) + bgu_ref[0]
        gate = jnp.minimum(gu[:, :d_ff], SWIGLU_LIMIT)
        up = jnp.clip(gu[:, d_ff:], -SWIGLU_LIMIT, SWIGLU_LIMIT)
        act = (up + 1.0) * (gate * jax.nn.sigmoid(SWIGLU_ALPHA * gate))
        y = jnp.dot(act.astype(BF16), wd_bf[...], preferred_element_type=F32) + bd_ref[0]
        for s in range(n_sub):
            y_ref[pl.ds(s, rows, stride=n_sub), :] = y[:, s * LANES:(s + 1) * LANES]

    few = nv_ref[j] <= half

    @pl.when(jnp.logical_and(open_tile, jnp.logical_not(few)))
    def _():
        mlp(tm)

    @pl.when(jnp.logical_and(open_tile, few))
    def _():
        mlp(half)
        y_ref[pl.ds(half * n_sub, half * n_sub), :] = jnp.zeros((half * n_sub, LANES), F32)

    @pl.when(j >= nt_ref[0])
    def _():
        y_ref[...] = jnp.zeros_like(y_ref)


def _experts(tile_expert, xs_block, y_block, rows_valid, n_tiles, xs_tiles,
             w_gate_up, b_gate_up, w_down, b_down):
    n_exp, d, d_gu = w_gate_up.shape
    d_ff = w_down.shape[1]
    n_sub = d // LANES
    tm = TM_EXP
    max_tiles = tile_expert.shape[0]
    by_expert = lambda j, te, xb, yb, nt, nv: (te[j], 0, 0)
    by_block = lambda j, te, xb, yb, nt, nv: (yb[j], 0)
    grid_spec = pltpu.PrefetchScalarGridSpec(
        num_scalar_prefetch=5,
        grid=(max_tiles,),
        in_specs=[
            pl.BlockSpec((tm * n_sub, LANES), lambda j, te, xb, yb, nt, nv: (xb[j], 0)),
            pl.BlockSpec((1, d, d_gu), by_expert),
            pl.BlockSpec((1, 1, d_gu), by_expert),
            pl.BlockSpec((1, d_ff, d), by_expert),
            pl.BlockSpec((1, 1, d), by_expert),
        ],
        out_specs=pl.BlockSpec((tm * n_sub, LANES), by_block),
        scratch_shapes=[
            pltpu.VMEM((d, d_gu), BF16),
            pltpu.VMEM((d_ff, d), BF16),
        ],
    )
    return pl.pallas_call(
        functools.partial(_experts_kernel, n_sub=n_sub),
        grid_spec=grid_spec,
        out_shape=jax.ShapeDtypeStruct((max_tiles * tm * n_sub, LANES), F32),
        compiler_params=_cparams(("arbitrary",)),
        name="experts",
    )(tile_expert, xs_block, y_block, n_tiles, rows_valid, xs_tiles,
      w_gate_up, b_gate_up.reshape(n_exp, 1, d_gu), w_down, b_down.reshape(n_exp, 1, d))


def _combine_kernel(pos_ref, y_hbm, prob_ref, x1_ref, mod_ref, g_ref, o_ref, ybuf, sem,
                    *, n_sub, n_tokens):
    i = pl.program_id(0)
    n_steps = pl.num_programs(0)
    tm = TM_COMB
    slot = i % 2

    def gather(step, s):
        for r in range(tm):
            for k in range(TOP_K):
                row = pos_ref[k * n_tokens + step * tm + r]
                pltpu.make_async_copy(
                    y_hbm.at[pl.ds(pl.multiple_of(row * n_sub, n_sub), n_sub), :],
                    ybuf.at[s, k, pl.ds(r * n_sub, n_sub), :], sem.at[s]).start()

    @pl.when(i == 0)
    def _():
        gather(0, 0)

    @pl.when(i + 1 < n_steps)
    def _():
        gather(i + 1, 1 - slot)

    for k in range(TOP_K):
        pltpu.make_async_copy(y_hbm.at[pl.ds(0, tm * n_sub), :], ybuf.at[slot, k], sem.at[slot]).wait()

    prob = prob_ref[...]
    f = None
    for k in range(TOP_K):
        yk = jnp.concatenate(
            [ybuf[slot, k, pl.ds(s, tm, stride=n_sub), :] for s in range(n_sub)], axis=1)
        term = yk * prob[:, k:k + 1]
        f = term if f is None else f + term
    gate_f = mod_ref[0, 5:6, :]
    o_ref[...] = x1_ref[...] + gate_f * (_rms(f) * g_ref[...])


def _combine(pos_flat, y_tiles, probs, x1, mod3, g_post_ffn, seq):
    t, d = x1.shape
    tm = TM_COMB
    n_sub = d // LANES
    steps_per_seq = seq // tm
    grid_spec = pltpu.PrefetchScalarGridSpec(
        num_scalar_prefetch=1,
        grid=(t // tm,),
        in_specs=[
            pl.BlockSpec(memory_space=pl.ANY),
            pl.BlockSpec((tm, TOP_K), lambda i, pos: (i, 0)),
            pl.BlockSpec((tm, d), lambda i, pos: (i, 0)),
            pl.BlockSpec((1, N_MOD, d), lambda i, pos: (i // steps_per_seq, 0, 0)),
            pl.BlockSpec((1, d), lambda i, pos: (0, 0)),
        ],
        out_specs=pl.BlockSpec((tm, d), lambda i, pos: (i, 0)),
        scratch_shapes=[
            pltpu.VMEM((2, TOP_K, tm * n_sub, LANES), F32),
            pltpu.SemaphoreType.DMA((2,)),
        ],
    )
    return pl.pallas_call(
        functools.partial(_combine_kernel, n_sub=n_sub, n_tokens=t),
        grid_spec=grid_spec,
        out_shape=jax.ShapeDtypeStruct((t, d), F32),
        compiler_params=_cparams(("arbitrary",)),
        name="combine",
    )(pos_flat, y_tiles, probs, x1, mod3, g_post_ffn)


def _rope_freq_row():
    half = ROPE_DIM // 2
    inv_freq = ROPE_THETA ** (-jnp.arange(0, ROPE_DIM, 2, dtype=F32) / ROPE_DIM)
    head = jnp.concatenate([inv_freq, inv_freq, jnp.zeros((DA_HEAD_DIM - 2 * half,), F32)])
    return jnp.tile(head, LANES // DA_HEAD_DIM).reshape(1, LANES)


def _tile_tables(counts, block_tab, n_tokens):
    tm = TM_EXP
    n_exp = counts.shape[0]
    max_tiles = (n_tokens * TOP_K) // tm + n_exp
    tiles = (counts + tm - 1) // tm
    tile_end = jnp.cumsum(tiles)
    n_tiles = tile_end[-1]
    j = jnp.arange(max_tiles, dtype=jnp.int32)
    jj = jnp.minimum(j, n_tiles - 1)
    past = (jj[:, None] >= tile_end[None, :]).astype(jnp.int32)
    tile_expert = jnp.sum(past, axis=1)
    nth = jj - jnp.sum(past * tiles[None, :], axis=1)
    open_block = block_tab[tile_expert, nth]
    y_block = jnp.where(j < n_tiles, open_block, j)
    is_expert = (tile_expert[:, None] == jnp.arange(n_exp, dtype=jnp.int32)[None, :]).astype(jnp.int32)
    rows_valid = jnp.where(j < n_tiles,
                           jnp.minimum(jnp.sum(is_expert * counts[None, :], axis=1) - nth * tm, tm), 0)
    as_i32 = lambda a: a.astype(jnp.int32)
    return (as_i32(tile_expert), as_i32(open_block), as_i32(y_block), as_i32(rows_valid),
            as_i32(n_tiles.reshape(1)))


def kernel(x, c, positions, w_mod, b_mod, g_pre_mix, w_in, lambda_q1, lambda_k1, lambda_q2, lambda_k2, g_sub, w_pool_group, pool_scale, w_att_branch, w_pool_branch, w_out, g_post_mix, g_pre_ffn, w_router, b_router, w_gate_up, b_gate_up, w_down, b_down, g_post_ffn):
    bsz, seq, d = x.shape
    assert w_mod.shape[0] == 1, "single layer"
    assert seq % TQ == 0 and seq % TM_PROJ == 0 and seq % TM_MIX == 0 and seq % TM_COMB == 0
    t = bsz * seq
    assert t % TM_EXP == 0
    x2 = x.reshape(t, d)
    qk_w = 2 * DA_HEADS * DA_HEAD_DIM
    v_w = DA_HEADS * DA_V_DIM
    pool_w = len(POOL_WINDOWS) * POOL_GROUP_DIM
    n_experts = w_router.shape[2]

    mod3 = _mod(c, w_mod[0], b_mod[0]).reshape(bsz, N_MOD, d)

    bounds = [0, qk_w, 2 * qk_w, 2 * qk_w + v_w, 2 * qk_w + v_w + pool_w,
              2 * qk_w + v_w + pool_w + d, 2 * qk_w + v_w + pool_w + 2 * d]
    q, k, vt, p, sga, sgp = _in_proj(x2, mod3, g_pre_mix, positions.reshape(t, 1), _rope_freq_row(),
                                     w_in[0], bounds, seq)

    row64 = lambda a: a.reshape(1, DA_HEAD_DIM)
    o = _attention(q, k, vt, row64(lambda_q1[0]), row64(lambda_k1[0]), row64(lambda_q2[0]),
                   row64(lambda_k2[0]), g_sub.reshape(DA_V_DIM, 1), bsz, seq)

    x1, probs_t, pos_t, counts, block_tab, xs_tiles = _post_mix(
        o, p, sga, sgp, x2, mod3, w_att_branch[0], w_pool_group[0].reshape(pool_w, POOL_GROUP_DIM),
        pool_scale, w_pool_branch[0], w_out[0],
        g_post_mix, g_pre_ffn, w_router[0].T, b_router.reshape(n_experts, 1), seq)

    tile_expert, xs_block, y_block, rows_valid, n_tiles = _tile_tables(counts[:, 0], block_tab, t)
    y_tiles = _experts(tile_expert, xs_block, y_block, rows_valid, n_tiles, xs_tiles,
                       w_gate_up[0], b_gate_up[0], w_down[0], b_down[0])
    out = _combine(pos_t.reshape(-1), y_tiles, probs_t.T, x1, mod3, g_post_ffn, seq)
    return out.reshape(bsz, seq, d)
```

```python
import functools

import jax
import jax.numpy as jnp
from jax import lax
from jax.experimental import pallas as pl
from jax.experimental.pallas import tpu as pltpu

F32 = jnp.float32
BF16 = jnp.bfloat16

NORM_EPS = 1e-6
CHUNK = 64
DA_HEADS = 8
DA_HEAD_DIM = 64
DA_V_DIM = 2 * DA_HEAD_DIM
ROPE_THETA = 500000.0
ROPE_DIM = DA_HEAD_DIM // 4
POOL_WINDOWS = (2, 4, 8, 16)
POOL_GROUP_DIM = 128
TOP_K = 4
SWIGLU_LIMIT = 7.0
SWIGLU_ALPHA = 1.702
N_MOD = 6
LAMBDA_INIT = 0.8 - 0.6 * 1.0
Q_SCALE = (DA_HEAD_DIM ** -0.5) * 1.4426950408889634

LANES = 128
SUBLANES = 8
VMEM_LIMIT = 56 * 1024 * 1024

TM_PROJ = 512
TQ = 512
TK = 256
ATT_HEADS_PER_STEP = 2
ATT_SUM_ROWS = 16
TM_MIX = 256
TM_EXP = 512
TM_COMB = 256
WEIGHT_CHUNK = 512
POOL_HALO = 16
NEG_BIG = -1e30


def _cparams(sem):
    return pltpu.CompilerParams(dimension_semantics=sem, vmem_limit_bytes=VMEM_LIMIT)


def _split(a):
    hi = a.astype(BF16)
    return hi, (a - hi.astype(F32)).astype(BF16)


def _split_dot(a, b, dims=(((1,), (0,)), ((), ()))):
    a_hi, a_lo = _split(a)
    b_hi, b_lo = _split(b)
    dot = lambda x, y: lax.dot_general(x, y, dims, preferred_element_type=F32)
    return dot(a_hi, b_hi) + (dot(a_hi, b_lo) + dot(a_lo, b_hi))


def _rms(x):
    return x * lax.rsqrt(jnp.mean(x * x, axis=-1, keepdims=True) + NORM_EPS)


def _mod_kernel(c_ref, w_ref, b_ref, o_ref):
    c = c_ref[...]
    c_act = c * jax.nn.sigmoid(c)
    o_ref[...] = _split_dot(c_act, w_ref[...]) + b_ref[...]


def _mod(c, w_mod, b_mod):
    bsz, d = c.shape
    n = w_mod.shape[1]
    tn = 1024
    return pl.pallas_call(
        _mod_kernel,
        grid=(n // tn,),
        in_specs=[
            pl.BlockSpec((bsz, d), lambda j: (0, 0)),
            pl.BlockSpec((d, tn), lambda j: (0, j)),
            pl.BlockSpec((1, tn), lambda j: (0, j)),
        ],
        out_specs=pl.BlockSpec((bsz, tn), lambda j: (0, j)),
        out_shape=jax.ShapeDtypeStruct((bsz, n), F32),
        compiler_params=_cparams(("parallel",)),
        name="mod",
    )(c, w_mod, b_mod.reshape(1, n))


def _rope(t, cos_t, sin_a, sin_b):
    n = t.shape[1]
    up = pltpu.roll(t, n - ROPE_DIM // 2, axis=1)
    dn = pltpu.roll(t, ROPE_DIM // 2, axis=1)
    reps = n // LANES
    tile = lambda a: jnp.concatenate([a] * reps, axis=1)
    return t * tile(cos_t) + up * tile(sin_a) + dn * tile(sin_b)


def _stream_windows(src_hbm, windows, stage, sem, sink):
    def copy(n):
        r0, nr, c0, nc = windows[n]
        return pltpu.make_async_copy(src_hbm.at[pl.ds(r0, nr), pl.ds(c0, nc)],
                                     stage.at[n % 2, pl.ds(0, nr), pl.ds(0, nc)], sem.at[n % 2])
    copy(0).start()
    for n, (_, nr, _, nc) in enumerate(windows):
        if n + 1 < len(windows):
            copy(n + 1).start()
        copy(n).wait()
        sink(n, stage[n % 2, 0:nr, 0:nc])


def _in_proj_kernel(x_ref, mod_ref, g_ref, pos_ref, freq_ref, w_hbm,
                    q_ref, k_ref, v_ref, p_ref, sga_ref, sgp_ref,
                    w_ref, wvt_ref, stage, sem, *, bounds):
    @pl.when(pl.program_id(0) == 0)
    def _():
        d_in, width = w_hbm.shape
        windows = [(0, d_in, c0, WEIGHT_CHUNK) for c0 in range(0, width, WEIGHT_CHUNK)]

        def sink(n, tile):
            c0 = windows[n][2]
            w_ref[:, c0:c0 + WEIGHT_CHUNK] = tile.astype(BF16)
            if bounds[2] <= c0 < bounds[3]:
                wvt_ref[c0 - bounds[2]:c0 - bounds[2] + WEIGHT_CHUNK, :] = tile.T.astype(BF16)
        _stream_windows(w_hbm, windows, stage, sem, sink)

    x = x_ref[...]
    shift = mod_ref[0, 0:1, :]
    scale = mod_ref[0, 1:2, :]
    u = (_rms(x) * g_ref[...]) * (1.0 + scale) + shift
    ub = u.astype(BF16)
    dot = functools.partial(jnp.dot, preferred_element_type=F32)
    part = lambda n: w_ref[:, bounds[n]:bounds[n + 1]]

    ang = pos_ref[...].astype(F32) * freq_ref[...]
    cos_t, sn = jnp.cos(ang), jnp.sin(ang)
    in_head = lax.broadcasted_iota(jnp.int32, ang.shape, 1) % DA_HEAD_DIM
    sin_a = jnp.where(in_head < ROPE_DIM // 2, -sn, 0.0)
    sin_b = jnp.where(in_head >= ROPE_DIM // 2, sn, 0.0)

    q = _rope(dot(ub, part(0)), cos_t, sin_a, sin_b)
    q_ref[...] = (q * Q_SCALE).astype(BF16)
    k = _rope(dot(ub, part(1)), cos_t, sin_a, sin_b)
    k_ref[...] = k.astype(BF16)
    vt = lax.dot_general(wvt_ref[...], ub, (((1,), (1,)), ((), ())), preferred_element_type=F32)
    for n in range(v_ref.shape[0]):
        v_ref[n] = vt[:, n * TK:(n + 1) * TK].astype(BF16)
    p_ref[...] = dot(ub, part(3))
    sga_ref[...] = jax.nn.sigmoid(dot(ub, part(4))).astype(BF16)
    sgp_ref[...] = jax.nn.sigmoid(dot(ub, part(5))).astype(BF16)


def _in_proj(x2, mod3, g_pre, pos_col, freq_row, w_in, bounds, seq):
    t, d = x2.shape
    width = w_in.shape[1]
    assert width % WEIGHT_CHUNK == 0 and all(b % WEIGHT_CHUNK == 0 for b in bounds[2:4])
    tm = TM_PROJ
    assert tm % TK == 0, "v is emitted as transposed (channels, TK) slabs"
    steps_per_seq = seq // tm
    widths = [b - a for a, b in zip(bounds[:-1], bounds[1:])]
    row = lambda w: pl.BlockSpec((tm, w), lambda i: (i, 0))
    full = lambda a: pl.BlockSpec(a.shape, lambda i: (0, 0))
    out_specs = [row(widths[0]), row(widths[1]),
                 pl.BlockSpec((tm // TK, widths[2], TK), lambda i: (i, 0, 0)),
                 row(widths[3]), row(widths[4]), row(widths[5])]
    out_shape = [jax.ShapeDtypeStruct((t, widths[0]), BF16),
                 jax.ShapeDtypeStruct((t, widths[1]), BF16),
                 jax.ShapeDtypeStruct((t // TK, widths[2], TK), BF16),
                 jax.ShapeDtypeStruct((t, widths[3]), F32),
                 jax.ShapeDtypeStruct((t, widths[4]), BF16),
                 jax.ShapeDtypeStruct((t, widths[5]), BF16)]
    return pl.pallas_call(
        functools.partial(_in_proj_kernel, bounds=tuple(bounds)),
        grid=(t // tm,),
        in_specs=[
            row(d),
            pl.BlockSpec((1, N_MOD, d), lambda i: (i // steps_per_seq, 0, 0)),
            full(g_pre), row(1), full(freq_row), pl.BlockSpec(memory_space=pl.ANY),
        ],
        out_specs=out_specs,
        out_shape=out_shape,
        scratch_shapes=[
            pltpu.VMEM((d, width), BF16),
            pltpu.VMEM((widths[2], d), BF16),
            pltpu.VMEM((2, d, WEIGHT_CHUNK), F32),
            pltpu.SemaphoreType.DMA((2,)),
        ],
        compiler_params=_cparams(("arbitrary",)),
        name="in_proj",
    )(x2, mod3, g_pre, pos_col, freq_row, w_in)


def _attn_kernel(lq1_ref, lk1_ref, lq2_ref, lk2_ref, g_ref, q_ref, k_ref, vt_ref, o_ref,
                 s_a, s_b, p_a, p_b, acc_buf):
    assert TQ == 2 * TK and TK % CHUNK == 0
    seq = q_ref.shape[0]
    lam = (jnp.exp(jnp.sum(lq1_ref[...] * lk1_ref[...], axis=-1, keepdims=True))
           - jnp.exp(jnp.sum(lq2_ref[...] * lk2_ref[...], axis=-1, keepdims=True))
           + LAMBDA_INIT)
    dot = functools.partial(jnp.dot, preferred_element_type=F32)
    heads = range(ATT_HEADS_PER_STEP)
    lanes = lambda h: slice(h * LANES, (h + 1) * LANES)

    def q_tile(qi, c):
        q_rows = pl.ds(pl.multiple_of(qi * TQ, TQ), TQ)
        qqt = []
        for h in heads:
            qt = q_ref[q_rows, lanes(h)].astype(F32).T
            row = lax.broadcasted_iota(jnp.int32, qt.shape, 0)
            zero = jnp.zeros_like(qt)
            qqt.append(jnp.concatenate([jnp.where(row < DA_HEAD_DIM, qt, zero),
                                        jnp.where(row >= DA_HEAD_DIM, qt, zero)],
                                       axis=1).astype(BF16))

        def scores(j, h):
            return dot(k_ref[pl.ds(pl.multiple_of(j * TK, TK), TK), lanes(h)], qqt[h])

        def softmax_step(s, m):
            m_new = jnp.maximum(m, jnp.max(s, axis=0, keepdims=True))
            return m_new, jnp.exp2(m - m_new), jnp.exp2(s - m_new).astype(BF16)

        ones_rows = jnp.ones((ATT_SUM_ROWS, TK), BF16)

        def pv(j, h, p):
            return dot(jnp.concatenate([vt_ref[j, lanes(h), :], ones_rows], axis=0), p)

        kk = lax.broadcasted_iota(jnp.int32, (TK, 2 * TQ), 0)
        qq = lax.broadcasted_iota(jnp.int32, (TK, 2 * TQ), 1)
        rel_chunk = jnp.where(qq >= TQ, qq - TQ, qq) // CHUNK - kk // CHUNK

        def masked(s, j):
            return jnp.where(rel_chunk >= j * (TK // CHUNK) - qi * (TQ // CHUNK), s, NEG_BIG)

        def step(j, carries, s_cur, p_cur, s_nxt, p_prev):
            pend = [pv(jnp.maximum(j - 1, 0), h, p_prev[h]) for h in heads]
            for h in heads:
                s_nxt[h] = scores(j + 1, h)
            out = []
            for h in heads:
                m, alpha = carries[h]
                acc_buf[h] = alpha * acc_buf[h] + pend[h]
                m, alpha, p = softmax_step(s_cur[h], m)
                p_cur[h] = p
                out.append((m, alpha))
            return tuple(out)

        def pair(i, carries):
            carries = step(2 * i, carries, s_a, p_a, s_b, p_b)
            return step(2 * i + 1, carries, s_b, p_b, s_a, p_a)

        for h in heads:
            s_a[h] = scores(0, h)
            p_b[h] = jnp.zeros((TK, 2 * TQ), BF16)
            acc_buf[h] = jnp.zeros(acc_buf.shape[1:], F32)
        init = tuple((jnp.full((1, 2 * TQ), NEG_BIG, F32), jnp.ones((1, 2 * TQ), F32))
                     for _ in heads)
        carries = lax.fori_loop(0, qi, pair, init)
        ja = 2 * qi
        late = lambda a: jnp.concatenate([a[:, TK:TQ], a[:, TQ + TK:]], axis=1)
        pend = [pv(jnp.maximum(ja - 1, 0), h, p_b[h]) for h in heads]
        for h in heads:
            k_last = k_ref[pl.ds(pl.multiple_of((ja + 1) * TK, TK), TK), lanes(h)]
            s_b[h, :, 0:TQ] = dot(k_last, late(qqt[h]))
        mid = []
        for h in heads:
            m, alpha = carries[h]
            acc_buf[h] = alpha * acc_buf[h] + pend[h]
            m, alpha, p = softmax_step(masked(s_a[h], ja), m)
            p_a[h] = p
            mid.append((m, alpha))
        pend = [pv(ja, h, p_a[h]) for h in heads]
        k_in = lax.broadcasted_iota(jnp.int32, (TK, TQ), 0)
        q_in = lax.broadcasted_iota(jnp.int32, (TK, TQ), 1) % TK
        diagonal = k_in // CHUNK <= q_in // CHUNK
        for h in heads:
            m, alpha = mid[h]
            acc = alpha * acc_buf[h] + pend[h]
            m_l, alpha_l, p_l = softmax_step(jnp.where(diagonal, s_b[h, :, 0:TQ], NEG_BIG), late(m))
            acc_l = alpha_l * late(acc) + pv(ja + 1, h, p_l)
            acc = jnp.concatenate([acc[:, :TK], acc_l[:, :TK], acc[:, TQ:TQ + TK], acc_l[:, TK:]], axis=1)
            o = acc[:DA_V_DIM] / acc[DA_V_DIM:DA_V_DIM + 1]
            a = o[:, :TQ] - lam * o[:, TQ:]
            y = a * lax.rsqrt(jnp.mean(a * a, axis=0, keepdims=True) + NORM_EPS)
            y = (y * g_ref[...]) * (1.0 - LAMBDA_INIT)
            o_ref[q_rows, lanes(h)] = y.T.astype(BF16)
        return c

    lax.fori_loop(0, seq // TQ, q_tile, 0)


def _attention(q, k, vt, lq1, lk1, lq2, lk2, g_col, bsz, seq):
    t = q.shape[0]
    nkv = seq // TK
    hw = ATT_HEADS_PER_STEP * LANES
    vec = lambda a: pl.BlockSpec(a.shape, lambda b, h: (0, 0))
    return pl.pallas_call(
        _attn_kernel,
        grid=(bsz, DA_HEADS // ATT_HEADS_PER_STEP),
        in_specs=[
            vec(lq1), vec(lk1), vec(lq2), vec(lk2), vec(g_col),
            pl.BlockSpec((seq, hw), lambda b, h: (b, h)),
            pl.BlockSpec((seq, hw), lambda b, h: (b, h)),
            pl.BlockSpec((nkv, hw, TK), lambda b, h: (b, h, 0)),
        ],
        out_specs=pl.BlockSpec((seq, hw), lambda b, h: (b, h)),
        out_shape=jax.ShapeDtypeStruct((t, DA_HEADS * DA_V_DIM), BF16),
        scratch_shapes=[
            pltpu.VMEM((ATT_HEADS_PER_STEP, TK, 2 * TQ), F32),
            pltpu.VMEM((ATT_HEADS_PER_STEP, TK, 2 * TQ), F32),
            pltpu.VMEM((ATT_HEADS_PER_STEP, TK, 2 * TQ), BF16),
            pltpu.VMEM((ATT_HEADS_PER_STEP, TK, 2 * TQ), BF16),
            pltpu.VMEM((ATT_HEADS_PER_STEP, DA_V_DIM + ATT_SUM_ROWS, 2 * TQ), F32),
        ],
        compiler_params=_cparams(("parallel", "parallel")),
        name="attention",
    )(lq1, lk1, lq2, lk2, g_col, q, k, vt)


def _post_mix_kernel(o_ref, p_ref, ph_ref, sga_ref, sgp_ref, x_ref, mod_ref,
                     watt_hbm, wpg_hbm, ps_ref, wpb_hbm, wout_hbm, gpm_ref, gpf_ref,
                     wrt_ref, br_ref,
                     x1_ref, prob_ref, pos_ref, cnt_ref, tab_ref, xs_hbm,
                     carry_ref, cur_ref, free_ref, u2t, zeros_v, pos_v, pos_s, meta_v, meta_s,
                     watt_ref, wpg_ref, wpb_ref, wout_ref, stage, sem_w,
                     sem_rows, sem_pos, sem_zero,
                     *, steps_per_seq, n_blocks):
    i = pl.program_id(0)
    n_steps = pl.num_programs(0)
    tm = x_ref.shape[0]
    n_experts = wrt_ref.shape[0]
    n_sub = x_ref.shape[1] // LANES
    slot = i % 2
    prev = 1 - slot
    dot = functools.partial(jnp.dot, preferred_element_type=F32)

    def pos_copy(s):
        return pltpu.make_async_copy(pos_v.at[s], pos_s.at[s], sem_pos.at[s])

    def row_copy(s, k, r):
        dst = pos_s[s, k, r]
        return pltpu.make_async_copy(
            u2t.at[s, pl.ds(r * n_sub, n_sub), :],
            xs_hbm.at[pl.ds(pl.multiple_of(dst * n_sub, n_sub), n_sub), :], sem_rows.at[s])

    def wait_rows(s):
        for _ in range(TOP_K):
            pltpu.make_async_copy(u2t.at[s], xs_hbm.at[pl.ds(0, tm * n_sub), :], sem_rows.at[s]).wait()

    @pl.when(i == 0)
    def _():
        carry_ref[...] = jnp.zeros_like(carry_ref)
        cur_ref[...] = jnp.zeros_like(cur_ref)
        free_ref[...] = jnp.zeros_like(free_ref)
        tab_ref[...] = jnp.zeros_like(tab_ref)
        zeros_v[...] = jnp.zeros_like(zeros_v)
        for src, dst in ((watt_hbm, watt_ref), (wpg_hbm, wpg_ref), (wpb_hbm, wpb_ref),
                         (wout_hbm, wout_ref)):
            rows, cols = src.shape
            step_c = min(cols, WEIGHT_CHUNK)
            windows = [(0, rows, c0, step_c) for c0 in range(0, cols, step_c)]

            def sink(n, tile, dst=dst, windows=windows):
                c0, nc = windows[n][2], windows[n][3]
                dst[:, c0:c0 + nc] = tile.astype(BF16)
            _stream_windows(src, windows, stage, sem_w, sink)
        u2t[1] = jnp.zeros(u2t.shape[1:], F32)
        spare = (n_blocks * TM_EXP
                 + lax.broadcasted_iota(jnp.int32, (TOP_K, tm), 0) * tm
                 + lax.broadcasted_iota(jnp.int32, (TOP_K, tm), 1))
        pos_v[1] = spare
        pos_copy(1).start()

    pos_copy(prev).wait()

    for r in range(tm):
        for k in range(TOP_K):
            row_copy(prev, k, r).start()

    y_att = dot(o_ref[...], watt_ref[...])

    first = (i % steps_per_seq) == 0
    halo = jnp.where(first, 0.0, ph_ref[...])
    ext = jnp.concatenate([halo, p_ref[...]], axis=0)
    t_in_seq = (i % steps_per_seq) * tm + lax.broadcasted_iota(jnp.int32, (tm, 1), 0)
    pooled = []
    for g, w in enumerate(POOL_WINDOWS):
        e = ext[:, g * POOL_GROUP_DIM:(g + 1) * POOL_GROUP_DIM]
        acc, span = e, 1
        while span < w:
            acc = acc[span:] + acc[:-span]
            span *= 2
        win = acc[POOL_HALO - (w - 1):]
        cnt = jnp.minimum(t_in_seq + 1, w).astype(F32)
        mixed = win / cnt - e[POOL_HALO:]
        pooled.append(dot(mixed.astype(BF16), wpg_ref[g * POOL_GROUP_DIM:(g + 1) * POOL_GROUP_DIM, :]))
    y_pool_in = jnp.concatenate(pooled, axis=1) * ps_ref[...]
    y_pool = dot(y_pool_in.astype(BF16), wpb_ref[...])

    merged = sga_ref[...].astype(F32) * y_att + sgp_ref[...].astype(F32) * y_pool
    mix_out = dot(merged.astype(BF16), wout_ref[...])
    gate_m = mod_ref[0, 2:3, :]
    shift_f = mod_ref[0, 3:4, :]
    scale_f = mod_ref[0, 4:5, :]
    x1 = x_ref[...] + gate_m * (_rms(mix_out) * gpm_ref[...])
    x1_ref[...] = x1
    u2 = (_rms(x1) * gpf_ref[...]) * (1.0 + scale_f) + shift_f

    logits = _split_dot(wrt_ref[...], u2, (((1,), (1,)), ((), ()))) + br_ref[...]
    erow = lax.broadcasted_iota(jnp.int32, logits.shape, 0)
    work = logits
    vals, idxs = [], []
    for _ in range(TOP_K):
        mx = jnp.max(work, axis=0, keepdims=True)
        ix = jnp.min(jnp.where(work == mx, erow, n_experts), axis=0, keepdims=True)
        vals.append(mx)
        idxs.append(ix)
        work = jnp.where(erow == ix, -jnp.inf, work)
    exps = [jnp.exp(vv - vals[0]) for vv in vals]
    denom = exps[0] + exps[1] + exps[2] + exps[3]
    prob_ref[...] = jnp.concatenate([e / denom for e in exps], axis=0)

    onehot = jnp.zeros(logits.shape, F32)
    for ix in idxs:
        onehot = onehot + (erow == ix).astype(F32)
    rr = lax.broadcasted_iota(jnp.int32, (tm, tm), 0)
    cc = lax.broadcasted_iota(jnp.int32, (tm, tm), 1)
    earlier = (rr < cc).astype(BF16)
    carry = carry_ref[...]
    before = dot(onehot.astype(BF16), earlier) + carry

    blk_rows = float(TM_EXP)
    total = carry + jnp.sum(onehot, axis=1, keepdims=True)
    blocks_old = jnp.floor((carry + (blk_rows - 1.0)) / blk_rows)
    opened = jnp.floor((total + (blk_rows - 1.0)) / blk_rows) - blocks_old
    e_r = lax.broadcasted_iota(jnp.int32, (n_experts, n_experts), 0)
    e_c = lax.broadcasted_iota(jnp.int32, (n_experts, n_experts), 1)
    lower = (e_c < e_r).astype(BF16)
    opened_before = dot(lower, jnp.broadcast_to(opened, (n_experts, LANES)).astype(BF16))[:, 0:1]
    new_blk = free_ref[...] + opened_before
    cur_blk = cur_ref[...]
    boundary = blocks_old * blk_rows
    blk_of = jnp.where(before < boundary, cur_blk, new_blk)
    pos_rows = []
    for ix in idxs:
        pick = erow == ix
        rank = jnp.sum(jnp.where(pick, before, 0.0), axis=0, keepdims=True)
        blk_id = jnp.sum(jnp.where(pick, blk_of, 0.0), axis=0, keepdims=True)
        within = rank - jnp.floor(rank / blk_rows) * blk_rows
        pos_rows.append((blk_id * blk_rows + within).astype(jnp.int32))
    pos = jnp.concatenate(pos_rows, axis=0)
    pos_ref[...] = pos
    tab_col = lax.broadcasted_iota(jnp.int32, tab_ref.shape, 1).astype(F32)
    tab_ref[...] = jnp.where(jnp.logical_and(tab_col == blocks_old, opened > 0.0),
                             new_blk.astype(jnp.int32), tab_ref[...])
    cur_ref[...] = jnp.where(opened > 0.0, new_blk, cur_blk)
    free_ref[...] = free_ref[...] + jnp.sum(opened, axis=0, keepdims=True)
    carry_ref[...] = total
    cnt_ref[...] = total.astype(jnp.int32)

    @pl.when(i > 0)
    def _():
        wait_rows(slot)

    for s in range(n_sub):
        u2t[slot, pl.ds(s, tm, stride=n_sub), :] = u2[:, s * LANES:(s + 1) * LANES]
    pos_v[slot] = pos
    pos_copy(slot).start()

    @pl.when(i == n_steps - 1)
    def _():
        pos_copy(slot).wait()

        def issue(r, c):
            for k in range(TOP_K):
                row_copy(slot, k, r).start()
            return c
        lax.fori_loop(0, tm, issue, 0)

        eye = e_r == e_c
        as_row = lambda col: jnp.sum(jnp.where(eye, col, 0.0), axis=0, keepdims=True).astype(jnp.int32)
        used = total - jnp.floor(total / blk_rows) * blk_rows
        meta = jnp.concatenate([as_row(used), as_row(cur_ref[...]),
                                jnp.broadcast_to(free_ref[...].astype(jnp.int32), (1, n_experts))], axis=0)
        meta_v[...] = jnp.zeros_like(meta_v)
        meta_v[0:3, 0:n_experts] = meta
        meta_copy = pltpu.make_async_copy(meta_v, meta_s, sem_zero)
        meta_copy.start()
        meta_copy.wait()

        sizes = [TM_EXP >> (b + 1) for b in range(TM_EXP.bit_length() - 1)]
        zero_run = lambda row0, n: pltpu.make_async_copy(
            zeros_v.at[pl.ds(0, n * n_sub), :],
            xs_hbm.at[pl.ds(pl.multiple_of(row0 * n_sub, n_sub), n * n_sub), :], sem_zero)
        issued = [jnp.int32(0) for _ in sizes]
        for e in range(n_experts):
            first = meta_s[0, e]
            row0 = meta_s[1, e] * TM_EXP + first
            pad = jnp.where(first > 0, TM_EXP - first, 0)
            for b, n in enumerate(sizes):
                take = (pad & n) != 0

                @pl.when(take)
                def _(row0=row0, n=n):
                    zero_run(row0, n).start()
                issued[b] = issued[b] + take.astype(jnp.int32)
                row0 = row0 + jnp.where(take, n, 0)
        n_open = meta_s[2, 0]

        def whole(b, c):
            zero_run(b * TM_EXP, TM_EXP).start()
            return c
        lax.fori_loop(n_open, n_blocks, whole, 0)

        for b, n in enumerate(sizes):
            lax.fori_loop(0, issued[b], lambda _, c, n=n: (zero_run(0, n).wait(), c)[1], 0)
        lax.fori_loop(n_open, n_blocks, lambda _, c: (zero_run(0, TM_EXP).wait(), c)[1], 0)
        wait_rows(prev)
        wait_rows(slot)


def _post_mix(o, p, sga, sgp, x2, mod3, w_att, w_pg, pool_scale, w_pb, w_out,
              g_post_mix, g_pre_ffn, w_router_t, b_router_col, seq):
    t, d = x2.shape
    tm = TM_MIX
    steps_per_seq = seq // tm
    n_experts = w_router_t.shape[0]
    pw = p.shape[1]
    n_sub = d // LANES
    halo_blocks = tm // POOL_HALO
    row = lambda w: pl.BlockSpec((tm, w), lambda i: (i, 0))
    col = lambda h: pl.BlockSpec((h, tm), lambda i: (0, i))
    full2 = lambda a: pl.BlockSpec(a.shape, lambda i: (0, 0))
    hbm = pl.BlockSpec(memory_space=pl.ANY)
    assert tm <= TM_EXP and (tm * TOP_K) % TM_EXP == 0 and t % TM_EXP == 0
    n_blocks = (t * TOP_K) // TM_EXP + n_experts
    tab_w = -(-(t // TM_EXP) // LANES) * LANES
    kern = functools.partial(_post_mix_kernel, steps_per_seq=steps_per_seq, n_blocks=n_blocks)
    xs_rows = n_blocks * TM_EXP + TOP_K * tm
    return pl.pallas_call(
        kern,
        grid=(t // tm,),
        in_specs=[
            row(d), row(pw),
            pl.BlockSpec((POOL_HALO, pw), lambda i: (jnp.maximum(i * halo_blocks - 1, 0), 0)),
            row(d), row(d), row(d),
            pl.BlockSpec((1, N_MOD, d), lambda i: (i // steps_per_seq, 0, 0)),
            hbm, hbm, full2(pool_scale), hbm, hbm, full2(g_post_mix), full2(g_pre_ffn),
            full2(w_router_t), full2(b_router_col),
        ],
        out_specs=[
            row(d), col(TOP_K), col(TOP_K),
            pl.BlockSpec((n_experts, 1), lambda i: (0, 0)),
            pl.BlockSpec((n_experts, tab_w), lambda i: (0, 0)),
            pl.BlockSpec(memory_space=pl.ANY),
        ],
        out_shape=[
            jax.ShapeDtypeStruct((t, d), F32),
            jax.ShapeDtypeStruct((TOP_K, t), F32),
            jax.ShapeDtypeStruct((TOP_K, t), jnp.int32),
            jax.ShapeDtypeStruct((n_experts, 1), jnp.int32),
            jax.ShapeDtypeStruct((n_experts, tab_w), jnp.int32),
            jax.ShapeDtypeStruct((xs_rows * n_sub, LANES), F32),
        ],
        scratch_shapes=[
            pltpu.VMEM((n_experts, 1), F32),
            pltpu.VMEM((n_experts, 1), F32),
            pltpu.VMEM((1, 1), F32),
            pltpu.VMEM((2, tm * n_sub, LANES), F32),
            pltpu.VMEM((TM_EXP * n_sub, LANES), F32),
            pltpu.VMEM((2, TOP_K, tm), jnp.int32),
            pltpu.SMEM((2, TOP_K, tm), jnp.int32),
            pltpu.VMEM((SUBLANES, LANES), jnp.int32),
            pltpu.SMEM((SUBLANES, LANES), jnp.int32),
            pltpu.VMEM(w_att.shape, BF16),
            pltpu.VMEM(w_pg.shape, BF16),
            pltpu.VMEM(w_pb.shape, BF16),
            pltpu.VMEM(w_out.shape, BF16),
            pltpu.VMEM((2, max(w_att.shape[0], w_out.shape[0]), WEIGHT_CHUNK), F32),
            pltpu.SemaphoreType.DMA((2,)),
            pltpu.SemaphoreType.DMA((2,)),
            pltpu.SemaphoreType.DMA((2,)),
            pltpu.SemaphoreType.DMA,
        ],
        compiler_params=_cparams(("arbitrary",)),
        name="post_mix",
    )(o, p, p, sga, sgp, x2, mod3, w_att, w_pg, pool_scale, w_pb, w_out,
      g_post_mix, g_pre_ffn, w_router_t, b_router_col)


def _experts_kernel(te_ref, xb_ref, yb_ref, nt_ref, nv_ref, xs_ref, wgu_ref, bgu_ref, wd_ref, bd_ref,
                    y_ref, wgu_bf, wd_bf, *, n_sub):
    j = pl.program_id(0)
    tm = TM_EXP
    half = tm // 2
    d_ff = wd_ref.shape[1]
    open_tile = j < nt_ref[0]

    @pl.when(open_tile)
    def _():
        changed = jnp.logical_or(j == 0, te_ref[j] != te_ref[jnp.maximum(j - 1, 0)])

        @pl.when(changed)
        def _():
            wgu_bf[...] = wgu_ref[0].astype(BF16)
            wd_bf[...] = wd_ref[0].astype(BF16)

    def mlp(rows):
        chunks = range(0, rows, half)
        gus = []
        for r0 in chunks:
            xs = jnp.concatenate(
                [xs_ref[pl.ds(r0 * n_sub + s, half, stride=n_sub), :] for s in range(n_sub)],
                axis=1).astype(BF16)
            gus.append(jnp.dot(xs, wgu_bf[...], preferred_element_type=F32) + bgu_ref[0])
        for r0, gu in zip(chunks, gus):
            gate = jnp.minimum(gu[:, :d_ff], SWIGLU_LIMIT)
            up = jnp.clip(gu[:, d_ff:], -SWIGLU_LIMIT, SWIGLU_LIMIT)
            act = (up + 1.0) * (gate * jax.nn.sigmoid(SWIGLU_ALPHA * gate))
            y = jnp.dot(act.astype(BF16), wd_bf[...], preferred_element_type=F32) + bd_ref[0]
            for s in range(n_sub):
                y_ref[pl.ds(r0 * n_sub + s, half, stride=n_sub), :] = y[:, s * LANES:(s + 1) * LANES]

    few = nv_ref[j] <= half

    @pl.when(jnp.logical_and(open_tile, jnp.logical_not(few)))
    def _():
        mlp(tm)

    @pl.when(jnp.logical_and(open_tile, few))
    def _():
        mlp(half)
        y_ref[pl.ds(half * n_sub, half * n_sub), :] = jnp.zeros((half * n_sub, LANES), F32)

    @pl.when(j >= nt_ref[0])
    def _():
        y_ref[...] = jnp.zeros_like(y_ref)


def _experts(tile_expert, xs_block, y_block, rows_valid, n_tiles, xs_tiles,
             w_gate_up, b_gate_up, w_down, b_down):
    n_exp, d, d_gu = w_gate_up.shape
    d_ff = w_down.shape[1]
    n_sub = d // LANES
    tm = TM_EXP
    max_tiles = tile_expert.shape[0]
    by_expert = lambda j, te, xb, yb, nt, nv: (te[j], 0, 0)
    by_block = lambda j, te, xb, yb, nt, nv: (yb[j], 0)
    grid_spec = pltpu.PrefetchScalarGridSpec(
        num_scalar_prefetch=5,
        grid=(max_tiles,),
        in_specs=[
            pl.BlockSpec((tm * n_sub, LANES), lambda j, te, xb, yb, nt, nv: (xb[j], 0)),
            pl.BlockSpec((1, d, d_gu), by_expert),
            pl.BlockSpec((1, 1, d_gu), by_expert),
            pl.BlockSpec((1, d_ff, d), by_expert),
            pl.BlockSpec((1, 1, d), by_expert),
        ],
        out_specs=pl.BlockSpec((tm * n_sub, LANES), by_block),
        scratch_shapes=[
            pltpu.VMEM((d, d_gu), BF16),
            pltpu.VMEM((d_ff, d), BF16),
        ],
    )
    return pl.pallas_call(
        functools.partial(_experts_kernel, n_sub=n_sub),
        grid_spec=grid_spec,
        out_shape=jax.ShapeDtypeStruct((max_tiles * tm * n_sub, LANES), F32),
        compiler_params=_cparams(("arbitrary",)),
        name="experts",
    )(tile_expert, xs_block, y_block, n_tiles, rows_valid, xs_tiles,
      w_gate_up, b_gate_up.reshape(n_exp, 1, d_gu), w_down, b_down.reshape(n_exp, 1, d))


def _combine_kernel(pos_ref, y_hbm, prob_ref, x1_ref, mod_ref, g_ref, o_ref, ybuf, sem,
                    *, n_sub, n_tokens):
    i = pl.program_id(0)
    n_steps = pl.num_programs(0)
    tm = TM_COMB
    slot = i % 2

    def gather(step, s):
        for r in range(tm):
            for k in range(TOP_K):
                row = pos_ref[k * n_tokens + step * tm + r]
                pltpu.make_async_copy(
                    y_hbm.at[pl.ds(pl.multiple_of(row * n_sub, n_sub), n_sub), :],
                    ybuf.at[s, k, pl.ds(r * n_sub, n_sub), :], sem.at[s]).start()

    @pl.when(i == 0)
    def _():
        gather(0, 0)

    @pl.when(i + 1 < n_steps)
    def _():
        gather(i + 1, 1 - slot)

    for k in range(TOP_K):
        pltpu.make_async_copy(y_hbm.at[pl.ds(0, tm * n_sub), :], ybuf.at[slot, k], sem.at[slot]).wait()

    prob = prob_ref[...]
    f = None
    for k in range(TOP_K):
        yk = jnp.concatenate(
            [ybuf[slot, k, pl.ds(s, tm, stride=n_sub), :] for s in range(n_sub)], axis=1)
        term = yk * prob[:, k:k + 1]
        f = term if f is None else f + term
    gate_f = mod_ref[0, 5:6, :]
    o_ref[...] = x1_ref[...] + gate_f * (_rms(f) * g_ref[...])


def _combine(pos_flat, y_tiles, probs, x1, mod3, g_post_ffn, seq):
    t, d = x1.shape
    tm = TM_COMB
    n_sub = d // LANES
    steps_per_seq = seq // tm
    grid_spec = pltpu.PrefetchScalarGridSpec(
        num_scalar_prefetch=1,
        grid=(t // tm,),
        in_specs=[
            pl.BlockSpec(memory_space=pl.ANY),
            pl.BlockSpec((tm, TOP_K), lambda i, pos: (i, 0)),
            pl.BlockSpec((tm, d), lambda i, pos: (i, 0)),
            pl.BlockSpec((1, N_MOD, d), lambda i, pos: (i // steps_per_seq, 0, 0)),
            pl.BlockSpec((1, d), lambda i, pos: (0, 0)),
        ],
        out_specs=pl.BlockSpec((tm, d), lambda i, pos: (i, 0)),
        scratch_shapes=[
            pltpu.VMEM((2, TOP_K, tm * n_sub, LANES), F32),
            pltpu.SemaphoreType.DMA((2,)),
        ],
    )
    return pl.pallas_call(
        functools.partial(_combine_kernel, n_sub=n_sub, n_tokens=t),
        grid_spec=grid_spec,
        out_shape=jax.ShapeDtypeStruct((t, d), F32),
        compiler_params=_cparams(("arbitrary",)),
        name="combine",
    )(pos_flat, y_tiles, probs, x1, mod3, g_post_ffn)


def _rope_freq_row():
    half = ROPE_DIM // 2
    inv_freq = ROPE_THETA ** (-jnp.arange(0, ROPE_DIM, 2, dtype=F32) / ROPE_DIM)
    head = jnp.concatenate([inv_freq, inv_freq, jnp.zeros((DA_HEAD_DIM - 2 * half,), F32)])
    return jnp.tile(head, LANES // DA_HEAD_DIM).reshape(1, LANES)


def _tile_tables(counts, block_tab, n_tokens):
    tm = TM_EXP
    n_exp = counts.shape[0]
    max_tiles = (n_tokens * TOP_K) // tm + n_exp
    tiles = (counts + tm - 1) // tm
    tile_end = jnp.cumsum(tiles)
    n_tiles = tile_end[-1]
    j = jnp.arange(max_tiles, dtype=jnp.int32)
    jj = jnp.minimum(j, n_tiles - 1)
    past = (jj[:, None] >= tile_end[None, :]).astype(jnp.int32)
    tile_expert = jnp.sum(past, axis=1)
    nth = jj - jnp.sum(past * tiles[None, :], axis=1)
    open_block = block_tab[tile_expert, nth]
    y_block = jnp.where(j < n_tiles, open_block, j)
    is_expert = (tile_expert[:, None] == jnp.arange(n_exp, dtype=jnp.int32)[None, :]).astype(jnp.int32)
    rows_valid = jnp.where(j < n_tiles,
                           jnp.minimum(jnp.sum(is_expert * counts[None, :], axis=1) - nth * tm, tm), 0)
    as_i32 = lambda a: a.astype(jnp.int32)
    return (as_i32(tile_expert), as_i32(open_block), as_i32(y_block), as_i32(rows_valid),
            as_i32(n_tiles.reshape(1)))


def kernel(x, c, positions, w_mod, b_mod, g_pre_mix, w_in, lambda_q1, lambda_k1, lambda_q2, lambda_k2, g_sub, w_pool_group, pool_scale, w_att_branch, w_pool_branch, w_out, g_post_mix, g_pre_ffn, w_router, b_router, w_gate_up, b_gate_up, w_down, b_down, g_post_ffn):
    bsz, seq, d = x.shape
    assert w_mod.shape[0] == 1, "single layer"
    assert seq % TQ == 0 and seq % TM_PROJ == 0 and seq % TM_MIX == 0 and seq % TM_COMB == 0
    t = bsz * seq
    assert t % TM_EXP == 0
    x2 = x.reshape(t, d)
    qk_w = 2 * DA_HEADS * DA_HEAD_DIM
    v_w = DA_HEADS * DA_V_DIM
    pool_w = len(POOL_WINDOWS) * POOL_GROUP_DIM
    n_experts = w_router.shape[2]

    mod3 = _mod(c, w_mod[0], b_mod[0]).reshape(bsz, N_MOD, d)

    bounds = [0, qk_w, 2 * qk_w, 2 * qk_w + v_w, 2 * qk_w + v_w + pool_w,
              2 * qk_w + v_w + pool_w + d, 2 * qk_w + v_w + pool_w + 2 * d]
    q, k, vt, p, sga, sgp = _in_proj(x2, mod3, g_pre_mix, positions.reshape(t, 1), _rope_freq_row(),
                                     w_in[0], bounds, seq)

    row64 = lambda a: a.reshape(1, DA_HEAD_DIM)
    o = _attention(q, k, vt, row64(lambda_q1[0]), row64(lambda_k1[0]), row64(lambda_q2[0]),
                   row64(lambda_k2[0]), g_sub.reshape(DA_V_DIM, 1), bsz, seq)

    x1, probs_t, pos_t, counts, block_tab, xs_tiles = _post_mix(
        o, p, sga, sgp, x2, mod3, w_att_branch[0], w_pool_group[0].reshape(pool_w, POOL_GROUP_DIM),
        pool_scale, w_pool_branch[0], w_out[0],
        g_post_mix, g_pre_ffn, w_router[0].T, b_router.reshape(n_experts, 1), seq)

    tile_expert, xs_block, y_block, rows_valid, n_tiles = _tile_tables(counts[:, 0], block_tab, t)
    y_tiles = _experts(tile_expert, xs_block, y_block, rows_valid, n_tiles, xs_tiles,
                       w_gate_up[0], b_gate_up[0], w_down[0], b_down[0])
    out = _combine(pos_t.reshape(-1), y_tiles, probs_t.T, x1, mod3, g_post_ffn, seq)
    return out.reshape(bsz, seq, d)
```

```python
import functools

import jax
import jax.numpy as jnp
from jax import lax
from jax.experimental import pallas as pl
from jax.experimental.pallas import tpu as pltpu

F32 = jnp.float32
BF16 = jnp.bfloat16

NORM_EPS = 1e-6
CHUNK = 64
DA_HEADS = 8
DA_HEAD_DIM = 64
DA_V_DIM = 2 * DA_HEAD_DIM
ROPE_THETA = 500000.0
ROPE_DIM = DA_HEAD_DIM // 4
POOL_WINDOWS = (2, 4, 8, 16)
POOL_GROUP_DIM = 128
TOP_K = 4
SWIGLU_LIMIT = 7.0
SWIGLU_ALPHA = 1.702
N_MOD = 6
LAMBDA_INIT = 0.8 - 0.6 * 1.0
Q_SCALE = (DA_HEAD_DIM ** -0.5) * 1.4426950408889634

LANES = 128
SUBLANES = 8
VMEM_LIMIT = 56 * 1024 * 1024

TM_PROJ = 512
TQ = 512
TK = 256
ATT_HEADS_PER_STEP = 4
ATT_SUM_ROWS = 16
TM_MIX = 256
TM_EXP = 512
TM_COMB = 256
WEIGHT_CHUNK = 512
POOL_HALO = 16
NEG_BIG = -1e30


def _cparams(sem):
    return pltpu.CompilerParams(dimension_semantics=sem, vmem_limit_bytes=VMEM_LIMIT)


def _split(a):
    hi = a.astype(BF16)
    return hi, (a - hi.astype(F32)).astype(BF16)


def _split_dot(a, b, dims=(((1,), (0,)), ((), ()))):
    a_hi, a_lo = _split(a)
    b_hi, b_lo = _split(b)
    dot = lambda x, y: lax.dot_general(x, y, dims, preferred_element_type=F32)
    return dot(a_hi, b_hi) + (dot(a_hi, b_lo) + dot(a_lo, b_hi))


def _rms(x):
    return x * lax.rsqrt(jnp.mean(x * x, axis=-1, keepdims=True) + NORM_EPS)


def _mod_kernel(c_ref, w_ref, b_ref, o_ref):
    c = c_ref[...]
    c_act = c * jax.nn.sigmoid(c)
    o_ref[...] = _split_dot(c_act, w_ref[...]) + b_ref[...]


def _mod(c, w_mod, b_mod):
    bsz, d = c.shape
    n = w_mod.shape[1]
    tn = 1024
    return pl.pallas_call(
        _mod_kernel,
        grid=(n // tn,),
        in_specs=[
            pl.BlockSpec((bsz, d), lambda j: (0, 0)),
            pl.BlockSpec((d, tn), lambda j: (0, j)),
            pl.BlockSpec((1, tn), lambda j: (0, j)),
        ],
        out_specs=pl.BlockSpec((bsz, tn), lambda j: (0, j)),
        out_shape=jax.ShapeDtypeStruct((bsz, n), F32),
        compiler_params=_cparams(("parallel",)),
        name="mod",
    )(c, w_mod, b_mod.reshape(1, n))


def _rope(t, cos_t, sin_a, sin_b):
    n = t.shape[1]
    up = pltpu.roll(t, n - ROPE_DIM // 2, axis=1)
    dn = pltpu.roll(t, ROPE_DIM // 2, axis=1)
    reps = n // LANES
    tile = lambda a: jnp.concatenate([a] * reps, axis=1)
    return t * tile(cos_t) + up * tile(sin_a) + dn * tile(sin_b)


def _stream_windows(src_hbm, windows, stage, sem, sink):
    def copy(n):
        r0, nr, c0, nc = windows[n]
        return pltpu.make_async_copy(src_hbm.at[pl.ds(r0, nr), pl.ds(c0, nc)],
                                     stage.at[n % 2, pl.ds(0, nr), pl.ds(0, nc)], sem.at[n % 2])
    copy(0).start()
    for n, (_, nr, _, nc) in enumerate(windows):
        if n + 1 < len(windows):
            copy(n + 1).start()
        copy(n).wait()
        sink(n, stage[n % 2, 0:nr, 0:nc])


def _in_proj_kernel(x_ref, mod_ref, g_ref, pos_ref, freq_ref, w_hbm,
                    q_ref, k_ref, v_ref, p_ref, sga_ref, sgp_ref,
                    w_ref, wvt_ref, stage, sem, *, bounds):
    @pl.when(pl.program_id(0) == 0)
    def _():
        d_in, width = w_hbm.shape
        windows = [(0, d_in, c0, WEIGHT_CHUNK) for c0 in range(0, width, WEIGHT_CHUNK)]

        def sink(n, tile):
            c0 = windows[n][2]
            w_ref[:, c0:c0 + WEIGHT_CHUNK] = tile.astype(BF16)
            if bounds[2] <= c0 < bounds[3]:
                wvt_ref[c0 - bounds[2]:c0 - bounds[2] + WEIGHT_CHUNK, :] = tile.T.astype(BF16)
        _stream_windows(w_hbm, windows, stage, sem, sink)

    x = x_ref[...]
    shift = mod_ref[0, 0:1, :]
    scale = mod_ref[0, 1:2, :]
    u = (_rms(x) * g_ref[...]) * (1.0 + scale) + shift
    ub = u.astype(BF16)
    dot = functools.partial(jnp.dot, preferred_element_type=F32)
    part = lambda n: w_ref[:, bounds[n]:bounds[n + 1]]

    ang = pos_ref[...].astype(F32) * freq_ref[...]
    cos_t, sn = jnp.cos(ang), jnp.sin(ang)
    in_head = lax.broadcasted_iota(jnp.int32, ang.shape, 1) % DA_HEAD_DIM
    sin_a = jnp.where(in_head < ROPE_DIM // 2, -sn, 0.0)
    sin_b = jnp.where(in_head >= ROPE_DIM // 2, sn, 0.0)

    q = _rope(dot(ub, part(0)), cos_t, sin_a, sin_b)
    q_ref[...] = (q * Q_SCALE).astype(BF16)
    k = _rope(dot(ub, part(1)), cos_t, sin_a, sin_b)
    k_ref[...] = k.astype(BF16)
    vt = lax.dot_general(wvt_ref[...], ub, (((1,), (1,)), ((), ())), preferred_element_type=F32)
    for n in range(v_ref.shape[0]):
        v_ref[n] = vt[:, n * TK:(n + 1) * TK].astype(BF16)
    p_ref[...] = dot(ub, part(3))
    sga_ref[...] = jax.nn.sigmoid(dot(ub, part(4))).astype(BF16)
    sgp_ref[...] = jax.nn.sigmoid(dot(ub, part(5))).astype(BF16)


def _in_proj(x2, mod3, g_pre, pos_col, freq_row, w_in, bounds, seq):
    t, d = x2.shape
    width = w_in.shape[1]
    assert width % WEIGHT_CHUNK == 0 and all(b % WEIGHT_CHUNK == 0 for b in bounds[2:4])
    tm = TM_PROJ
    assert tm % TK == 0, "v is emitted as transposed (channels, TK) slabs"
    steps_per_seq = seq // tm
    widths = [b - a for a, b in zip(bounds[:-1], bounds[1:])]
    row = lambda w: pl.BlockSpec((tm, w), lambda i: (i, 0))
    full = lambda a: pl.BlockSpec(a.shape, lambda i: (0, 0))
    out_specs = [row(widths[0]), row(widths[1]),
                 pl.BlockSpec((tm // TK, widths[2], TK), lambda i: (i, 0, 0)),
                 row(widths[3]), row(widths[4]), row(widths[5])]
    out_shape = [jax.ShapeDtypeStruct((t, widths[0]), BF16),
                 jax.ShapeDtypeStruct((t, widths[1]), BF16),
                 jax.ShapeDtypeStruct((t // TK, widths[2], TK), BF16),
                 jax.ShapeDtypeStruct((t, widths[3]), F32),
                 jax.ShapeDtypeStruct((t, widths[4]), BF16),
                 jax.ShapeDtypeStruct((t, widths[5]), BF16)]
    return pl.pallas_call(
        functools.partial(_in_proj_kernel, bounds=tuple(bounds)),
        grid=(t // tm,),
        in_specs=[
            row(d),
            pl.BlockSpec((1, N_MOD, d), lambda i: (i // steps_per_seq, 0, 0)),
            full(g_pre), row(1), full(freq_row), pl.BlockSpec(memory_space=pl.ANY),
        ],
        out_specs=out_specs,
        out_shape=out_shape,
        scratch_shapes=[
            pltpu.VMEM((d, width), BF16),
            pltpu.VMEM((widths[2], d), BF16),
            pltpu.VMEM((2, d, WEIGHT_CHUNK), F32),
            pltpu.SemaphoreType.DMA((2,)),
        ],
        compiler_params=_cparams(("arbitrary",)),
        name="in_proj",
    )(x2, mod3, g_pre, pos_col, freq_row, w_in)


def _attn_kernel(lq1_ref, lk1_ref, lq2_ref, lk2_ref, g_ref, q_ref, k_ref, vt_ref, o_ref,
                 s_a, s_b, p_a, p_b, acc_buf):
    assert TQ == 2 * TK and TK % CHUNK == 0
    seq = q_ref.shape[0]
    lam = (jnp.exp(jnp.sum(lq1_ref[...] * lk1_ref[...], axis=-1, keepdims=True))
           - jnp.exp(jnp.sum(lq2_ref[...] * lk2_ref[...], axis=-1, keepdims=True))
           + LAMBDA_INIT)
    dot = functools.partial(jnp.dot, preferred_element_type=F32)
    heads = range(ATT_HEADS_PER_STEP)
    lanes = lambda h: slice(h * LANES, (h + 1) * LANES)

    def q_tile(qi, c):
        q_rows = pl.ds(pl.multiple_of(qi * TQ, TQ), TQ)
        qqt = []
        for h in heads:
            qt = q_ref[q_rows, lanes(h)].astype(F32).T
            row = lax.broadcasted_iota(jnp.int32, qt.shape, 0)
            zero = jnp.zeros_like(qt)
            qqt.append(jnp.concatenate([jnp.where(row < DA_HEAD_DIM, qt, zero),
                                        jnp.where(row >= DA_HEAD_DIM, qt, zero)],
                                       axis=1).astype(BF16))

        def scores(j, h):
            return dot(k_ref[pl.ds(pl.multiple_of(j * TK, TK), TK), lanes(h)], qqt[h])

        def softmax_step(s, m):
            m_new = jnp.maximum(m, jnp.max(s, axis=0, keepdims=True))
            return m_new, jnp.exp2(m - m_new), jnp.exp2(s - m_new).astype(BF16)

        ones_rows = jnp.ones((ATT_SUM_ROWS, TK), BF16)

        def pv(j, h, p):
            return dot(jnp.concatenate([vt_ref[j, lanes(h), :], ones_rows], axis=0), p)

        kk = lax.broadcasted_iota(jnp.int32, (TK, 2 * TQ), 0)
        qq = lax.broadcasted_iota(jnp.int32, (TK, 2 * TQ), 1)
        rel_chunk = jnp.where(qq >= TQ, qq - TQ, qq) // CHUNK - kk // CHUNK

        def masked(s, j):
            return jnp.where(rel_chunk >= j * (TK // CHUNK) - qi * (TQ // CHUNK), s, NEG_BIG)

        def step(j, carries, s_cur, p_cur, s_nxt, p_prev):
            pend = [pv(jnp.maximum(j - 1, 0), h, p_prev[h]) for h in heads]
            for h in heads:
                s_nxt[h] = scores(j + 1, h)
            out = []
            for h in heads:
                m, alpha = carries[h]
                acc_buf[h] = alpha * acc_buf[h] + pend[h]
                m, alpha, p = softmax_step(s_cur[h], m)
                p_cur[h] = p
                out.append((m, alpha))
            return tuple(out)

        def pair(i, carries):
            carries = step(2 * i, carries, s_a, p_a, s_b, p_b)
            return step(2 * i + 1, carries, s_b, p_b, s_a, p_a)

        for h in heads:
            s_a[h] = scores(0, h)
            p_b[h] = jnp.zeros((TK, 2 * TQ), BF16)
            acc_buf[h] = jnp.zeros(acc_buf.shape[1:], F32)
        init = tuple((jnp.full((1, 2 * TQ), NEG_BIG, F32), jnp.ones((1, 2 * TQ), F32))
                     for _ in heads)
        carries = lax.fori_loop(0, qi, pair, init)
        ja = 2 * qi
        late = lambda a: jnp.concatenate([a[:, TK:TQ], a[:, TQ + TK:]], axis=1)
        pend = [pv(jnp.maximum(ja - 1, 0), h, p_b[h]) for h in heads]
        for h in heads:
            k_last = k_ref[pl.ds(pl.multiple_of((ja + 1) * TK, TK), TK), lanes(h)]
            s_b[h, :, 0:TQ] = dot(k_last, late(qqt[h]))
        mid = []
        for h in heads:
            m, alpha = carries[h]
            acc_buf[h] = alpha * acc_buf[h] + pend[h]
            m, alpha, p = softmax_step(masked(s_a[h], ja), m)
            p_a[h] = p
            mid.append((m, alpha))
        pend = [pv(ja, h, p_a[h]) for h in heads]
        k_in = lax.broadcasted_iota(jnp.int32, (TK, TQ), 0)
        q_in = lax.broadcasted_iota(jnp.int32, (TK, TQ), 1) % TK
        diagonal = k_in // CHUNK <= q_in // CHUNK
        for h in heads:
            m, alpha = mid[h]
            acc = alpha * acc_buf[h] + pend[h]
            m_l, alpha_l, p_l = softmax_step(jnp.where(diagonal, s_b[h, :, 0:TQ], NEG_BIG), late(m))
            acc_l = alpha_l * late(acc) + pv(ja + 1, h, p_l)
            acc = jnp.concatenate([acc[:, :TK], acc_l[:, :TK], acc[:, TQ:TQ + TK], acc_l[:, TK:]], axis=1)
            o = acc[:DA_V_DIM] / acc[DA_V_DIM:DA_V_DIM + 1]
            a = o[:, :TQ] - lam * o[:, TQ:]
            y = a * lax.rsqrt(jnp.mean(a * a, axis=0, keepdims=True) + NORM_EPS)
            y = (y * g_ref[...]) * (1.0 - LAMBDA_INIT)
            o_ref[q_rows, lanes(h)] = y.T.astype(BF16)
        return c

    lax.fori_loop(0, seq // TQ, q_tile, 0)


def _attention(q, k, vt, lq1, lk1, lq2, lk2, g_col, bsz, seq):
    t = q.shape[0]
    nkv = seq // TK
    hw = ATT_HEADS_PER_STEP * LANES
    vec = lambda a: pl.BlockSpec(a.shape, lambda b, h: (0, 0))
    return pl.pallas_call(
        _attn_kernel,
        grid=(bsz, DA_HEADS // ATT_HEADS_PER_STEP),
        in_specs=[
            vec(lq1), vec(lk1), vec(lq2), vec(lk2), vec(g_col),
            pl.BlockSpec((seq, hw), lambda b, h: (b, h)),
            pl.BlockSpec((seq, hw), lambda b, h: (b, h)),
            pl.BlockSpec((nkv, hw, TK), lambda b, h: (b, h, 0)),
        ],
        out_specs=pl.BlockSpec((seq, hw), lambda b, h: (b, h)),
        out_shape=jax.ShapeDtypeStruct((t, DA_HEADS * DA_V_DIM), BF16),
        scratch_shapes=[
            pltpu.VMEM((ATT_HEADS_PER_STEP, TK, 2 * TQ), F32),
            pltpu.VMEM((ATT_HEADS_PER_STEP, TK, 2 * TQ), F32),
            pltpu.VMEM((ATT_HEADS_PER_STEP, TK, 2 * TQ), BF16),
            pltpu.VMEM((ATT_HEADS_PER_STEP, TK, 2 * TQ), BF16),
            pltpu.VMEM((ATT_HEADS_PER_STEP, DA_V_DIM + ATT_SUM_ROWS, 2 * TQ), F32),
        ],
        compiler_params=_cparams(("parallel", "parallel")),
        name="attention",
    )(lq1, lk1, lq2, lk2, g_col, q, k, vt)


def _post_mix_kernel(o_ref, p_ref, ph_ref, sga_ref, sgp_ref, x_ref, mod_ref,
                     watt_hbm, wpg_hbm, ps_ref, wpb_hbm, wout_hbm, gpm_ref, gpf_ref,
                     wrt_ref, br_ref,
                     x1_ref, prob_ref, pos_ref, cnt_ref, tab_ref, xs_hbm,
                     carry_ref, cur_ref, free_ref, u2t, zeros_v, pos_v, pos_s, meta_v, meta_s,
                     watt_ref, wpg_ref, wpb_ref, wout_ref, stage, sem_w,
                     sem_rows, sem_pos, sem_zero,
                     *, steps_per_seq, n_blocks):
    i = pl.program_id(0)
    n_steps = pl.num_programs(0)
    tm = x_ref.shape[0]
    n_experts = wrt_ref.shape[0]
    n_sub = x_ref.shape[1] // LANES
    slot = i % 2
    prev = 1 - slot
    dot = functools.partial(jnp.dot, preferred_element_type=F32)

    def pos_copy(s):
        return pltpu.make_async_copy(pos_v.at[s], pos_s.at[s], sem_pos.at[s])

    def row_copy(s, k, r):
        dst = pos_s[s, k, r]
        return pltpu.make_async_copy(
            u2t.at[s, pl.ds(r * n_sub, n_sub), :],
            xs_hbm.at[pl.ds(pl.multiple_of(dst * n_sub, n_sub), n_sub), :], sem_rows.at[s])

    def wait_rows(s):
        for _ in range(TOP_K):
            pltpu.make_async_copy(u2t.at[s], xs_hbm.at[pl.ds(0, tm * n_sub), :], sem_rows.at[s]).wait()

    @pl.when(i == 0)
    def _():
        carry_ref[...] = jnp.zeros_like(carry_ref)
        cur_ref[...] = jnp.zeros_like(cur_ref)
        free_ref[...] = jnp.zeros_like(free_ref)
        tab_ref[...] = jnp.zeros_like(tab_ref)
        zeros_v[...] = jnp.zeros_like(zeros_v)
        for src, dst in ((watt_hbm, watt_ref), (wpg_hbm, wpg_ref), (wpb_hbm, wpb_ref),
                         (wout_hbm, wout_ref)):
            rows, cols = src.shape
            step_c = min(cols, WEIGHT_CHUNK)
            windows = [(0, rows, c0, step_c) for c0 in range(0, cols, step_c)]

            def sink(n, tile, dst=dst, windows=windows):
                c0, nc = windows[n][2], windows[n][3]
                dst[:, c0:c0 + nc] = tile.astype(BF16)
            _stream_windows(src, windows, stage, sem_w, sink)
        u2t[1] = jnp.zeros(u2t.shape[1:], F32)
        spare = (n_blocks * TM_EXP
                 + lax.broadcasted_iota(jnp.int32, (TOP_K, tm), 0) * tm
                 + lax.broadcasted_iota(jnp.int32, (TOP_K, tm), 1))
        pos_v[1] = spare
        pos_copy(1).start()

    pos_copy(prev).wait()

    for r in range(tm):
        for k in range(TOP_K):
            row_copy(prev, k, r).start()

    y_att = dot(o_ref[...], watt_ref[...])

    first = (i % steps_per_seq) == 0
    halo = jnp.where(first, 0.0, ph_ref[...])
    ext = jnp.concatenate([halo, p_ref[...]], axis=0)
    t_in_seq = (i % steps_per_seq) * tm + lax.broadcasted_iota(jnp.int32, (tm, 1), 0)
    pooled = []
    for g, w in enumerate(POOL_WINDOWS):
        e = ext[:, g * POOL_GROUP_DIM:(g + 1) * POOL_GROUP_DIM]
        acc, span = e, 1
        while span < w:
            acc = acc[span:] + acc[:-span]
            span *= 2
        win = acc[POOL_HALO - (w - 1):]
        cnt = jnp.minimum(t_in_seq + 1, w).astype(F32)
        mixed = win / cnt - e[POOL_HALO:]
        pooled.append(dot(mixed.astype(BF16), wpg_ref[g * POOL_GROUP_DIM:(g + 1) * POOL_GROUP_DIM, :]))
    y_pool_in = jnp.concatenate(pooled, axis=1) * ps_ref[...]
    y_pool = dot(y_pool_in.astype(BF16), wpb_ref[...])

    merged = sga_ref[...].astype(F32) * y_att + sgp_ref[...].astype(F32) * y_pool
    mix_out = dot(merged.astype(BF16), wout_ref[...])
    gate_m = mod_ref[0, 2:3, :]
    shift_f = mod_ref[0, 3:4, :]
    scale_f = mod_ref[0, 4:5, :]
    x1 = x_ref[...] + gate_m * (_rms(mix_out) * gpm_ref[...])
    x1_ref[...] = x1
    u2 = (_rms(x1) * gpf_ref[...]) * (1.0 + scale_f) + shift_f

    logits = _split_dot(wrt_ref[...], u2, (((1,), (1,)), ((), ()))) + br_ref[...]
    erow = lax.broadcasted_iota(jnp.int32, logits.shape, 0)
    work = logits
    vals, idxs = [], []
    for _ in range(TOP_K):
        mx = jnp.max(work, axis=0, keepdims=True)
        ix = jnp.min(jnp.where(work == mx, erow, n_experts), axis=0, keepdims=True)
        vals.append(mx)
        idxs.append(ix)
        work = jnp.where(erow == ix, -jnp.inf, work)
    exps = [jnp.exp(vv - vals[0]) for vv in vals]
    denom = exps[0] + exps[1] + exps[2] + exps[3]
    prob_ref[...] = jnp.concatenate([e / denom for e in exps], axis=0)

    onehot = jnp.zeros(logits.shape, F32)
    for ix in idxs:
        onehot = onehot + (erow == ix).astype(F32)
    rr = lax.broadcasted_iota(jnp.int32, (tm, tm), 0)
    cc = lax.broadcasted_iota(jnp.int32, (tm, tm), 1)
    earlier = (rr < cc).astype(BF16)
    carry = carry_ref[...]
    before = dot(onehot.astype(BF16), earlier) + carry

    blk_rows = float(TM_EXP)
    total = carry + jnp.sum(onehot, axis=1, keepdims=True)
    blocks_old = jnp.floor((carry + (blk_rows - 1.0)) / blk_rows)
    opened = jnp.floor((total + (blk_rows - 1.0)) / blk_rows) - blocks_old
    e_r = lax.broadcasted_iota(jnp.int32, (n_experts, n_experts), 0)
    e_c = lax.broadcasted_iota(jnp.int32, (n_experts, n_experts), 1)
    lower = (e_c < e_r).astype(BF16)
    opened_before = dot(lower, jnp.broadcast_to(opened, (n_experts, LANES)).astype(BF16))[:, 0:1]
    new_blk = free_ref[...] + opened_before
    cur_blk = cur_ref[...]
    boundary = blocks_old * blk_rows
    blk_of = jnp.where(before < boundary, cur_blk, new_blk)
    pos_rows = []
    for ix in idxs:
        pick = erow == ix
        rank = jnp.sum(jnp.where(pick, before, 0.0), axis=0, keepdims=True)
        blk_id = jnp.sum(jnp.where(pick, blk_of, 0.0), axis=0, keepdims=True)
        within = rank - jnp.floor(rank / blk_rows) * blk_rows
        pos_rows.append((blk_id * blk_rows + within).astype(jnp.int32))
    pos = jnp.concatenate(pos_rows, axis=0)
    pos_ref[...] = pos
    tab_col = lax.broadcasted_iota(jnp.int32, tab_ref.shape, 1).astype(F32)
    tab_ref[...] = jnp.where(jnp.logical_and(tab_col == blocks_old, opened > 0.0),
                             new_blk.astype(jnp.int32), tab_ref[...])
    cur_ref[...] = jnp.where(opened > 0.0, new_blk, cur_blk)
    free_ref[...] = free_ref[...] + jnp.sum(opened, axis=0, keepdims=True)
    carry_ref[...] = total
    cnt_ref[...] = total.astype(jnp.int32)

    @pl.when(i > 0)
    def _():
        wait_rows(slot)

    for s in range(n_sub):
        u2t[slot, pl.ds(s, tm, stride=n_sub), :] = u2[:, s * LANES:(s + 1) * LANES]
    pos_v[slot] = pos
    pos_copy(slot).start()

    @pl.when(i == n_steps - 1)
    def _():
        pos_copy(slot).wait()

        def issue(r, c):
            for k in range(TOP_K):
                row_copy(slot, k, r).start()
            return c
        lax.fori_loop(0, tm, issue, 0)

        eye = e_r == e_c
        as_row = lambda col: jnp.sum(jnp.where(eye, col, 0.0), axis=0, keepdims=True).astype(jnp.int32)
        used = total - jnp.floor(total / blk_rows) * blk_rows
        meta = jnp.concatenate([as_row(used), as_row(cur_ref[...]),
                                jnp.broadcast_to(free_ref[...].astype(jnp.int32), (1, n_experts))], axis=0)
        meta_v[...] = jnp.zeros_like(meta_v)
        meta_v[0:3, 0:n_experts] = meta
        meta_copy = pltpu.make_async_copy(meta_v, meta_s, sem_zero)
        meta_copy.start()
        meta_copy.wait()

        sizes = [TM_EXP >> (b + 1) for b in range(TM_EXP.bit_length() - 1)]
        zero_run = lambda row0, n: pltpu.make_async_copy(
            zeros_v.at[pl.ds(0, n * n_sub), :],
            xs_hbm.at[pl.ds(pl.multiple_of(row0 * n_sub, n_sub), n * n_sub), :], sem_zero)
        issued = [jnp.int32(0) for _ in sizes]
        for e in range(n_experts):
            first = meta_s[0, e]
            row0 = meta_s[1, e] * TM_EXP + first
            pad = jnp.where(first > 0, TM_EXP - first, 0)
            for b, n in enumerate(sizes):
                take = (pad & n) != 0

                @pl.when(take)
                def _(row0=row0, n=n):
                    zero_run(row0, n).start()
                issued[b] = issued[b] + take.astype(jnp.int32)
                row0 = row0 + jnp.where(take, n, 0)
        n_open = meta_s[2, 0]

        def whole(b, c):
            zero_run(b * TM_EXP, TM_EXP).start()
            return c
        lax.fori_loop(n_open, n_blocks, whole, 0)

        for b, n in enumerate(sizes):
            lax.fori_loop(0, issued[b], lambda _, c, n=n: (zero_run(0, n).wait(), c)[1], 0)
        lax.fori_loop(n_open, n_blocks, lambda _, c: (zero_run(0, TM_EXP).wait(), c)[1], 0)
        wait_rows(prev)
        wait_rows(slot)


def _post_mix(o, p, sga, sgp, x2, mod3, w_att, w_pg, pool_scale, w_pb, w_out,
              g_post_mix, g_pre_ffn, w_router_t, b_router_col, seq):
    t, d = x2.shape
    tm = TM_MIX
    steps_per_seq = seq // tm
    n_experts = w_router_t.shape[0]
    pw = p.shape[1]
    n_sub = d // LANES
    halo_blocks = tm // POOL_HALO
    row = lambda w: pl.BlockSpec((tm, w), lambda i: (i, 0))
    col = lambda h: pl.BlockSpec((h, tm), lambda i: (0, i))
    full2 = lambda a: pl.BlockSpec(a.shape, lambda i: (0, 0))
    hbm = pl.BlockSpec(memory_space=pl.ANY)
    assert tm <= TM_EXP and (tm * TOP_K) % TM_EXP == 0 and t % TM_EXP == 0
    n_blocks = (t * TOP_K) // TM_EXP + n_experts
    tab_w = -(-(t // TM_EXP) // LANES) * LANES
    kern = functools.partial(_post_mix_kernel, steps_per_seq=steps_per_seq, n_blocks=n_blocks)
    xs_rows = n_blocks * TM_EXP + TOP_K * tm
    return pl.pallas_call(
        kern,
        grid=(t // tm,),
        in_specs=[
            row(d), row(pw),
            pl.BlockSpec((POOL_HALO, pw), lambda i: (jnp.maximum(i * halo_blocks - 1, 0), 0)),
            row(d), row(d), row(d),
            pl.BlockSpec((1, N_MOD, d), lambda i: (i // steps_per_seq, 0, 0)),
            hbm, hbm, full2(pool_scale), hbm, hbm, full2(g_post_mix), full2(g_pre_ffn),
            full2(w_router_t), full2(b_router_col),
        ],
        out_specs=[
            row(d), col(TOP_K), col(TOP_K),
            pl.BlockSpec((n_experts, 1), lambda i: (0, 0)),
            pl.BlockSpec((n_experts, tab_w), lambda i: (0, 0)),
            pl.BlockSpec(memory_space=pl.ANY),
        ],
        out_shape=[
            jax.ShapeDtypeStruct((t, d), F32),
            jax.ShapeDtypeStruct((TOP_K, t), F32),
            jax.ShapeDtypeStruct((TOP_K, t), jnp.int32),
            jax.ShapeDtypeStruct((n_experts, 1), jnp.int32),
            jax.ShapeDtypeStruct((n_experts, tab_w), jnp.int32),
            jax.ShapeDtypeStruct((xs_rows * n_sub, LANES), F32),
        ],
        scratch_shapes=[
            pltpu.VMEM((n_experts, 1), F32),
            pltpu.VMEM((n_experts, 1), F32),
            pltpu.VMEM((1, 1), F32),
            pltpu.VMEM((2, tm * n_sub, LANES), F32),
            pltpu.VMEM((TM_EXP * n_sub, LANES), F32),
            pltpu.VMEM((2, TOP_K, tm), jnp.int32),
            pltpu.SMEM((2, TOP_K, tm), jnp.int32),
            pltpu.VMEM((SUBLANES, LANES), jnp.int32),
            pltpu.SMEM((SUBLANES, LANES), jnp.int32),
            pltpu.VMEM(w_att.shape, BF16),
            pltpu.VMEM(w_pg.shape, BF16),
            pltpu.VMEM(w_pb.shape, BF16),
            pltpu.VMEM(w_out.shape, BF16),
            pltpu.VMEM((2, max(w_att.shape[0], w_out.shape[0]), WEIGHT_CHUNK), F32),
            pltpu.SemaphoreType.DMA((2,)),
            pltpu.SemaphoreType.DMA((2,)),
            pltpu.SemaphoreType.DMA((2,)),
            pltpu.SemaphoreType.DMA,
        ],
        compiler_params=_cparams(("arbitrary",)),
        name="post_mix",
    )(o, p, p, sga, sgp, x2, mod3, w_att, w_pg, pool_scale, w_pb, w_out,
      g_post_mix, g_pre_ffn, w_router_t, b_router_col)


def _experts_kernel(te_ref, xb_ref, yb_ref, nt_ref, nv_ref, xs_ref, wgu_ref, bgu_ref, wd_ref, bd_ref,
                    y_ref, wgu_bf, wd_bf, *, n_sub):
    j = pl.program_id(0)
    tm = TM_EXP
    half = tm // 2
    d_ff = wd_ref.shape[1]
    open_tile = j < nt_ref[0]

    @pl.when(open_tile)
    def _():
        changed = jnp.logical_or(j == 0, te_ref[j] != te_ref[jnp.maximum(j - 1, 0)])

        @pl.when(changed)
        def _():
            wgu_bf[...] = wgu_ref[0].astype(BF16)
            wd_bf[...] = wd_ref[0].astype(BF16)

    def mlp(rows):
        xs = jnp.concatenate(
            [xs_ref[pl.ds(s, rows, stride=n_sub), :] for s in range(n_sub)], axis=1).astype(BF16)
        gu = jnp.dot(xs, wgu_bf[...], preferred_element_type=F32) + bgu_ref[0]
        gate = jnp.minimum(gu[:, :d_ff], SWIGLU_LIMIT)
        up = jnp.clip(gu[:, d_ff:], -SWIGLU_LIMIT, SWIGLU_LIMIT)
        act = (up + 1.0) * (gate * jax.nn.sigmoid(SWIGLU_ALPHA * gate))
        y = jnp.dot(act.astype(BF16), wd_bf[...], preferred_element_type=F32) + bd_ref[0]
        for s in range(n_sub):
            y_ref[pl.ds(s, rows, stride=n_sub), :] = y[:, s * LANES:(s + 1) * LANES]

    few = nv_ref[j] <= half

    @pl.when(jnp.logical_and(open_tile, jnp.logical_not(few)))
    def _():
        mlp(tm)

    @pl.when(jnp.logical_and(open_tile, few))
    def _():
        mlp(half)
        y_ref[pl.ds(half * n_sub, half * n_sub), :] = jnp.zeros((half * n_sub, LANES), F32)

    @pl.when(j >= nt_ref[0])
    def _():
        y_ref[...] = jnp.zeros_like(y_ref)


def _experts(tile_expert, xs_block, y_block, rows_valid, n_tiles, xs_tiles,
             w_gate_up, b_gate_up, w_down, b_down):
    n_exp, d, d_gu = w_gate_up.shape
    d_ff = w_down.shape[1]
    n_sub = d // LANES
    tm = TM_EXP
    max_tiles = tile_expert.shape[0]
    by_expert = lambda j, te, xb, yb, nt, nv: (te[j], 0, 0)
    by_block = lambda j, te, xb, yb, nt, nv: (yb[j], 0)
    grid_spec = pltpu.PrefetchScalarGridSpec(
        num_scalar_prefetch=5,
        grid=(max_tiles,),
        in_specs=[
            pl.BlockSpec((tm * n_sub, LANES), lambda j, te, xb, yb, nt, nv: (xb[j], 0)),
            pl.BlockSpec((1, d, d_gu), by_expert),
            pl.BlockSpec((1, 1, d_gu), by_expert),
            pl.BlockSpec((1, d_ff, d), by_expert),
            pl.BlockSpec((1, 1, d), by_expert),
        ],
        out_specs=pl.BlockSpec((tm * n_sub, LANES), by_block),
        scratch_shapes=[
            pltpu.VMEM((d, d_gu), BF16),
            pltpu.VMEM((d_ff, d), BF16),
        ],
    )
    return pl.pallas_call(
        functools.partial(_experts_kernel, n_sub=n_sub),
        grid_spec=grid_spec,
        out_shape=jax.ShapeDtypeStruct((max_tiles * tm * n_sub, LANES), F32),
        compiler_params=_cparams(("arbitrary",)),
        name="experts",
    )(tile_expert, xs_block, y_block, n_tiles, rows_valid, xs_tiles,
      w_gate_up, b_gate_up.reshape(n_exp, 1, d_gu), w_down, b_down.reshape(n_exp, 1, d))


def _combine_kernel(pos_ref, y_hbm, prob_ref, x1_ref, mod_ref, g_ref, o_ref, ybuf, sem,
                    *, n_sub, n_tokens):
    i = pl.program_id(0)
    n_steps = pl.num_programs(0)
    tm = TM_COMB
    slot = i % 2

    def gather(step, s):
        for r in range(tm):
            for k in range(TOP_K):
                row = pos_ref[k * n_tokens + step * tm + r]
                pltpu.make_async_copy(
                    y_hbm.at[pl.ds(pl.multiple_of(row * n_sub, n_sub), n_sub), :],
                    ybuf.at[s, k, pl.ds(r * n_sub, n_sub), :], sem.at[s]).start()

    @pl.when(i == 0)
    def _():
        gather(0, 0)

    @pl.when(i + 1 < n_steps)
    def _():
        gather(i + 1, 1 - slot)

    for k in range(TOP_K):
        pltpu.make_async_copy(y_hbm.at[pl.ds(0, tm * n_sub), :], ybuf.at[slot, k], sem.at[slot]).wait()

    prob = prob_ref[...]
    f = None
    for k in range(TOP_K):
        yk = jnp.concatenate(
            [ybuf[slot, k, pl.ds(s, tm, stride=n_sub), :] for s in range(n_sub)], axis=1)
        term = yk * prob[:, k:k + 1]
        f = term if f is None else f + term
    gate_f = mod_ref[0, 5:6, :]
    o_ref[...] = x1_ref[...] + gate_f * (_rms(f) * g_ref[...])


def _combine(pos_flat, y_tiles, probs, x1, mod3, g_post_ffn, seq):
    t, d = x1.shape
    tm = TM_COMB
    n_sub = d // LANES
    steps_per_seq = seq // tm
    grid_spec = pltpu.PrefetchScalarGridSpec(
        num_scalar_prefetch=1,
        grid=(t // tm,),
        in_specs=[
            pl.BlockSpec(memory_space=pl.ANY),
            pl.BlockSpec((tm, TOP_K), lambda i, pos: (i, 0)),
            pl.BlockSpec((tm, d), lambda i, pos: (i, 0)),
            pl.BlockSpec((1, N_MOD, d), lambda i, pos: (i // steps_per_seq, 0, 0)),
            pl.BlockSpec((1, d), lambda i, pos: (0, 0)),
        ],
        out_specs=pl.BlockSpec((tm, d), lambda i, pos: (i, 0)),
        scratch_shapes=[
            pltpu.VMEM((2, TOP_K, tm * n_sub, LANES), F32),
            pltpu.SemaphoreType.DMA((2,)),
        ],
    )
    return pl.pallas_call(
        functools.partial(_combine_kernel, n_sub=n_sub, n_tokens=t),
        grid_spec=grid_spec,
        out_shape=jax.ShapeDtypeStruct((t, d), F32),
        compiler_params=_cparams(("arbitrary",)),
        name="combine",
    )(pos_flat, y_tiles, probs, x1, mod3, g_post_ffn)


def _rope_freq_row():
    half = ROPE_DIM // 2
    inv_freq = ROPE_THETA ** (-jnp.arange(0, ROPE_DIM, 2, dtype=F32) / ROPE_DIM)
    head = jnp.concatenate([inv_freq, inv_freq, jnp.zeros((DA_HEAD_DIM - 2 * half,), F32)])
    return jnp.tile(head, LANES // DA_HEAD_DIM).reshape(1, LANES)


def _tile_tables(counts, block_tab, n_tokens):
    tm = TM_EXP
    n_exp = counts.shape[0]
    max_tiles = (n_tokens * TOP_K) // tm + n_exp
    tiles = (counts + tm - 1) // tm
    tile_end = jnp.cumsum(tiles)
    n_tiles = tile_end[-1]
    j = jnp.arange(max_tiles, dtype=jnp.int32)
    jj = jnp.minimum(j, n_tiles - 1)
    past = (jj[:, None] >= tile_end[None, :]).astype(jnp.int32)
    tile_expert = jnp.sum(past, axis=1)
    nth = jj - jnp.sum(past * tiles[None, :], axis=1)
    open_block = block_tab[tile_expert, nth]
    y_block = jnp.where(j < n_tiles, open_block, j)
    is_expert = (tile_expert[:, None] == jnp.arange(n_exp, dtype=jnp.int32)[None, :]).astype(jnp.int32)
    rows_valid = jnp.where(j < n_tiles,
                           jnp.minimum(jnp.sum(is_expert * counts[None, :], axis=1) - nth * tm, tm), 0)
    as_i32 = lambda a: a.astype(jnp.int32)
    return (as_i32(tile_expert), as_i32(open_block), as_i32(y_block), as_i32(rows_valid),
            as_i32(n_tiles.reshape(1)))


def kernel(x, c, positions, w_mod, b_mod, g_pre_mix, w_in, lambda_q1, lambda_k1, lambda_q2, lambda_k2, g_sub, w_pool_group, pool_scale, w_att_branch, w_pool_branch, w_out, g_post_mix, g_pre_ffn, w_router, b_router, w_gate_up, b_gate_up, w_down, b_down, g_post_ffn):
    bsz, seq, d = x.shape
    assert w_mod.shape[0] == 1, "single layer"
    assert seq % TQ == 0 and seq % TM_PROJ == 0 and seq % TM_MIX == 0 and seq % TM_COMB == 0
    t = bsz * seq
    assert t % TM_EXP == 0
    x2 = x.reshape(t, d)
    qk_w = 2 * DA_HEADS * DA_HEAD_DIM
    v_w = DA_HEADS * DA_V_DIM
    pool_w = len(POOL_WINDOWS) * POOL_GROUP_DIM
    n_experts = w_router.shape[2]

    mod3 = _mod(c, w_mod[0], b_mod[0]).reshape(bsz, N_MOD, d)

    bounds = [0, qk_w, 2 * qk_w, 2 * qk_w + v_w, 2 * qk_w + v_w + pool_w,
              2 * qk_w + v_w + pool_w + d, 2 * qk_w + v_w + pool_w + 2 * d]
    q, k, vt, p, sga, sgp = _in_proj(x2, mod3, g_pre_mix, positions.reshape(t, 1), _rope_freq_row(),
                                     w_in[0], bounds, seq)

    row64 = lambda a: a.reshape(1, DA_HEAD_DIM)
    o = _attention(q, k, vt, row64(lambda_q1[0]), row64(lambda_k1[0]), row64(lambda_q2[0]),
                   row64(lambda_k2[0]), g_sub.reshape(DA_V_DIM, 1), bsz, seq)

    x1, probs_t, pos_t, counts, block_tab, xs_tiles = _post_mix(
        o, p, sga, sgp, x2, mod3, w_att_branch[0], w_pool_group[0].reshape(pool_w, POOL_GROUP_DIM),
        pool_scale, w_pool_branch[0], w_out[0],
        g_post_mix, g_pre_ffn, w_router[0].T, b_router.reshape(n_experts, 1), seq)

    tile_expert, xs_block, y_block, rows_valid, n_tiles = _tile_tables(counts[:, 0], block_tab, t)
    y_tiles = _experts(tile_expert, xs_block, y_block, rows_valid, n_tiles, xs_tiles,
                       w_gate_up[0], b_gate_up[0], w_down[0], b_down[0])
    out = _combine(pos_t.reshape(-1), y_tiles, probs_t.T, x1, mod3, g_post_ffn, seq)
    return out.reshape(bsz, seq, d)
```

```python
import functools

import jax
import jax.numpy as jnp
from jax import lax
from jax.experimental import pallas as pl
from jax.experimental.pallas import tpu as pltpu

F32 = jnp.float32
BF16 = jnp.bfloat16

NORM_EPS = 1e-6
CHUNK = 64
DA_HEADS = 8
DA_HEAD_DIM = 64
DA_V_DIM = 2 * DA_HEAD_DIM
ROPE_THETA = 500000.0
ROPE_DIM = DA_HEAD_DIM // 4
POOL_WINDOWS = (2, 4, 8, 16)
POOL_GROUP_DIM = 128
TOP_K = 4
SWIGLU_LIMIT = 7.0
SWIGLU_ALPHA = 1.702
N_MOD = 6
LAMBDA_INIT = 0.8 - 0.6 * 1.0
Q_SCALE = (DA_HEAD_DIM ** -0.5) * 1.4426950408889634

LANES = 128
SUBLANES = 8
VMEM_LIMIT = 56 * 1024 * 1024

TM_PROJ = 512
TQ = 512
TK = 256
ATT_HEADS_PER_STEP = 4
ATT_SUM_ROWS = 16
TM_MIX = 512
TM_EXP = 512
TM_COMB = 512
WEIGHT_CHUNK = 512
POOL_HALO = 16
NEG_BIG = -1e30


def _cparams(sem):
    return pltpu.CompilerParams(dimension_semantics=sem, vmem_limit_bytes=VMEM_LIMIT)


def _split(a):
    hi = a.astype(BF16)
    return hi, (a - hi.astype(F32)).astype(BF16)


def _split_dot(a, b, dims=(((1,), (0,)), ((), ()))):
    a_hi, a_lo = _split(a)
    b_hi, b_lo = _split(b)
    dot = lambda x, y: lax.dot_general(x, y, dims, preferred_element_type=F32)
    return dot(a_hi, b_hi) + (dot(a_hi, b_lo) + dot(a_lo, b_hi))


def _rms(x):
    return x * lax.rsqrt(jnp.mean(x * x, axis=-1, keepdims=True) + NORM_EPS)


def _mod_kernel(c_ref, w_ref, b_ref, o_ref):
    c = c_ref[...]
    c_act = c * jax.nn.sigmoid(c)
    o_ref[...] = _split_dot(c_act, w_ref[...]) + b_ref[...]


def _mod(c, w_mod, b_mod):
    bsz, d = c.shape
    n = w_mod.shape[1]
    tn = 1024
    return pl.pallas_call(
        _mod_kernel,
        grid=(n // tn,),
        in_specs=[
            pl.BlockSpec((bsz, d), lambda j: (0, 0)),
            pl.BlockSpec((d, tn), lambda j: (0, j)),
            pl.BlockSpec((1, tn), lambda j: (0, j)),
        ],
        out_specs=pl.BlockSpec((bsz, tn), lambda j: (0, j)),
        out_shape=jax.ShapeDtypeStruct((bsz, n), F32),
        compiler_params=_cparams(("parallel",)),
        name="mod",
    )(c, w_mod, b_mod.reshape(1, n))


def _rope(t, cos_t, sin_a, sin_b):
    n = t.shape[1]
    up = pltpu.roll(t, n - ROPE_DIM // 2, axis=1)
    dn = pltpu.roll(t, ROPE_DIM // 2, axis=1)
    reps = n // LANES
    tile = lambda a: jnp.concatenate([a] * reps, axis=1)
    return t * tile(cos_t) + up * tile(sin_a) + dn * tile(sin_b)


def _stream_windows(src_hbm, windows, stage, sem, sink):
    def copy(n):
        r0, nr, c0, nc = windows[n]
        return pltpu.make_async_copy(src_hbm.at[pl.ds(r0, nr), pl.ds(c0, nc)],
                                     stage.at[n % 2, pl.ds(0, nr), pl.ds(0, nc)], sem.at[n % 2])
    copy(0).start()
    for n, (_, nr, _, nc) in enumerate(windows):
        if n + 1 < len(windows):
            copy(n + 1).start()
        copy(n).wait()
        sink(n, stage[n % 2, 0:nr, 0:nc])


def _in_proj_kernel(x_ref, mod_ref, g_ref, pos_ref, freq_ref, w_hbm,
                    q_ref, k_ref, v_ref, p_ref, sga_ref, sgp_ref,
                    w_ref, wvt_ref, stage, sem, *, bounds):
    @pl.when(pl.program_id(0) == 0)
    def _():
        d_in, width = w_hbm.shape
        windows = [(0, d_in, c0, WEIGHT_CHUNK) for c0 in range(0, width, WEIGHT_CHUNK)]

        def sink(n, tile):
            c0 = windows[n][2]
            w_ref[:, c0:c0 + WEIGHT_CHUNK] = tile.astype(BF16)
            if bounds[2] <= c0 < bounds[3]:
                wvt_ref[c0 - bounds[2]:c0 - bounds[2] + WEIGHT_CHUNK, :] = tile.T.astype(BF16)
        _stream_windows(w_hbm, windows, stage, sem, sink)

    x = x_ref[...]
    shift = mod_ref[0, 0:1, :]
    scale = mod_ref[0, 1:2, :]
    u = (_rms(x) * g_ref[...]) * (1.0 + scale) + shift
    ub = u.astype(BF16)
    dot = functools.partial(jnp.dot, preferred_element_type=F32)
    part = lambda n: w_ref[:, bounds[n]:bounds[n + 1]]

    ang = pos_ref[...].astype(F32) * freq_ref[...]
    cos_t, sn = jnp.cos(ang), jnp.sin(ang)
    in_head = lax.broadcasted_iota(jnp.int32, ang.shape, 1) % DA_HEAD_DIM
    sin_a = jnp.where(in_head < ROPE_DIM // 2, -sn, 0.0)
    sin_b = jnp.where(in_head >= ROPE_DIM // 2, sn, 0.0)

    q = _rope(dot(ub, part(0)), cos_t, sin_a, sin_b)
    q_ref[...] = (q * Q_SCALE).astype(BF16)
    k = _rope(dot(ub, part(1)), cos_t, sin_a, sin_b)
    k_ref[...] = k.astype(BF16)
    vt = lax.dot_general(wvt_ref[...], ub, (((1,), (1,)), ((), ())), preferred_element_type=F32)
    for n in range(v_ref.shape[0]):
        v_ref[n] = vt[:, n * TK:(n + 1) * TK].astype(BF16)
    p_ref[...] = dot(ub, part(3))
    sga_ref[...] = jax.nn.sigmoid(dot(ub, part(4))).astype(BF16)
    sgp_ref[...] = jax.nn.sigmoid(dot(ub, part(5))).astype(BF16)


def _in_proj(x2, mod3, g_pre, pos_col, freq_row, w_in, bounds, seq):
    t, d = x2.shape
    width = w_in.shape[1]
    assert width % WEIGHT_CHUNK == 0 and all(b % WEIGHT_CHUNK == 0 for b in bounds[2:4])
    tm = TM_PROJ
    assert tm % TK == 0, "v is emitted as transposed (channels, TK) slabs"
    steps_per_seq = seq // tm
    widths = [b - a for a, b in zip(bounds[:-1], bounds[1:])]
    row = lambda w: pl.BlockSpec((tm, w), lambda i: (i, 0))
    full = lambda a: pl.BlockSpec(a.shape, lambda i: (0, 0))
    out_specs = [row(widths[0]), row(widths[1]),
                 pl.BlockSpec((tm // TK, widths[2], TK), lambda i: (i, 0, 0)),
                 row(widths[3]), row(widths[4]), row(widths[5])]
    out_shape = [jax.ShapeDtypeStruct((t, widths[0]), BF16),
                 jax.ShapeDtypeStruct((t, widths[1]), BF16),
                 jax.ShapeDtypeStruct((t // TK, widths[2], TK), BF16),
                 jax.ShapeDtypeStruct((t, widths[3]), F32),
                 jax.ShapeDtypeStruct((t, widths[4]), BF16),
                 jax.ShapeDtypeStruct((t, widths[5]), BF16)]
    return pl.pallas_call(
        functools.partial(_in_proj_kernel, bounds=tuple(bounds)),
        grid=(t // tm,),
        in_specs=[
            row(d),
            pl.BlockSpec((1, N_MOD, d), lambda i: (i // steps_per_seq, 0, 0)),
            full(g_pre), row(1), full(freq_row), pl.BlockSpec(memory_space=pl.ANY),
        ],
        out_specs=out_specs,
        out_shape=out_shape,
        scratch_shapes=[
            pltpu.VMEM((d, width), BF16),
            pltpu.VMEM((widths[2], d), BF16),
            pltpu.VMEM((2, d, WEIGHT_CHUNK), F32),
            pltpu.SemaphoreType.DMA((2,)),
        ],
        compiler_params=_cparams(("arbitrary",)),
        name="in_proj",
    )(x2, mod3, g_pre, pos_col, freq_row, w_in)


def _attn_kernel(lq1_ref, lk1_ref, lq2_ref, lk2_ref, g_ref, q_ref, k_ref, vt_ref, o_ref,
                 s_a, s_b, p_a, p_b, acc_buf):
    assert TQ == 2 * TK and TK % CHUNK == 0
    seq = q_ref.shape[0]
    lam = (jnp.exp(jnp.sum(lq1_ref[...] * lk1_ref[...], axis=-1, keepdims=True))
           - jnp.exp(jnp.sum(lq2_ref[...] * lk2_ref[...], axis=-1, keepdims=True))
           + LAMBDA_INIT)
    dot = functools.partial(jnp.dot, preferred_element_type=F32)
    heads = range(ATT_HEADS_PER_STEP)
    lanes = lambda h: slice(h * LANES, (h + 1) * LANES)

    def q_tile(qi, c):
        q_rows = pl.ds(pl.multiple_of(qi * TQ, TQ), TQ)
        qqt = []
        for h in heads:
            qt = q_ref[q_rows, lanes(h)].astype(F32).T
            row = lax.broadcasted_iota(jnp.int32, qt.shape, 0)
            zero = jnp.zeros_like(qt)
            qqt.append(jnp.concatenate([jnp.where(row < DA_HEAD_DIM, qt, zero),
                                        jnp.where(row >= DA_HEAD_DIM, qt, zero)],
                                       axis=1).astype(BF16))

        def scores(j, h):
            return dot(k_ref[pl.ds(pl.multiple_of(j * TK, TK), TK), lanes(h)], qqt[h])

        def softmax_step(s, m):
            m_new = jnp.maximum(m, jnp.max(s, axis=0, keepdims=True))
            return m_new, jnp.exp2(m - m_new), jnp.exp2(s - m_new).astype(BF16)

        ones_rows = jnp.ones((ATT_SUM_ROWS, TK), BF16)

        def pv(j, h, p):
            return dot(jnp.concatenate([vt_ref[j, lanes(h), :], ones_rows], axis=0), p)

        kk = lax.broadcasted_iota(jnp.int32, (TK, 2 * TQ), 0)
        qq = lax.broadcasted_iota(jnp.int32, (TK, 2 * TQ), 1)
        rel_chunk = jnp.where(qq >= TQ, qq - TQ, qq) // CHUNK - kk // CHUNK

        def masked(s, j):
            return jnp.where(rel_chunk >= j * (TK // CHUNK) - qi * (TQ // CHUNK), s, NEG_BIG)

        def step(j, carries, s_cur, p_cur, s_nxt, p_prev):
            pend = [pv(jnp.maximum(j - 1, 0), h, p_prev[h]) for h in heads]
            for h in heads:
                s_nxt[h] = scores(j + 1, h)
            out = []
            for h in heads:
                m, alpha = carries[h]
                acc_buf[h] = alpha * acc_buf[h] + pend[h]
                m, alpha, p = softmax_step(s_cur[h], m)
                p_cur[h] = p
                out.append((m, alpha))
            return tuple(out)

        def pair(i, carries):
            carries = step(2 * i, carries, s_a, p_a, s_b, p_b)
            return step(2 * i + 1, carries, s_b, p_b, s_a, p_a)

        for h in heads:
            s_a[h] = scores(0, h)
            p_b[h] = jnp.zeros((TK, 2 * TQ), BF16)
            acc_buf[h] = jnp.zeros(acc_buf.shape[1:], F32)
        init = tuple((jnp.full((1, 2 * TQ), NEG_BIG, F32), jnp.ones((1, 2 * TQ), F32))
                     for _ in heads)
        carries = lax.fori_loop(0, qi, pair, init)
        ja = 2 * qi
        late = lambda a: jnp.concatenate([a[:, TK:TQ], a[:, TQ + TK:]], axis=1)
        pend = [pv(jnp.maximum(ja - 1, 0), h, p_b[h]) for h in heads]
        for h in heads:
            k_last = k_ref[pl.ds(pl.multiple_of((ja + 1) * TK, TK), TK), lanes(h)]
            s_b[h, :, 0:TQ] = dot(k_last, late(qqt[h]))
        mid = []
        for h in heads:
            m, alpha = carries[h]
            acc_buf[h] = alpha * acc_buf[h] + pend[h]
            m, alpha, p = softmax_step(masked(s_a[h], ja), m)
            p_a[h] = p
            mid.append((m, alpha))
        pend = [pv(ja, h, p_a[h]) for h in heads]
        k_in = lax.broadcasted_iota(jnp.int32, (TK, TQ), 0)
        q_in = lax.broadcasted_iota(jnp.int32, (TK, TQ), 1) % TK
        diagonal = k_in // CHUNK <= q_in // CHUNK
        for h in heads:
            m, alpha = mid[h]
            acc = alpha * acc_buf[h] + pend[h]
            m_l, alpha_l, p_l = softmax_step(jnp.where(diagonal, s_b[h, :, 0:TQ], NEG_BIG), late(m))
            acc_l = alpha_l * late(acc) + pv(ja + 1, h, p_l)
            acc = jnp.concatenate([acc[:, :TK], acc_l[:, :TK], acc[:, TQ:TQ + TK], acc_l[:, TK:]], axis=1)
            o = acc[:DA_V_DIM] / acc[DA_V_DIM:DA_V_DIM + 1]
            a = o[:, :TQ] - lam * o[:, TQ:]
            y = a * lax.rsqrt(jnp.mean(a * a, axis=0, keepdims=True) + NORM_EPS)
            y = (y * g_ref[...]) * (1.0 - LAMBDA_INIT)
            o_ref[q_rows, lanes(h)] = y.T.astype(BF16)
        return c

    lax.fori_loop(0, seq // TQ, q_tile, 0)


def _attention(q, k, vt, lq1, lk1, lq2, lk2, g_col, bsz, seq):
    t = q.shape[0]
    nkv = seq // TK
    hw = ATT_HEADS_PER_STEP * LANES
    vec = lambda a: pl.BlockSpec(a.shape, lambda b, h: (0, 0))
    return pl.pallas_call(
        _attn_kernel,
        grid=(bsz, DA_HEADS // ATT_HEADS_PER_STEP),
        in_specs=[
            vec(lq1), vec(lk1), vec(lq2), vec(lk2), vec(g_col),
            pl.BlockSpec((seq, hw), lambda b, h: (b, h)),
            pl.BlockSpec((seq, hw), lambda b, h: (b, h)),
            pl.BlockSpec((nkv, hw, TK), lambda b, h: (b, h, 0)),
        ],
        out_specs=pl.BlockSpec((seq, hw), lambda b, h: (b, h)),
        out_shape=jax.ShapeDtypeStruct((t, DA_HEADS * DA_V_DIM), BF16),
        scratch_shapes=[
            pltpu.VMEM((ATT_HEADS_PER_STEP, TK, 2 * TQ), F32),
            pltpu.VMEM((ATT_HEADS_PER_STEP, TK, 2 * TQ), F32),
            pltpu.VMEM((ATT_HEADS_PER_STEP, TK, 2 * TQ), BF16),
            pltpu.VMEM((ATT_HEADS_PER_STEP, TK, 2 * TQ), BF16),
            pltpu.VMEM((ATT_HEADS_PER_STEP, DA_V_DIM + ATT_SUM_ROWS, 2 * TQ), F32),
        ],
        compiler_params=_cparams(("parallel", "parallel")),
        name="attention",
    )(lq1, lk1, lq2, lk2, g_col, q, k, vt)


def _post_mix_kernel(o_ref, p_ref, ph_ref, sga_ref, sgp_ref, x_ref, mod_ref,
                     watt_hbm, wpg_hbm, ps_ref, wpb_hbm, wout_hbm, gpm_ref, gpf_ref,
                     wrt_ref, br_ref,
                     x1_ref, prob_ref, pos_ref, cnt_ref, tab_ref, xs_hbm,
                     carry_ref, cur_ref, free_ref, u2t, zeros_v, pos_v, pos_s, meta_v, meta_s,
                     watt_ref, wpg_ref, wpb_ref, wout_ref, stage, sem_w,
                     sem_rows, sem_pos, sem_zero,
                     *, steps_per_seq, n_blocks):
    i = pl.program_id(0)
    n_steps = pl.num_programs(0)
    tm = x_ref.shape[0]
    n_experts = wrt_ref.shape[0]
    n_sub = x_ref.shape[1] // LANES
    slot = i % 2
    prev = 1 - slot
    dot = functools.partial(jnp.dot, preferred_element_type=F32)

    def pos_copy(s):
        return pltpu.make_async_copy(pos_v.at[s], pos_s.at[s], sem_pos.at[s])

    def row_copy(s, k, r):
        dst = pos_s[s, k, r]
        return pltpu.make_async_copy(
            u2t.at[s, pl.ds(r * n_sub, n_sub), :],
            xs_hbm.at[pl.ds(pl.multiple_of(dst * n_sub, n_sub), n_sub), :], sem_rows.at[s])

    def wait_rows(s):
        for _ in range(TOP_K):
            pltpu.make_async_copy(u2t.at[s], xs_hbm.at[pl.ds(0, tm * n_sub), :], sem_rows.at[s]).wait()

    @pl.when(i == 0)
    def _():
        carry_ref[...] = jnp.zeros_like(carry_ref)
        cur_ref[...] = jnp.zeros_like(cur_ref)
        free_ref[...] = jnp.zeros_like(free_ref)
        tab_ref[...] = jnp.zeros_like(tab_ref)
        zeros_v[...] = jnp.zeros_like(zeros_v)
        for src, dst in ((watt_hbm, watt_ref), (wpg_hbm, wpg_ref), (wpb_hbm, wpb_ref),
                         (wout_hbm, wout_ref)):
            rows, cols = src.shape
            step_c = min(cols, WEIGHT_CHUNK)
            windows = [(0, rows, c0, step_c) for c0 in range(0, cols, step_c)]

            def sink(n, tile, dst=dst, windows=windows):
                c0, nc = windows[n][2], windows[n][3]
                dst[:, c0:c0 + nc] = tile.astype(BF16)
            _stream_windows(src, windows, stage, sem_w, sink)
        u2t[1] = jnp.zeros(u2t.shape[1:], F32)
        spare = (n_blocks * TM_EXP
                 + lax.broadcasted_iota(jnp.int32, (TOP_K, tm), 0) * tm
                 + lax.broadcasted_iota(jnp.int32, (TOP_K, tm), 1))
        pos_v[1] = spare
        pos_copy(1).start()

    pos_copy(prev).wait()

    for r in range(tm):
        for k in range(TOP_K):
            row_copy(prev, k, r).start()

    y_att = dot(o_ref[...], watt_ref[...])

    first = (i % steps_per_seq) == 0
    halo = jnp.where(first, 0.0, ph_ref[...])
    ext = jnp.concatenate([halo, p_ref[...]], axis=0)
    t_in_seq = (i % steps_per_seq) * tm + lax.broadcasted_iota(jnp.int32, (tm, 1), 0)
    pooled = []
    for g, w in enumerate(POOL_WINDOWS):
        e = ext[:, g * POOL_GROUP_DIM:(g + 1) * POOL_GROUP_DIM]
        acc, span = e, 1
        while span < w:
            acc = acc[span:] + acc[:-span]
            span *= 2
        win = acc[POOL_HALO - (w - 1):]
        cnt = jnp.minimum(t_in_seq + 1, w).astype(F32)
        mixed = win / cnt - e[POOL_HALO:]
        pooled.append(dot(mixed.astype(BF16), wpg_ref[g * POOL_GROUP_DIM:(g + 1) * POOL_GROUP_DIM, :]))
    y_pool_in = jnp.concatenate(pooled, axis=1) * ps_ref[...]
    y_pool = dot(y_pool_in.astype(BF16), wpb_ref[...])

    merged = sga_ref[...].astype(F32) * y_att + sgp_ref[...].astype(F32) * y_pool
    mix_out = dot(merged.astype(BF16), wout_ref[...])
    gate_m = mod_ref[0, 2:3, :]
    shift_f = mod_ref[0, 3:4, :]
    scale_f = mod_ref[0, 4:5, :]
    x1 = x_ref[...] + gate_m * (_rms(mix_out) * gpm_ref[...])
    x1_ref[...] = x1
    u2 = (_rms(x1) * gpf_ref[...]) * (1.0 + scale_f) + shift_f

    logits = _split_dot(wrt_ref[...], u2, (((1,), (1,)), ((), ()))) + br_ref[...]
    erow = lax.broadcasted_iota(jnp.int32, logits.shape, 0)
    work = logits
    vals, idxs = [], []
    for _ in range(TOP_K):
        mx = jnp.max(work, axis=0, keepdims=True)
        ix = jnp.min(jnp.where(work == mx, erow, n_experts), axis=0, keepdims=True)
        vals.append(mx)
        idxs.append(ix)
        work = jnp.where(erow == ix, -jnp.inf, work)
    exps = [jnp.exp(vv - vals[0]) for vv in vals]
    denom = exps[0] + exps[1] + exps[2] + exps[3]
    prob_ref[...] = jnp.concatenate([e / denom for e in exps], axis=0)

    onehot = jnp.zeros(logits.shape, F32)
    for ix in idxs:
        onehot = onehot + (erow == ix).astype(F32)
    rr = lax.broadcasted_iota(jnp.int32, (tm, tm), 0)
    cc = lax.broadcasted_iota(jnp.int32, (tm, tm), 1)
    earlier = (rr < cc).astype(BF16)
    carry = carry_ref[...]
    before = dot(onehot.astype(BF16), earlier) + carry

    blk_rows = float(TM_EXP)
    total = carry + jnp.sum(onehot, axis=1, keepdims=True)
    blocks_old = jnp.floor((carry + (blk_rows - 1.0)) / blk_rows)
    opened = jnp.floor((total + (blk_rows - 1.0)) / blk_rows) - blocks_old
    e_r = lax.broadcasted_iota(jnp.int32, (n_experts, n_experts), 0)
    e_c = lax.broadcasted_iota(jnp.int32, (n_experts, n_experts), 1)
    lower = (e_c < e_r).astype(BF16)
    opened_before = dot(lower, jnp.broadcast_to(opened, (n_experts, LANES)).astype(BF16))[:, 0:1]
    new_blk = free_ref[...] + opened_before
    cur_blk = cur_ref[...]
    boundary = blocks_old * blk_rows
    blk_of = jnp.where(before < boundary, cur_blk, new_blk)
    pos_rows = []
    for ix in idxs:
        pick = erow == ix
        rank = jnp.sum(jnp.where(pick, before, 0.0), axis=0, keepdims=True)
        blk_id = jnp.sum(jnp.where(pick, blk_of, 0.0), axis=0, keepdims=True)
        within = rank - jnp.floor(rank / blk_rows) * blk_rows
        pos_rows.append((blk_id * blk_rows + within).astype(jnp.int32))
    pos = jnp.concatenate(pos_rows, axis=0)
    pos_ref[...] = pos
    tab_col = lax.broadcasted_iota(jnp.int32, tab_ref.shape, 1).astype(F32)
    tab_ref[...] = jnp.where(jnp.logical_and(tab_col == blocks_old, opened > 0.0),
                             new_blk.astype(jnp.int32), tab_ref[...])
    cur_ref[...] = jnp.where(opened > 0.0, new_blk, cur_blk)
    free_ref[...] = free_ref[...] + jnp.sum(opened, axis=0, keepdims=True)
    carry_ref[...] = total
    cnt_ref[...] = total.astype(jnp.int32)

    @pl.when(i > 0)
    def _():
        wait_rows(slot)

    for s in range(n_sub):
        u2t[slot, pl.ds(s, tm, stride=n_sub), :] = u2[:, s * LANES:(s + 1) * LANES]
    pos_v[slot] = pos
    pos_copy(slot).start()

    @pl.when(i == n_steps - 1)
    def _():
        pos_copy(slot).wait()

        def issue(r, c):
            for k in range(TOP_K):
                row_copy(slot, k, r).start()
            return c
        lax.fori_loop(0, tm, issue, 0)

        eye = e_r == e_c
        as_row = lambda col: jnp.sum(jnp.where(eye, col, 0.0), axis=0, keepdims=True).astype(jnp.int32)
        used = total - jnp.floor(total / blk_rows) * blk_rows
        meta = jnp.concatenate([as_row(used), as_row(cur_ref[...]),
                                jnp.broadcast_to(free_ref[...].astype(jnp.int32), (1, n_experts))], axis=0)
        meta_v[...] = jnp.zeros_like(meta_v)
        meta_v[0:3, 0:n_experts] = meta
        meta_copy = pltpu.make_async_copy(meta_v, meta_s, sem_zero)
        meta_copy.start()
        meta_copy.wait()

        sizes = [TM_EXP >> (b + 1) for b in range(TM_EXP.bit_length() - 1)]
        zero_run = lambda row0, n: pltpu.make_async_copy(
            zeros_v.at[pl.ds(0, n * n_sub), :],
            xs_hbm.at[pl.ds(pl.multiple_of(row0 * n_sub, n_sub), n * n_sub), :], sem_zero)
        issued = [jnp.int32(0) for _ in sizes]
        for e in range(n_experts):
            first = meta_s[0, e]
            row0 = meta_s[1, e] * TM_EXP + first
            pad = jnp.where(first > 0, TM_EXP - first, 0)
            for b, n in enumerate(sizes):
                take = (pad & n) != 0

                @pl.when(take)
                def _(row0=row0, n=n):
                    zero_run(row0, n).start()
                issued[b] = issued[b] + take.astype(jnp.int32)
                row0 = row0 + jnp.where(take, n, 0)
        n_open = meta_s[2, 0]

        def whole(b, c):
            zero_run(b * TM_EXP, TM_EXP).start()
            return c
        lax.fori_loop(n_open, n_blocks, whole, 0)

        for b, n in enumerate(sizes):
            lax.fori_loop(0, issued[b], lambda _, c, n=n: (zero_run(0, n).wait(), c)[1], 0)
        lax.fori_loop(n_open, n_blocks, lambda _, c: (zero_run(0, TM_EXP).wait(), c)[1], 0)
        wait_rows(prev)
        wait_rows(slot)


def _post_mix(o, p, sga, sgp, x2, mod3, w_att, w_pg, pool_scale, w_pb, w_out,
              g_post_mix, g_pre_ffn, w_router_t, b_router_col, seq):
    t, d = x2.shape
    tm = TM_MIX
    steps_per_seq = seq // tm
    n_experts = w_router_t.shape[0]
    pw = p.shape[1]
    n_sub = d // LANES
    halo_blocks = tm // POOL_HALO
    row = lambda w: pl.BlockSpec((tm, w), lambda i: (i, 0))
    col = lambda h: pl.BlockSpec((h, tm), lambda i: (0, i))
    full2 = lambda a: pl.BlockSpec(a.shape, lambda i: (0, 0))
    hbm = pl.BlockSpec(memory_space=pl.ANY)
    assert tm <= TM_EXP and (tm * TOP_K) % TM_EXP == 0 and t % TM_EXP == 0
    n_blocks = (t * TOP_K) // TM_EXP + n_experts
    tab_w = -(-(t // TM_EXP) // LANES) * LANES
    kern = functools.partial(_post_mix_kernel, steps_per_seq=steps_per_seq, n_blocks=n_blocks)
    xs_rows = n_blocks * TM_EXP + TOP_K * tm
    return pl.pallas_call(
        kern,
        grid=(t // tm,),
        in_specs=[
            row(d), row(pw),
            pl.BlockSpec((POOL_HALO, pw), lambda i: (jnp.maximum(i * halo_blocks - 1, 0), 0)),
            row(d), row(d), row(d),
            pl.BlockSpec((1, N_MOD, d), lambda i: (i // steps_per_seq, 0, 0)),
            hbm, hbm, full2(pool_scale), hbm, hbm, full2(g_post_mix), full2(g_pre_ffn),
            full2(w_router_t), full2(b_router_col),
        ],
        out_specs=[
            row(d), col(TOP_K), col(TOP_K),
            pl.BlockSpec((n_experts, 1), lambda i: (0, 0)),
            pl.BlockSpec((n_experts, tab_w), lambda i: (0, 0)),
            pl.BlockSpec(memory_space=pl.ANY),
        ],
        out_shape=[
            jax.ShapeDtypeStruct((t, d), F32),
            jax.ShapeDtypeStruct((TOP_K, t), F32),
            jax.ShapeDtypeStruct((TOP_K, t), jnp.int32),
            jax.ShapeDtypeStruct((n_experts, 1), jnp.int32),
            jax.ShapeDtypeStruct((n_experts, tab_w), jnp.int32),
            jax.ShapeDtypeStruct((xs_rows * n_sub, LANES), F32),
        ],
        scratch_shapes=[
            pltpu.VMEM((n_experts, 1), F32),
            pltpu.VMEM((n_experts, 1), F32),
            pltpu.VMEM((1, 1), F32),
            pltpu.VMEM((2, tm * n_sub, LANES), F32),
            pltpu.VMEM((TM_EXP * n_sub, LANES), F32),
            pltpu.VMEM((2, TOP_K, tm), jnp.int32),
            pltpu.SMEM((2, TOP_K, tm), jnp.int32),
            pltpu.VMEM((SUBLANES, LANES), jnp.int32),
            pltpu.SMEM((SUBLANES, LANES), jnp.int32),
            pltpu.VMEM(w_att.shape, BF16),
            pltpu.VMEM(w_pg.shape, BF16),
            pltpu.VMEM(w_pb.shape, BF16),
            pltpu.VMEM(w_out.shape, BF16),
            pltpu.VMEM((2, max(w_att.shape[0], w_out.shape[0]), WEIGHT_CHUNK), F32),
            pltpu.SemaphoreType.DMA((2,)),
            pltpu.SemaphoreType.DMA((2,)),
            pltpu.SemaphoreType.DMA((2,)),
            pltpu.SemaphoreType.DMA,
        ],
        compiler_params=_cparams(("arbitrary",)),
        name="post_mix",
    )(o, p, p, sga, sgp, x2, mod3, w_att, w_pg, pool_scale, w_pb, w_out,
      g_post_mix, g_pre_ffn, w_router_t, b_router_col)


def _experts_kernel(te_ref, xb_ref, yb_ref, nt_ref, nv_ref, xs_ref, wgu_ref, bgu_ref, wd_ref, bd_ref,
                    y_ref, wgu_bf, wd_bf, *, n_sub):
    j = pl.program_id(0)
    tm = TM_EXP
    half = tm // 2
    d_ff = wd_ref.shape[1]
    open_tile = j < nt_ref[0]

    @pl.when(open_tile)
    def _():
        changed = jnp.logical_or(j == 0, te_ref[j] != te_ref[jnp.maximum(j - 1, 0)])

        @pl.when(changed)
        def _():
            wgu_bf[...] = wgu_ref[0].astype(BF16)
            wd_bf[...] = wd_ref[0].astype(BF16)

    def mlp(rows):
        xs = jnp.concatenate(
            [xs_ref[pl.ds(s, rows, stride=n_sub), :] for s in range(n_sub)], axis=1).astype(BF16)
        gu = jnp.dot(xs, wgu_bf[...], preferred_element_type=F32) + bgu_ref[0]
        gate = jnp.minimum(gu[:, :d_ff], SWIGLU_LIMIT)
        up = jnp.clip(gu[:, d_ff:], -SWIGLU_LIMIT, SWIGLU_LIMIT)
        act = (up + 1.0) * (gate * jax.nn.sigmoid(SWIGLU_ALPHA * gate))
        y = jnp.dot(act.astype(BF16), wd_bf[...], preferred_element_type=F32) + bd_ref[0]
        for s in range(n_sub):
            y_ref[pl.ds(s, rows, stride=n_sub), :] = y[:, s * LANES:(s + 1) * LANES]

    few = nv_ref[j] <= half

    @pl.when(jnp.logical_and(open_tile, jnp.logical_not(few)))
    def _():
        mlp(tm)

    @pl.when(jnp.logical_and(open_tile, few))
    def _():
        mlp(half)
        y_ref[pl.ds(half * n_sub, half * n_sub), :] = jnp.zeros((half * n_sub, LANES), F32)

    @pl.when(j >= nt_ref[0])
    def _():
        y_ref[...] = jnp.zeros_like(y_ref)


def _experts(tile_expert, xs_block, y_block, rows_valid, n_tiles, xs_tiles,
             w_gate_up, b_gate_up, w_down, b_down):
    n_exp, d, d_gu = w_gate_up.shape
    d_ff = w_down.shape[1]
    n_sub = d // LANES
    tm = TM_EXP
    max_tiles = tile_expert.shape[0]
    by_expert = lambda j, te, xb, yb, nt, nv: (te[j], 0, 0)
    by_block = lambda j, te, xb, yb, nt, nv: (yb[j], 0)
    grid_spec = pltpu.PrefetchScalarGridSpec(
        num_scalar_prefetch=5,
        grid=(max_tiles,),
        in_specs=[
            pl.BlockSpec((tm * n_sub, LANES), lambda j, te, xb, yb, nt, nv: (xb[j], 0)),
            pl.BlockSpec((1, d, d_gu), by_expert),
            pl.BlockSpec((1, 1, d_gu), by_expert),
            pl.BlockSpec((1, d_ff, d), by_expert),
            pl.BlockSpec((1, 1, d), by_expert),
        ],
        out_specs=pl.BlockSpec((tm * n_sub, LANES), by_block),
        scratch_shapes=[
            pltpu.VMEM((d, d_gu), BF16),
            pltpu.VMEM((d_ff, d), BF16),
        ],
    )
    return pl.pallas_call(
        functools.partial(_experts_kernel, n_sub=n_sub),
        grid_spec=grid_spec,
        out_shape=jax.ShapeDtypeStruct((max_tiles * tm * n_sub, LANES), F32),
        compiler_params=_cparams(("arbitrary",)),
        name="experts",
    )(tile_expert, xs_block, y_block, n_tiles, rows_valid, xs_tiles,
      w_gate_up, b_gate_up.reshape(n_exp, 1, d_gu), w_down, b_down.reshape(n_exp, 1, d))


def _combine_kernel(pos_ref, y_hbm, prob_ref, x1_ref, mod_ref, g_ref, o_ref, ybuf, sem,
                    *, n_sub, n_tokens):
    i = pl.program_id(0)
    n_steps = pl.num_programs(0)
    tm = TM_COMB
    slot = i % 2

    def gather(step, s):
        for r in range(tm):
            for k in range(TOP_K):
                row = pos_ref[k * n_tokens + step * tm + r]
                pltpu.make_async_copy(
                    y_hbm.at[pl.ds(pl.multiple_of(row * n_sub, n_sub), n_sub), :],
                    ybuf.at[s, k, pl.ds(r * n_sub, n_sub), :], sem.at[s]).start()

    @pl.when(i == 0)
    def _():
        gather(0, 0)

    @pl.when(i + 1 < n_steps)
    def _():
        gather(i + 1, 1 - slot)

    for k in range(TOP_K):
        pltpu.make_async_copy(y_hbm.at[pl.ds(0, tm * n_sub), :], ybuf.at[slot, k], sem.at[slot]).wait()

    prob = prob_ref[...]
    f = None
    for k in range(TOP_K):
        yk = jnp.concatenate(
            [ybuf[slot, k, pl.ds(s, tm, stride=n_sub), :] for s in range(n_sub)], axis=1)
        term = yk * prob[:, k:k + 1]
        f = term if f is None else f + term
    gate_f = mod_ref[0, 5:6, :]
    o_ref[...] = x1_ref[...] + gate_f * (_rms(f) * g_ref[...])


def _combine(pos_flat, y_tiles, probs, x1, mod3, g_post_ffn, seq):
    t, d = x1.shape
    tm = TM_COMB
    n_sub = d // LANES
    steps_per_seq = seq // tm
    grid_spec = pltpu.PrefetchScalarGridSpec(
        num_scalar_prefetch=1,
        grid=(t // tm,),
        in_specs=[
            pl.BlockSpec(memory_space=pl.ANY),
            pl.BlockSpec((tm, TOP_K), lambda i, pos: (i, 0)),
            pl.BlockSpec((tm, d), lambda i, pos: (i, 0)),
            pl.BlockSpec((1, N_MOD, d), lambda i, pos: (i // steps_per_seq, 0, 0)),
            pl.BlockSpec((1, d), lambda i, pos: (0, 0)),
        ],
        out_specs=pl.BlockSpec((tm, d), lambda i, pos: (i, 0)),
        scratch_shapes=[
            pltpu.VMEM((2, TOP_K, tm * n_sub, LANES), F32),
            pltpu.SemaphoreType.DMA((2,)),
        ],
    )
    return pl.pallas_call(
        functools.partial(_combine_kernel, n_sub=n_sub, n_tokens=t),
        grid_spec=grid_spec,
        out_shape=jax.ShapeDtypeStruct((t, d), F32),
        compiler_params=_cparams(("arbitrary",)),
        name="combine",
    )(pos_flat, y_tiles, probs, x1, mod3, g_post_ffn)


def _rope_freq_row():
    half = ROPE_DIM // 2
    inv_freq = ROPE_THETA ** (-jnp.arange(0, ROPE_DIM, 2, dtype=F32) / ROPE_DIM)
    head = jnp.concatenate([inv_freq, inv_freq, jnp.zeros((DA_HEAD_DIM - 2 * half,), F32)])
    return jnp.tile(head, LANES // DA_HEAD_DIM).reshape(1, LANES)


def _tile_tables(counts, block_tab, n_tokens):
    tm = TM_EXP
    n_exp = counts.shape[0]
    max_tiles = (n_tokens * TOP_K) // tm + n_exp
    tiles = (counts + tm - 1) // tm
    tile_end = jnp.cumsum(tiles)
    n_tiles = tile_end[-1]
    j = jnp.arange(max_tiles, dtype=jnp.int32)
    jj = jnp.minimum(j, n_tiles - 1)
    past = (jj[:, None] >= tile_end[None, :]).astype(jnp.int32)
    tile_expert = jnp.sum(past, axis=1)
    nth = jj - jnp.sum(past * tiles[None, :], axis=1)
    open_block = block_tab[tile_expert, nth]
    y_block = jnp.where(j < n_tiles, open_block, j)
    is_expert = (tile_expert[:, None] == jnp.arange(n_exp, dtype=jnp.int32)[None, :]).astype(jnp.int32)
    rows_valid = jnp.where(j < n_tiles,
                           jnp.minimum(jnp.sum(is_expert * counts[None, :], axis=1) - nth * tm, tm), 0)
    as_i32 = lambda a: a.astype(jnp.int32)
    return (as_i32(tile_expert), as_i32(open_block), as_i32(y_block), as_i32(rows_valid),
            as_i32(n_tiles.reshape(1)))


def kernel(x, c, positions, w_mod, b_mod, g_pre_mix, w_in, lambda_q1, lambda_k1, lambda_q2, lambda_k2, g_sub, w_pool_group, pool_scale, w_att_branch, w_pool_branch, w_out, g_post_mix, g_pre_ffn, w_router, b_router, w_gate_up, b_gate_up, w_down, b_down, g_post_ffn):
    bsz, seq, d = x.shape
    assert w_mod.shape[0] == 1, "single layer"
    assert seq % TQ == 0 and seq % TM_PROJ == 0 and seq % TM_MIX == 0 and seq % TM_COMB == 0
    t = bsz * seq
    assert t % TM_EXP == 0
    x2 = x.reshape(t, d)
    qk_w = 2 * DA_HEADS * DA_HEAD_DIM
    v_w = DA_HEADS * DA_V_DIM
    pool_w = len(POOL_WINDOWS) * POOL_GROUP_DIM
    n_experts = w_router.shape[2]

    mod3 = _mod(c, w_mod[0], b_mod[0]).reshape(bsz, N_MOD, d)

    bounds = [0, qk_w, 2 * qk_w, 2 * qk_w + v_w, 2 * qk_w + v_w + pool_w,
              2 * qk_w + v_w + pool_w + d, 2 * qk_w + v_w + pool_w + 2 * d]
    q, k, vt, p, sga, sgp = _in_proj(x2, mod3, g_pre_mix, positions.reshape(t, 1), _rope_freq_row(),
                                     w_in[0], bounds, seq)

    row64 = lambda a: a.reshape(1, DA_HEAD_DIM)
    o = _attention(q, k, vt, row64(lambda_q1[0]), row64(lambda_k1[0]), row64(lambda_q2[0]),
                   row64(lambda_k2[0]), g_sub.reshape(DA_V_DIM, 1), bsz, seq)

    x1, probs_t, pos_t, counts, block_tab, xs_tiles = _post_mix(
        o, p, sga, sgp, x2, mod3, w_att_branch[0], w_pool_group[0].reshape(pool_w, POOL_GROUP_DIM),
        pool_scale, w_pool_branch[0], w_out[0],
        g_post_mix, g_pre_ffn, w_router[0].T, b_router.reshape(n_experts, 1), seq)

    tile_expert, xs_block, y_block, rows_valid, n_tiles = _tile_tables(counts[:, 0], block_tab, t)
    y_tiles = _experts(tile_expert, xs_block, y_block, rows_valid, n_tiles, xs_tiles,
                       w_gate_up[0], b_gate_up[0], w_down[0], b_down[0])
    out = _combine(pos_t.reshape(-1), y_tiles, probs_t.T, x1, mod3, g_post_ffn, seq)
    return out.reshape(bsz, seq, d)
```

```python
import functools

import jax
import jax.numpy as jnp
from jax import lax
from jax.experimental import pallas as pl
from jax.experimental.pallas import tpu as pltpu

F32 = jnp.float32
BF16 = jnp.bfloat16

NORM_EPS = 1e-6
CHUNK = 64
DA_HEADS = 8
DA_HEAD_DIM = 64
DA_V_DIM = 2 * DA_HEAD_DIM
ROPE_THETA = 500000.0
ROPE_DIM = DA_HEAD_DIM // 4
POOL_WINDOWS = (2, 4, 8, 16)
POOL_GROUP_DIM = 128
TOP_K = 4
SWIGLU_LIMIT = 7.0
SWIGLU_ALPHA = 1.702
N_MOD = 6
LAMBDA_INIT = 0.8 - 0.6 * 1.0
Q_SCALE = (DA_HEAD_DIM ** -0.5) * 1.4426950408889634

LANES = 128
SUBLANES = 8
VMEM_LIMIT = 56 * 1024 * 1024

TM_PROJ = 512
TQ = 512
TK = 256
ATT_HEADS_PER_STEP = 4
ATT_SUM_ROWS = 16
TM_MIX = 256
TM_EXP = 512
TM_COMB = 256
WEIGHT_CHUNK = 512
POOL_HALO = 16
NEG_BIG = -1e30


def _cparams(sem):
    return pltpu.CompilerParams(dimension_semantics=sem, vmem_limit_bytes=VMEM_LIMIT)


def _split(a):
    hi = a.astype(BF16)
    return hi, (a - hi.astype(F32)).astype(BF16)


def _split_dot(a, b, dims=(((1,), (0,)), ((), ()))):
    a_hi, a_lo = _split(a)
    b_hi, b_lo = _split(b)
    dot = lambda x, y: lax.dot_general(x, y, dims, preferred_element_type=F32)
    return dot(a_hi, b_hi) + (dot(a_hi, b_lo) + dot(a_lo, b_hi))


def _rms(x):
    return x * lax.rsqrt(jnp.mean(x * x, axis=-1, keepdims=True) + NORM_EPS)


def _mod_kernel(c_ref, w_ref, b_ref, o_ref):
    c = c_ref[...]
    c_act = c * jax.nn.sigmoid(c)
    o_ref[...] = _split_dot(c_act, w_ref[...]) + b_ref[...]


def _mod(c, w_mod, b_mod):
    bsz, d = c.shape
    n = w_mod.shape[1]
    tn = 1024
    return pl.pallas_call(
        _mod_kernel,
        grid=(n // tn,),
        in_specs=[
            pl.BlockSpec((bsz, d), lambda j: (0, 0)),
            pl.BlockSpec((d, tn), lambda j: (0, j)),
            pl.BlockSpec((1, tn), lambda j: (0, j)),
        ],
        out_specs=pl.BlockSpec((bsz, tn), lambda j: (0, j)),
        out_shape=jax.ShapeDtypeStruct((bsz, n), F32),
        compiler_params=_cparams(("parallel",)),
        name="mod",
    )(c, w_mod, b_mod.reshape(1, n))


def _rope(t, cos_t, sin_a, sin_b):
    n = t.shape[1]
    up = pltpu.roll(t, n - ROPE_DIM // 2, axis=1)
    dn = pltpu.roll(t, ROPE_DIM // 2, axis=1)
    reps = n // LANES
    tile = lambda a: jnp.concatenate([a] * reps, axis=1)
    return t * tile(cos_t) + up * tile(sin_a) + dn * tile(sin_b)


def _stream_windows(src_hbm, windows, stage, sem, sink):
    def copy(n):
        r0, nr, c0, nc = windows[n]
        return pltpu.make_async_copy(src_hbm.at[pl.ds(r0, nr), pl.ds(c0, nc)],
                                     stage.at[n % 2, pl.ds(0, nr), pl.ds(0, nc)], sem.at[n % 2])
    copy(0).start()
    for n, (_, nr, _, nc) in enumerate(windows):
        if n + 1 < len(windows):
            copy(n + 1).start()
        copy(n).wait()
        sink(n, stage[n % 2, 0:nr, 0:nc])


def _in_proj_kernel(x_ref, mod_ref, g_ref, pos_ref, freq_ref, w_hbm,
                    q_ref, k_ref, v_ref, p_ref, sga_ref, sgp_ref,
                    w_ref, wvt_ref, stage, sem, *, bounds):
    @pl.when(pl.program_id(0) == 0)
    def _():
        d_in, width = w_hbm.shape
        windows = [(0, d_in, c0, WEIGHT_CHUNK) for c0 in range(0, width, WEIGHT_CHUNK)]

        def sink(n, tile):
            c0 = windows[n][2]
            w_ref[:, c0:c0 + WEIGHT_CHUNK] = tile.astype(BF16)
            if bounds[2] <= c0 < bounds[3]:
                wvt_ref[c0 - bounds[2]:c0 - bounds[2] + WEIGHT_CHUNK, :] = tile.T.astype(BF16)
        _stream_windows(w_hbm, windows, stage, sem, sink)

    x = x_ref[...]
    shift = mod_ref[0, 0:1, :]
    scale = mod_ref[0, 1:2, :]
    u = (_rms(x) * g_ref[...]) * (1.0 + scale) + shift
    ub = u.astype(BF16)
    dot = functools.partial(jnp.dot, preferred_element_type=F32)
    part = lambda n: w_ref[:, bounds[n]:bounds[n + 1]]

    ang = pos_ref[...].astype(F32) * freq_ref[...]
    cos_t, sn = jnp.cos(ang), jnp.sin(ang)
    in_head = lax.broadcasted_iota(jnp.int32, ang.shape, 1) % DA_HEAD_DIM
    sin_a = jnp.where(in_head < ROPE_DIM // 2, -sn, 0.0)
    sin_b = jnp.where(in_head >= ROPE_DIM // 2, sn, 0.0)

    q = _rope(dot(ub, part(0)), cos_t, sin_a, sin_b)
    q_ref[...] = (q * Q_SCALE).astype(BF16)
    k = _rope(dot(ub, part(1)), cos_t, sin_a, sin_b)
    k_ref[...] = k.astype(BF16)
    vt = lax.dot_general(wvt_ref[...], ub, (((1,), (1,)), ((), ())), preferred_element_type=F32)
    for n in range(v_ref.shape[0]):
        v_ref[n] = vt[:, n * TK:(n + 1) * TK].astype(BF16)
    p_ref[...] = dot(ub, part(3))
    sga_ref[...] = jax.nn.sigmoid(dot(ub, part(4))).astype(BF16)
    sgp_ref[...] = jax.nn.sigmoid(dot(ub, part(5))).astype(BF16)


def _in_proj(x2, mod3, g_pre, pos_col, freq_row, w_in, bounds, seq):
    t, d = x2.shape
    width = w_in.shape[1]
    assert width % WEIGHT_CHUNK == 0 and all(b % WEIGHT_CHUNK == 0 for b in bounds[2:4])
    tm = TM_PROJ
    assert tm % TK == 0, "v is emitted as transposed (channels, TK) slabs"
    steps_per_seq = seq // tm
    widths = [b - a for a, b in zip(bounds[:-1], bounds[1:])]
    row = lambda w: pl.BlockSpec((tm, w), lambda i: (i, 0))
    full = lambda a: pl.BlockSpec(a.shape, lambda i: (0, 0))
    out_specs = [row(widths[0]), row(widths[1]),
                 pl.BlockSpec((tm // TK, widths[2], TK), lambda i: (i, 0, 0)),
                 row(widths[3]), row(widths[4]), row(widths[5])]
    out_shape = [jax.ShapeDtypeStruct((t, widths[0]), BF16),
                 jax.ShapeDtypeStruct((t, widths[1]), BF16),
                 jax.ShapeDtypeStruct((t // TK, widths[2], TK), BF16),
                 jax.ShapeDtypeStruct((t, widths[3]), F32),
                 jax.ShapeDtypeStruct((t, widths[4]), BF16),
                 jax.ShapeDtypeStruct((t, widths[5]), BF16)]
    return pl.pallas_call(
        functools.partial(_in_proj_kernel, bounds=tuple(bounds)),
        grid=(t // tm,),
        in_specs=[
            row(d),
            pl.BlockSpec((1, N_MOD, d), lambda i: (i // steps_per_seq, 0, 0)),
            full(g_pre), row(1), full(freq_row), pl.BlockSpec(memory_space=pl.ANY),
        ],
        out_specs=out_specs,
        out_shape=out_shape,
        scratch_shapes=[
            pltpu.VMEM((d, width), BF16),
            pltpu.VMEM((widths[2], d), BF16),
            pltpu.VMEM((2, d, WEIGHT_CHUNK), F32),
            pltpu.SemaphoreType.DMA((2,)),
        ],
        compiler_params=_cparams(("arbitrary",)),
        name="in_proj",
    )(x2, mod3, g_pre, pos_col, freq_row, w_in)


def _attn_kernel(lq1_ref, lk1_ref, lq2_ref, lk2_ref, g_ref, q_ref, k_ref, vt_ref, o_ref,
                 s_a, s_b, p_a, p_b, acc_buf):
    assert TQ == 2 * TK and TK % CHUNK == 0
    seq = q_ref.shape[0]
    lam = (jnp.exp(jnp.sum(lq1_ref[...] * lk1_ref[...], axis=-1, keepdims=True))
           - jnp.exp(jnp.sum(lq2_ref[...] * lk2_ref[...], axis=-1, keepdims=True))
           + LAMBDA_INIT)
    dot = functools.partial(jnp.dot, preferred_element_type=F32)
    heads = range(ATT_HEADS_PER_STEP)
    lanes = lambda h: slice(h * LANES, (h + 1) * LANES)

    def q_tile(qi, c):
        q_rows = pl.ds(pl.multiple_of(qi * TQ, TQ), TQ)
        qqt = []
        for h in heads:
            qt = q_ref[q_rows, lanes(h)].astype(F32).T
            row = lax.broadcasted_iota(jnp.int32, qt.shape, 0)
            zero = jnp.zeros_like(qt)
            qqt.append(jnp.concatenate([jnp.where(row < DA_HEAD_DIM, qt, zero),
                                        jnp.where(row >= DA_HEAD_DIM, qt, zero)],
                                       axis=1).astype(BF16))

        def scores(j, h):
            return dot(k_ref[pl.ds(pl.multiple_of(j * TK, TK), TK), lanes(h)], qqt[h])

        def softmax_step(s, m):
            m_new = jnp.maximum(m, jnp.max(s, axis=0, keepdims=True))
            return m_new, jnp.exp2(m - m_new), jnp.exp2(s - m_new).astype(BF16)

        ones_rows = jnp.ones((ATT_SUM_ROWS, TK), BF16)

        def pv(j, h, p):
            return dot(jnp.concatenate([vt_ref[j, lanes(h), :], ones_rows], axis=0), p)

        kk = lax.broadcasted_iota(jnp.int32, (TK, 2 * TQ), 0)
        qq = lax.broadcasted_iota(jnp.int32, (TK, 2 * TQ), 1)
        rel_chunk = jnp.where(qq >= TQ, qq - TQ, qq) // CHUNK - kk // CHUNK

        def masked(s, j):
            return jnp.where(rel_chunk >= j * (TK // CHUNK) - qi * (TQ // CHUNK), s, NEG_BIG)

        def step(j, carries, s_cur, p_cur, s_nxt, p_prev):
            pend = [pv(jnp.maximum(j - 1, 0), h, p_prev[h]) for h in heads]
            for h in heads:
                s_nxt[h] = scores(j + 1, h)
            out = []
            for h in heads:
                m, alpha = carries[h]
                acc_buf[h] = alpha * acc_buf[h] + pend[h]
                m, alpha, p = softmax_step(s_cur[h], m)
                p_cur[h] = p
                out.append((m, alpha))
            return tuple(out)

        def pair(i, carries):
            carries = step(2 * i, carries, s_a, p_a, s_b, p_b)
            return step(2 * i + 1, carries, s_b, p_b, s_a, p_a)

        for h in heads:
            s_a[h] = scores(0, h)
            p_b[h] = jnp.zeros((TK, 2 * TQ), BF16)
            acc_buf[h] = jnp.zeros(acc_buf.shape[1:], F32)
        init = tuple((jnp.full((1, 2 * TQ), NEG_BIG, F32), jnp.ones((1, 2 * TQ), F32))
                     for _ in heads)
        carries = lax.fori_loop(0, qi, pair, init)
        ja = 2 * qi
        late = lambda a: jnp.concatenate([a[:, TK:TQ], a[:, TQ + TK:]], axis=1)
        pend = [pv(jnp.maximum(ja - 1, 0), h, p_b[h]) for h in heads]
        for h in heads:
            k_last = k_ref[pl.ds(pl.multiple_of((ja + 1) * TK, TK), TK), lanes(h)]
            s_b[h, :, 0:TQ] = dot(k_last, late(qqt[h]))
        mid = []
        for h in heads:
            m, alpha = carries[h]
            acc_buf[h] = alpha * acc_buf[h] + pend[h]
            m, alpha, p = softmax_step(masked(s_a[h], ja), m)
            p_a[h] = p
            mid.append((m, alpha))
        pend = [pv(ja, h, p_a[h]) for h in heads]
        k_in = lax.broadcasted_iota(jnp.int32, (TK, TQ), 0)
        q_in = lax.broadcasted_iota(jnp.int32, (TK, TQ), 1) % TK
        diagonal = k_in // CHUNK <= q_in // CHUNK
        for h in heads:
            m, alpha = mid[h]
            acc = alpha * acc_buf[h] + pend[h]
            m_l, alpha_l, p_l = softmax_step(jnp.where(diagonal, s_b[h, :, 0:TQ], NEG_BIG), late(m))
            acc_l = alpha_l * late(acc) + pv(ja + 1, h, p_l)
            acc = jnp.concatenate([acc[:, :TK], acc_l[:, :TK], acc[:, TQ:TQ + TK], acc_l[:, TK:]], axis=1)
            o = acc[:DA_V_DIM] / acc[DA_V_DIM:DA_V_DIM + 1]
            a = o[:, :TQ] - lam * o[:, TQ:]
            y = a * lax.rsqrt(jnp.mean(a * a, axis=0, keepdims=True) + NORM_EPS)
            y = (y * g_ref[...]) * (1.0 - LAMBDA_INIT)
            o_ref[q_rows, lanes(h)] = y.T.astype(BF16)
        return c

    lax.fori_loop(0, seq // TQ, q_tile, 0)


def _attention(q, k, vt, lq1, lk1, lq2, lk2, g_col, bsz, seq):
    t = q.shape[0]
    nkv = seq // TK
    hw = ATT_HEADS_PER_STEP * LANES
    vec = lambda a: pl.BlockSpec(a.shape, lambda b, h: (0, 0))
    return pl.pallas_call(
        _attn_kernel,
        grid=(bsz, DA_HEADS // ATT_HEADS_PER_STEP),
        in_specs=[
            vec(lq1), vec(lk1), vec(lq2), vec(lk2), vec(g_col),
            pl.BlockSpec((seq, hw), lambda b, h: (b, h)),
            pl.BlockSpec((seq, hw), lambda b, h: (b, h)),
            pl.BlockSpec((nkv, hw, TK), lambda b, h: (b, h, 0)),
        ],
        out_specs=pl.BlockSpec((seq, hw), lambda b, h: (b, h)),
        out_shape=jax.ShapeDtypeStruct((t, DA_HEADS * DA_V_DIM), BF16),
        scratch_shapes=[
            pltpu.VMEM((ATT_HEADS_PER_STEP, TK, 2 * TQ), F32),
            pltpu.VMEM((ATT_HEADS_PER_STEP, TK, 2 * TQ), F32),
            pltpu.VMEM((ATT_HEADS_PER_STEP, TK, 2 * TQ), BF16),
            pltpu.VMEM((ATT_HEADS_PER_STEP, TK, 2 * TQ), BF16),
            pltpu.VMEM((ATT_HEADS_PER_STEP, DA_V_DIM + ATT_SUM_ROWS, 2 * TQ), F32),
        ],
        compiler_params=_cparams(("parallel", "parallel")),
        name="attention",
    )(lq1, lk1, lq2, lk2, g_col, q, k, vt)


def _post_mix_kernel(o_ref, p_ref, ph_ref, sga_ref, sgp_ref, x_ref, mod_ref,
                     watt_hbm, wpg_hbm, ps_ref, wpb_hbm, wout_hbm, gpm_ref, gpf_ref,
                     wrt_ref, br_ref,
                     x1_ref, prob_ref, pos_ref, cnt_ref, tab_ref, xs_hbm,
                     carry_ref, cur_ref, free_ref, u2t, zeros_v, pos_v, pos_s, meta_v, meta_s,
                     watt_ref, wpg_ref, wpb_ref, wout_ref, stage, sem_w,
                     sem_rows, sem_pos, sem_zero,
                     *, steps_per_seq, n_blocks):
    i = pl.program_id(0)
    n_steps = pl.num_programs(0)
    tm = x_ref.shape[0]
    n_experts = wrt_ref.shape[0]
    n_sub = x_ref.shape[1] // LANES
    slot = i % 2
    prev = 1 - slot
    dot = functools.partial(jnp.dot, preferred_element_type=F32)

    def pos_copy(s):
        return pltpu.make_async_copy(pos_v.at[s], pos_s.at[s], sem_pos.at[s])

    def row_copy(s, k, r):
        dst = pos_s[s, k, r]
        return pltpu.make_async_copy(
            u2t.at[s, pl.ds(r * n_sub, n_sub), :],
            xs_hbm.at[pl.ds(pl.multiple_of(dst * n_sub, n_sub), n_sub), :], sem_rows.at[s])

    def wait_rows(s):
        for _ in range(TOP_K):
            pltpu.make_async_copy(u2t.at[s], xs_hbm.at[pl.ds(0, tm * n_sub), :], sem_rows.at[s]).wait()

    @pl.when(i == 0)
    def _():
        carry_ref[...] = jnp.zeros_like(carry_ref)
        cur_ref[...] = jnp.zeros_like(cur_ref)
        free_ref[...] = jnp.zeros_like(free_ref)
        tab_ref[...] = jnp.zeros_like(tab_ref)
        zeros_v[...] = jnp.zeros_like(zeros_v)
        for src, dst in ((watt_hbm, watt_ref), (wpg_hbm, wpg_ref), (wpb_hbm, wpb_ref),
                         (wout_hbm, wout_ref)):
            rows, cols = src.shape
            step_c = min(cols, WEIGHT_CHUNK)
            windows = [(0, rows, c0, step_c) for c0 in range(0, cols, step_c)]

            def sink(n, tile, dst=dst, windows=windows):
                c0, nc = windows[n][2], windows[n][3]
                dst[:, c0:c0 + nc] = tile.astype(BF16)
            _stream_windows(src, windows, stage, sem_w, sink)
        u2t[1] = jnp.zeros(u2t.shape[1:], F32)
        spare = (n_blocks * TM_EXP
                 + lax.broadcasted_iota(jnp.int32, (TOP_K, tm), 0) * tm
                 + lax.broadcasted_iota(jnp.int32, (TOP_K, tm), 1))
        pos_v[1] = spare
        pos_copy(1).start()

    pos_copy(prev).wait()

    for r in range(tm):
        for k in range(TOP_K):
            row_copy(prev, k, r).start()

    y_att = dot(o_ref[...], watt_ref[...])

    first = (i % steps_per_seq) == 0
    halo = jnp.where(first, 0.0, ph_ref[...])
    ext = jnp.concatenate([halo, p_ref[...]], axis=0)
    t_in_seq = (i % steps_per_seq) * tm + lax.broadcasted_iota(jnp.int32, (tm, 1), 0)
    pooled = []
    for g, w in enumerate(POOL_WINDOWS):
        e = ext[:, g * POOL_GROUP_DIM:(g + 1) * POOL_GROUP_DIM]
        acc, span = e, 1
        while span < w:
            acc = acc[span:] + acc[:-span]
            span *= 2
        win = acc[POOL_HALO - (w - 1):]
        cnt = jnp.minimum(t_in_seq + 1, w).astype(F32)
        mixed = win / cnt - e[POOL_HALO:]
        pooled.append(dot(mixed.astype(BF16), wpg_ref[g * POOL_GROUP_DIM:(g + 1) * POOL_GROUP_DIM, :]))
    y_pool_in = jnp.concatenate(pooled, axis=1) * ps_ref[...]
    y_pool = dot(y_pool_in.astype(BF16), wpb_ref[...])

    merged = sga_ref[...].astype(F32) * y_att + sgp_ref[...].astype(F32) * y_pool
    mix_out = dot(merged.astype(BF16), wout_ref[...])
    gate_m = mod_ref[0, 2:3, :]
    shift_f = mod_ref[0, 3:4, :]
    scale_f = mod_ref[0, 4:5, :]
    x1 = x_ref[...] + gate_m * (_rms(mix_out) * gpm_ref[...])
    x1_ref[...] = x1
    u2 = (_rms(x1) * gpf_ref[...]) * (1.0 + scale_f) + shift_f

    logits = _split_dot(wrt_ref[...], u2, (((1,), (1,)), ((), ()))) + br_ref[...]
    erow = lax.broadcasted_iota(jnp.int32, logits.shape, 0)
    work = logits
    vals, idxs = [], []
    for _ in range(TOP_K):
        mx = jnp.max(work, axis=0, keepdims=True)
        ix = jnp.min(jnp.where(work == mx, erow, n_experts), axis=0, keepdims=True)
        vals.append(mx)
        idxs.append(ix)
        work = jnp.where(erow == ix, -jnp.inf, work)
    exps = [jnp.exp(vv - vals[0]) for vv in vals]
    denom = exps[0] + exps[1] + exps[2] + exps[3]
    prob_ref[...] = jnp.concatenate([e / denom for e in exps], axis=0)

    onehot = jnp.zeros(logits.shape, F32)
    for ix in idxs:
        onehot = onehot + (erow == ix).astype(F32)
    rr = lax.broadcasted_iota(jnp.int32, (tm, tm), 0)
    cc = lax.broadcasted_iota(jnp.int32, (tm, tm), 1)
    earlier = (rr < cc).astype(BF16)
    carry = carry_ref[...]
    before = dot(onehot.astype(BF16), earlier) + carry

    blk_rows = float(TM_EXP)
    total = carry + jnp.sum(onehot, axis=1, keepdims=True)
    blocks_old = jnp.floor((carry + (blk_rows - 1.0)) / blk_rows)
    opened = jnp.floor((total + (blk_rows - 1.0)) / blk_rows) - blocks_old
    e_r = lax.broadcasted_iota(jnp.int32, (n_experts, n_experts), 0)
    e_c = lax.broadcasted_iota(jnp.int32, (n_experts, n_experts), 1)
    lower = (e_c < e_r).astype(BF16)
    opened_before = dot(lower, jnp.broadcast_to(opened, (n_experts, LANES)).astype(BF16))[:, 0:1]
    new_blk = free_ref[...] + opened_before
    cur_blk = cur_ref[...]
    boundary = blocks_old * blk_rows
    blk_of = jnp.where(before < boundary, cur_blk, new_blk)
    pos_rows = []
    for ix in idxs:
        pick = erow == ix
        rank = jnp.sum(jnp.where(pick, before, 0.0), axis=0, keepdims=True)
        blk_id = jnp.sum(jnp.where(pick, blk_of, 0.0), axis=0, keepdims=True)
        within = rank - jnp.floor(rank / blk_rows) * blk_rows
        pos_rows.append((blk_id * blk_rows + within).astype(jnp.int32))
    pos = jnp.concatenate(pos_rows, axis=0)
    pos_ref[...] = pos
    tab_col = lax.broadcasted_iota(jnp.int32, tab_ref.shape, 1).astype(F32)
    tab_ref[...] = jnp.where(jnp.logical_and(tab_col == blocks_old, opened > 0.0),
                             new_blk.astype(jnp.int32), tab_ref[...])
    cur_ref[...] = jnp.where(opened > 0.0, new_blk, cur_blk)
    free_ref[...] = free_ref[...] + jnp.sum(opened, axis=0, keepdims=True)
    carry_ref[...] = total
    cnt_ref[...] = total.astype(jnp.int32)

    @pl.when(i > 0)
    def _():
        wait_rows(slot)

    for s in range(n_sub):
        u2t[slot, pl.ds(s, tm, stride=n_sub), :] = u2[:, s * LANES:(s + 1) * LANES]
    pos_v[slot] = pos
    pos_copy(slot).start()

    @pl.when(i == n_steps - 1)
    def _():
        pos_copy(slot).wait()

        def issue(r, c):
            for k in range(TOP_K):
                row_copy(slot, k, r).start()
            return c
        lax.fori_loop(0, tm, issue, 0)

        eye = e_r == e_c
        as_row = lambda col: jnp.sum(jnp.where(eye, col, 0.0), axis=0, keepdims=True).astype(jnp.int32)
        used = total - jnp.floor(total / blk_rows) * blk_rows
        meta = jnp.concatenate([as_row(used), as_row(cur_ref[...]),
                                jnp.broadcast_to(free_ref[...].astype(jnp.int32), (1, n_experts))], axis=0)
        meta_v[...] = jnp.zeros_like(meta_v)
        meta_v[0:3, 0:n_experts] = meta
        meta_copy = pltpu.make_async_copy(meta_v, meta_s, sem_zero)
        meta_copy.start()
        meta_copy.wait()

        sizes = [TM_EXP >> (b + 1) for b in range(TM_EXP.bit_length() - 1)]
        zero_run = lambda row0, n: pltpu.make_async_copy(
            zeros_v.at[pl.ds(0, n * n_sub), :],
            xs_hbm.at[pl.ds(pl.multiple_of(row0 * n_sub, n_sub), n * n_sub), :], sem_zero)
        issued = [jnp.int32(0) for _ in sizes]
        for e in range(n_experts):
            first = meta_s[0, e]
            row0 = meta_s[1, e] * TM_EXP + first
            pad = jnp.where(first > 0, TM_EXP - first, 0)
            for b, n in enumerate(sizes):
                take = (pad & n) != 0

                @pl.when(take)
                def _(row0=row0, n=n):
                    zero_run(row0, n).start()
                issued[b] = issued[b] + take.astype(jnp.int32)
                row0 = row0 + jnp.where(take, n, 0)
        n_open = meta_s[2, 0]

        def whole(b, c):
            zero_run(b * TM_EXP, TM_EXP).start()
            return c
        lax.fori_loop(n_open, n_blocks, whole, 0)

        for b, n in enumerate(sizes):
            lax.fori_loop(0, issued[b], lambda _, c, n=n: (zero_run(0, n).wait(), c)[1], 0)
        lax.fori_loop(n_open, n_blocks, lambda _, c: (zero_run(0, TM_EXP).wait(), c)[1], 0)
        wait_rows(prev)
        wait_rows(slot)


def _post_mix(o, p, sga, sgp, x2, mod3, w_att, w_pg, pool_scale, w_pb, w_out,
              g_post_mix, g_pre_ffn, w_router_t, b_router_col, seq):
    t, d = x2.shape
    tm = TM_MIX
    steps_per_seq = seq // tm
    n_experts = w_router_t.shape[0]
    pw = p.shape[1]
    n_sub = d // LANES
    halo_blocks = tm // POOL_HALO
    row = lambda w: pl.BlockSpec((tm, w), lambda i: (i, 0))
    col = lambda h: pl.BlockSpec((h, tm), lambda i: (0, i))
    full2 = lambda a: pl.BlockSpec(a.shape, lambda i: (0, 0))
    hbm = pl.BlockSpec(memory_space=pl.ANY)
    assert tm <= TM_EXP and (tm * TOP_K) % TM_EXP == 0 and t % TM_EXP == 0
    n_blocks = (t * TOP_K) // TM_EXP + n_experts
    tab_w = -(-(t // TM_EXP) // LANES) * LANES
    kern = functools.partial(_post_mix_kernel, steps_per_seq=steps_per_seq, n_blocks=n_blocks)
    xs_rows = n_blocks * TM_EXP + TOP_K * tm
    return pl.pallas_call(
        kern,
        grid=(t // tm,),
        in_specs=[
            row(d), row(pw),
            pl.BlockSpec((POOL_HALO, pw), lambda i: (jnp.maximum(i * halo_blocks - 1, 0), 0)),
            row(d), row(d), row(d),
            pl.BlockSpec((1, N_MOD, d), lambda i: (i // steps_per_seq, 0, 0)),
            hbm, hbm, full2(pool_scale), hbm, hbm, full2(g_post_mix), full2(g_pre_ffn),
            full2(w_router_t), full2(b_router_col),
        ],
        out_specs=[
            row(d), col(TOP_K), col(TOP_K),
            pl.BlockSpec((n_experts, 1), lambda i: (0, 0)),
            pl.BlockSpec((n_experts, tab_w), lambda i: (0, 0)),
            pl.BlockSpec(memory_space=pl.ANY),
        ],
        out_shape=[
            jax.ShapeDtypeStruct((t, d), F32),
            jax.ShapeDtypeStruct((TOP_K, t), F32),
            jax.ShapeDtypeStruct((TOP_K, t), jnp.int32),
            jax.ShapeDtypeStruct((n_experts, 1), jnp.int32),
            jax.ShapeDtypeStruct((n_experts, tab_w), jnp.int32),
            jax.ShapeDtypeStruct((xs_rows * n_sub, LANES), F32),
        ],
        scratch_shapes=[
            pltpu.VMEM((n_experts, 1), F32),
            pltpu.VMEM((n_experts, 1), F32),
            pltpu.VMEM((1, 1), F32),
            pltpu.VMEM((2, tm * n_sub, LANES), F32),
            pltpu.VMEM((TM_EXP * n_sub, LANES), F32),
            pltpu.VMEM((2, TOP_K, tm), jnp.int32),
            pltpu.SMEM((2, TOP_K, tm), jnp.int32),
            pltpu.VMEM((SUBLANES, LANES), jnp.int32),
            pltpu.SMEM((SUBLANES, LANES), jnp.int32),
            pltpu.VMEM(w_att.shape, BF16),
            pltpu.VMEM(w_pg.shape, BF16),
            pltpu.VMEM(w_pb.shape, BF16),
            pltpu.VMEM(w_out.shape, BF16),
            pltpu.VMEM((2, max(w_att.shape[0], w_out.shape[0]), WEIGHT_CHUNK), F32),
            pltpu.SemaphoreType.DMA((2,)),
            pltpu.SemaphoreType.DMA((2,)),
            pltpu.SemaphoreType.DMA((2,)),
            pltpu.SemaphoreType.DMA,
        ],
        compiler_params=_cparams(("arbitrary",)),
        name="post_mix",
    )(o, p, p, sga, sgp, x2, mod3, w_att, w_pg, pool_scale, w_pb, w_out,
      g_post_mix, g_pre_ffn, w_router_t, b_router_col)


def _experts_kernel(te_ref, xb_ref, yb_ref, nt_ref, nv_ref, nx_ref, xs_ref, wgu_hbm, bgu_ref, wd_hbm,
                    bd_ref, y_ref, wgu_bf, wd_bf, wgu_f32, wd_f32, sem, *, n_sub):
    j = pl.program_id(0)
    tm = TM_EXP
    half = tm // 2
    d_ff = wd_bf.shape[0]
    open_tile = j < nt_ref[0]

    def fetch(expert):
        return (pltpu.make_async_copy(wgu_hbm.at[expert], wgu_f32, sem.at[0]),
                pltpu.make_async_copy(wd_hbm.at[expert], wd_f32, sem.at[1]))

    @pl.when(open_tile)
    def _():
        changed = jnp.logical_or(j == 0, te_ref[j] != te_ref[jnp.maximum(j - 1, 0)])

        @pl.when(changed)
        def _():
            @pl.when(j == 0)
            def _():
                for c in fetch(te_ref[j]):
                    c.start()
            for c in fetch(te_ref[j]):
                c.wait()
            wgu_bf[...] = wgu_f32[...].astype(BF16)
            wd_bf[...] = wd_f32[...].astype(BF16)

            @pl.when(nx_ref[j] >= 0)
            def _():
                for c in fetch(nx_ref[j]):
                    c.start()

    def mlp(rows):
        xs = jnp.concatenate(
            [xs_ref[pl.ds(s, rows, stride=n_sub), :] for s in range(n_sub)], axis=1).astype(BF16)
        gu = jnp.dot(xs, wgu_bf[...], preferred_element_type=F32) + bgu_ref[0]
        gate = jnp.minimum(gu[:, :d_ff], SWIGLU_LIMIT)
        up = jnp.clip(gu[:, d_ff:], -SWIGLU_LIMIT, SWIGLU_LIMIT)
        act = (up + 1.0) * (gate * jax.nn.sigmoid(SWIGLU_ALPHA * gate))
        y = jnp.dot(act.astype(BF16), wd_bf[...], preferred_element_type=F32) + bd_ref[0]
        for s in range(n_sub):
            y_ref[pl.ds(s, rows, stride=n_sub), :] = y[:, s * LANES:(s + 1) * LANES]

    few = nv_ref[j] <= half

    @pl.when(jnp.logical_and(open_tile, jnp.logical_not(few)))
    def _():
        mlp(tm)

    @pl.when(jnp.logical_and(open_tile, few))
    def _():
        mlp(half)
        y_ref[pl.ds(half * n_sub, half * n_sub), :] = jnp.zeros((half * n_sub, LANES), F32)

    @pl.when(j >= nt_ref[0])
    def _():
        y_ref[...] = jnp.zeros_like(y_ref)


def _experts(tile_expert, xs_block, y_block, rows_valid, next_expert, n_tiles, xs_tiles,
             w_gate_up, b_gate_up, w_down, b_down):
    n_exp, d, d_gu = w_gate_up.shape
    d_ff = w_down.shape[1]
    n_sub = d // LANES
    tm = TM_EXP
    max_tiles = tile_expert.shape[0]
    by_expert = lambda j, te, xb, yb, nt, nv, nx: (te[j], 0, 0)
    by_block = lambda j, te, xb, yb, nt, nv, nx: (yb[j], 0)
    hbm = pl.BlockSpec(memory_space=pl.ANY)
    grid_spec = pltpu.PrefetchScalarGridSpec(
        num_scalar_prefetch=6,
        grid=(max_tiles,),
        in_specs=[
            pl.BlockSpec((tm * n_sub, LANES), lambda j, te, xb, yb, nt, nv, nx: (xb[j], 0)),
            hbm,
            pl.BlockSpec((1, 1, d_gu), by_expert),
            hbm,
            pl.BlockSpec((1, 1, d), by_expert),
        ],
        out_specs=pl.BlockSpec((tm * n_sub, LANES), by_block),
        scratch_shapes=[
            pltpu.VMEM((d, d_gu), BF16),
            pltpu.VMEM((d_ff, d), BF16),
            pltpu.VMEM((d, d_gu), F32),
            pltpu.VMEM((d_ff, d), F32),
            pltpu.SemaphoreType.DMA((2,)),
        ],
    )
    return pl.pallas_call(
        functools.partial(_experts_kernel, n_sub=n_sub),
        grid_spec=grid_spec,
        out_shape=jax.ShapeDtypeStruct((max_tiles * tm * n_sub, LANES), F32),
        compiler_params=_cparams(("arbitrary",)),
        name="experts",
    )(tile_expert, xs_block, y_block, n_tiles, rows_valid, next_expert, xs_tiles,
      w_gate_up, b_gate_up.reshape(n_exp, 1, d_gu), w_down, b_down.reshape(n_exp, 1, d))


def _combine_kernel(pos_ref, y_hbm, prob_ref, x1_ref, mod_ref, g_ref, o_ref, ybuf, sem,
                    *, n_sub, n_tokens):
    i = pl.program_id(0)
    n_steps = pl.num_programs(0)
    tm = TM_COMB
    slot = i % 2

    def gather(step, s):
        for r in range(tm):
            for k in range(TOP_K):
                row = pos_ref[k * n_tokens + step * tm + r]
                pltpu.make_async_copy(
                    y_hbm.at[pl.ds(pl.multiple_of(row * n_sub, n_sub), n_sub), :],
                    ybuf.at[s, k, pl.ds(r * n_sub, n_sub), :], sem.at[s]).start()

    @pl.when(i == 0)
    def _():
        gather(0, 0)

    @pl.when(i + 1 < n_steps)
    def _():
        gather(i + 1, 1 - slot)

    for k in range(TOP_K):
        pltpu.make_async_copy(y_hbm.at[pl.ds(0, tm * n_sub), :], ybuf.at[slot, k], sem.at[slot]).wait()

    prob = prob_ref[...]
    f = None
    for k in range(TOP_K):
        yk = jnp.concatenate(
            [ybuf[slot, k, pl.ds(s, tm, stride=n_sub), :] for s in range(n_sub)], axis=1)
        term = yk * prob[:, k:k + 1]
        f = term if f is None else f + term
    gate_f = mod_ref[0, 5:6, :]
    o_ref[...] = x1_ref[...] + gate_f * (_rms(f) * g_ref[...])


def _combine(pos_flat, y_tiles, probs, x1, mod3, g_post_ffn, seq):
    t, d = x1.shape
    tm = TM_COMB
    n_sub = d // LANES
    steps_per_seq = seq // tm
    grid_spec = pltpu.PrefetchScalarGridSpec(
        num_scalar_prefetch=1,
        grid=(t // tm,),
        in_specs=[
            pl.BlockSpec(memory_space=pl.ANY),
            pl.BlockSpec((tm, TOP_K), lambda i, pos: (i, 0)),
            pl.BlockSpec((tm, d), lambda i, pos: (i, 0)),
            pl.BlockSpec((1, N_MOD, d), lambda i, pos: (i // steps_per_seq, 0, 0)),
            pl.BlockSpec((1, d), lambda i, pos: (0, 0)),
        ],
        out_specs=pl.BlockSpec((tm, d), lambda i, pos: (i, 0)),
        scratch_shapes=[
            pltpu.VMEM((2, TOP_K, tm * n_sub, LANES), F32),
            pltpu.SemaphoreType.DMA((2,)),
        ],
    )
    return pl.pallas_call(
        functools.partial(_combine_kernel, n_sub=n_sub, n_tokens=t),
        grid_spec=grid_spec,
        out_shape=jax.ShapeDtypeStruct((t, d), F32),
        compiler_params=_cparams(("arbitrary",)),
        name="combine",
    )(pos_flat, y_tiles, probs, x1, mod3, g_post_ffn)


def _rope_freq_row():
    half = ROPE_DIM // 2
    inv_freq = ROPE_THETA ** (-jnp.arange(0, ROPE_DIM, 2, dtype=F32) / ROPE_DIM)
    head = jnp.concatenate([inv_freq, inv_freq, jnp.zeros((DA_HEAD_DIM - 2 * half,), F32)])
    return jnp.tile(head, LANES // DA_HEAD_DIM).reshape(1, LANES)


def _tile_tables(counts, block_tab, n_tokens):
    tm = TM_EXP
    n_exp = counts.shape[0]
    max_tiles = (n_tokens * TOP_K) // tm + n_exp
    tiles = (counts + tm - 1) // tm
    tile_end = jnp.cumsum(tiles)
    n_tiles = tile_end[-1]
    j = jnp.arange(max_tiles, dtype=jnp.int32)
    jj = jnp.minimum(j, n_tiles - 1)
    past = (jj[:, None] >= tile_end[None, :]).astype(jnp.int32)
    tile_expert = jnp.sum(past, axis=1)
    nth = jj - jnp.sum(past * tiles[None, :], axis=1)
    open_block = block_tab[tile_expert, nth]
    y_block = jnp.where(j < n_tiles, open_block, j)
    is_expert = (tile_expert[:, None] == jnp.arange(n_exp, dtype=jnp.int32)[None, :]).astype(jnp.int32)
    rows_valid = jnp.where(j < n_tiles,
                           jnp.minimum(jnp.sum(is_expert * counts[None, :], axis=1) - nth * tm, tm), 0)
    ids = jnp.arange(n_exp, dtype=jnp.int32)
    later = jnp.logical_and(ids[None, :] > ids[:, None], tiles[None, :] > 0)
    nxt = jnp.min(jnp.where(later, ids[None, :], n_exp), axis=1)
    nxt = jnp.where(nxt == n_exp, -1, nxt)
    next_expert = jnp.sum(is_expert * nxt[None, :], axis=1)
    as_i32 = lambda a: a.astype(jnp.int32)
    return (as_i32(tile_expert), as_i32(open_block), as_i32(y_block), as_i32(rows_valid),
            as_i32(next_expert), as_i32(n_tiles.reshape(1)))


def kernel(x, c, positions, w_mod, b_mod, g_pre_mix, w_in, lambda_q1, lambda_k1, lambda_q2, lambda_k2, g_sub, w_pool_group, pool_scale, w_att_branch, w_pool_branch, w_out, g_post_mix, g_pre_ffn, w_router, b_router, w_gate_up, b_gate_up, w_down, b_down, g_post_ffn):
    bsz, seq, d = x.shape
    assert w_mod.shape[0] == 1, "single layer"
    assert seq % TQ == 0 and seq % TM_PROJ == 0 and seq % TM_MIX == 0 and seq % TM_COMB == 0
    t = bsz * seq
    assert t % TM_EXP == 0
    x2 = x.reshape(t, d)
    qk_w = 2 * DA_HEADS * DA_HEAD_DIM
    v_w = DA_HEADS * DA_V_DIM
    pool_w = len(POOL_WINDOWS) * POOL_GROUP_DIM
    n_experts = w_router.shape[2]

    mod3 = _mod(c, w_mod[0], b_mod[0]).reshape(bsz, N_MOD, d)

    bounds = [0, qk_w, 2 * qk_w, 2 * qk_w + v_w, 2 * qk_w + v_w + pool_w,
              2 * qk_w + v_w + pool_w + d, 2 * qk_w + v_w + pool_w + 2 * d]
    q, k, vt, p, sga, sgp = _in_proj(x2, mod3, g_pre_mix, positions.reshape(t, 1), _rope_freq_row(),
                                     w_in[0], bounds, seq)

    row64 = lambda a: a.reshape(1, DA_HEAD_DIM)
    o = _attention(q, k, vt, row64(lambda_q1[0]), row64(lambda_k1[0]), row64(lambda_q2[0]),
                   row64(lambda_k2[0]), g_sub.reshape(DA_V_DIM, 1), bsz, seq)

    x1, probs_t, pos_t, counts, block_tab, xs_tiles = _post_mix(
        o, p, sga, sgp, x2, mod3, w_att_branch[0], w_pool_group[0].reshape(pool_w, POOL_GROUP_DIM),
        pool_scale, w_pool_branch[0], w_out[0],
        g_post_mix, g_pre_ffn, w_router[0].T, b_router.reshape(n_experts, 1), seq)

    tile_expert, xs_block, y_block, rows_valid, next_expert, n_tiles = _tile_tables(
        counts[:, 0], block_tab, t)
    y_tiles = _experts(tile_expert, xs_block, y_block, rows_valid, next_expert, n_tiles, xs_tiles,
                       w_gate_up[0], b_gate_up[0], w_down[0], b_down[0])
    out = _combine(pos_t.reshape(-1), y_tiles, probs_t.T, x1, mod3, g_post_ffn, seq)
    return out.reshape(bsz, seq, d)
```

```python
import functools

import jax
import jax.numpy as jnp
from jax import lax
from jax.experimental import pallas as pl
from jax.experimental.pallas import tpu as pltpu

F32 = jnp.float32
BF16 = jnp.bfloat16

NORM_EPS = 1e-6
CHUNK = 64
DA_HEADS = 8
DA_HEAD_DIM = 64
DA_V_DIM = 2 * DA_HEAD_DIM
ROPE_THETA = 500000.0
ROPE_DIM = DA_HEAD_DIM // 4
POOL_WINDOWS = (2, 4, 8, 16)
POOL_GROUP_DIM = 128
TOP_K = 4
SWIGLU_LIMIT = 7.0
SWIGLU_ALPHA = 1.702
N_MOD = 6
LAMBDA_INIT = 0.8 - 0.6 * 1.0
Q_SCALE = (DA_HEAD_DIM ** -0.5) * 1.4426950408889634

LANES = 128
SUBLANES = 8
VMEM_LIMIT = 56 * 1024 * 1024

TM_PROJ = 512
TQ = 512
TK = 256
ATT_HEADS_PER_STEP = 4
ATT_SUM_ROWS = 16
TM_MIX = 256
TM_EXP = 512
TM_COMB = 256
WEIGHT_CHUNK = 512
POOL_HALO = 16
NEG_BIG = -1e30


def _cparams(sem):
    return pltpu.CompilerParams(dimension_semantics=sem, vmem_limit_bytes=VMEM_LIMIT)


def _split(a):
    hi = a.astype(BF16)
    return hi, (a - hi.astype(F32)).astype(BF16)


def _split_dot(a, b, dims=(((1,), (0,)), ((), ()))):
    a_hi, a_lo = _split(a)
    b_hi, b_lo = _split(b)
    dot = lambda x, y: lax.dot_general(x, y, dims, preferred_element_type=F32)
    return dot(a_hi, b_hi) + (dot(a_hi, b_lo) + dot(a_lo, b_hi))


def _rms(x):
    return x * lax.rsqrt(jnp.mean(x * x, axis=-1, keepdims=True) + NORM_EPS)


def _mod_kernel(c_ref, w_ref, b_ref, o_ref):
    c = c_ref[...]
    c_act = c * jax.nn.sigmoid(c)
    o_ref[...] = _split_dot(c_act, w_ref[...]) + b_ref[...]


def _mod(c, w_mod, b_mod):
    bsz, d = c.shape
    n = w_mod.shape[1]
    tn = 1024
    return pl.pallas_call(
        _mod_kernel,
        grid=(n // tn,),
        in_specs=[
            pl.BlockSpec((bsz, d), lambda j: (0, 0)),
            pl.BlockSpec((d, tn), lambda j: (0, j)),
            pl.BlockSpec((1, tn), lambda j: (0, j)),
        ],
        out_specs=pl.BlockSpec((bsz, tn), lambda j: (0, j)),
        out_shape=jax.ShapeDtypeStruct((bsz, n), F32),
        compiler_params=_cparams(("parallel",)),
        name="mod",
    )(c, w_mod, b_mod.reshape(1, n))


def _rope(t, cos_t, sin_a, sin_b):
    n = t.shape[1]
    up = pltpu.roll(t, n - ROPE_DIM // 2, axis=1)
    dn = pltpu.roll(t, ROPE_DIM // 2, axis=1)
    reps = n // LANES
    tile = lambda a: jnp.concatenate([a] * reps, axis=1)
    return t * tile(cos_t) + up * tile(sin_a) + dn * tile(sin_b)


def _stream_windows(src_hbm, windows, stage, sem, sink):
    def copy(n):
        r0, nr, c0, nc = windows[n]
        return pltpu.make_async_copy(src_hbm.at[pl.ds(r0, nr), pl.ds(c0, nc)],
                                     stage.at[n % 2, pl.ds(0, nr), pl.ds(0, nc)], sem.at[n % 2])
    copy(0).start()
    for n, (_, nr, _, nc) in enumerate(windows):
        if n + 1 < len(windows):
            copy(n + 1).start()
        copy(n).wait()
        sink(n, stage[n % 2, 0:nr, 0:nc])


def _in_proj_kernel(x_ref, mod_ref, g_ref, pos_ref, freq_ref, w_hbm,
                    q_ref, k_ref, v_ref, p_ref, sga_ref, sgp_ref,
                    w_ref, wvt_ref, stage, sem, *, bounds):
    @pl.when(pl.program_id(0) == 0)
    def _():
        d_in, width = w_hbm.shape
        windows = [(0, d_in, c0, WEIGHT_CHUNK) for c0 in range(0, width, WEIGHT_CHUNK)]

        def sink(n, tile):
            c0 = windows[n][2]
            w_ref[:, c0:c0 + WEIGHT_CHUNK] = tile.astype(BF16)
            if bounds[2] <= c0 < bounds[3]:
                wvt_ref[c0 - bounds[2]:c0 - bounds[2] + WEIGHT_CHUNK, :] = tile.T.astype(BF16)
        _stream_windows(w_hbm, windows, stage, sem, sink)

    x = x_ref[...]
    shift = mod_ref[0, 0:1, :]
    scale = mod_ref[0, 1:2, :]
    u = (_rms(x) * g_ref[...]) * (1.0 + scale) + shift
    ub = u.astype(BF16)
    dot = functools.partial(jnp.dot, preferred_element_type=F32)
    part = lambda n: w_ref[:, bounds[n]:bounds[n + 1]]

    ang = pos_ref[...].astype(F32) * freq_ref[...]
    cos_t, sn = jnp.cos(ang), jnp.sin(ang)
    in_head = lax.broadcasted_iota(jnp.int32, ang.shape, 1) % DA_HEAD_DIM
    sin_a = jnp.where(in_head < ROPE_DIM // 2, -sn, 0.0)
    sin_b = jnp.where(in_head >= ROPE_DIM // 2, sn, 0.0)

    q = _rope(dot(ub, part(0)), cos_t, sin_a, sin_b)
    q_ref[...] = (q * Q_SCALE).astype(BF16)
    k = _rope(dot(ub, part(1)), cos_t, sin_a, sin_b)
    k_ref[...] = k.astype(BF16)
    vt = lax.dot_general(wvt_ref[...], ub, (((1,), (1,)), ((), ())), preferred_element_type=F32)
    for n in range(v_ref.shape[0]):
        v_ref[n] = vt[:, n * TK:(n + 1) * TK].astype(BF16)
    p_ref[...] = dot(ub, part(3))
    sga_ref[...] = jax.nn.sigmoid(dot(ub, part(4))).astype(BF16)
    sgp_ref[...] = jax.nn.sigmoid(dot(ub, part(5))).astype(BF16)


def _in_proj(x2, mod3, g_pre, pos_col, freq_row, w_in, bounds, seq):
    t, d = x2.shape
    width = w_in.shape[1]
    assert width % WEIGHT_CHUNK == 0 and all(b % WEIGHT_CHUNK == 0 for b in bounds[2:4])
    tm = TM_PROJ
    assert tm % TK == 0, "v is emitted as transposed (channels, TK) slabs"
    steps_per_seq = seq // tm
    widths = [b - a for a, b in zip(bounds[:-1], bounds[1:])]
    row = lambda w: pl.BlockSpec((tm, w), lambda i: (i, 0))
    full = lambda a: pl.BlockSpec(a.shape, lambda i: (0, 0))
    out_specs = [row(widths[0]), row(widths[1]),
                 pl.BlockSpec((tm // TK, widths[2], TK), lambda i: (i, 0, 0)),
                 row(widths[3]), row(widths[4]), row(widths[5])]
    out_shape = [jax.ShapeDtypeStruct((t, widths[0]), BF16),
                 jax.ShapeDtypeStruct((t, widths[1]), BF16),
                 jax.ShapeDtypeStruct((t // TK, widths[2], TK), BF16),
                 jax.ShapeDtypeStruct((t, widths[3]), F32),
                 jax.ShapeDtypeStruct((t, widths[4]), BF16),
                 jax.ShapeDtypeStruct((t, widths[5]), BF16)]
    return pl.pallas_call(
        functools.partial(_in_proj_kernel, bounds=tuple(bounds)),
        grid=(t // tm,),
        in_specs=[
            row(d),
            pl.BlockSpec((1, N_MOD, d), lambda i: (i // steps_per_seq, 0, 0)),
            full(g_pre), row(1), full(freq_row), pl.BlockSpec(memory_space=pl.ANY),
        ],
        out_specs=out_specs,
        out_shape=out_shape,
        scratch_shapes=[
            pltpu.VMEM((d, width), BF16),
            pltpu.VMEM((widths[2], d), BF16),
            pltpu.VMEM((2, d, WEIGHT_CHUNK), F32),
            pltpu.SemaphoreType.DMA((2,)),
        ],
        compiler_params=_cparams(("arbitrary",)),
        name="in_proj",
    )(x2, mod3, g_pre, pos_col, freq_row, w_in)


def _attn_kernel(lq1_ref, lk1_ref, lq2_ref, lk2_ref, g_ref, q_ref, k_ref, vt_ref, o_ref,
                 s_a, s_b, p_a, p_b, acc_buf):
    assert TQ == 2 * TK and TK % CHUNK == 0
    seq = q_ref.shape[0]
    lam = (jnp.exp(jnp.sum(lq1_ref[...] * lk1_ref[...], axis=-1, keepdims=True))
           - jnp.exp(jnp.sum(lq2_ref[...] * lk2_ref[...], axis=-1, keepdims=True))
           + LAMBDA_INIT)
    dot = functools.partial(jnp.dot, preferred_element_type=F32)
    heads = range(ATT_HEADS_PER_STEP)
    lanes = lambda h: slice(h * LANES, (h + 1) * LANES)

    def q_tile(qi, c):
        q_rows = pl.ds(pl.multiple_of(qi * TQ, TQ), TQ)
        qqt = []
        for h in heads:
            qt = q_ref[q_rows, lanes(h)].astype(F32).T
            row = lax.broadcasted_iota(jnp.int32, qt.shape, 0)
            zero = jnp.zeros_like(qt)
            qqt.append(jnp.concatenate([jnp.where(row < DA_HEAD_DIM, qt, zero),
                                        jnp.where(row >= DA_HEAD_DIM, qt, zero)],
                                       axis=1).astype(BF16))

        def scores(j, h):
            return dot(k_ref[pl.ds(pl.multiple_of(j * TK, TK), TK), lanes(h)], qqt[h])

        def softmax_step(s, m):
            m_new = jnp.maximum(m, jnp.max(s, axis=0, keepdims=True))
            return m_new, jnp.exp2(m - m_new), jnp.exp2(s - m_new).astype(BF16)

        ones_rows = jnp.ones((ATT_SUM_ROWS, TK), BF16)

        def pv(j, h, p):
            return dot(jnp.concatenate([vt_ref[j, lanes(h), :], ones_rows], axis=0), p)

        kk = lax.broadcasted_iota(jnp.int32, (TK, 2 * TQ), 0)
        qq = lax.broadcasted_iota(jnp.int32, (TK, 2 * TQ), 1)
        rel_chunk = jnp.where(qq >= TQ, qq - TQ, qq) // CHUNK - kk // CHUNK

        def masked(s, j):
            return jnp.where(rel_chunk >= j * (TK // CHUNK) - qi * (TQ // CHUNK), s, NEG_BIG)

        def step(j, carries, s_cur, p_cur, s_nxt, p_prev):
            pend = [pv(jnp.maximum(j - 1, 0), h, p_prev[h]) for h in heads]
            for h in heads:
                s_nxt[h] = scores(j + 1, h)
            out = []
            for h in heads:
                m, alpha = carries[h]
                acc_buf[h] = alpha * acc_buf[h] + pend[h]
                m, alpha, p = softmax_step(s_cur[h], m)
                p_cur[h] = p
                out.append((m, alpha))
            return tuple(out)

        def pair(i, carries):
            carries = step(2 * i, carries, s_a, p_a, s_b, p_b)
            return step(2 * i + 1, carries, s_b, p_b, s_a, p_a)

        for h in heads:
            s_a[h] = scores(0, h)
            p_b[h] = jnp.zeros((TK, 2 * TQ), BF16)
            acc_buf[h] = jnp.zeros(acc_buf.shape[1:], F32)
        init = tuple((jnp.full((1, 2 * TQ), NEG_BIG, F32), jnp.ones((1, 2 * TQ), F32))
                     for _ in heads)
        carries = lax.fori_loop(0, qi, pair, init)
        ja = 2 * qi
        late = lambda a: jnp.concatenate([a[:, TK:TQ], a[:, TQ + TK:]], axis=1)
        pend = [pv(jnp.maximum(ja - 1, 0), h, p_b[h]) for h in heads]
        for h in heads:
            k_last = k_ref[pl.ds(pl.multiple_of((ja + 1) * TK, TK), TK), lanes(h)]
            s_b[h, :, 0:TQ] = dot(k_last, late(qqt[h]))
        mid = []
        for h in heads:
            m, alpha = carries[h]
            acc_buf[h] = alpha * acc_buf[h] + pend[h]
            m, alpha, p = softmax_step(masked(s_a[h], ja), m)
            p_a[h] = p
            mid.append((m, alpha))
        pend = [pv(ja, h, p_a[h]) for h in heads]
        k_in = lax.broadcasted_iota(jnp.int32, (TK, TQ), 0)
        q_in = lax.broadcasted_iota(jnp.int32, (TK, TQ), 1) % TK
        diagonal = k_in // CHUNK <= q_in // CHUNK
        for h in heads:
            m, alpha = mid[h]
            acc = alpha * acc_buf[h] + pend[h]
            m_l, alpha_l, p_l = softmax_step(jnp.where(diagonal, s_b[h, :, 0:TQ], NEG_BIG), late(m))
            acc_l = alpha_l * late(acc) + pv(ja + 1, h, p_l)
            acc = jnp.concatenate([acc[:, :TK], acc_l[:, :TK], acc[:, TQ:TQ + TK], acc_l[:, TK:]], axis=1)
            o = acc[:DA_V_DIM] / acc[DA_V_DIM:DA_V_DIM + 1]
            a = o[:, :TQ] - lam * o[:, TQ:]
            y = a * lax.rsqrt(jnp.mean(a * a, axis=0, keepdims=True) + NORM_EPS)
            y = (y * g_ref[...]) * (1.0 - LAMBDA_INIT)
            o_ref[q_rows, lanes(h)] = y.T.astype(BF16)
        return c

    lax.fori_loop(0, seq // TQ, q_tile, 0)


def _attention(q, k, vt, lq1, lk1, lq2, lk2, g_col, bsz, seq):
    t = q.shape[0]
    nkv = seq // TK
    hw = ATT_HEADS_PER_STEP * LANES
    vec = lambda a: pl.BlockSpec(a.shape, lambda b, h: (0, 0))
    return pl.pallas_call(
        _attn_kernel,
        grid=(bsz, DA_HEADS // ATT_HEADS_PER_STEP),
        in_specs=[
            vec(lq1), vec(lk1), vec(lq2), vec(lk2), vec(g_col),
            pl.BlockSpec((seq, hw), lambda b, h: (b, h)),
            pl.BlockSpec((seq, hw), lambda b, h: (b, h)),
            pl.BlockSpec((nkv, hw, TK), lambda b, h: (b, h, 0)),
        ],
        out_specs=pl.BlockSpec((seq, hw), lambda b, h: (b, h)),
        out_shape=jax.ShapeDtypeStruct((t, DA_HEADS * DA_V_DIM), BF16),
        scratch_shapes=[
            pltpu.VMEM((ATT_HEADS_PER_STEP, TK, 2 * TQ), F32),
            pltpu.VMEM((ATT_HEADS_PER_STEP, TK, 2 * TQ), F32),
            pltpu.VMEM((ATT_HEADS_PER_STEP, TK, 2 * TQ), BF16),
            pltpu.VMEM((ATT_HEADS_PER_STEP, TK, 2 * TQ), BF16),
            pltpu.VMEM((ATT_HEADS_PER_STEP, DA_V_DIM + ATT_SUM_ROWS, 2 * TQ), F32),
        ],
        compiler_params=_cparams(("parallel", "parallel")),
        name="attention",
    )(lq1, lk1, lq2, lk2, g_col, q, k, vt)


def _post_mix_kernel(o_ref, p_ref, ph_ref, sga_ref, sgp_ref, x_ref, mod_ref,
                     watt_hbm, wpg_hbm, ps_ref, wpb_hbm, wout_hbm, gpm_ref, gpf_ref,
                     wrt_ref, br_ref,
                     x1_ref, prob_ref, pos_ref, cnt_ref, tab_ref, xs_hbm,
                     carry_ref, cur_ref, free_ref, u2t, zeros_v, pos_v, pos_s, meta_v, meta_s,
                     watt_ref, wpg_ref, wpb_ref, wout_ref, stage, sem_w,
                     sem_rows, sem_pos, sem_zero,
                     *, steps_per_seq, n_blocks):
    i = pl.program_id(0)
    n_steps = pl.num_programs(0)
    tm = x_ref.shape[0]
    n_experts = wrt_ref.shape[0]
    n_sub = x_ref.shape[1] // LANES
    slot = i % 2
    prev = 1 - slot
    dot = functools.partial(jnp.dot, preferred_element_type=F32)

    def pos_copy(s):
        return pltpu.make_async_copy(pos_v.at[s], pos_s.at[s], sem_pos.at[s])

    def row_copy(s, k, r):
        dst = pos_s[s, k, r]
        return pltpu.make_async_copy(
            u2t.at[s, pl.ds(r * n_sub, n_sub), :],
            xs_hbm.at[pl.ds(pl.multiple_of(dst * n_sub, n_sub), n_sub), :], sem_rows.at[s])

    def wait_rows(s):
        for _ in range(TOP_K):
            pltpu.make_async_copy(u2t.at[s], xs_hbm.at[pl.ds(0, tm * n_sub), :], sem_rows.at[s]).wait()

    @pl.when(i == 0)
    def _():
        carry_ref[...] = jnp.zeros_like(carry_ref)
        cur_ref[...] = jnp.zeros_like(cur_ref)
        free_ref[...] = jnp.zeros_like(free_ref)
        tab_ref[...] = jnp.zeros_like(tab_ref)
        zeros_v[...] = jnp.zeros_like(zeros_v)
        for src, dst in ((watt_hbm, watt_ref), (wpg_hbm, wpg_ref), (wpb_hbm, wpb_ref),
                         (wout_hbm, wout_ref)):
            rows, cols = src.shape
            step_c = min(cols, WEIGHT_CHUNK)
            windows = [(0, rows, c0, step_c) for c0 in range(0, cols, step_c)]

            def sink(n, tile, dst=dst, windows=windows):
                c0, nc = windows[n][2], windows[n][3]
                dst[:, c0:c0 + nc] = tile.astype(BF16)
            _stream_windows(src, windows, stage, sem_w, sink)
        u2t[1] = jnp.zeros(u2t.shape[1:], F32)
        spare = (n_blocks * TM_EXP
                 + lax.broadcasted_iota(jnp.int32, (TOP_K, tm), 0) * tm
                 + lax.broadcasted_iota(jnp.int32, (TOP_K, tm), 1))
        pos_v[1] = spare
        pos_copy(1).start()

    pos_copy(prev).wait()

    for r in range(tm):
        for k in range(TOP_K):
            row_copy(prev, k, r).start(priority=k % 2)

    y_att = dot(o_ref[...], watt_ref[...])

    first = (i % steps_per_seq) == 0
    halo = jnp.where(first, 0.0, ph_ref[...])
    ext = jnp.concatenate([halo, p_ref[...]], axis=0)
    t_in_seq = (i % steps_per_seq) * tm + lax.broadcasted_iota(jnp.int32, (tm, 1), 0)
    pooled = []
    for g, w in enumerate(POOL_WINDOWS):
        e = ext[:, g * POOL_GROUP_DIM:(g + 1) * POOL_GROUP_DIM]
        acc, span = e, 1
        while span < w:
            acc = acc[span:] + acc[:-span]
            span *= 2
        win = acc[POOL_HALO - (w - 1):]
        cnt = jnp.minimum(t_in_seq + 1, w).astype(F32)
        mixed = win / cnt - e[POOL_HALO:]
        pooled.append(dot(mixed.astype(BF16), wpg_ref[g * POOL_GROUP_DIM:(g + 1) * POOL_GROUP_DIM, :]))
    y_pool_in = jnp.concatenate(pooled, axis=1) * ps_ref[...]
    y_pool = dot(y_pool_in.astype(BF16), wpb_ref[...])

    merged = sga_ref[...].astype(F32) * y_att + sgp_ref[...].astype(F32) * y_pool
    mix_out = dot(merged.astype(BF16), wout_ref[...])
    gate_m = mod_ref[0, 2:3, :]
    shift_f = mod_ref[0, 3:4, :]
    scale_f = mod_ref[0, 4:5, :]
    x1 = x_ref[...] + gate_m * (_rms(mix_out) * gpm_ref[...])
    x1_ref[...] = x1
    u2 = (_rms(x1) * gpf_ref[...]) * (1.0 + scale_f) + shift_f

    logits = _split_dot(wrt_ref[...], u2, (((1,), (1,)), ((), ()))) + br_ref[...]
    erow = lax.broadcasted_iota(jnp.int32, logits.shape, 0)
    work = logits
    vals, idxs = [], []
    for _ in range(TOP_K):
        mx = jnp.max(work, axis=0, keepdims=True)
        ix = jnp.min(jnp.where(work == mx, erow, n_experts), axis=0, keepdims=True)
        vals.append(mx)
        idxs.append(ix)
        work = jnp.where(erow == ix, -jnp.inf, work)
    exps = [jnp.exp(vv - vals[0]) for vv in vals]
    denom = exps[0] + exps[1] + exps[2] + exps[3]
    prob_ref[...] = jnp.concatenate([e / denom for e in exps], axis=0)

    onehot = jnp.zeros(logits.shape, F32)
    for ix in idxs:
        onehot = onehot + (erow == ix).astype(F32)
    rr = lax.broadcasted_iota(jnp.int32, (tm, tm), 0)
    cc = lax.broadcasted_iota(jnp.int32, (tm, tm), 1)
    earlier = (rr < cc).astype(BF16)
    carry = carry_ref[...]
    before = dot(onehot.astype(BF16), earlier) + carry

    blk_rows = float(TM_EXP)
    total = carry + jnp.sum(onehot, axis=1, keepdims=True)
    blocks_old = jnp.floor((carry + (blk_rows - 1.0)) / blk_rows)
    opened = jnp.floor((total + (blk_rows - 1.0)) / blk_rows) - blocks_old
    e_r = lax.broadcasted_iota(jnp.int32, (n_experts, n_experts), 0)
    e_c = lax.broadcasted_iota(jnp.int32, (n_experts, n_experts), 1)
    lower = (e_c < e_r).astype(BF16)
    opened_before = dot(lower, jnp.broadcast_to(opened, (n_experts, LANES)).astype(BF16))[:, 0:1]
    new_blk = free_ref[...] + opened_before
    cur_blk = cur_ref[...]
    boundary = blocks_old * blk_rows
    blk_of = jnp.where(before < boundary, cur_blk, new_blk)
    pos_rows = []
    for ix in idxs:
        pick = erow == ix
        rank = jnp.sum(jnp.where(pick, before, 0.0), axis=0, keepdims=True)
        blk_id = jnp.sum(jnp.where(pick, blk_of, 0.0), axis=0, keepdims=True)
        within = rank - jnp.floor(rank / blk_rows) * blk_rows
        pos_rows.append((blk_id * blk_rows + within).astype(jnp.int32))
    pos = jnp.concatenate(pos_rows, axis=0)
    pos_ref[...] = pos
    tab_col = lax.broadcasted_iota(jnp.int32, tab_ref.shape, 1).astype(F32)
    tab_ref[...] = jnp.where(jnp.logical_and(tab_col == blocks_old, opened > 0.0),
                             new_blk.astype(jnp.int32), tab_ref[...])
    cur_ref[...] = jnp.where(opened > 0.0, new_blk, cur_blk)
    free_ref[...] = free_ref[...] + jnp.sum(opened, axis=0, keepdims=True)
    carry_ref[...] = total
    cnt_ref[...] = total.astype(jnp.int32)

    @pl.when(i > 0)
    def _():
        wait_rows(slot)

    for s in range(n_sub):
        u2t[slot, pl.ds(s, tm, stride=n_sub), :] = u2[:, s * LANES:(s + 1) * LANES]
    pos_v[slot] = pos
    pos_copy(slot).start()

    @pl.when(i == n_steps - 1)
    def _():
        pos_copy(slot).wait()

        def issue(r, c):
            for k in range(TOP_K):
                row_copy(slot, k, r).start(priority=k % 2)
            return c
        lax.fori_loop(0, tm, issue, 0)

        eye = e_r == e_c
        as_row = lambda col: jnp.sum(jnp.where(eye, col, 0.0), axis=0, keepdims=True).astype(jnp.int32)
        used = total - jnp.floor(total / blk_rows) * blk_rows
        meta = jnp.concatenate([as_row(used), as_row(cur_ref[...]),
                                jnp.broadcast_to(free_ref[...].astype(jnp.int32), (1, n_experts))], axis=0)
        meta_v[...] = jnp.zeros_like(meta_v)
        meta_v[0:3, 0:n_experts] = meta
        meta_copy = pltpu.make_async_copy(meta_v, meta_s, sem_zero)
        meta_copy.start()
        meta_copy.wait()

        sizes = [TM_EXP >> (b + 1) for b in range(TM_EXP.bit_length() - 1)]
        zero_run = lambda row0, n: pltpu.make_async_copy(
            zeros_v.at[pl.ds(0, n * n_sub), :],
            xs_hbm.at[pl.ds(pl.multiple_of(row0 * n_sub, n_sub), n * n_sub), :], sem_zero)
        issued = [jnp.int32(0) for _ in sizes]
        for e in range(n_experts):
            first = meta_s[0, e]
            row0 = meta_s[1, e] * TM_EXP + first
            pad = jnp.where(first > 0, TM_EXP - first, 0)
            for b, n in enumerate(sizes):
                take = (pad & n) != 0

                @pl.when(take)
                def _(row0=row0, n=n):
                    zero_run(row0, n).start()
                issued[b] = issued[b] + take.astype(jnp.int32)
                row0 = row0 + jnp.where(take, n, 0)
        n_open = meta_s[2, 0]

        def whole(b, c):
            zero_run(b * TM_EXP, TM_EXP).start()
            return c
        lax.fori_loop(n_open, n_blocks, whole, 0)

        for b, n in enumerate(sizes):
            lax.fori_loop(0, issued[b], lambda _, c, n=n: (zero_run(0, n).wait(), c)[1], 0)
        lax.fori_loop(n_open, n_blocks, lambda _, c: (zero_run(0, TM_EXP).wait(), c)[1], 0)
        wait_rows(prev)
        wait_rows(slot)


def _post_mix(o, p, sga, sgp, x2, mod3, w_att, w_pg, pool_scale, w_pb, w_out,
              g_post_mix, g_pre_ffn, w_router_t, b_router_col, seq):
    t, d = x2.shape
    tm = TM_MIX
    steps_per_seq = seq // tm
    n_experts = w_router_t.shape[0]
    pw = p.shape[1]
    n_sub = d // LANES
    halo_blocks = tm // POOL_HALO
    row = lambda w: pl.BlockSpec((tm, w), lambda i: (i, 0))
    col = lambda h: pl.BlockSpec((h, tm), lambda i: (0, i))
    full2 = lambda a: pl.BlockSpec(a.shape, lambda i: (0, 0))
    hbm = pl.BlockSpec(memory_space=pl.ANY)
    assert tm <= TM_EXP and (tm * TOP_K) % TM_EXP == 0 and t % TM_EXP == 0
    n_blocks = (t * TOP_K) // TM_EXP + n_experts
    tab_w = -(-(t // TM_EXP) // LANES) * LANES
    kern = functools.partial(_post_mix_kernel, steps_per_seq=steps_per_seq, n_blocks=n_blocks)
    xs_rows = n_blocks * TM_EXP + TOP_K * tm
    return pl.pallas_call(
        kern,
        grid=(t // tm,),
        in_specs=[
            row(d), row(pw),
            pl.BlockSpec((POOL_HALO, pw), lambda i: (jnp.maximum(i * halo_blocks - 1, 0), 0)),
            row(d), row(d), row(d),
            pl.BlockSpec((1, N_MOD, d), lambda i: (i // steps_per_seq, 0, 0)),
            hbm, hbm, full2(pool_scale), hbm, hbm, full2(g_post_mix), full2(g_pre_ffn),
            full2(w_router_t), full2(b_router_col),
        ],
        out_specs=[
            row(d), col(TOP_K), col(TOP_K),
            pl.BlockSpec((n_experts, 1), lambda i: (0, 0)),
            pl.BlockSpec((n_experts, tab_w), lambda i: (0, 0)),
            pl.BlockSpec(memory_space=pl.ANY),
        ],
        out_shape=[
            jax.ShapeDtypeStruct((t, d), F32),
            jax.ShapeDtypeStruct((TOP_K, t), F32),
            jax.ShapeDtypeStruct((TOP_K, t), jnp.int32),
            jax.ShapeDtypeStruct((n_experts, 1), jnp.int32),
            jax.ShapeDtypeStruct((n_experts, tab_w), jnp.int32),
            jax.ShapeDtypeStruct((xs_rows * n_sub, LANES), F32),
        ],
        scratch_shapes=[
            pltpu.VMEM((n_experts, 1), F32),
            pltpu.VMEM((n_experts, 1), F32),
            pltpu.VMEM((1, 1), F32),
            pltpu.VMEM((2, tm * n_sub, LANES), F32),
            pltpu.VMEM((TM_EXP * n_sub, LANES), F32),
            pltpu.VMEM((2, TOP_K, tm), jnp.int32),
            pltpu.SMEM((2, TOP_K, tm), jnp.int32),
            pltpu.VMEM((SUBLANES, LANES), jnp.int32),
            pltpu.SMEM((SUBLANES, LANES), jnp.int32),
            pltpu.VMEM(w_att.shape, BF16),
            pltpu.VMEM(w_pg.shape, BF16),
            pltpu.VMEM(w_pb.shape, BF16),
            pltpu.VMEM(w_out.shape, BF16),
            pltpu.VMEM((2, max(w_att.shape[0], w_out.shape[0]), WEIGHT_CHUNK), F32),
            pltpu.SemaphoreType.DMA((2,)),
            pltpu.SemaphoreType.DMA((2,)),
            pltpu.SemaphoreType.DMA((2,)),
            pltpu.SemaphoreType.DMA,
        ],
        compiler_params=_cparams(("arbitrary",)),
        name="post_mix",
    )(o, p, p, sga, sgp, x2, mod3, w_att, w_pg, pool_scale, w_pb, w_out,
      g_post_mix, g_pre_ffn, w_router_t, b_router_col)


def _experts_kernel(te_ref, xb_ref, yb_ref, nt_ref, nv_ref, nx_ref, xs_ref, wgu_hbm, bgu_ref, wd_hbm,
                    bd_ref, y_ref, wgu_bf, wd_bf, wgu_f32, wd_f32, sem, *, n_sub):
    j = pl.program_id(0)
    tm = TM_EXP
    half = tm // 2
    d_ff = wd_bf.shape[0]
    open_tile = j < nt_ref[0]

    def fetch(expert):
        return (pltpu.make_async_copy(wgu_hbm.at[expert], wgu_f32, sem.at[0]),
                pltpu.make_async_copy(wd_hbm.at[expert], wd_f32, sem.at[1]))

    @pl.when(open_tile)
    def _():
        changed = jnp.logical_or(j == 0, te_ref[j] != te_ref[jnp.maximum(j - 1, 0)])

        @pl.when(changed)
        def _():
            @pl.when(j == 0)
            def _():
                for c in fetch(te_ref[j]):
                    c.start()
            for c in fetch(te_ref[j]):
                c.wait()
            wgu_bf[...] = wgu_f32[...].astype(BF16)
            wd_bf[...] = wd_f32[...].astype(BF16)

            @pl.when(nx_ref[j] >= 0)
            def _():
                for c in fetch(nx_ref[j]):
                    c.start()

    def mlp(rows):
        xs = jnp.concatenate(
            [xs_ref[pl.ds(s, rows, stride=n_sub), :] for s in range(n_sub)], axis=1).astype(BF16)
        gu = jnp.dot(xs, wgu_bf[...], preferred_element_type=F32) + bgu_ref[0]
        gate = jnp.minimum(gu[:, :d_ff], SWIGLU_LIMIT)
        up = jnp.clip(gu[:, d_ff:], -SWIGLU_LIMIT, SWIGLU_LIMIT)
        act = (up + 1.0) * (gate * jax.nn.sigmoid(SWIGLU_ALPHA * gate))
        y = jnp.dot(act.astype(BF16), wd_bf[...], preferred_element_type=F32) + bd_ref[0]
        for s in range(n_sub):
            y_ref[pl.ds(s, rows, stride=n_sub), :] = y[:, s * LANES:(s + 1) * LANES]

    few = nv_ref[j] <= half

    @pl.when(jnp.logical_and(open_tile, jnp.logical_not(few)))
    def _():
        mlp(tm)

    @pl.when(jnp.logical_and(open_tile, few))
    def _():
        mlp(half)
        y_ref[pl.ds(half * n_sub, half * n_sub), :] = jnp.zeros((half * n_sub, LANES), F32)

    @pl.when(j >= nt_ref[0])
    def _():
        y_ref[...] = jnp.zeros_like(y_ref)


def _experts(tile_expert, xs_block, y_block, rows_valid, next_expert, n_tiles, xs_tiles,
             w_gate_up, b_gate_up, w_down, b_down):
    n_exp, d, d_gu = w_gate_up.shape
    d_ff = w_down.shape[1]
    n_sub = d // LANES
    tm = TM_EXP
    max_tiles = tile_expert.shape[0]
    by_expert = lambda j, te, xb, yb, nt, nv, nx: (te[j], 0, 0)
    by_block = lambda j, te, xb, yb, nt, nv, nx: (yb[j], 0)
    hbm = pl.BlockSpec(memory_space=pl.ANY)
    grid_spec = pltpu.PrefetchScalarGridSpec(
        num_scalar_prefetch=6,
        grid=(max_tiles,),
        in_specs=[
            pl.BlockSpec((tm * n_sub, LANES), lambda j, te, xb, yb, nt, nv, nx: (xb[j], 0)),
            hbm,
            pl.BlockSpec((1, 1, d_gu), by_expert),
            hbm,
            pl.BlockSpec((1, 1, d), by_expert),
        ],
        out_specs=pl.BlockSpec((tm * n_sub, LANES), by_block),
        scratch_shapes=[
            pltpu.VMEM((d, d_gu), BF16),
            pltpu.VMEM((d_ff, d), BF16),
            pltpu.VMEM((d, d_gu), F32),
            pltpu.VMEM((d_ff, d), F32),
            pltpu.SemaphoreType.DMA((2,)),
        ],
    )
    return pl.pallas_call(
        functools.partial(_experts_kernel, n_sub=n_sub),
        grid_spec=grid_spec,
        out_shape=jax.ShapeDtypeStruct((max_tiles * tm * n_sub, LANES), F32),
        compiler_params=_cparams(("arbitrary",)),
        name="experts",
    )(tile_expert, xs_block, y_block, n_tiles, rows_valid, next_expert, xs_tiles,
      w_gate_up, b_gate_up.reshape(n_exp, 1, d_gu), w_down, b_down.reshape(n_exp, 1, d))


def _combine_kernel(pos_ref, y_hbm, prob_ref, x1_ref, mod_ref, g_ref, o_ref, ybuf, sem,
                    *, n_sub, n_tokens):
    i = pl.program_id(0)
    n_steps = pl.num_programs(0)
    tm = TM_COMB
    slot = i % 2

    def gather(step, s):
        for r in range(tm):
            for k in range(TOP_K):
                row = pos_ref[k * n_tokens + step * tm + r]
                pltpu.make_async_copy(
                    y_hbm.at[pl.ds(pl.multiple_of(row * n_sub, n_sub), n_sub), :],
                    ybuf.at[s, k, pl.ds(r * n_sub, n_sub), :], sem.at[s]).start(priority=k % 2)

    @pl.when(i == 0)
    def _():
        gather(0, 0)

    @pl.when(i + 1 < n_steps)
    def _():
        gather(i + 1, 1 - slot)

    for k in range(TOP_K):
        pltpu.make_async_copy(y_hbm.at[pl.ds(0, tm * n_sub), :], ybuf.at[slot, k], sem.at[slot]).wait()

    prob = prob_ref[...]
    f = None
    for k in range(TOP_K):
        yk = jnp.concatenate(
            [ybuf[slot, k, pl.ds(s, tm, stride=n_sub), :] for s in range(n_sub)], axis=1)
        term = yk * prob[:, k:k + 1]
        f = term if f is None else f + term
    gate_f = mod_ref[0, 5:6, :]
    o_ref[...] = x1_ref[...] + gate_f * (_rms(f) * g_ref[...])


def _combine(pos_flat, y_tiles, probs, x1, mod3, g_post_ffn, seq):
    t, d = x1.shape
    tm = TM_COMB
    n_sub = d // LANES
    steps_per_seq = seq // tm
    grid_spec = pltpu.PrefetchScalarGridSpec(
        num_scalar_prefetch=1,
        grid=(t // tm,),
        in_specs=[
            pl.BlockSpec(memory_space=pl.ANY),
            pl.BlockSpec((tm, TOP_K), lambda i, pos: (i, 0)),
            pl.BlockSpec((tm, d), lambda i, pos: (i, 0)),
            pl.BlockSpec((1, N_MOD, d), lambda i, pos: (i // steps_per_seq, 0, 0)),
            pl.BlockSpec((1, d), lambda i, pos: (0, 0)),
        ],
        out_specs=pl.BlockSpec((tm, d), lambda i, pos: (i, 0)),
        scratch_shapes=[
            pltpu.VMEM((2, TOP_K, tm * n_sub, LANES), F32),
            pltpu.SemaphoreType.DMA((2,)),
        ],
    )
    return pl.pallas_call(
        functools.partial(_combine_kernel, n_sub=n_sub, n_tokens=t),
        grid_spec=grid_spec,
        out_shape=jax.ShapeDtypeStruct((t, d), F32),
        compiler_params=_cparams(("arbitrary",)),
        name="combine",
    )(pos_flat, y_tiles, probs, x1, mod3, g_post_ffn)


def _rope_freq_row():
    half = ROPE_DIM // 2
    inv_freq = ROPE_THETA ** (-jnp.arange(0, ROPE_DIM, 2, dtype=F32) / ROPE_DIM)
    head = jnp.concatenate([inv_freq, inv_freq, jnp.zeros((DA_HEAD_DIM - 2 * half,), F32)])
    return jnp.tile(head, LANES // DA_HEAD_DIM).reshape(1, LANES)


def _tile_tables(counts, block_tab, n_tokens):
    tm = TM_EXP
    n_exp = counts.shape[0]
    max_tiles = (n_tokens * TOP_K) // tm + n_exp
    tiles = (counts + tm - 1) // tm
    tile_end = jnp.cumsum(tiles)
    n_tiles = tile_end[-1]
    j = jnp.arange(max_tiles, dtype=jnp.int32)
    jj = jnp.minimum(j, n_tiles - 1)
    past = (jj[:, None] >= tile_end[None, :]).astype(jnp.int32)
    tile_expert = jnp.sum(past, axis=1)
    nth = jj - jnp.sum(past * tiles[None, :], axis=1)
    open_block = block_tab[tile_expert, nth]
    y_block = jnp.where(j < n_tiles, open_block, j)
    is_expert = (tile_expert[:, None] == jnp.arange(n_exp, dtype=jnp.int32)[None, :]).astype(jnp.int32)
    rows_valid = jnp.where(j < n_tiles,
                           jnp.minimum(jnp.sum(is_expert * counts[None, :], axis=1) - nth * tm, tm), 0)
    ids = jnp.arange(n_exp, dtype=jnp.int32)
    later = jnp.logical_and(ids[None, :] > ids[:, None], tiles[None, :] > 0)
    nxt = jnp.min(jnp.where(later, ids[None, :], n_exp), axis=1)
    nxt = jnp.where(nxt == n_exp, -1, nxt)
    next_expert = jnp.sum(is_expert * nxt[None, :], axis=1)
    as_i32 = lambda a: a.astype(jnp.int32)
    return (as_i32(tile_expert), as_i32(open_block), as_i32(y_block), as_i32(rows_valid),
            as_i32(next_expert), as_i32(n_tiles.reshape(1)))


def kernel(x, c, positions, w_mod, b_mod, g_pre_mix, w_in, lambda_q1, lambda_k1, lambda_q2, lambda_k2, g_sub, w_pool_group, pool_scale, w_att_branch, w_pool_branch, w_out, g_post_mix, g_pre_ffn, w_router, b_router, w_gate_up, b_gate_up, w_down, b_down, g_post_ffn):
    bsz, seq, d = x.shape
    assert w_mod.shape[0] == 1, "single layer"
    assert seq % TQ == 0 and seq % TM_PROJ == 0 and seq % TM_MIX == 0 and seq % TM_COMB == 0
    t = bsz * seq
    assert t % TM_EXP == 0
    x2 = x.reshape(t, d)
    qk_w = 2 * DA_HEADS * DA_HEAD_DIM
    v_w = DA_HEADS * DA_V_DIM
    pool_w = len(POOL_WINDOWS) * POOL_GROUP_DIM
    n_experts = w_router.shape[2]

    mod3 = _mod(c, w_mod[0], b_mod[0]).reshape(bsz, N_MOD, d)

    bounds = [0, qk_w, 2 * qk_w, 2 * qk_w + v_w, 2 * qk_w + v_w + pool_w,
              2 * qk_w + v_w + pool_w + d, 2 * qk_w + v_w + pool_w + 2 * d]
    q, k, vt, p, sga, sgp = _in_proj(x2, mod3, g_pre_mix, positions.reshape(t, 1), _rope_freq_row(),
                                     w_in[0], bounds, seq)

    row64 = lambda a: a.reshape(1, DA_HEAD_DIM)
    o = _attention(q, k, vt, row64(lambda_q1[0]), row64(lambda_k1[0]), row64(lambda_q2[0]),
                   row64(lambda_k2[0]), g_sub.reshape(DA_V_DIM, 1), bsz, seq)

    x1, probs_t, pos_t, counts, block_tab, xs_tiles = _post_mix(
        o, p, sga, sgp, x2, mod3, w_att_branch[0], w_pool_group[0].reshape(pool_w, POOL_GROUP_DIM),
        pool_scale, w_pool_branch[0], w_out[0],
        g_post_mix, g_pre_ffn, w_router[0].T, b_router.reshape(n_experts, 1), seq)

    tile_expert, xs_block, y_block, rows_valid, next_expert, n_tiles = _tile_tables(
        counts[:, 0], block_tab, t)
    y_tiles = _experts(tile_expert, xs_block, y_block, rows_valid, next_expert, n_tiles, xs_tiles,
                       w_gate_up[0], b_gate_up[0], w_down[0], b_down[0])
    out = _combine(pos_t.reshape(-1), y_tiles, probs_t.T, x1, mod3, g_post_ffn, seq)
    return out.reshape(bsz, seq, d)
```

```python
import functools

import jax
import jax.numpy as jnp
from jax import lax
from jax.experimental import pallas as pl
from jax.experimental.pallas import tpu as pltpu

F32 = jnp.float32
BF16 = jnp.bfloat16

NORM_EPS = 1e-6
CHUNK = 64
DA_HEADS = 8
DA_HEAD_DIM = 64
DA_V_DIM = 2 * DA_HEAD_DIM
ROPE_THETA = 500000.0
ROPE_DIM = DA_HEAD_DIM // 4
POOL_WINDOWS = (2, 4, 8, 16)
POOL_GROUP_DIM = 128
TOP_K = 4
SWIGLU_LIMIT = 7.0
SWIGLU_ALPHA = 1.702
N_MOD = 6
LAMBDA_INIT = 0.8 - 0.6 * 1.0
Q_SCALE = (DA_HEAD_DIM ** -0.5) * 1.4426950408889634

LANES = 128
SUBLANES = 8
VMEM_LIMIT = 56 * 1024 * 1024

TM_PROJ = 512
TQ = 512
TK = 256
ATT_HEADS_PER_STEP = 4
ATT_SUM_ROWS = 16
TM_MIX = 256
TM_EXP = 512
TM_COMB = 256
WEIGHT_CHUNK = 512
POOL_HALO = 16
NEG_BIG = -1e30


def _cparams(sem):
    return pltpu.CompilerParams(dimension_semantics=sem, vmem_limit_bytes=VMEM_LIMIT)


def _split(a):
    hi = a.astype(BF16)
    return hi, (a - hi.astype(F32)).astype(BF16)


def _split_dot(a, b, dims=(((1,), (0,)), ((), ()))):
    a_hi, a_lo = _split(a)
    b_hi, b_lo = _split(b)
    dot = lambda x, y: lax.dot_general(x, y, dims, preferred_element_type=F32)
    return dot(a_hi, b_hi) + (dot(a_hi, b_lo) + dot(a_lo, b_hi))


def _rms(x):
    return x * lax.rsqrt(jnp.mean(x * x, axis=-1, keepdims=True) + NORM_EPS)


def _mod_kernel(c_ref, w_ref, b_ref, o_ref):
    c = c_ref[...]
    c_act = c * jax.nn.sigmoid(c)
    o_ref[...] = _split_dot(c_act, w_ref[...]) + b_ref[...]


def _mod(c, w_mod, b_mod):
    bsz, d = c.shape
    n = w_mod.shape[1]
    tn = 1024
    return pl.pallas_call(
        _mod_kernel,
        grid=(n // tn,),
        in_specs=[
            pl.BlockSpec((bsz, d), lambda j: (0, 0)),
            pl.BlockSpec((d, tn), lambda j: (0, j)),
            pl.BlockSpec((1, tn), lambda j: (0, j)),
        ],
        out_specs=pl.BlockSpec((bsz, tn), lambda j: (0, j)),
        out_shape=jax.ShapeDtypeStruct((bsz, n), F32),
        compiler_params=_cparams(("parallel",)),
        name="mod",
    )(c, w_mod, b_mod.reshape(1, n))


def _rope(t, cos_t, sin_a, sin_b):
    n = t.shape[1]
    up = pltpu.roll(t, n - ROPE_DIM // 2, axis=1)
    dn = pltpu.roll(t, ROPE_DIM // 2, axis=1)
    reps = n // LANES
    tile = lambda a: jnp.concatenate([a] * reps, axis=1)
    return t * tile(cos_t) + up * tile(sin_a) + dn * tile(sin_b)


def _stream_windows(src_hbm, windows, stage, sem, sink):
    def copy(n):
        r0, nr, c0, nc = windows[n]
        return pltpu.make_async_copy(src_hbm.at[pl.ds(r0, nr), pl.ds(c0, nc)],
                                     stage.at[n % 2, pl.ds(0, nr), pl.ds(0, nc)], sem.at[n % 2])
    copy(0).start()
    for n, (_, nr, _, nc) in enumerate(windows):
        if n + 1 < len(windows):
            copy(n + 1).start()
        copy(n).wait()
        sink(n, stage[n % 2, 0:nr, 0:nc])


def _in_proj_kernel(x_ref, mod_ref, g_ref, pos_ref, freq_ref, w_hbm,
                    q_ref, k_ref, v_ref, p_ref, sga_ref, sgp_ref,
                    w_ref, wvt_ref, stage, sem, *, bounds):
    @pl.when(pl.program_id(0) == 0)
    def _():
        d_in, width = w_hbm.shape
        windows = [(0, d_in, c0, WEIGHT_CHUNK) for c0 in range(0, width, WEIGHT_CHUNK)]

        def sink(n, tile):
            c0 = windows[n][2]
            w_ref[:, c0:c0 + WEIGHT_CHUNK] = tile.astype(BF16)
            if bounds[2] <= c0 < bounds[3]:
                wvt_ref[c0 - bounds[2]:c0 - bounds[2] + WEIGHT_CHUNK, :] = tile.T.astype(BF16)
        _stream_windows(w_hbm, windows, stage, sem, sink)

    x = x_ref[...]
    shift = mod_ref[0, 0:1, :]
    scale = mod_ref[0, 1:2, :]
    u = (_rms(x) * g_ref[...]) * (1.0 + scale) + shift
    ub = u.astype(BF16)
    dot = functools.partial(jnp.dot, preferred_element_type=F32)
    part = lambda n: w_ref[:, bounds[n]:bounds[n + 1]]

    ang = pos_ref[...].astype(F32) * freq_ref[...]
    cos_t, sn = jnp.cos(ang), jnp.sin(ang)
    in_head = lax.broadcasted_iota(jnp.int32, ang.shape, 1) % DA_HEAD_DIM
    sin_a = jnp.where(in_head < ROPE_DIM // 2, -sn, 0.0)
    sin_b = jnp.where(in_head >= ROPE_DIM // 2, sn, 0.0)

    q = _rope(dot(ub, part(0)), cos_t, sin_a, sin_b)
    q_ref[...] = (q * Q_SCALE).astype(BF16)
    k = _rope(dot(ub, part(1)), cos_t, sin_a, sin_b)
    k_ref[...] = k.astype(BF16)
    vt = lax.dot_general(wvt_ref[...], ub, (((1,), (1,)), ((), ())), preferred_element_type=F32)
    for n in range(v_ref.shape[0]):
        v_ref[n] = vt[:, n * TK:(n + 1) * TK].astype(BF16)
    p_ref[...] = dot(ub, part(3))
    sga_ref[...] = jax.nn.sigmoid(dot(ub, part(4))).astype(BF16)
    sgp_ref[...] = jax.nn.sigmoid(dot(ub, part(5))).astype(BF16)


def _in_proj(x2, mod3, g_pre, pos_col, freq_row, w_in, bounds, seq):
    t, d = x2.shape
    width = w_in.shape[1]
    assert width % WEIGHT_CHUNK == 0 and all(b % WEIGHT_CHUNK == 0 for b in bounds[2:4])
    tm = TM_PROJ
    assert tm % TK == 0, "v is emitted as transposed (channels, TK) slabs"
    steps_per_seq = seq // tm
    widths = [b - a for a, b in zip(bounds[:-1], bounds[1:])]
    row = lambda w: pl.BlockSpec((tm, w), lambda i: (i, 0))
    full = lambda a: pl.BlockSpec(a.shape, lambda i: (0, 0))
    out_specs = [row(widths[0]), row(widths[1]),
                 pl.BlockSpec((tm // TK, widths[2], TK), lambda i: (i, 0, 0)),
                 row(widths[3]), row(widths[4]), row(widths[5])]
    out_shape = [jax.ShapeDtypeStruct((t, widths[0]), BF16),
                 jax.ShapeDtypeStruct((t, widths[1]), BF16),
                 jax.ShapeDtypeStruct((t // TK, widths[2], TK), BF16),
                 jax.ShapeDtypeStruct((t, widths[3]), F32),
                 jax.ShapeDtypeStruct((t, widths[4]), BF16),
                 jax.ShapeDtypeStruct((t, widths[5]), BF16)]
    return pl.pallas_call(
        functools.partial(_in_proj_kernel, bounds=tuple(bounds)),
        grid=(t // tm,),
        in_specs=[
            row(d),
            pl.BlockSpec((1, N_MOD, d), lambda i: (i // steps_per_seq, 0, 0)),
            full(g_pre), row(1), full(freq_row), pl.BlockSpec(memory_space=pl.ANY),
        ],
        out_specs=out_specs,
        out_shape=out_shape,
        scratch_shapes=[
            pltpu.VMEM((d, width), BF16),
            pltpu.VMEM((widths[2], d), BF16),
            pltpu.VMEM((2, d, WEIGHT_CHUNK), F32),
            pltpu.SemaphoreType.DMA((2,)),
        ],
        compiler_params=_cparams(("arbitrary",)),
        name="in_proj",
    )(x2, mod3, g_pre, pos_col, freq_row, w_in)


def _attn_kernel(lq1_ref, lk1_ref, lq2_ref, lk2_ref, g_ref, q_ref, k_ref, vt_ref, o_ref,
                 s_a, s_b, p_a, p_b, acc_buf):
    assert TQ == 2 * TK and TK % CHUNK == 0
    seq = q_ref.shape[0]
    lam = (jnp.exp(jnp.sum(lq1_ref[...] * lk1_ref[...], axis=-1, keepdims=True))
           - jnp.exp(jnp.sum(lq2_ref[...] * lk2_ref[...], axis=-1, keepdims=True))
           + LAMBDA_INIT)
    dot = functools.partial(jnp.dot, preferred_element_type=F32)
    heads = range(ATT_HEADS_PER_STEP)
    lanes = lambda h: slice(h * LANES, (h + 1) * LANES)

    def q_tile(qi, c):
        q_rows = pl.ds(pl.multiple_of(qi * TQ, TQ), TQ)
        qqt = []
        for h in heads:
            qt = q_ref[q_rows, lanes(h)].astype(F32).T
            row = lax.broadcasted_iota(jnp.int32, qt.shape, 0)
            zero = jnp.zeros_like(qt)
            qqt.append(jnp.concatenate([jnp.where(row < DA_HEAD_DIM, qt, zero),
                                        jnp.where(row >= DA_HEAD_DIM, qt, zero)],
                                       axis=1).astype(BF16))

        def scores(j, h):
            return dot(k_ref[pl.ds(pl.multiple_of(j * TK, TK), TK), lanes(h)], qqt[h])

        def softmax_step(s, m):
            m_new = jnp.maximum(m, jnp.max(s, axis=0, keepdims=True))
            return m_new, jnp.exp2(m - m_new), jnp.exp2(s - m_new).astype(BF16)

        ones_rows = jnp.ones((ATT_SUM_ROWS, TK), BF16)

        def pv(j, h, p):
            return dot(jnp.concatenate([vt_ref[j, lanes(h), :], ones_rows], axis=0), p)

        kk = lax.broadcasted_iota(jnp.int32, (TK, 2 * TQ), 0)
        qq = lax.broadcasted_iota(jnp.int32, (TK, 2 * TQ), 1)
        rel_chunk = jnp.where(qq >= TQ, qq - TQ, qq) // CHUNK - kk // CHUNK

        def masked(s, j):
            return jnp.where(rel_chunk >= j * (TK // CHUNK) - qi * (TQ // CHUNK), s, NEG_BIG)

        def step(j, carries, s_cur, p_cur, s_nxt, p_prev):
            pend = [pv(jnp.maximum(j - 1, 0), h, p_prev[h]) for h in heads]
            for h in heads:
                s_nxt[h] = scores(j + 1, h)
            out = []
            for h in heads:
                m, alpha = carries[h]
                acc_buf[h] = alpha * acc_buf[h] + pend[h]
                m, alpha, p = softmax_step(s_cur[h], m)
                p_cur[h] = p
                out.append((m, alpha))
            return tuple(out)

        def pair(i, carries):
            carries = step(2 * i, carries, s_a, p_a, s_b, p_b)
            return step(2 * i + 1, carries, s_b, p_b, s_a, p_a)

        for h in heads:
            s_a[h] = scores(0, h)
            p_b[h] = jnp.zeros((TK, 2 * TQ), BF16)
            acc_buf[h] = jnp.zeros(acc_buf.shape[1:], F32)
        init = tuple((jnp.full((1, 2 * TQ), NEG_BIG, F32), jnp.ones((1, 2 * TQ), F32))
                     for _ in heads)
        carries = lax.fori_loop(0, qi, pair, init)
        ja = 2 * qi
        late = lambda a: jnp.concatenate([a[:, TK:TQ], a[:, TQ + TK:]], axis=1)
        pend = [pv(jnp.maximum(ja - 1, 0), h, p_b[h]) for h in heads]
        for h in heads:
            k_last = k_ref[pl.ds(pl.multiple_of((ja + 1) * TK, TK), TK), lanes(h)]
            s_b[h, :, 0:TQ] = dot(k_last, late(qqt[h]))
        mid = []
        for h in heads:
            m, alpha = carries[h]
            acc_buf[h] = alpha * acc_buf[h] + pend[h]
            m, alpha, p = softmax_step(masked(s_a[h], ja), m)
            p_a[h] = p
            mid.append((m, alpha))
        pend = [pv(ja, h, p_a[h]) for h in heads]
        k_in = lax.broadcasted_iota(jnp.int32, (TK, TQ), 0)
        q_in = lax.broadcasted_iota(jnp.int32, (TK, TQ), 1) % TK
        diagonal = k_in // CHUNK <= q_in // CHUNK
        for h in heads:
            m, alpha = mid[h]
            acc = alpha * acc_buf[h] + pend[h]
            m_l, alpha_l, p_l = softmax_step(jnp.where(diagonal, s_b[h, :, 0:TQ], NEG_BIG), late(m))
            acc_l = alpha_l * late(acc) + pv(ja + 1, h, p_l)
            acc = jnp.concatenate([acc[:, :TK], acc_l[:, :TK], acc[:, TQ:TQ + TK], acc_l[:, TK:]], axis=1)
            o = acc[:DA_V_DIM] / acc[DA_V_DIM:DA_V_DIM + 1]
            a = o[:, :TQ] - lam * o[:, TQ:]
            y = a * lax.rsqrt(jnp.mean(a * a, axis=0, keepdims=True) + NORM_EPS)
            y = (y * g_ref[...]) * (1.0 - LAMBDA_INIT)
            o_ref[q_rows, lanes(h)] = y.T.astype(BF16)
        return c

    lax.fori_loop(0, seq // TQ, q_tile, 0)


def _attention(q, k, vt, lq1, lk1, lq2, lk2, g_col, bsz, seq):
    t = q.shape[0]
    nkv = seq // TK
    hw = ATT_HEADS_PER_STEP * LANES
    vec = lambda a: pl.BlockSpec(a.shape, lambda b, h: (0, 0))
    return pl.pallas_call(
        _attn_kernel,
        grid=(bsz, DA_HEADS // ATT_HEADS_PER_STEP),
        in_specs=[
            vec(lq1), vec(lk1), vec(lq2), vec(lk2), vec(g_col),
            pl.BlockSpec((seq, hw), lambda b, h: (b, h)),
            pl.BlockSpec((seq, hw), lambda b, h: (b, h)),
            pl.BlockSpec((nkv, hw, TK), lambda b, h: (b, h, 0)),
        ],
        out_specs=pl.BlockSpec((seq, hw), lambda b, h: (b, h)),
        out_shape=jax.ShapeDtypeStruct((t, DA_HEADS * DA_V_DIM), BF16),
        scratch_shapes=[
            pltpu.VMEM((ATT_HEADS_PER_STEP, TK, 2 * TQ), F32),
            pltpu.VMEM((ATT_HEADS_PER_STEP, TK, 2 * TQ), F32),
            pltpu.VMEM((ATT_HEADS_PER_STEP, TK, 2 * TQ), BF16),
            pltpu.VMEM((ATT_HEADS_PER_STEP, TK, 2 * TQ), BF16),
            pltpu.VMEM((ATT_HEADS_PER_STEP, DA_V_DIM + ATT_SUM_ROWS, 2 * TQ), F32),
        ],
        compiler_params=_cparams(("parallel", "parallel")),
        name="attention",
    )(lq1, lk1, lq2, lk2, g_col, q, k, vt)


def _post_mix_kernel(o_ref, p_ref, ph_ref, sga_ref, sgp_ref, x_ref, mod_ref,
                     watt_hbm, wpg_hbm, ps_ref, wpb_hbm, wout_hbm, gpm_ref, gpf_ref,
                     wrt_ref, br_ref,
                     x1_ref, prob_ref, pos_ref, cnt_ref, tab_ref, xs_hbm,
                     carry_ref, cur_ref, free_ref, u2t, zeros_v, pos_v, pos_s, meta_v, meta_s,
                     watt_ref, wpg_ref, wpb_ref, wout_ref, stage, sem_w,
                     sem_rows, sem_pos, sem_zero,
                     *, steps_per_seq, n_blocks):
    i = pl.program_id(0)
    n_steps = pl.num_programs(0)
    tm = x_ref.shape[0]
    n_experts = wrt_ref.shape[0]
    n_sub = x_ref.shape[1] // LANES
    slot = i % 2
    prev = 1 - slot
    dot = functools.partial(jnp.dot, preferred_element_type=F32)

    def pos_copy(s):
        return pltpu.make_async_copy(pos_v.at[s], pos_s.at[s], sem_pos.at[s])

    def row_copy(s, k, r):
        dst = pos_s[s, k, r]
        return pltpu.make_async_copy(
            u2t.at[s, pl.ds(r * n_sub, n_sub), :],
            xs_hbm.at[pl.ds(pl.multiple_of(dst * n_sub, n_sub), n_sub), :], sem_rows.at[s])

    def wait_rows(s):
        for _ in range(TOP_K):
            pltpu.make_async_copy(u2t.at[s], xs_hbm.at[pl.ds(0, tm * n_sub), :], sem_rows.at[s]).wait()

    @pl.when(i == 0)
    def _():
        carry_ref[...] = jnp.zeros_like(carry_ref)
        cur_ref[...] = jnp.zeros_like(cur_ref)
        free_ref[...] = jnp.zeros_like(free_ref)
        tab_ref[...] = jnp.zeros_like(tab_ref)
        zeros_v[...] = jnp.zeros_like(zeros_v)
        for src, dst in ((watt_hbm, watt_ref), (wpg_hbm, wpg_ref), (wpb_hbm, wpb_ref),
                         (wout_hbm, wout_ref)):
            rows, cols = src.shape
            step_c = min(cols, WEIGHT_CHUNK)
            windows = [(0, rows, c0, step_c) for c0 in range(0, cols, step_c)]

            def sink(n, tile, dst=dst, windows=windows):
                c0, nc = windows[n][2], windows[n][3]
                dst[:, c0:c0 + nc] = tile.astype(BF16)
            _stream_windows(src, windows, stage, sem_w, sink)
        u2t[1] = jnp.zeros(u2t.shape[1:], F32)
        spare = (n_blocks * TM_EXP
                 + lax.broadcasted_iota(jnp.int32, (TOP_K, tm), 0) * tm
                 + lax.broadcasted_iota(jnp.int32, (TOP_K, tm), 1))
        pos_v[1] = spare
        pos_copy(1).start()

    pos_copy(prev).wait()

    for r in range(tm):
        for k in range(TOP_K):
            row_copy(prev, k, r).start(priority=k % 2)

    y_att = dot(o_ref[...], watt_ref[...])

    first = (i % steps_per_seq) == 0
    halo = jnp.where(first, 0.0, ph_ref[...])
    ext = jnp.concatenate([halo, p_ref[...]], axis=0)
    t_in_seq = (i % steps_per_seq) * tm + lax.broadcasted_iota(jnp.int32, (tm, 1), 0)
    pooled = []
    for g, w in enumerate(POOL_WINDOWS):
        e = ext[:, g * POOL_GROUP_DIM:(g + 1) * POOL_GROUP_DIM]
        acc, span = e, 1
        while span < w:
            acc = acc[span:] + acc[:-span]
            span *= 2
        win = acc[POOL_HALO - (w - 1):]
        cnt = jnp.minimum(t_in_seq + 1, w).astype(F32)
        mixed = win / cnt - e[POOL_HALO:]
        pooled.append(dot(mixed.astype(BF16), wpg_ref[g * POOL_GROUP_DIM:(g + 1) * POOL_GROUP_DIM, :]))
    y_pool_in = jnp.concatenate(pooled, axis=1) * ps_ref[...]
    y_pool = dot(y_pool_in.astype(BF16), wpb_ref[...])

    merged = sga_ref[...].astype(F32) * y_att + sgp_ref[...].astype(F32) * y_pool
    mix_out = dot(merged.astype(BF16), wout_ref[...])
    gate_m = mod_ref[0, 2:3, :]
    shift_f = mod_ref[0, 3:4, :]
    scale_f = mod_ref[0, 4:5, :]
    x1 = x_ref[...] + gate_m * (_rms(mix_out) * gpm_ref[...])
    x1_ref[...] = x1
    u2 = (_rms(x1) * gpf_ref[...]) * (1.0 + scale_f) + shift_f

    logits = _split_dot(wrt_ref[...], u2, (((1,), (1,)), ((), ()))) + br_ref[...]
    erow = lax.broadcasted_iota(jnp.int32, logits.shape, 0)
    work = logits
    vals, idxs = [], []
    for _ in range(TOP_K):
        mx = jnp.max(work, axis=0, keepdims=True)
        ix = jnp.min(jnp.where(work == mx, erow, n_experts), axis=0, keepdims=True)
        vals.append(mx)
        idxs.append(ix)
        work = jnp.where(erow == ix, -jnp.inf, work)
    exps = [jnp.exp(vv - vals[0]) for vv in vals]
    denom = exps[0] + exps[1] + exps[2] + exps[3]
    prob_ref[...] = jnp.concatenate([e / denom for e in exps], axis=0)

    onehot = jnp.zeros(logits.shape, F32)
    for ix in idxs:
        onehot = onehot + (erow == ix).astype(F32)
    rr = lax.broadcasted_iota(jnp.int32, (tm, tm), 0)
    cc = lax.broadcasted_iota(jnp.int32, (tm, tm), 1)
    earlier = (rr < cc).astype(BF16)
    carry = carry_ref[...]
    before = dot(onehot.astype(BF16), earlier) + carry

    blk_rows = float(TM_EXP)
    total = carry + jnp.sum(onehot, axis=1, keepdims=True)
    blocks_old = jnp.floor((carry + (blk_rows - 1.0)) / blk_rows)
    opened = jnp.floor((total + (blk_rows - 1.0)) / blk_rows) - blocks_old
    e_r = lax.broadcasted_iota(jnp.int32, (n_experts, n_experts), 0)
    e_c = lax.broadcasted_iota(jnp.int32, (n_experts, n_experts), 1)
    lower = (e_c < e_r).astype(BF16)
    opened_before = dot(lower, jnp.broadcast_to(opened, (n_experts, LANES)).astype(BF16))[:, 0:1]
    new_blk = free_ref[...] + opened_before
    cur_blk = cur_ref[...]
    boundary = blocks_old * blk_rows
    blk_of = jnp.where(before < boundary, cur_blk, new_blk)
    pos_rows = []
    for ix in idxs:
        pick = erow == ix
        rank = jnp.sum(jnp.where(pick, before, 0.0), axis=0, keepdims=True)
        blk_id = jnp.sum(jnp.where(pick, blk_of, 0.0), axis=0, keepdims=True)
        within = rank - jnp.floor(rank / blk_rows) * blk_rows
        pos_rows.append((blk_id * blk_rows + within).astype(jnp.int32))
    pos = jnp.concatenate(pos_rows, axis=0)
    pos_ref[...] = pos
    tab_col = lax.broadcasted_iota(jnp.int32, tab_ref.shape, 1).astype(F32)
    tab_ref[...] = jnp.where(jnp.logical_and(tab_col == blocks_old, opened > 0.0),
                             new_blk.astype(jnp.int32), tab_ref[...])
    cur_ref[...] = jnp.where(opened > 0.0, new_blk, cur_blk)
    free_ref[...] = free_ref[...] + jnp.sum(opened, axis=0, keepdims=True)
    carry_ref[...] = total
    cnt_ref[...] = total.astype(jnp.int32)

    @pl.when(i > 0)
    def _():
        wait_rows(slot)

    for s in range(n_sub):
        u2t[slot, pl.ds(s, tm, stride=n_sub), :] = u2[:, s * LANES:(s + 1) * LANES]
    pos_v[slot] = pos
    pos_copy(slot).start()

    @pl.when(i == n_steps - 1)
    def _():
        pos_copy(slot).wait()

        def issue(r, c):
            for k in range(TOP_K):
                row_copy(slot, k, r).start(priority=k % 2)
            return c
        lax.fori_loop(0, tm, issue, 0)

        eye = e_r == e_c
        as_row = lambda col: jnp.sum(jnp.where(eye, col, 0.0), axis=0, keepdims=True).astype(jnp.int32)
        used = total - jnp.floor(total / blk_rows) * blk_rows
        meta = jnp.concatenate([as_row(used), as_row(cur_ref[...]),
                                jnp.broadcast_to(free_ref[...].astype(jnp.int32), (1, n_experts))], axis=0)
        meta_v[...] = jnp.zeros_like(meta_v)
        meta_v[0:3, 0:n_experts] = meta
        meta_copy = pltpu.make_async_copy(meta_v, meta_s, sem_zero)
        meta_copy.start()
        meta_copy.wait()

        sizes = [TM_EXP >> (b + 1) for b in range(TM_EXP.bit_length() - 1)]
        zero_run = lambda row0, n: pltpu.make_async_copy(
            zeros_v.at[pl.ds(0, n * n_sub), :],
            xs_hbm.at[pl.ds(pl.multiple_of(row0 * n_sub, n_sub), n * n_sub), :], sem_zero)
        issued = [jnp.int32(0) for _ in sizes]
        for e in range(n_experts):
            first = meta_s[0, e]
            row0 = meta_s[1, e] * TM_EXP + first
            pad = jnp.where(first > 0, TM_EXP - first, 0)
            for b, n in enumerate(sizes):
                take = (pad & n) != 0

                @pl.when(take)
                def _(row0=row0, n=n):
                    zero_run(row0, n).start()
                issued[b] = issued[b] + take.astype(jnp.int32)
                row0 = row0 + jnp.where(take, n, 0)
        n_open = meta_s[2, 0]

        def whole(b, c):
            zero_run(b * TM_EXP, TM_EXP).start()
            return c
        lax.fori_loop(n_open, n_blocks, whole, 0)

        for b, n in enumerate(sizes):
            lax.fori_loop(0, issued[b], lambda _, c, n=n: (zero_run(0, n).wait(), c)[1], 0)
        lax.fori_loop(n_open, n_blocks, lambda _, c: (zero_run(0, TM_EXP).wait(), c)[1], 0)
        wait_rows(prev)
        wait_rows(slot)


def _post_mix(o, p, sga, sgp, x2, mod3, w_att, w_pg, pool_scale, w_pb, w_out,
              g_post_mix, g_pre_ffn, w_router_t, b_router_col, seq):
    t, d = x2.shape
    tm = TM_MIX
    steps_per_seq = seq // tm
    n_experts = w_router_t.shape[0]
    pw = p.shape[1]
    n_sub = d // LANES
    halo_blocks = tm // POOL_HALO
    row = lambda w: pl.BlockSpec((tm, w), lambda i: (i, 0))
    col = lambda h: pl.BlockSpec((h, tm), lambda i: (0, i))
    full2 = lambda a: pl.BlockSpec(a.shape, lambda i: (0, 0))
    hbm = pl.BlockSpec(memory_space=pl.ANY)
    assert tm <= TM_EXP and (tm * TOP_K) % TM_EXP == 0 and t % TM_EXP == 0
    n_blocks = (t * TOP_K) // TM_EXP + n_experts
    tab_w = -(-(t // TM_EXP) // LANES) * LANES
    kern = functools.partial(_post_mix_kernel, steps_per_seq=steps_per_seq, n_blocks=n_blocks)
    xs_rows = n_blocks * TM_EXP + TOP_K * tm
    return pl.pallas_call(
        kern,
        grid=(t // tm,),
        in_specs=[
            row(d), row(pw),
            pl.BlockSpec((POOL_HALO, pw), lambda i: (jnp.maximum(i * halo_blocks - 1, 0), 0)),
            row(d), row(d), row(d),
            pl.BlockSpec((1, N_MOD, d), lambda i: (i // steps_per_seq, 0, 0)),
            hbm, hbm, full2(pool_scale), hbm, hbm, full2(g_post_mix), full2(g_pre_ffn),
            full2(w_router_t), full2(b_router_col),
        ],
        out_specs=[
            row(d), col(TOP_K), col(TOP_K),
            pl.BlockSpec((n_experts, 1), lambda i: (0, 0)),
            pl.BlockSpec((n_experts, tab_w), lambda i: (0, 0)),
            pl.BlockSpec(memory_space=pl.ANY),
        ],
        out_shape=[
            jax.ShapeDtypeStruct((t, d), F32),
            jax.ShapeDtypeStruct((TOP_K, t), F32),
            jax.ShapeDtypeStruct((TOP_K, t), jnp.int32),
            jax.ShapeDtypeStruct((n_experts, 1), jnp.int32),
            jax.ShapeDtypeStruct((n_experts, tab_w), jnp.int32),
            jax.ShapeDtypeStruct((xs_rows * n_sub, LANES), F32),
        ],
        scratch_shapes=[
            pltpu.VMEM((n_experts, 1), F32),
            pltpu.VMEM((n_experts, 1), F32),
            pltpu.VMEM((1, 1), F32),
            pltpu.VMEM((2, tm * n_sub, LANES), F32),
            pltpu.VMEM((TM_EXP * n_sub, LANES), F32),
            pltpu.VMEM((2, TOP_K, tm), jnp.int32),
            pltpu.SMEM((2, TOP_K, tm), jnp.int32),
            pltpu.VMEM((SUBLANES, LANES), jnp.int32),
            pltpu.SMEM((SUBLANES, LANES), jnp.int32),
            pltpu.VMEM(w_att.shape, BF16),
            pltpu.VMEM(w_pg.shape, BF16),
            pltpu.VMEM(w_pb.shape, BF16),
            pltpu.VMEM(w_out.shape, BF16),
            pltpu.VMEM((2, max(w_att.shape[0], w_out.shape[0]), WEIGHT_CHUNK), F32),
            pltpu.SemaphoreType.DMA((2,)),
            pltpu.SemaphoreType.DMA((2,)),
            pltpu.SemaphoreType.DMA((2,)),
            pltpu.SemaphoreType.DMA,
        ],
        compiler_params=_cparams(("arbitrary",)),
        name="post_mix",
    )(o, p, p, sga, sgp, x2, mod3, w_att, w_pg, pool_scale, w_pb, w_out,
      g_post_mix, g_pre_ffn, w_router_t, b_router_col)


def _experts_kernel(te_ref, xb_ref, yb_ref, nt_ref, nv_ref, nx_ref, xs_ref, wgu_hbm, bgu_ref, wd_hbm,
                    bd_ref, y_ref, wgu_bf, wd_bf, wgu_f32, wd_f32, sem, *, n_sub):
    j = pl.program_id(0)
    tm = TM_EXP
    half = tm // 2
    d_ff = wd_bf.shape[0]
    open_tile = j < nt_ref[0]

    def fetch(expert):
        return (pltpu.make_async_copy(wgu_hbm.at[expert], wgu_f32, sem.at[0]),
                pltpu.make_async_copy(wd_hbm.at[expert], wd_f32, sem.at[1]))

    @pl.when(open_tile)
    def _():
        changed = jnp.logical_or(j == 0, te_ref[j] != te_ref[jnp.maximum(j - 1, 0)])

        @pl.when(changed)
        def _():
            @pl.when(j == 0)
            def _():
                for c in fetch(te_ref[j]):
                    c.start()
            for c in fetch(te_ref[j]):
                c.wait()
            wgu_bf[...] = wgu_f32[...].astype(BF16)
            wd_bf[...] = wd_f32[...].astype(BF16)

            @pl.when(nx_ref[j] >= 0)
            def _():
                for c in fetch(nx_ref[j]):
                    c.start(priority=1)

    def mlp(rows):
        xs = jnp.concatenate(
            [xs_ref[pl.ds(s, rows, stride=n_sub), :] for s in range(n_sub)], axis=1).astype(BF16)
        gu = jnp.dot(xs, wgu_bf[...], preferred_element_type=F32) + bgu_ref[0]
        gate = jnp.minimum(gu[:, :d_ff], SWIGLU_LIMIT)
        up = jnp.clip(gu[:, d_ff:], -SWIGLU_LIMIT, SWIGLU_LIMIT)
        act = (up + 1.0) * (gate * jax.nn.sigmoid(SWIGLU_ALPHA * gate))
        y = jnp.dot(act.astype(BF16), wd_bf[...], preferred_element_type=F32) + bd_ref[0]
        for s in range(n_sub):
            y_ref[pl.ds(s, rows, stride=n_sub), :] = y[:, s * LANES:(s + 1) * LANES]

    few = nv_ref[j] <= half

    @pl.when(jnp.logical_and(open_tile, jnp.logical_not(few)))
    def _():
        mlp(tm)

    @pl.when(jnp.logical_and(open_tile, few))
    def _():
        mlp(half)
        y_ref[pl.ds(half * n_sub, half * n_sub), :] = jnp.zeros((half * n_sub, LANES), F32)

    @pl.when(j >= nt_ref[0])
    def _():
        y_ref[...] = jnp.zeros_like(y_ref)


def _experts(tile_expert, xs_block, y_block, rows_valid, next_expert, n_tiles, xs_tiles,
             w_gate_up, b_gate_up, w_down, b_down):
    n_exp, d, d_gu = w_gate_up.shape
    d_ff = w_down.shape[1]
    n_sub = d // LANES
    tm = TM_EXP
    max_tiles = tile_expert.shape[0]
    by_expert = lambda j, te, xb, yb, nt, nv, nx: (te[j], 0, 0)
    by_block = lambda j, te, xb, yb, nt, nv, nx: (yb[j], 0)
    hbm = pl.BlockSpec(memory_space=pl.ANY)
    grid_spec = pltpu.PrefetchScalarGridSpec(
        num_scalar_prefetch=6,
        grid=(max_tiles,),
        in_specs=[
            pl.BlockSpec((tm * n_sub, LANES), lambda j, te, xb, yb, nt, nv, nx: (xb[j], 0)),
            hbm,
            pl.BlockSpec((1, 1, d_gu), by_expert),
            hbm,
            pl.BlockSpec((1, 1, d), by_expert),
        ],
        out_specs=pl.BlockSpec((tm * n_sub, LANES), by_block),
        scratch_shapes=[
            pltpu.VMEM((d, d_gu), BF16),
            pltpu.VMEM((d_ff, d), BF16),
            pltpu.VMEM((d, d_gu), F32),
            pltpu.VMEM((d_ff, d), F32),
            pltpu.SemaphoreType.DMA((2,)),
        ],
    )
    return pl.pallas_call(
        functools.partial(_experts_kernel, n_sub=n_sub),
        grid_spec=grid_spec,
        out_shape=jax.ShapeDtypeStruct((max_tiles * tm * n_sub, LANES), F32),
        compiler_params=_cparams(("arbitrary",)),
        name="experts",
    )(tile_expert, xs_block, y_block, n_tiles, rows_valid, next_expert, xs_tiles,
      w_gate_up, b_gate_up.reshape(n_exp, 1, d_gu), w_down, b_down.reshape(n_exp, 1, d))


def _combine_kernel(pos_ref, y_hbm, prob_ref, x1_ref, mod_ref, g_ref, o_ref, ybuf, sem,
                    *, n_sub, n_tokens):
    i = pl.program_id(0)
    n_steps = pl.num_programs(0)
    tm = TM_COMB
    slot = i % 2

    def gather(step, s):
        for r in range(tm):
            for k in range(TOP_K):
                row = pos_ref[k * n_tokens + step * tm + r]
                pltpu.make_async_copy(
                    y_hbm.at[pl.ds(pl.multiple_of(row * n_sub, n_sub), n_sub), :],
                    ybuf.at[s, k, pl.ds(r * n_sub, n_sub), :], sem.at[s]).start(priority=k % 2)

    @pl.when(i == 0)
    def _():
        gather(0, 0)

    @pl.when(i + 1 < n_steps)
    def _():
        gather(i + 1, 1 - slot)

    for k in range(TOP_K):
        pltpu.make_async_copy(y_hbm.at[pl.ds(0, tm * n_sub), :], ybuf.at[slot, k], sem.at[slot]).wait()

    prob = prob_ref[...]
    f = None
    for k in range(TOP_K):
        yk = jnp.concatenate(
            [ybuf[slot, k, pl.ds(s, tm, stride=n_sub), :] for s in range(n_sub)], axis=1)
        term = yk * prob[:, k:k + 1]
        f = term if f is None else f + term
    gate_f = mod_ref[0, 5:6, :]
    o_ref[...] = x1_ref[...] + gate_f * (_rms(f) * g_ref[...])


def _combine(pos_flat, y_tiles, probs, x1, mod3, g_post_ffn, seq):
    t, d = x1.shape
    tm = TM_COMB
    n_sub = d // LANES
    steps_per_seq = seq // tm
    grid_spec = pltpu.PrefetchScalarGridSpec(
        num_scalar_prefetch=1,
        grid=(t // tm,),
        in_specs=[
            pl.BlockSpec(memory_space=pl.ANY),
            pl.BlockSpec((tm, TOP_K), lambda i, pos: (i, 0)),
            pl.BlockSpec((tm, d), lambda i, pos: (i, 0)),
            pl.BlockSpec((1, N_MOD, d), lambda i, pos: (i // steps_per_seq, 0, 0)),
            pl.BlockSpec((1, d), lambda i, pos: (0, 0)),
        ],
        out_specs=pl.BlockSpec((tm, d), lambda i, pos: (i, 0)),
        scratch_shapes=[
            pltpu.VMEM((2, TOP_K, tm * n_sub, LANES), F32),
            pltpu.SemaphoreType.DMA((2,)),
        ],
    )
    return pl.pallas_call(
        functools.partial(_combine_kernel, n_sub=n_sub, n_tokens=t),
        grid_spec=grid_spec,
        out_shape=jax.ShapeDtypeStruct((t, d), F32),
        compiler_params=_cparams(("arbitrary",)),
        name="combine",
    )(pos_flat, y_tiles, probs, x1, mod3, g_post_ffn)


def _rope_freq_row():
    half = ROPE_DIM // 2
    inv_freq = ROPE_THETA ** (-jnp.arange(0, ROPE_DIM, 2, dtype=F32) / ROPE_DIM)
    head = jnp.concatenate([inv_freq, inv_freq, jnp.zeros((DA_HEAD_DIM - 2 * half,), F32)])
    return jnp.tile(head, LANES // DA_HEAD_DIM).reshape(1, LANES)


def _tile_tables(counts, block_tab, n_tokens):
    tm = TM_EXP
    n_exp = counts.shape[0]
    max_tiles = (n_tokens * TOP_K) // tm + n_exp
    tiles = (counts + tm - 1) // tm
    tile_end = jnp.cumsum(tiles)
    n_tiles = tile_end[-1]
    j = jnp.arange(max_tiles, dtype=jnp.int32)
    jj = jnp.minimum(j, n_tiles - 1)
    past = (jj[:, None] >= tile_end[None, :]).astype(jnp.int32)
    tile_expert = jnp.sum(past, axis=1)
    nth = jj - jnp.sum(past * tiles[None, :], axis=1)
    open_block = block_tab[tile_expert, nth]
    y_block = jnp.where(j < n_tiles, open_block, j)
    is_expert = (tile_expert[:, None] == jnp.arange(n_exp, dtype=jnp.int32)[None, :]).astype(jnp.int32)
    rows_valid = jnp.where(j < n_tiles,
                           jnp.minimum(jnp.sum(is_expert * counts[None, :], axis=1) - nth * tm, tm), 0)
    ids = jnp.arange(n_exp, dtype=jnp.int32)
    later = jnp.logical_and(ids[None, :] > ids[:, None], tiles[None, :] > 0)
    nxt = jnp.min(jnp.where(later, ids[None, :], n_exp), axis=1)
    nxt = jnp.where(nxt == n_exp, -1, nxt)
    next_expert = jnp.sum(is_expert * nxt[None, :], axis=1)
    as_i32 = lambda a: a.astype(jnp.int32)
    return (as_i32(tile_expert), as_i32(open_block), as_i32(y_block), as_i32(rows_valid),
            as_i32(next_expert), as_i32(n_tiles.reshape(1)))


def kernel(x, c, positions, w_mod, b_mod, g_pre_mix, w_in, lambda_q1, lambda_k1, lambda_q2, lambda_k2, g_sub, w_pool_group, pool_scale, w_att_branch, w_pool_branch, w_out, g_post_mix, g_pre_ffn, w_router, b_router, w_gate_up, b_gate_up, w_down, b_down, g_post_ffn):
    bsz, seq, d = x.shape
    assert w_mod.shape[0] == 1, "single layer"
    assert seq % TQ == 0 and seq % TM_PROJ == 0 and seq % TM_MIX == 0 and seq % TM_COMB == 0
    t = bsz * seq
    assert t % TM_EXP == 0
    x2 = x.reshape(t, d)
    qk_w = 2 * DA_HEADS * DA_HEAD_DIM
    v_w = DA_HEADS * DA_V_DIM
    pool_w = len(POOL_WINDOWS) * POOL_GROUP_DIM
    n_experts = w_router.shape[2]

    mod3 = _mod(c, w_mod[0], b_mod[0]).reshape(bsz, N_MOD, d)

    bounds = [0, qk_w, 2 * qk_w, 2 * qk_w + v_w, 2 * qk_w + v_w + pool_w,
              2 * qk_w + v_w + pool_w + d, 2 * qk_w + v_w + pool_w + 2 * d]
    q, k, vt, p, sga, sgp = _in_proj(x2, mod3, g_pre_mix, positions.reshape(t, 1), _rope_freq_row(),
                                     w_in[0], bounds, seq)

    row64 = lambda a: a.reshape(1, DA_HEAD_DIM)
    o = _attention(q, k, vt, row64(lambda_q1[0]), row64(lambda_k1[0]), row64(lambda_q2[0]),
                   row64(lambda_k2[0]), g_sub.reshape(DA_V_DIM, 1), bsz, seq)

    x1, probs_t, pos_t, counts, block_tab, xs_tiles = _post_mix(
        o, p, sga, sgp, x2, mod3, w_att_branch[0], w_pool_group[0].reshape(pool_w, POOL_GROUP_DIM),
        pool_scale, w_pool_branch[0], w_out[0],
        g_post_mix, g_pre_ffn, w_router[0].T, b_router.reshape(n_experts, 1), seq)

    tile_expert, xs_block, y_block, rows_valid, next_expert, n_tiles = _tile_tables(
        counts[:, 0], block_tab, t)
    y_tiles = _experts(tile_expert, xs_block, y_block, rows_valid, next_expert, n_tiles, xs_tiles,
                       w_gate_up[0], b_gate_up[0], w_down[0], b_down[0])
    out = _combine(pos_t.reshape(-1), y_tiles, probs_t.T, x1, mod3, g_post_ffn, seq)
    return out.reshape(bsz, seq, d)
```

```python
import functools

import jax
import jax.numpy as jnp
from jax import lax
from jax.experimental import pallas as pl
from jax.experimental.pallas import tpu as pltpu

F32 = jnp.float32
BF16 = jnp.bfloat16

NORM_EPS = 1e-6
CHUNK = 64
DA_HEADS = 8
DA_HEAD_DIM = 64
DA_V_DIM = 2 * DA_HEAD_DIM
ROPE_THETA = 500000.0
ROPE_DIM = DA_HEAD_DIM // 4
POOL_WINDOWS = (2, 4, 8, 16)
POOL_GROUP_DIM = 128
TOP_K = 4
SWIGLU_LIMIT = 7.0
SWIGLU_ALPHA = 1.702
N_MOD = 6
LAMBDA_INIT = 0.8 - 0.6 * 1.0
Q_SCALE = (DA_HEAD_DIM ** -0.5) * 1.4426950408889634

LANES = 128
SUBLANES = 8
VMEM_LIMIT = 60 * 1024 * 1024

TM_PROJ = 1024
TQ = 512
TK = 256
ATT_HEADS_PER_STEP = 4
ATT_SUM_ROWS = 16
TM_MIX = 256
TM_EXP = 512
TM_COMB = 256
WEIGHT_CHUNK = 256
POOL_HALO = 16
NEG_BIG = -1e30


def _cparams(sem):
    return pltpu.CompilerParams(dimension_semantics=sem, vmem_limit_bytes=VMEM_LIMIT)


def _split(a):
    hi = a.astype(BF16)
    return hi, (a - hi.astype(F32)).astype(BF16)


def _split_dot(a, b, dims=(((1,), (0,)), ((), ()))):
    a_hi, a_lo = _split(a)
    b_hi, b_lo = _split(b)
    dot = lambda x, y: lax.dot_general(x, y, dims, preferred_element_type=F32)
    return dot(a_hi, b_hi) + (dot(a_hi, b_lo) + dot(a_lo, b_hi))


def _rms(x):
    return x * lax.rsqrt(jnp.mean(x * x, axis=-1, keepdims=True) + NORM_EPS)


def _mod_kernel(c_ref, w_ref, b_ref, o_ref):
    c = c_ref[...]
    c_act = c * jax.nn.sigmoid(c)
    o_ref[...] = _split_dot(c_act, w_ref[...]) + b_ref[...]


def _mod(c, w_mod, b_mod):
    bsz, d = c.shape
    n = w_mod.shape[1]
    tn = 1024
    return pl.pallas_call(
        _mod_kernel,
        grid=(n // tn,),
        in_specs=[
            pl.BlockSpec((bsz, d), lambda j: (0, 0)),
            pl.BlockSpec((d, tn), lambda j: (0, j)),
            pl.BlockSpec((1, tn), lambda j: (0, j)),
        ],
        out_specs=pl.BlockSpec((bsz, tn), lambda j: (0, j)),
        out_shape=jax.ShapeDtypeStruct((bsz, n), F32),
        compiler_params=_cparams(("parallel",)),
        name="mod",
    )(c, w_mod, b_mod.reshape(1, n))


def _rope(t, cos_t, sin_a, sin_b):
    n = t.shape[1]
    up = pltpu.roll(t, n - ROPE_DIM // 2, axis=1)
    dn = pltpu.roll(t, ROPE_DIM // 2, axis=1)
    reps = n // LANES
    tile = lambda a: jnp.concatenate([a] * reps, axis=1)
    return t * tile(cos_t) + up * tile(sin_a) + dn * tile(sin_b)


def _stream_windows(src_hbm, windows, stage, sem, sink):
    def copy(n):
        r0, nr, c0, nc = windows[n]
        return pltpu.make_async_copy(src_hbm.at[pl.ds(r0, nr), pl.ds(c0, nc)],
                                     stage.at[n % 2, pl.ds(0, nr), pl.ds(0, nc)], sem.at[n % 2])
    copy(0).start()
    for n, (_, nr, _, nc) in enumerate(windows):
        if n + 1 < len(windows):
            copy(n + 1).start()
        copy(n).wait()
        sink(n, stage[n % 2, 0:nr, 0:nc])


def _in_proj_kernel(x_ref, mod_ref, g_ref, pos_ref, freq_ref, w_hbm,
                    q_ref, k_ref, v_ref, p_ref, sga_ref, sgp_ref,
                    w_ref, wvt_ref, stage, sem, *, bounds):
    @pl.when(pl.program_id(0) == 0)
    def _():
        d_in, width = w_hbm.shape
        windows = [(0, d_in, c0, WEIGHT_CHUNK) for c0 in range(0, width, WEIGHT_CHUNK)]

        def sink(n, tile):
            c0 = windows[n][2]
            w_ref[:, c0:c0 + WEIGHT_CHUNK] = tile.astype(BF16)
            if bounds[2] <= c0 < bounds[3]:
                wvt_ref[c0 - bounds[2]:c0 - bounds[2] + WEIGHT_CHUNK, :] = tile.T.astype(BF16)
        _stream_windows(w_hbm, windows, stage, sem, sink)

    x = x_ref[...]
    shift = mod_ref[0, 0:1, :]
    scale = mod_ref[0, 1:2, :]
    u = (_rms(x) * g_ref[...]) * (1.0 + scale) + shift
    ub = u.astype(BF16)
    dot = functools.partial(jnp.dot, preferred_element_type=F32)
    part = lambda n: w_ref[:, bounds[n]:bounds[n + 1]]

    ang = pos_ref[...].astype(F32) * freq_ref[...]
    cos_t, sn = jnp.cos(ang), jnp.sin(ang)
    in_head = lax.broadcasted_iota(jnp.int32, ang.shape, 1) % DA_HEAD_DIM
    sin_a = jnp.where(in_head < ROPE_DIM // 2, -sn, 0.0)
    sin_b = jnp.where(in_head >= ROPE_DIM // 2, sn, 0.0)

    q = _rope(dot(ub, part(0)), cos_t, sin_a, sin_b)
    q_ref[...] = (q * Q_SCALE).astype(BF16)
    k = _rope(dot(ub, part(1)), cos_t, sin_a, sin_b)
    k_ref[...] = k.astype(BF16)
    vt = lax.dot_general(wvt_ref[...], ub, (((1,), (1,)), ((), ())), preferred_element_type=F32)
    for n in range(v_ref.shape[0]):
        v_ref[n] = vt[:, n * TK:(n + 1) * TK].astype(BF16)
    p_ref[...] = dot(ub, part(3))
    sga_ref[...] = jax.nn.sigmoid(dot(ub, part(4))).astype(BF16)
    sgp_ref[...] = jax.nn.sigmoid(dot(ub, part(5))).astype(BF16)


def _in_proj(x2, mod3, g_pre, pos_col, freq_row, w_in, bounds, seq):
    t, d = x2.shape
    width = w_in.shape[1]
    assert width % WEIGHT_CHUNK == 0 and all(b % WEIGHT_CHUNK == 0 for b in bounds[2:4])
    tm = TM_PROJ
    assert tm % TK == 0, "v is emitted as transposed (channels, TK) slabs"
    steps_per_seq = seq // tm
    widths = [b - a for a, b in zip(bounds[:-1], bounds[1:])]
    row = lambda w: pl.BlockSpec((tm, w), lambda i: (i, 0))
    full = lambda a: pl.BlockSpec(a.shape, lambda i: (0, 0))
    out_specs = [row(widths[0]), row(widths[1]),
                 pl.BlockSpec((tm // TK, widths[2], TK), lambda i: (i, 0, 0)),
                 row(widths[3]), row(widths[4]), row(widths[5])]
    out_shape = [jax.ShapeDtypeStruct((t, widths[0]), BF16),
                 jax.ShapeDtypeStruct((t, widths[1]), BF16),
                 jax.ShapeDtypeStruct((t // TK, widths[2], TK), BF16),
                 jax.ShapeDtypeStruct((t, widths[3]), F32),
                 jax.ShapeDtypeStruct((t, widths[4]), BF16),
                 jax.ShapeDtypeStruct((t, widths[5]), BF16)]
    return pl.pallas_call(
        functools.partial(_in_proj_kernel, bounds=tuple(bounds)),
        grid=(t // tm,),
        in_specs=[
            row(d),
            pl.BlockSpec((1, N_MOD, d), lambda i: (i // steps_per_seq, 0, 0)),
            full(g_pre), row(1), full(freq_row), pl.BlockSpec(memory_space=pl.ANY),
        ],
        out_specs=out_specs,
        out_shape=out_shape,
        scratch_shapes=[
            pltpu.VMEM((d, width), BF16),
            pltpu.VMEM((widths[2], d), BF16),
            pltpu.VMEM((2, d, WEIGHT_CHUNK), F32),
            pltpu.SemaphoreType.DMA((2,)),
        ],
        compiler_params=_cparams(("arbitrary",)),
        name="in_proj",
    )(x2, mod3, g_pre, pos_col, freq_row, w_in)


def _attn_kernel(lq1_ref, lk1_ref, lq2_ref, lk2_ref, g_ref, q_ref, k_ref, vt_ref, o_ref,
                 s_a, s_b, p_a, p_b, acc_buf):
    assert TQ == 2 * TK and TK % CHUNK == 0
    seq = q_ref.shape[0]
    lam = (jnp.exp(jnp.sum(lq1_ref[...] * lk1_ref[...], axis=-1, keepdims=True))
           - jnp.exp(jnp.sum(lq2_ref[...] * lk2_ref[...], axis=-1, keepdims=True))
           + LAMBDA_INIT)
    dot = functools.partial(jnp.dot, preferred_element_type=F32)
    heads = range(ATT_HEADS_PER_STEP)
    lanes = lambda h: slice(h * LANES, (h + 1) * LANES)

    def q_tile(qi, c):
        q_rows = pl.ds(pl.multiple_of(qi * TQ, TQ), TQ)
        qqt = []
        for h in heads:
            qt = q_ref[q_rows, lanes(h)].astype(F32).T
            row = lax.broadcasted_iota(jnp.int32, qt.shape, 0)
            zero = jnp.zeros_like(qt)
            qqt.append(jnp.concatenate([jnp.where(row < DA_HEAD_DIM, qt, zero),
                                        jnp.where(row >= DA_HEAD_DIM, qt, zero)],
                                       axis=1).astype(BF16))

        def scores(j, h):
            return dot(k_ref[pl.ds(pl.multiple_of(j * TK, TK), TK), lanes(h)], qqt[h])

        def softmax_step(s, m):
            m_new = jnp.maximum(m, jnp.max(s, axis=0, keepdims=True))
            return m_new, jnp.exp2(m - m_new), jnp.exp2(s - m_new).astype(BF16)

        ones_rows = jnp.ones((ATT_SUM_ROWS, TK), BF16)

        def pv(j, h, p):
            return dot(jnp.concatenate([vt_ref[j, lanes(h), :], ones_rows], axis=0), p)

        kk = lax.broadcasted_iota(jnp.int32, (TK, 2 * TQ), 0)
        qq = lax.broadcasted_iota(jnp.int32, (TK, 2 * TQ), 1)
        rel_chunk = jnp.where(qq >= TQ, qq - TQ, qq) // CHUNK - kk // CHUNK

        def masked(s, j):
            return jnp.where(rel_chunk >= j * (TK // CHUNK) - qi * (TQ // CHUNK), s, NEG_BIG)

        def step(j, carries, s_cur, p_cur, s_nxt, p_prev):
            pend = [pv(jnp.maximum(j - 1, 0), h, p_prev[h]) for h in heads]
            for h in heads:
                s_nxt[h] = scores(j + 1, h)
            out = []
            for h in heads:
                m, alpha = carries[h]
                acc_buf[h] = alpha * acc_buf[h] + pend[h]
                m, alpha, p = softmax_step(s_cur[h], m)
                p_cur[h] = p
                out.append((m, alpha))
            return tuple(out)

        def pair(i, carries):
            carries = step(2 * i, carries, s_a, p_a, s_b, p_b)
            return step(2 * i + 1, carries, s_b, p_b, s_a, p_a)

        for h in heads:
            s_a[h] = scores(0, h)
            p_b[h] = jnp.zeros((TK, 2 * TQ), BF16)
            acc_buf[h] = jnp.zeros(acc_buf.shape[1:], F32)
        init = tuple((jnp.full((1, 2 * TQ), NEG_BIG, F32), jnp.ones((1, 2 * TQ), F32))
                     for _ in heads)
        carries = lax.fori_loop(0, qi, pair, init)
        ja = 2 * qi
        late = lambda a: jnp.concatenate([a[:, TK:TQ], a[:, TQ + TK:]], axis=1)
        pend = [pv(jnp.maximum(ja - 1, 0), h, p_b[h]) for h in heads]
        for h in heads:
            k_last = k_ref[pl.ds(pl.multiple_of((ja + 1) * TK, TK), TK), lanes(h)]
            s_b[h, :, 0:TQ] = dot(k_last, late(qqt[h]))
        mid = []
        for h in heads:
            m, alpha = carries[h]
            acc_buf[h] = alpha * acc_buf[h] + pend[h]
            m, alpha, p = softmax_step(masked(s_a[h], ja), m)
            p_a[h] = p
            mid.append((m, alpha))
        pend = [pv(ja, h, p_a[h]) for h in heads]
        k_in = lax.broadcasted_iota(jnp.int32, (TK, TQ), 0)
        q_in = lax.broadcasted_iota(jnp.int32, (TK, TQ), 1) % TK
        diagonal = k_in // CHUNK <= q_in // CHUNK
        for h in heads:
            m, alpha = mid[h]
            acc = alpha * acc_buf[h] + pend[h]
            m_l, alpha_l, p_l = softmax_step(jnp.where(diagonal, s_b[h, :, 0:TQ], NEG_BIG), late(m))
            acc_l = alpha_l * late(acc) + pv(ja + 1, h, p_l)
            acc = jnp.concatenate([acc[:, :TK], acc_l[:, :TK], acc[:, TQ:TQ + TK], acc_l[:, TK:]], axis=1)
            o = acc[:DA_V_DIM] / acc[DA_V_DIM:DA_V_DIM + 1]
            a = o[:, :TQ] - lam * o[:, TQ:]
            y = a * lax.rsqrt(jnp.mean(a * a, axis=0, keepdims=True) + NORM_EPS)
            y = (y * g_ref[...]) * (1.0 - LAMBDA_INIT)
            o_ref[q_rows, lanes(h)] = y.T.astype(BF16)
        return c

    lax.fori_loop(0, seq // TQ, q_tile, 0)


def _attention(q, k, vt, lq1, lk1, lq2, lk2, g_col, bsz, seq):
    t = q.shape[0]
    nkv = seq // TK
    hw = ATT_HEADS_PER_STEP * LANES
    vec = lambda a: pl.BlockSpec(a.shape, lambda b, h: (0, 0))
    return pl.pallas_call(
        _attn_kernel,
        grid=(bsz, DA_HEADS // ATT_HEADS_PER_STEP),
        in_specs=[
            vec(lq1), vec(lk1), vec(lq2), vec(lk2), vec(g_col),
            pl.BlockSpec((seq, hw), lambda b, h: (b, h)),
            pl.BlockSpec((seq, hw), lambda b, h: (b, h)),
            pl.BlockSpec((nkv, hw, TK), lambda b, h: (b, h, 0)),
        ],
        out_specs=pl.BlockSpec((seq, hw), lambda b, h: (b, h)),
        out_shape=jax.ShapeDtypeStruct((t, DA_HEADS * DA_V_DIM), BF16),
        scratch_shapes=[
            pltpu.VMEM((ATT_HEADS_PER_STEP, TK, 2 * TQ), F32),
            pltpu.VMEM((ATT_HEADS_PER_STEP, TK, 2 * TQ), F32),
            pltpu.VMEM((ATT_HEADS_PER_STEP, TK, 2 * TQ), BF16),
            pltpu.VMEM((ATT_HEADS_PER_STEP, TK, 2 * TQ), BF16),
            pltpu.VMEM((ATT_HEADS_PER_STEP, DA_V_DIM + ATT_SUM_ROWS, 2 * TQ), F32),
        ],
        compiler_params=_cparams(("parallel", "parallel")),
        name="attention",
    )(lq1, lk1, lq2, lk2, g_col, q, k, vt)


def _post_mix_kernel(o_ref, p_ref, ph_ref, sga_ref, sgp_ref, x_ref, mod_ref,
                     watt_hbm, wpg_hbm, ps_ref, wpb_hbm, wout_hbm, gpm_ref, gpf_ref,
                     wrt_ref, br_ref,
                     x1_ref, prob_ref, pos_ref, cnt_ref, tab_ref, xs_hbm,
                     carry_ref, cur_ref, free_ref, u2t, zeros_v, pos_v, pos_s, meta_v, meta_s,
                     watt_ref, wpg_ref, wpb_ref, wout_ref, stage, sem_w,
                     sem_rows, sem_pos, sem_zero,
                     *, steps_per_seq, n_blocks):
    i = pl.program_id(0)
    n_steps = pl.num_programs(0)
    tm = x_ref.shape[0]
    n_experts = wrt_ref.shape[0]
    n_sub = x_ref.shape[1] // LANES
    slot = i % 2
    prev = 1 - slot
    dot = functools.partial(jnp.dot, preferred_element_type=F32)

    def pos_copy(s):
        return pltpu.make_async_copy(pos_v.at[s], pos_s.at[s], sem_pos.at[s])

    def row_copy(s, k, r):
        dst = pos_s[s, k, r]
        return pltpu.make_async_copy(
            u2t.at[s, pl.ds(r * n_sub, n_sub), :],
            xs_hbm.at[pl.ds(pl.multiple_of(dst * n_sub, n_sub), n_sub), :], sem_rows.at[s])

    def wait_rows(s):
        for _ in range(TOP_K):
            pltpu.make_async_copy(u2t.at[s], xs_hbm.at[pl.ds(0, tm * n_sub), :], sem_rows.at[s]).wait()

    @pl.when(i == 0)
    def _():
        carry_ref[...] = jnp.zeros_like(carry_ref)
        cur_ref[...] = jnp.zeros_like(cur_ref)
        free_ref[...] = jnp.zeros_like(free_ref)
        tab_ref[...] = jnp.zeros_like(tab_ref)
        zeros_v[...] = jnp.zeros_like(zeros_v)
        for src, dst in ((watt_hbm, watt_ref), (wpg_hbm, wpg_ref), (wpb_hbm, wpb_ref),
                         (wout_hbm, wout_ref)):
            rows, cols = src.shape
            step_c = min(cols, WEIGHT_CHUNK)
            windows = [(0, rows, c0, step_c) for c0 in range(0, cols, step_c)]

            def sink(n, tile, dst=dst, windows=windows):
                c0, nc = windows[n][2], windows[n][3]
                dst[:, c0:c0 + nc] = tile.astype(BF16)
            _stream_windows(src, windows, stage, sem_w, sink)
        u2t[1] = jnp.zeros(u2t.shape[1:], F32)
        spare = (n_blocks * TM_EXP
                 + lax.broadcasted_iota(jnp.int32, (TOP_K, tm), 0) * tm
                 + lax.broadcasted_iota(jnp.int32, (TOP_K, tm), 1))
        pos_v[1] = spare
        pos_copy(1).start()

    pos_copy(prev).wait()

    for r in range(tm):
        for k in range(TOP_K):
            row_copy(prev, k, r).start(priority=k % 2)

    y_att = dot(o_ref[...], watt_ref[...])

    first = (i % steps_per_seq) == 0
    halo = jnp.where(first, 0.0, ph_ref[...])
    ext = jnp.concatenate([halo, p_ref[...]], axis=0)
    t_in_seq = (i % steps_per_seq) * tm + lax.broadcasted_iota(jnp.int32, (tm, 1), 0)
    pooled = []
    for g, w in enumerate(POOL_WINDOWS):
        e = ext[:, g * POOL_GROUP_DIM:(g + 1) * POOL_GROUP_DIM]
        acc, span = e, 1
        while span < w:
            acc = acc[span:] + acc[:-span]
            span *= 2
        win = acc[POOL_HALO - (w - 1):]
        cnt = jnp.minimum(t_in_seq + 1, w).astype(F32)
        mixed = win / cnt - e[POOL_HALO:]
        pooled.append(dot(mixed.astype(BF16), wpg_ref[g * POOL_GROUP_DIM:(g + 1) * POOL_GROUP_DIM, :]))
    y_pool_in = jnp.concatenate(pooled, axis=1) * ps_ref[...]
    y_pool = dot(y_pool_in.astype(BF16), wpb_ref[...])

    merged = sga_ref[...].astype(F32) * y_att + sgp_ref[...].astype(F32) * y_pool
    mix_out = dot(merged.astype(BF16), wout_ref[...])
    gate_m = mod_ref[0, 2:3, :]
    shift_f = mod_ref[0, 3:4, :]
    scale_f = mod_ref[0, 4:5, :]
    x1 = x_ref[...] + gate_m * (_rms(mix_out) * gpm_ref[...])
    x1_ref[...] = x1
    u2 = (_rms(x1) * gpf_ref[...]) * (1.0 + scale_f) + shift_f

    logits = _split_dot(wrt_ref[...], u2, (((1,), (1,)), ((), ()))) + br_ref[...]
    erow = lax.broadcasted_iota(jnp.int32, logits.shape, 0)
    work = logits
    vals, idxs = [], []
    for _ in range(TOP_K):
        mx = jnp.max(work, axis=0, keepdims=True)
        ix = jnp.min(jnp.where(work == mx, erow, n_experts), axis=0, keepdims=True)
        vals.append(mx)
        idxs.append(ix)
        work = jnp.where(erow == ix, -jnp.inf, work)
    exps = [jnp.exp(vv - vals[0]) for vv in vals]
    denom = exps[0] + exps[1] + exps[2] + exps[3]
    prob_ref[...] = jnp.concatenate([e / denom for e in exps], axis=0)

    onehot = jnp.zeros(logits.shape, F32)
    for ix in idxs:
        onehot = onehot + (erow == ix).astype(F32)
    rr = lax.broadcasted_iota(jnp.int32, (tm, tm), 0)
    cc = lax.broadcasted_iota(jnp.int32, (tm, tm), 1)
    earlier = (rr < cc).astype(BF16)
    carry = carry_ref[...]
    before = dot(onehot.astype(BF16), earlier) + carry

    blk_rows = float(TM_EXP)
    total = carry + jnp.sum(onehot, axis=1, keepdims=True)
    blocks_old = jnp.floor((carry + (blk_rows - 1.0)) / blk_rows)
    opened = jnp.floor((total + (blk_rows - 1.0)) / blk_rows) - blocks_old
    e_r = lax.broadcasted_iota(jnp.int32, (n_experts, n_experts), 0)
    e_c = lax.broadcasted_iota(jnp.int32, (n_experts, n_experts), 1)
    lower = (e_c < e_r).astype(BF16)
    opened_before = dot(lower, jnp.broadcast_to(opened, (n_experts, LANES)).astype(BF16))[:, 0:1]
    new_blk = free_ref[...] + opened_before
    cur_blk = cur_ref[...]
    boundary = blocks_old * blk_rows
    blk_of = jnp.where(before < boundary, cur_blk, new_blk)
    pos_rows = []
    for ix in idxs:
        pick = erow == ix
        rank = jnp.sum(jnp.where(pick, before, 0.0), axis=0, keepdims=True)
        blk_id = jnp.sum(jnp.where(pick, blk_of, 0.0), axis=0, keepdims=True)
        within = rank - jnp.floor(rank / blk_rows) * blk_rows
        pos_rows.append((blk_id * blk_rows + within).astype(jnp.int32))
    pos = jnp.concatenate(pos_rows, axis=0)
    pos_ref[...] = pos
    tab_col = lax.broadcasted_iota(jnp.int32, tab_ref.shape, 1).astype(F32)
    tab_ref[...] = jnp.where(jnp.logical_and(tab_col == blocks_old, opened > 0.0),
                             new_blk.astype(jnp.int32), tab_ref[...])
    cur_ref[...] = jnp.where(opened > 0.0, new_blk, cur_blk)
    free_ref[...] = free_ref[...] + jnp.sum(opened, axis=0, keepdims=True)
    carry_ref[...] = total
    cnt_ref[...] = total.astype(jnp.int32)

    @pl.when(i > 0)
    def _():
        wait_rows(slot)

    for s in range(n_sub):
        u2t[slot, pl.ds(s, tm, stride=n_sub), :] = u2[:, s * LANES:(s + 1) * LANES]
    pos_v[slot] = pos
    pos_copy(slot).start()

    @pl.when(i == n_steps - 1)
    def _():
        pos_copy(slot).wait()

        def issue(r, c):
            for k in range(TOP_K):
                row_copy(slot, k, r).start(priority=k % 2)
            return c
        lax.fori_loop(0, tm, issue, 0)

        eye = e_r == e_c
        as_row = lambda col: jnp.sum(jnp.where(eye, col, 0.0), axis=0, keepdims=True).astype(jnp.int32)
        used = total - jnp.floor(total / blk_rows) * blk_rows
        meta = jnp.concatenate([as_row(used), as_row(cur_ref[...]),
                                jnp.broadcast_to(free_ref[...].astype(jnp.int32), (1, n_experts))], axis=0)
        meta_v[...] = jnp.zeros_like(meta_v)
        meta_v[0:3, 0:n_experts] = meta
        meta_copy = pltpu.make_async_copy(meta_v, meta_s, sem_zero)
        meta_copy.start()
        meta_copy.wait()

        sizes = [TM_EXP >> (b + 1) for b in range(TM_EXP.bit_length() - 1)]
        zero_run = lambda row0, n: pltpu.make_async_copy(
            zeros_v.at[pl.ds(0, n * n_sub), :],
            xs_hbm.at[pl.ds(pl.multiple_of(row0 * n_sub, n_sub), n * n_sub), :], sem_zero)
        issued = [jnp.int32(0) for _ in sizes]
        for e in range(n_experts):
            first = meta_s[0, e]
            row0 = meta_s[1, e] * TM_EXP + first
            pad = jnp.where(first > 0, TM_EXP - first, 0)
            for b, n in enumerate(sizes):
                take = (pad & n) != 0

                @pl.when(take)
                def _(row0=row0, n=n):
                    zero_run(row0, n).start()
                issued[b] = issued[b] + take.astype(jnp.int32)
                row0 = row0 + jnp.where(take, n, 0)
        n_open = meta_s[2, 0]

        def whole(b, c):
            zero_run(b * TM_EXP, TM_EXP).start()
            return c
        lax.fori_loop(n_open, n_blocks, whole, 0)

        for b, n in enumerate(sizes):
            lax.fori_loop(0, issued[b], lambda _, c, n=n: (zero_run(0, n).wait(), c)[1], 0)
        lax.fori_loop(n_open, n_blocks, lambda _, c: (zero_run(0, TM_EXP).wait(), c)[1], 0)
        wait_rows(prev)
        wait_rows(slot)


def _post_mix(o, p, sga, sgp, x2, mod3, w_att, w_pg, pool_scale, w_pb, w_out,
              g_post_mix, g_pre_ffn, w_router_t, b_router_col, seq):
    t, d = x2.shape
    tm = TM_MIX
    steps_per_seq = seq // tm
    n_experts = w_router_t.shape[0]
    pw = p.shape[1]
    n_sub = d // LANES
    halo_blocks = tm // POOL_HALO
    row = lambda w: pl.BlockSpec((tm, w), lambda i: (i, 0))
    col = lambda h: pl.BlockSpec((h, tm), lambda i: (0, i))
    full2 = lambda a: pl.BlockSpec(a.shape, lambda i: (0, 0))
    hbm = pl.BlockSpec(memory_space=pl.ANY)
    assert tm <= TM_EXP and (tm * TOP_K) % TM_EXP == 0 and t % TM_EXP == 0
    n_blocks = (t * TOP_K) // TM_EXP + n_experts
    tab_w = -(-(t // TM_EXP) // LANES) * LANES
    kern = functools.partial(_post_mix_kernel, steps_per_seq=steps_per_seq, n_blocks=n_blocks)
    xs_rows = n_blocks * TM_EXP + TOP_K * tm
    return pl.pallas_call(
        kern,
        grid=(t // tm,),
        in_specs=[
            row(d), row(pw),
            pl.BlockSpec((POOL_HALO, pw), lambda i: (jnp.maximum(i * halo_blocks - 1, 0), 0)),
            row(d), row(d), row(d),
            pl.BlockSpec((1, N_MOD, d), lambda i: (i // steps_per_seq, 0, 0)),
            hbm, hbm, full2(pool_scale), hbm, hbm, full2(g_post_mix), full2(g_pre_ffn),
            full2(w_router_t), full2(b_router_col),
        ],
        out_specs=[
            row(d), col(TOP_K), col(TOP_K),
            pl.BlockSpec((n_experts, 1), lambda i: (0, 0)),
            pl.BlockSpec((n_experts, tab_w), lambda i: (0, 0)),
            pl.BlockSpec(memory_space=pl.ANY),
        ],
        out_shape=[
            jax.ShapeDtypeStruct((t, d), F32),
            jax.ShapeDtypeStruct((TOP_K, t), F32),
            jax.ShapeDtypeStruct((TOP_K, t), jnp.int32),
            jax.ShapeDtypeStruct((n_experts, 1), jnp.int32),
            jax.ShapeDtypeStruct((n_experts, tab_w), jnp.int32),
            jax.ShapeDtypeStruct((xs_rows * n_sub, LANES), F32),
        ],
        scratch_shapes=[
            pltpu.VMEM((n_experts, 1), F32),
            pltpu.VMEM((n_experts, 1), F32),
            pltpu.VMEM((1, 1), F32),
            pltpu.VMEM((2, tm * n_sub, LANES), F32),
            pltpu.VMEM((TM_EXP * n_sub, LANES), F32),
            pltpu.VMEM((2, TOP_K, tm), jnp.int32),
            pltpu.SMEM((2, TOP_K, tm), jnp.int32),
            pltpu.VMEM((SUBLANES, LANES), jnp.int32),
            pltpu.SMEM((SUBLANES, LANES), jnp.int32),
            pltpu.VMEM(w_att.shape, BF16),
            pltpu.VMEM(w_pg.shape, BF16),
            pltpu.VMEM(w_pb.shape, BF16),
            pltpu.VMEM(w_out.shape, BF16),
            pltpu.VMEM((2, max(w_att.shape[0], w_out.shape[0]), WEIGHT_CHUNK), F32),
            pltpu.SemaphoreType.DMA((2,)),
            pltpu.SemaphoreType.DMA((2,)),
            pltpu.SemaphoreType.DMA((2,)),
            pltpu.SemaphoreType.DMA,
        ],
        compiler_params=_cparams(("arbitrary",)),
        name="post_mix",
    )(o, p, p, sga, sgp, x2, mod3, w_att, w_pg, pool_scale, w_pb, w_out,
      g_post_mix, g_pre_ffn, w_router_t, b_router_col)


def _experts_kernel(te_ref, xb_ref, yb_ref, nt_ref, nv_ref, nx_ref, xs_ref, wgu_hbm, bgu_ref, wd_hbm,
                    bd_ref, y_ref, wgu_bf, wd_bf, wgu_f32, wd_f32, sem, *, n_sub):
    j = pl.program_id(0)
    tm = TM_EXP
    half = tm // 2
    d_ff = wd_bf.shape[0]
    open_tile = j < nt_ref[0]

    def fetch(expert):
        return (pltpu.make_async_copy(wgu_hbm.at[expert], wgu_f32, sem.at[0]),
                pltpu.make_async_copy(wd_hbm.at[expert], wd_f32, sem.at[1]))

    @pl.when(open_tile)
    def _():
        changed = jnp.logical_or(j == 0, te_ref[j] != te_ref[jnp.maximum(j - 1, 0)])

        @pl.when(changed)
        def _():
            @pl.when(j == 0)
            def _():
                for c in fetch(te_ref[j]):
                    c.start()
            for c in fetch(te_ref[j]):
                c.wait()
            wgu_bf[...] = wgu_f32[...].astype(BF16)
            wd_bf[...] = wd_f32[...].astype(BF16)

            @pl.when(nx_ref[j] >= 0)
            def _():
                for c in fetch(nx_ref[j]):
                    c.start()

    def mlp(rows):
        xs = jnp.concatenate(
            [xs_ref[pl.ds(s, rows, stride=n_sub), :] for s in range(n_sub)], axis=1).astype(BF16)
        gu = jnp.dot(xs, wgu_bf[...], preferred_element_type=F32) + bgu_ref[0]
        gate = jnp.minimum(gu[:, :d_ff], SWIGLU_LIMIT)
        up = jnp.clip(gu[:, d_ff:], -SWIGLU_LIMIT, SWIGLU_LIMIT)
        act = (up + 1.0) * (gate * jax.nn.sigmoid(SWIGLU_ALPHA * gate))
        y = jnp.dot(act.astype(BF16), wd_bf[...], preferred_element_type=F32) + bd_ref[0]
        for s in range(n_sub):
            y_ref[pl.ds(s, rows, stride=n_sub), :] = y[:, s * LANES:(s + 1) * LANES]

    few = nv_ref[j] <= half

    @pl.when(jnp.logical_and(open_tile, jnp.logical_not(few)))
    def _():
        mlp(tm)

    @pl.when(jnp.logical_and(open_tile, few))
    def _():
        mlp(half)
        y_ref[pl.ds(half * n_sub, half * n_sub), :] = jnp.zeros((half * n_sub, LANES), F32)

    @pl.when(j >= nt_ref[0])
    def _():
        y_ref[...] = jnp.zeros_like(y_ref)


def _experts(tile_expert, xs_block, y_block, rows_valid, next_expert, n_tiles, xs_tiles,
             w_gate_up, b_gate_up, w_down, b_down):
    n_exp, d, d_gu = w_gate_up.shape
    d_ff = w_down.shape[1]
    n_sub = d // LANES
    tm = TM_EXP
    max_tiles = tile_expert.shape[0]
    by_expert = lambda j, te, xb, yb, nt, nv, nx: (te[j], 0, 0)
    by_block = lambda j, te, xb, yb, nt, nv, nx: (yb[j], 0)
    hbm = pl.BlockSpec(memory_space=pl.ANY)
    grid_spec = pltpu.PrefetchScalarGridSpec(
        num_scalar_prefetch=6,
        grid=(max_tiles,),
        in_specs=[
            pl.BlockSpec((tm * n_sub, LANES), lambda j, te, xb, yb, nt, nv, nx: (xb[j], 0)),
            hbm,
            pl.BlockSpec((1, 1, d_gu), by_expert),
            hbm,
            pl.BlockSpec((1, 1, d), by_expert),
        ],
        out_specs=pl.BlockSpec((tm * n_sub, LANES), by_block),
        scratch_shapes=[
            pltpu.VMEM((d, d_gu), BF16),
            pltpu.VMEM((d_ff, d), BF16),
            pltpu.VMEM((d, d_gu), F32),
            pltpu.VMEM((d_ff, d), F32),
            pltpu.SemaphoreType.DMA((2,)),
        ],
    )
    return pl.pallas_call(
        functools.partial(_experts_kernel, n_sub=n_sub),
        grid_spec=grid_spec,
        out_shape=jax.ShapeDtypeStruct((max_tiles * tm * n_sub, LANES), F32),
        compiler_params=_cparams(("arbitrary",)),
        name="experts",
    )(tile_expert, xs_block, y_block, n_tiles, rows_valid, next_expert, xs_tiles,
      w_gate_up, b_gate_up.reshape(n_exp, 1, d_gu), w_down, b_down.reshape(n_exp, 1, d))


def _combine_kernel(pos_ref, y_hbm, prob_ref, x1_ref, mod_ref, g_ref, o_ref, ybuf, sem,
                    *, n_sub, n_tokens):
    i = pl.program_id(0)
    n_steps = pl.num_programs(0)
    tm = TM_COMB
    slot = i % 2

    def gather(step, s):
        for r in range(tm):
            for k in range(TOP_K):
                row = pos_ref[k * n_tokens + step * tm + r]
                pltpu.make_async_copy(
                    y_hbm.at[pl.ds(pl.multiple_of(row * n_sub, n_sub), n_sub), :],
                    ybuf.at[s, k, pl.ds(r * n_sub, n_sub), :], sem.at[s]).start(priority=k % 2)

    @pl.when(i == 0)
    def _():
        gather(0, 0)

    @pl.when(i + 1 < n_steps)
    def _():
        gather(i + 1, 1 - slot)

    for k in range(TOP_K):
        pltpu.make_async_copy(y_hbm.at[pl.ds(0, tm * n_sub), :], ybuf.at[slot, k], sem.at[slot]).wait()

    prob = prob_ref[...]
    f = None
    for k in range(TOP_K):
        yk = jnp.concatenate(
            [ybuf[slot, k, pl.ds(s, tm, stride=n_sub), :] for s in range(n_sub)], axis=1)
        term = yk * prob[:, k:k + 1]
        f = term if f is None else f + term
    gate_f = mod_ref[0, 5:6, :]
    o_ref[...] = x1_ref[...] + gate_f * (_rms(f) * g_ref[...])


def _combine(pos_flat, y_tiles, probs, x1, mod3, g_post_ffn, seq):
    t, d = x1.shape
    tm = TM_COMB
    n_sub = d // LANES
    steps_per_seq = seq // tm
    grid_spec = pltpu.PrefetchScalarGridSpec(
        num_scalar_prefetch=1,
        grid=(t // tm,),
        in_specs=[
            pl.BlockSpec(memory_space=pl.ANY),
            pl.BlockSpec((tm, TOP_K), lambda i, pos: (i, 0)),
            pl.BlockSpec((tm, d), lambda i, pos: (i, 0)),
            pl.BlockSpec((1, N_MOD, d), lambda i, pos: (i // steps_per_seq, 0, 0)),
            pl.BlockSpec((1, d), lambda i, pos: (0, 0)),
        ],
        out_specs=pl.BlockSpec((tm, d), lambda i, pos: (i, 0)),
        scratch_shapes=[
            pltpu.VMEM((2, TOP_K, tm * n_sub, LANES), F32),
            pltpu.SemaphoreType.DMA((2,)),
        ],
    )
    return pl.pallas_call(
        functools.partial(_combine_kernel, n_sub=n_sub, n_tokens=t),
        grid_spec=grid_spec,
        out_shape=jax.ShapeDtypeStruct((t, d), F32),
        compiler_params=_cparams(("arbitrary",)),
        name="combine",
    )(pos_flat, y_tiles, probs, x1, mod3, g_post_ffn)


def _rope_freq_row():
    half = ROPE_DIM // 2
    inv_freq = ROPE_THETA ** (-jnp.arange(0, ROPE_DIM, 2, dtype=F32) / ROPE_DIM)
    head = jnp.concatenate([inv_freq, inv_freq, jnp.zeros((DA_HEAD_DIM - 2 * half,), F32)])
    return jnp.tile(head, LANES // DA_HEAD_DIM).reshape(1, LANES)


def _tile_tables(counts, block_tab, n_tokens):
    tm = TM_EXP
    n_exp = counts.shape[0]
    max_tiles = (n_tokens * TOP_K) // tm + n_exp
    tiles = (counts + tm - 1) // tm
    tile_end = jnp.cumsum(tiles)
    n_tiles = tile_end[-1]
    j = jnp.arange(max_tiles, dtype=jnp.int32)
    jj = jnp.minimum(j, n_tiles - 1)
    past = (jj[:, None] >= tile_end[None, :]).astype(jnp.int32)
    tile_expert = jnp.sum(past, axis=1)
    nth = jj - jnp.sum(past * tiles[None, :], axis=1)
    open_block = block_tab[tile_expert, nth]
    y_block = jnp.where(j < n_tiles, open_block, j)
    is_expert = (tile_expert[:, None] == jnp.arange(n_exp, dtype=jnp.int32)[None, :]).astype(jnp.int32)
    rows_valid = jnp.where(j < n_tiles,
                           jnp.minimum(jnp.sum(is_expert * counts[None, :], axis=1) - nth * tm, tm), 0)
    ids = jnp.arange(n_exp, dtype=jnp.int32)
    later = jnp.logical_and(ids[None, :] > ids[:, None], tiles[None, :] > 0)
    nxt = jnp.min(jnp.where(later, ids[None, :], n_exp), axis=1)
    nxt = jnp.where(nxt == n_exp, -1, nxt)
    next_expert = jnp.sum(is_expert * nxt[None, :], axis=1)
    as_i32 = lambda a: a.astype(jnp.int32)
    return (as_i32(tile_expert), as_i32(open_block), as_i32(y_block), as_i32(rows_valid),
            as_i32(next_expert), as_i32(n_tiles.reshape(1)))


def kernel(x, c, positions, w_mod, b_mod, g_pre_mix, w_in, lambda_q1, lambda_k1, lambda_q2, lambda_k2, g_sub, w_pool_group, pool_scale, w_att_branch, w_pool_branch, w_out, g_post_mix, g_pre_ffn, w_router, b_router, w_gate_up, b_gate_up, w_down, b_down, g_post_ffn):
    bsz, seq, d = x.shape
    assert w_mod.shape[0] == 1, "single layer"
    assert seq % TQ == 0 and seq % TM_PROJ == 0 and seq % TM_MIX == 0 and seq % TM_COMB == 0
    t = bsz * seq
    assert t % TM_EXP == 0
    x2 = x.reshape(t, d)
    qk_w = 2 * DA_HEADS * DA_HEAD_DIM
    v_w = DA_HEADS * DA_V_DIM
    pool_w = len(POOL_WINDOWS) * POOL_GROUP_DIM
    n_experts = w_router.shape[2]

    mod3 = _mod(c, w_mod[0], b_mod[0]).reshape(bsz, N_MOD, d)

    bounds = [0, qk_w, 2 * qk_w, 2 * qk_w + v_w, 2 * qk_w + v_w + pool_w,
              2 * qk_w + v_w + pool_w + d, 2 * qk_w + v_w + pool_w + 2 * d]
    q, k, vt, p, sga, sgp = _in_proj(x2, mod3, g_pre_mix, positions.reshape(t, 1), _rope_freq_row(),
                                     w_in[0], bounds, seq)

    row64 = lambda a: a.reshape(1, DA_HEAD_DIM)
    o = _attention(q, k, vt, row64(lambda_q1[0]), row64(lambda_k1[0]), row64(lambda_q2[0]),
                   row64(lambda_k2[0]), g_sub.reshape(DA_V_DIM, 1), bsz, seq)

    x1, probs_t, pos_t, counts, block_tab, xs_tiles = _post_mix(
        o, p, sga, sgp, x2, mod3, w_att_branch[0], w_pool_group[0].reshape(pool_w, POOL_GROUP_DIM),
        pool_scale, w_pool_branch[0], w_out[0],
        g_post_mix, g_pre_ffn, w_router[0].T, b_router.reshape(n_experts, 1), seq)

    tile_expert, xs_block, y_block, rows_valid, next_expert, n_tiles = _tile_tables(
        counts[:, 0], block_tab, t)
    y_tiles = _experts(tile_expert, xs_block, y_block, rows_valid, next_expert, n_tiles, xs_tiles,
                       w_gate_up[0], b_gate_up[0], w_down[0], b_down[0])
    out = _combine(pos_t.reshape(-1), y_tiles, probs_t.T, x1, mod3, g_post_ffn, seq)
    return out.reshape(bsz, seq, d)
```
